```python
import jax, jax.numpy as jnp
from jax import lax
import numpy as np

D_MODEL = 1024
BATCH = 2
SEQ = 8192
DEPTH = 2

N_META = 16
BLOCK = 128
META_PAD = BLOCK - N_META
RMS_EPS = 1e-6
NEG_INF = -1e30

LRU_WIDTH = D_MODEL // 2
LRU_HEADS = 8
LRU_BLOCK_W = LRU_WIDTH // LRU_HEADS
CONV_W = 4
LRU_C = 8.0

ATT_HEADS = 8
ATT_KV_HEADS = 2
ATT_HEAD_DIM = 64
WINDOW = 128
ROPE_THETA = 500000.0
ROT_DIM = ATT_HEAD_DIM // 4
Q_WIDTH = ATT_HEADS * ATT_HEAD_DIM
KV_WIDTH = ATT_KV_HEADS * ATT_HEAD_DIM
AB_IN_WIDTH = 2 * LRU_WIDTH + Q_WIDTH + 2 * KV_WIDTH
AB_OUT_WIDTH = LRU_WIDTH + Q_WIDTH
SPLIT_AB = (LRU_WIDTH, 2 * LRU_WIDTH, 2 * LRU_WIDTH + Q_WIDTH, 2 * LRU_WIDTH + Q_WIDTH + KV_WIDTH)

RET_HEADS = 4
RET_QK_DIM = D_MODEL // RET_HEADS
RET_V_DIM = 2 * D_MODEL // RET_HEADS
RET_THETA = 10000.0
RET_IN_WIDTH = 6 * D_MODEL
SPLIT_RET = (D_MODEL, 2 * D_MODEL, 4 * D_MODEL)

D_FF = -(-8 * D_MODEL // (3 * 256)) * 256

kernel_name = 'hybrid_rglru_swa_retention_block'

F32 = jnp.float32


def _rmsnorm(x, gain):
    xf = x.astype(F32)
    y = xf * lax.rsqrt(jnp.mean(xf * xf, axis=-1, keepdims=True) + RMS_EPS)
    return (y * gain.astype(F32)).astype(x.dtype)


def _rms_noaffine(x):
    xf = x.astype(F32)
    return xf * lax.rsqrt(jnp.mean(xf * xf, axis=-1, keepdims=True) + RMS_EPS)


def _rope_tables(n_pos, rot_dim, theta):
    half = rot_dim // 2
    inv_freq = jnp.power(jnp.asarray(theta, F32), -jnp.arange(half, dtype=F32) / half)
    ang = jnp.arange(n_pos, dtype=F32)[:, None] * inv_freq[None, :]
    return jnp.cos(ang), jnp.sin(ang)


def _rotary(x, cos, sin):
    half = cos.shape[-1]
    c = cos[None, :, None, :]
    s = sin[None, :, None, :]
    x1 = x[..., :half].astype(F32)
    x2 = x[..., half:2 * half].astype(F32)
    rot = jnp.concatenate([x1 * c - x2 * s, x2 * c + x1 * s], axis=-1).astype(x.dtype)
    return jnp.concatenate([rot, x[..., 2 * half:]], axis=-1)


def _rg_lru(xr, conv_w, conv_b, w_a, b_a, w_i, b_i, lam):
    B, T, W = xr.shape
    xp = jnp.pad(xr, ((0, 0), (CONV_W - 1, 0), (0, 0)))
    xc = sum((xp[:, j:j + T, :] * conv_w[j] for j in range(CONV_W)), conv_b)
    xh = xc.reshape(B, T, LRU_HEADS, LRU_BLOCK_W)
    r = jax.nn.sigmoid((jnp.einsum('bthi,hij->bthj', xh, w_a) + b_a).astype(F32)).reshape(B, T, W)
    i = jax.nn.sigmoid((jnp.einsum('bthi,hij->bthj', xh, w_i) + b_i).astype(F32)).reshape(B, T, W)
    log_a = -LRU_C * r * jax.nn.softplus(-lam.astype(F32))
    a = jnp.exp(log_a)
    mult = jnp.sqrt(-jnp.expm1(2.0 * log_a))
    mult = jnp.where(jnp.arange(T)[None, :, None] == 0, 1.0, mult)
    b = mult * i * xc.astype(F32)

    def combine(left, right):
        a_l, b_l = left
        a_r, b_r = right
        return a_l * a_r, a_r * b_l + b_r

    _, h = lax.associative_scan(combine, (a, b), axis=1)
    return h.astype(xr.dtype)


def _swa_with_sinks(q, k, v, sinks):
    B, T, Hq, Dh = q.shape
    G = ATT_HEADS // ATT_KV_HEADS
    Tp = T + META_PAD
    NB = Tp // BLOCK
    qb = jnp.pad(q, ((0, 0), (META_PAD, 0), (0, 0), (0, 0))).reshape(B, NB, BLOCK, ATT_KV_HEADS, G, Dh)
    kpad = ((0, 0), (META_PAD + BLOCK, 0), (0, 0), (0, 0))
    kb = jnp.pad(k, kpad).reshape(B, NB + 1, BLOCK, ATT_KV_HEADS, Dh)
    vb = jnp.pad(v, kpad).reshape(B, NB + 1, BLOCK, ATT_KV_HEADS, Dh)
    k_win = jnp.concatenate([kb[:, :-1], kb[:, 1:]], axis=2)
    v_win = jnp.concatenate([vb[:, :-1], vb[:, 1:]], axis=2)
    k_meta = k[:, :N_META]
    v_meta = v[:, :N_META]
    scale = Dh ** -0.5
    s_meta = jnp.einsum('bnqhgd,bkhd->bnhgqk', qb, k_meta)
    s_win = jnp.einsum('bnqhgd,bnkhd->bnhgqk', qb, k_win)
    s = jnp.concatenate([s_meta, s_win], axis=-1).astype(F32) * scale
    nb = jnp.arange(NB)[:, None]
    qpos = nb * BLOCK + jnp.arange(BLOCK)[None, :] - META_PAD
    kpos = (nb - 1) * BLOCK + jnp.arange(2 * BLOCK)[None, :] - META_PAD
    qp = qpos[:, :, None]
    kp = kpos[:, None, :]
    win_mask = (kp >= N_META) & (kp <= qp) & (qp - kp < WINDOW)
    meta_mask = jnp.arange(N_META)[None, None, :] <= qp
    mask = jnp.concatenate([meta_mask, win_mask], axis=-1)
    s = jnp.where(mask[None, :, None, None], s, NEG_INF)
    sink = jnp.broadcast_to(sinks.astype(F32).reshape(1, 1, ATT_KV_HEADS, G, 1, 1), s.shape[:-1] + (1,))
    p = jax.nn.softmax(jnp.concatenate([s, sink], axis=-1), axis=-1)[..., :-1].astype(v.dtype)
    o = (jnp.einsum('bnhgqk,bkhd->bnqhgd', p[..., :N_META], v_meta)
         + jnp.einsum('bnhgqk,bnkhd->bnqhgd', p[..., N_META:], v_win))
    return o.reshape(B, Tp, Hq * Dh)[:, META_PAD:]


def _mixer_rglru_swa(h, cos, sin, w_in, conv_w, conv_b, w_a, b_a, w_i, b_i, lam,
                     q_gain, k_gain, sinks, w_out):
    B, T, _ = h.shape
    x_rnn, gate, q, k, v = jnp.split(h @ w_in, SPLIT_AB, axis=-1)
    y_rnn = jax.nn.gelu(gate) * _rg_lru(x_rnn, conv_w, conv_b, w_a, b_a, w_i, b_i, lam)
    q = _rotary(_rmsnorm(q.reshape(B, T, ATT_HEADS, ATT_HEAD_DIM), q_gain), cos, sin)
    k = _rotary(_rmsnorm(k.reshape(B, T, ATT_KV_HEADS, ATT_HEAD_DIM), k_gain), cos, sin)
    v = v.reshape(B, T, ATT_KV_HEADS, ATT_HEAD_DIM)
    y_att = _swa_with_sinks(q, k, v, sinks)
    return jnp.concatenate([y_rnn, y_att], axis=-1) @ w_out


def _mixer_retention(h, cos, sin, w_in, w_out):
    B, T, _ = h.shape
    q, k, v, g = jnp.split(h @ w_in, SPLIT_RET, axis=-1)
    q = _rotary(q.reshape(B, T, RET_HEADS, RET_QK_DIM), cos, sin)
    k = _rotary(k.reshape(B, T, RET_HEADS, RET_QK_DIM), cos, sin) * (RET_QK_DIM ** -0.5)
    v = v.reshape(B, T, RET_HEADS, RET_V_DIM)
    Tp = T + META_PAD
    NC = Tp // BLOCK

    def to_chunks(t):
        d = t.shape[-1]
        t = jnp.pad(t, ((0, 0), (META_PAD, 0), (0, 0), (0, 0)))
        return t.reshape(B, NC, BLOCK, RET_HEADS, d).transpose(1, 0, 3, 2, 4)

    qc, kc, vc = to_chunks(q), to_chunks(k), to_chunks(v)
    log_g = jnp.log1p(-jnp.exp2(-5.0 - jnp.arange(RET_HEADS, dtype=F32)))
    idx = jnp.arange(BLOCK, dtype=F32)
    diff = idx[:, None] - idx[None, :]
    decay_intra = jnp.where(diff >= 0, jnp.exp(jnp.maximum(diff, 0.0)[None] * log_g[:, None, None]), 0.0)
    decay_q = jnp.exp((idx + 1.0)[None] * log_g[:, None])
    decay_k = jnp.exp((BLOCK - 1.0 - idx)[None] * log_g[:, None])
    decay_chunk = jnp.exp(BLOCK * log_g)

    def step(state, chunk):
        qn, kn, vn = chunk
        vf = vn.astype(F32)
        qk = jnp.einsum('bhid,bhjd->bhij', qn, kn).astype(F32) * decay_intra
        o = (jnp.einsum('bhij,bhjv->bhiv', qk, vf)
             + jnp.einsum('bhid,bhdv->bhiv', qn.astype(F32) * decay_q[..., None], state))
        state = (decay_chunk[:, None, None] * state
                 + jnp.einsum('bhjd,bhjv->bhdv', kn.astype(F32) * decay_k[..., None], vf))
        return state, o

    state0 = jnp.zeros((B, RET_HEADS, RET_QK_DIM, RET_V_DIM), F32)
    _, o = lax.scan(step, state0, (qc, kc, vc))
    o = o.transpose(1, 0, 3, 2, 4).reshape(B, Tp, RET_HEADS, RET_V_DIM)[:, META_PAD:]
    o = _rms_noaffine(o).reshape(B, T, 2 * D_MODEL).astype(h.dtype)
    return (jax.nn.silu(g) * o) @ w_out


def _swiglu(h, w_gu, w_down):
    gate, up = jnp.split(h @ w_gu, 2, axis=-1)
    return (jax.nn.silu(gate) * up) @ w_down


def setup_inputs(seed: int = 0) -> dict:
    key = jax.random.key(seed)
    keys = jax.random.split(key, 24)
    n_even = (DEPTH + 1) // 2
    n_odd = DEPTH // 2

    def normal(k, shape, scale):
        return jax.random.normal(k, shape, F32) * scale

    def gain(k, shape):
        return 1.0 + 0.02 * jax.random.normal(k, shape, F32)

    u = jax.random.uniform(keys[10], (n_even, LRU_WIDTH), F32, minval=0.9, maxval=0.999)
    a = u ** (1.0 / LRU_C)
    lam = jnp.log(a) - jnp.log1p(-a)
    return {
        'x': normal(keys[0], (BATCH, SEQ, D_MODEL), 1.0),
        'meta_tokens': normal(keys[1], (N_META, D_MODEL), 1.0),
        'mix_norm_ab': gain(keys[2], (n_even, D_MODEL)),
        'ab_w_in': normal(keys[3], (n_even, D_MODEL, AB_IN_WIDTH), D_MODEL ** -0.5),
        'lru_conv_w': normal(keys[4], (n_even, CONV_W, LRU_WIDTH), CONV_W ** -0.5),
        'lru_conv_b': normal(keys[5], (n_even, LRU_WIDTH), 0.01),
        'lru_w_a': normal(keys[6], (n_even, LRU_HEADS, LRU_BLOCK_W, LRU_BLOCK_W), LRU_BLOCK_W ** -0.5),
        'lru_b_a': normal(keys[7], (n_even, LRU_HEADS, LRU_BLOCK_W), 0.01),
        'lru_w_i': normal(keys[8], (n_even, LRU_HEADS, LRU_BLOCK_W, LRU_BLOCK_W), LRU_BLOCK_W ** -0.5),
        'lru_b_i': normal(keys[9], (n_even, LRU_HEADS, LRU_BLOCK_W), 0.01),
        'lru_lambda': lam,
        'q_norm': gain(keys[11], (n_even, ATT_HEAD_DIM)),
        'k_norm': gain(keys[12], (n_even, ATT_HEAD_DIM)),
        'attn_sinks': normal(keys[13], (n_even, ATT_HEADS), 0.5),
        'ab_w_out': normal(keys[14], (n_even, AB_OUT_WIDTH, D_MODEL), AB_OUT_WIDTH ** -0.5),
        'mix_norm_ret': gain(keys[15], (n_odd, D_MODEL)),
        'ret_w_in': normal(keys[16], (n_odd, D_MODEL, RET_IN_WIDTH), D_MODEL ** -0.5),
        'ret_w_out': normal(keys[17], (n_odd, 2 * D_MODEL, D_MODEL), (2 * D_MODEL) ** -0.5),
        'ffn_norm': gain(keys[18], (DEPTH, D_MODEL)),
        'ffn_w_gu': normal(keys[19], (DEPTH, D_MODEL, 2 * D_FF), D_MODEL ** -0.5),
        'ffn_w_down': normal(keys[20], (DEPTH, D_FF, D_MODEL), D_FF ** -0.5),
    }


def reference(x, meta_tokens, mix_norm_ab, ab_w_in, lru_conv_w, lru_conv_b, lru_w_a, lru_b_a,
              lru_w_i, lru_b_i, lru_lambda, q_norm, k_norm, attn_sinks, ab_w_out,
              mix_norm_ret, ret_w_in, ret_w_out, ffn_norm, ffn_w_gu, ffn_w_down):
    B = x.shape[0]
    meta = jnp.broadcast_to(meta_tokens[None].astype(x.dtype), (B, N_META, D_MODEL))
    h = jnp.concatenate([meta, x], axis=1)
    T = h.shape[1]
    att_cos, att_sin = _rope_tables(T, ROT_DIM, ROPE_THETA)
    ret_cos, ret_sin = _rope_tables(T, RET_QK_DIM, RET_THETA)
    for layer in range(DEPTH):
        j = layer // 2
        if layer % 2 == 0:
            h = h + _mixer_rglru_swa(_rmsnorm(h, mix_norm_ab[j]), att_cos, att_sin, ab_w_in[j],
                                     lru_conv_w[j], lru_conv_b[j], lru_w_a[j], lru_b_a[j],
                                     lru_w_i[j], lru_b_i[j], lru_lambda[j], q_norm[j], k_norm[j],
                                     attn_sinks[j], ab_w_out[j])
        else:
            h = h + _mixer_retention(_rmsnorm(h, mix_norm_ret[j]), ret_cos, ret_sin,
                                     ret_w_in[j], ret_w_out[j])
        h = h + _swiglu(_rmsnorm(h, ffn_norm[layer]), ffn_w_gu[layer], ffn_w_down[layer])
    return h[:, N_META:]
```

```python
import functools
import math

import jax
import jax.numpy as jnp
from jax import lax
from jax.experimental import pallas as pl
from jax.experimental.pallas import tpu as pltpu

F32 = jnp.float32
BF16 = jnp.bfloat16

D_MODEL = 1024
N_META = 16
BLOCK = 128
META_PAD = BLOCK - N_META
RMS_EPS = 1e-6
NEG_INF = -1e30

LRU_WIDTH = 512
LRU_HEADS = 8
LRU_BLOCK_W = 64
CONV_W = 4
LRU_C = 8.0

ATT_HEADS = 8
ATT_KV_HEADS = 2
ATT_GROUP = ATT_HEADS // ATT_KV_HEADS
ATT_HEAD_DIM = 64
ROPE_THETA = 500000.0
ROT_DIM = 16
Q_WIDTH = 512
KV_WIDTH = 128
AB_IN_WIDTH = 2 * LRU_WIDTH + Q_WIDTH + 2 * KV_WIDTH

RET_HEADS = 4
RET_QK_DIM = 256
RET_V_DIM = 512
RET_THETA = 10000.0
RET_LOG_G = tuple(math.log1p(-(2.0 ** (-5.0 - h))) for h in range(RET_HEADS))

D_FF = 2816

LANES = 128
SUBLANES = 8
ROW_TILE = 640
LRU_TILE = 128
FFN_CHUNK = 256


def _rms_scale(x):
    return lax.rsqrt(jnp.mean(x * x, axis=-1, keepdims=True) + RMS_EPS)


def _silu(x):
    return x * jax.nn.sigmoid(x)


def _gelu_tanh(x):
    return 0.5 * x * (1.0 + jnp.tanh(0.7978845608028654 * (x + 0.044715 * (x * x * x))))


def _const_spec(shape):
    zeros = (0,) * len(shape)
    return pl.BlockSpec(shape, lambda *_: zeros, pipeline_mode=pl.Buffered(1))


def _inproj0_kernel(h_ref, gain_ref, w_ref, qg_ref, kg_ref, c_ref, s1_ref, s2_ref,
                    xr_ref, gate_ref, q_ref, k_ref, v_ref):
    h = h_ref[...]
    xn = (h * _rms_scale(h) * gain_ref[...]).astype(BF16)
    y = jnp.dot(xn, w_ref[...], preferred_element_type=F32)
    xr_ref[...] = y[:, :LRU_WIDTH]
    gate_ref[...] = y[:, LRU_WIDTH:2 * LRU_WIDTH]

    rows = h.shape[0]
    cos = c_ref[...]
    sin_lo = s1_ref[...]
    sin_hi = s2_ref[...]
    first_head = lax.broadcasted_iota(jnp.int32, (rows, LANES), 1) < ATT_HEAD_DIM

    def norm_rope(x, gain, scale):
        sq = x * x
        s_a = jnp.sum(jnp.where(first_head, sq, 0.0), axis=-1, keepdims=True)
        s_b = jnp.sum(jnp.where(first_head, 0.0, sq), axis=-1, keepdims=True)
        inv = jnp.where(first_head,
                        lax.rsqrt(s_a * (1.0 / ATT_HEAD_DIM) + RMS_EPS),
                        lax.rsqrt(s_b * (1.0 / ATT_HEAD_DIM) + RMS_EPS))
        xg = x * gain
        rot = (xg * cos + pltpu.roll(xg, LANES - ROT_DIM // 2, 1) * sin_lo
               + pltpu.roll(xg, ROT_DIM // 2, 1) * sin_hi)
        return rot * (inv * scale)

    q0 = 2 * LRU_WIDTH
    for j in range(Q_WIDTH // LANES):
        xq = y[:, q0 + j * LANES:q0 + (j + 1) * LANES]
        q_ref[:, j * LANES:(j + 1) * LANES] = norm_rope(
            xq, qg_ref[...], ATT_HEAD_DIM ** -0.5).astype(BF16)
    k0 = q0 + Q_WIDTH
    k_ref[...] = norm_rope(y[:, k0:k0 + KV_WIDTH], kg_ref[...], 1.0).astype(BF16)
    v_ref[...] = y[:, k0 + KV_WIDTH:k0 + 2 * KV_WIDTH].astype(BF16)


def _inproj0(hp, gain, w_in, q_gain, k_gain, cos_t, sin_lo_t, sin_hi_t, tiles_per_seq):
    rows = hp.shape[0]
    tm = ROW_TILE
    row = lambda w: pl.BlockSpec((tm, w), lambda i: (i, 0))
    tab = pl.BlockSpec((tm, LANES), lambda i: (i % tiles_per_seq, 0))
    return pl.pallas_call(
        _inproj0_kernel,
        grid=(rows // tm,),
        in_specs=[row(D_MODEL), _const_spec((1, D_MODEL)), _const_spec((D_MODEL, AB_IN_WIDTH)),
                  _const_spec((1, LANES)), _const_spec((1, LANES)), tab, tab, tab],
        out_specs=[row(LRU_WIDTH), row(LRU_WIDTH), row(Q_WIDTH), row(KV_WIDTH), row(KV_WIDTH)],
        out_shape=[jax.ShapeDtypeStruct((rows, LRU_WIDTH), F32),
                   jax.ShapeDtypeStruct((rows, LRU_WIDTH), F32),
                   jax.ShapeDtypeStruct((rows, Q_WIDTH), BF16),
                   jax.ShapeDtypeStruct((rows, KV_WIDTH), BF16),
                   jax.ShapeDtypeStruct((rows, KV_WIDTH), BF16)],
        compiler_params=pltpu.CompilerParams(dimension_semantics=("arbitrary",)),
        name="l0_inproj",
    )(hp, gain, w_in, q_gain, k_gain, cos_t, sin_lo_t, sin_hi_t)


def _attn_kernel(sink_ref, q_ref, kc_ref, kp_ref, km_ref, vc_ref, vp_ref, vm_ref, o_ref):
    n = pl.program_id(1)
    q = q_ref[...]
    rows = ATT_GROUP * BLOCK
    i = lax.broadcasted_iota(jnp.int32, (rows, BLOCK), 0) & (BLOCK - 1)
    j = lax.broadcasted_iota(jnp.int32, (rows, BLOCK), 1)
    causal = j <= i
    win_ok = n >= jnp.where(causal, 1, 2)
    meta_ok = (j >= META_PAD) & (n >= jnp.where(causal, 0, 1))
    head_of_row = lax.broadcasted_iota(jnp.int32, (rows, 1), 0) // BLOCK
    contract_last = (((1,), (1,)), ((), ()))

    outs = []
    for g in range(ATT_KV_HEADS):
        qs = jnp.concatenate(
            [q[:, (g * ATT_GROUP + a) * ATT_HEAD_DIM:(g * ATT_GROUP + a + 1) * ATT_HEAD_DIM]
             for a in range(ATT_GROUP)], axis=0)
        lanes = slice(g * ATT_HEAD_DIM, (g + 1) * ATT_HEAD_DIM)
        s_c = lax.dot_general(qs, kc_ref[:, lanes], contract_last, preferred_element_type=F32)
        s_p = lax.dot_general(qs, kp_ref[:, lanes], contract_last, preferred_element_type=F32)
        s_m = lax.dot_general(qs, km_ref[:, lanes], contract_last, preferred_element_type=F32)
        s_w = jnp.where(win_ok, jnp.where(causal, s_c, s_p), NEG_INF)
        s_m = jnp.where(meta_ok, s_m, NEG_INF)
        sink = jnp.zeros((rows, 1), F32)
        for a in range(ATT_GROUP):
            sink = jnp.where(head_of_row == a, sink_ref[g * ATT_GROUP + a], sink)
        m = jnp.maximum(jnp.maximum(jnp.max(s_w, axis=-1, keepdims=True),
                                    jnp.max(s_m, axis=-1, keepdims=True)), sink)
        p_w = jnp.exp(s_w - m)
        p_m = jnp.exp(s_m - m)
        den = (jnp.sum(p_w, axis=-1, keepdims=True) + jnp.sum(p_m, axis=-1, keepdims=True)
               + jnp.exp(sink - m))
        p_c = jnp.where(causal, p_w, 0.0).astype(BF16)
        p_p = jnp.where(causal, 0.0, p_w).astype(BF16)
        o = (jnp.dot(p_c, vc_ref[:, lanes], preferred_element_type=F32)
             + jnp.dot(p_p, vp_ref[:, lanes], preferred_element_type=F32)
             + jnp.dot(p_m.astype(BF16), vm_ref[:, lanes], preferred_element_type=F32))
        o = o / den
        outs.extend(o[a * BLOCK:(a + 1) * BLOCK] for a in range(ATT_GROUP))
    o_ref[...] = jnp.concatenate(outs, axis=1).astype(BF16)


def _attention(sinks, q, k, v, batch, nblk):
    cur = lambda w: pl.BlockSpec((BLOCK, w), lambda b, n: (b * nblk + n, 0))
    prev = lambda w: pl.BlockSpec((BLOCK, w), lambda b, n: (b * nblk + jnp.maximum(n - 1, 0), 0))
    meta = lambda w: pl.BlockSpec((BLOCK, w), lambda b, n: (b * nblk, 0))
    return pl.pallas_call(
        _attn_kernel,
        grid=(batch, nblk),
        in_specs=[pl.BlockSpec(memory_space=pltpu.SMEM),
                  cur(Q_WIDTH), cur(KV_WIDTH), prev(KV_WIDTH), meta(KV_WIDTH),
                  cur(KV_WIDTH), prev(KV_WIDTH), meta(KV_WIDTH)],
        out_specs=cur(Q_WIDTH),
        out_shape=jax.ShapeDtypeStruct(q.shape, BF16),
        compiler_params=pltpu.CompilerParams(dimension_semantics=("arbitrary", "arbitrary")),
        name="l0_attention",
    )(sinks, q, k, k, k, v, v, v)


def _lru_kernel(xr_ref, gate_ref, cw_ref, cb_ref, wg_ref, bg_ref, lam_ref, y_ref, xbuf, hcar):
    n = pl.program_id(1)
    tl = xr_ref.shape[0]

    @pl.when(n == 0)
    def _():
        xbuf[0:SUBLANES, :] = jnp.zeros((SUBLANES, LRU_WIDTH), F32)
        hcar[...] = jnp.zeros_like(hcar)

    x = xr_ref[...]
    xbuf[SUBLANES:SUBLANES + tl, :] = x
    cw = cw_ref[...]
    xc = x * cw[CONV_W - 1:CONV_W] + cb_ref[...]
    for d in range(1, CONV_W):
        xc = xc + xbuf[SUBLANES - d:SUBLANES - d + tl, :] * cw[CONV_W - 1 - d:CONV_W - d]
    xbuf[0:SUBLANES, :] = x[tl - SUBLANES:tl]

    ga = jnp.dot(xc.astype(BF16), wg_ref[...], preferred_element_type=F32) + bg_ref[...]
    r = jax.nn.sigmoid(ga[:, :LRU_WIDTH])
    gi = jax.nn.sigmoid(ga[:, LRU_WIDTH:])
    z = -lam_ref[...]
    softplus = jnp.maximum(z, 0.0) + jnp.log1p(jnp.exp(-jnp.abs(z)))
    log_a = (-LRU_C * softplus) * r
    a = jnp.exp(log_a)
    mult = jnp.sqrt(jnp.tanh(-log_a) * (a * a + 1.0))
    row = lax.broadcasted_iota(jnp.int32, (tl, 1), 0)
    t = n * tl + row
    mult = jnp.where(t == META_PAD, 1.0, mult)
    b = jnp.where(t < META_PAD, 0.0, mult * gi * xc)

    d = 1
    while d < tl:
        keep = row >= d
        b = jnp.where(keep, a * pltpu.roll(b, d, 0), 0.0) + b
        a = jnp.where(keep, a * pltpu.roll(a, d, 0), a)
        d *= 2
    h = b + a * hcar[0:1, :]
    hcar[...] = jnp.broadcast_to(h[tl - 1:tl], hcar.shape)
    y_ref[...] = (_gelu_tanh(gate_ref[...]) * h).astype(BF16)


def _lru(xr, gate, conv_w, conv_b, w_gates, b_gates, lam, batch, seq_rows):
    tl = LRU_TILE
    nt = seq_rows // tl
    row = pl.BlockSpec((tl, LRU_WIDTH), lambda b, n: (b * nt + n, 0))
    return pl.pallas_call(
        _lru_kernel,
        grid=(batch, nt),
        in_specs=[row, row, _const_spec((CONV_W, LRU_WIDTH)), _const_spec((1, LRU_WIDTH)),
                  _const_spec((LRU_WIDTH, 2 * LRU_WIDTH)), _const_spec((1, 2 * LRU_WIDTH)),
                  _const_spec((1, LRU_WIDTH))],
        out_specs=row,
        out_shape=jax.ShapeDtypeStruct(xr.shape, BF16),
        scratch_shapes=[pltpu.VMEM((SUBLANES + tl, LRU_WIDTH), F32),
                        pltpu.VMEM((SUBLANES, LRU_WIDTH), F32)],
        compiler_params=pltpu.CompilerParams(dimension_semantics=("arbitrary", "arbitrary")),
        name="l0_rglru",
    )(xr, gate, conv_w, conv_b, w_gates, b_gates, lam)


def _outproj_ffn_kernel(*refs, n_mix):
    h_ref = refs[0]
    y_refs = refs[1:1 + n_mix]
    wo_ref, gain_ref, wgu_ref, wd_ref, o_ref = refs[1 + n_mix:]
    h1 = h_ref[...]
    k0 = 0
    for y_ref in y_refs:
        width = y_ref.shape[1]
        h1 = h1 + jnp.dot(y_ref[...], wo_ref[k0:k0 + width, :], preferred_element_type=F32)
        k0 += width
    xn = (h1 * _rms_scale(h1) * gain_ref[...]).astype(BF16)
    acts = []
    for c in range(D_FF // FFN_CHUNK):
        lo = c * FFN_CHUNK
        g = jnp.dot(xn, wgu_ref[:, lo:lo + FFN_CHUNK], preferred_element_type=F32)
        u = jnp.dot(xn, wgu_ref[:, D_FF + lo:D_FF + lo + FFN_CHUNK], preferred_element_type=F32)
        acts.append((_silu(g) * u).astype(BF16))
    act = jnp.concatenate(acts, axis=1)
    o_ref[...] = h1 + jnp.dot(act, wd_ref[...], preferred_element_type=F32)


def _outproj_ffn(h, ys, w_out, gain, w_gu, w_down, name):
    rows = h.shape[0]
    tm = ROW_TILE
    row = lambda w: pl.BlockSpec((tm, w), lambda i: (i, 0))
    return pl.pallas_call(
        functools.partial(_outproj_ffn_kernel, n_mix=len(ys)),
        grid=(rows // tm,),
        in_specs=[row(D_MODEL)] + [row(y.shape[1]) for y in ys]
        + [_const_spec(w_out.shape), _const_spec((1, D_MODEL)), _const_spec(w_gu.shape),
           _const_spec(w_down.shape)],
        out_specs=row(D_MODEL),
        out_shape=jax.ShapeDtypeStruct(h.shape, F32),
        compiler_params=pltpu.CompilerParams(dimension_semantics=("arbitrary",)),
        name=name,
    )(h, *ys, w_out, gain, w_gu, w_down)


def _inproj1_kernel(h_ref, gain_ref, w_ref, cos_ref, sin_ref, q_ref, k_ref, v_ref, g_ref):
    h = h_ref[...]
    xn = (h * _rms_scale(h) * gain_ref[...]).astype(BF16)
    cos = cos_ref[...]
    sin = sin_ref[...]
    half = RET_QK_DIM // 2

    def rope_heads(col0, out_ref, scale):
        for hd in range(RET_HEADS):
            lo = hd * RET_QK_DIM
            y = jnp.dot(xn, w_ref[:, col0 + lo:col0 + lo + RET_QK_DIM],
                        preferred_element_type=F32)
            x1 = y[:, :half]
            x2 = y[:, half:]
            out_ref[:, lo:lo + half] = ((x1 * cos - x2 * sin) * scale).astype(BF16)
            out_ref[:, lo + half:lo + RET_QK_DIM] = ((x2 * cos + x1 * sin) * scale).astype(BF16)

    rope_heads(0, q_ref, 1.0)
    rope_heads(D_MODEL, k_ref, RET_QK_DIM ** -0.5)
    for hd in range(RET_HEADS):
        lo = hd * RET_V_DIM
        v = jnp.dot(xn, w_ref[:, 2 * D_MODEL + lo:2 * D_MODEL + lo + RET_V_DIM],
                    preferred_element_type=F32)
        v_ref[:, lo:lo + RET_V_DIM] = v.astype(BF16)
        g = jnp.dot(xn, w_ref[:, 4 * D_MODEL + lo:4 * D_MODEL + lo + RET_V_DIM],
                    preferred_element_type=F32)
        g_ref[:, lo:lo + RET_V_DIM] = _silu(g).astype(BF16)


def _inproj1(h, gain, w_in, cos_t, sin_t, tiles_per_seq):
    rows = h.shape[0]
    tm = ROW_TILE
    row = lambda w: pl.BlockSpec((tm, w), lambda i: (i, 0))
    tab = pl.BlockSpec((tm, RET_QK_DIM // 2), lambda i: (i % tiles_per_seq, 0))
    return pl.pallas_call(
        _inproj1_kernel,
        grid=(rows // tm,),
        in_specs=[row(D_MODEL), _const_spec((1, D_MODEL)), _const_spec(w_in.shape), tab, tab],
        out_specs=[row(D_MODEL), row(D_MODEL), row(2 * D_MODEL), row(2 * D_MODEL)],
        out_shape=[jax.ShapeDtypeStruct((rows, D_MODEL), BF16),
                   jax.ShapeDtypeStruct((rows, D_MODEL), BF16),
                   jax.ShapeDtypeStruct((rows, 2 * D_MODEL), BF16),
                   jax.ShapeDtypeStruct((rows, 2 * D_MODEL), BF16)],
        compiler_params=pltpu.CompilerParams(dimension_semantics=("arbitrary",)),
        name="l1_inproj",
    )(h, gain, w_in, cos_t, sin_t)


def _retention_kernel(q_ref, k_ref, v_ref, g_ref, o_ref, state):
    @pl.when(pl.program_id(1) == 0)
    def _():
        state[...] = jnp.zeros_like(state)

    ii = lax.broadcasted_iota(jnp.int32, (BLOCK, BLOCK), 0)
    jj = lax.broadcasted_iota(jnp.int32, (BLOCK, BLOCK), 1)
    diff = (ii - jj).astype(F32)
    idx = lax.broadcasted_iota(jnp.int32, (BLOCK, 1), 0).astype(F32)
    contract_last = (((1,), (1,)), ((), ()))
    contract_rows = (((0,), (0,)), ((), ()))

    for hd in range(RET_HEADS):
        log_g = RET_LOG_G[hd]
        decay_intra = jnp.where(diff >= 0.0, jnp.exp(jnp.maximum(diff, 0.0) * log_g), 0.0)
        decay_q = jnp.exp((idx + 1.0) * log_g)
        decay_k = jnp.exp((BLOCK - 1.0 - idx) * log_g)
        decay_chunk = math.exp(BLOCK * log_g)
        q = q_ref[:, hd * RET_QK_DIM:(hd + 1) * RET_QK_DIM]
        k = k_ref[:, hd * RET_QK_DIM:(hd + 1) * RET_QK_DIM]
        v = v_ref[:, hd * RET_V_DIM:(hd + 1) * RET_V_DIM]
        qk = lax.dot_general(q, k, contract_last, preferred_element_type=F32) * decay_intra
        st = state[hd]
        o = (jnp.dot(qk.astype(BF16), v, preferred_element_type=F32)
             + jnp.dot(q, st.astype(BF16), preferred_element_type=F32) * decay_q)
        kd = (k.astype(F32) * decay_k).astype(BF16)
        state[hd] = decay_chunk * st + lax.dot_general(kd, v, contract_rows,
                                                       preferred_element_type=F32)
        gate = g_ref[:, hd * RET_V_DIM:(hd + 1) * RET_V_DIM].astype(F32)
        o_ref[:, hd * RET_V_DIM:(hd + 1) * RET_V_DIM] = (o * _rms_scale(o) * gate).astype(BF16)


def _retention(q, k, v, g, batch, nblk):
    row = lambda w: pl.BlockSpec((BLOCK, w), lambda b, n: (b * nblk + n, 0))
    return pl.pallas_call(
        _retention_kernel,
        grid=(batch, nblk),
        in_specs=[row(D_MODEL), row(D_MODEL), row(2 * D_MODEL), row(2 * D_MODEL)],
        out_specs=row(2 * D_MODEL),
        out_shape=jax.ShapeDtypeStruct(v.shape, BF16),
        scratch_shapes=[pltpu.VMEM((RET_HEADS, RET_QK_DIM, RET_V_DIM), F32)],
        compiler_params=pltpu.CompilerParams(dimension_semantics=("arbitrary", "arbitrary")),
        name="l1_retention",
    )(q, k, v, g)


def _rope_angles(n_rows, half, theta):
    inv_freq = jnp.power(jnp.asarray(theta, F32), -jnp.arange(half, dtype=F32) / half)
    pos = (jnp.arange(n_rows, dtype=jnp.int32) - META_PAD).astype(F32)
    return pos[:, None] * inv_freq[None, :]


def _block_diag(w):
    heads, wi, wo = w.shape
    eye = jnp.eye(heads, dtype=w.dtype)
    return (eye[:, None, :, None] * w[:, :, None, :]).reshape(heads * wi, heads * wo)


def kernel(x, meta_tokens, mix_norm_ab, ab_w_in, lru_conv_w, lru_conv_b, lru_w_a, lru_b_a, lru_w_i, lru_b_i, lru_lambda, q_norm, k_norm, attn_sinks, ab_w_out, mix_norm_ret, ret_w_in, ret_w_out, ffn_norm, ffn_w_gu, ffn_w_down):
    batch, seq, _ = x.shape
    seq_rows = META_PAD + N_META + seq
    nblk = seq_rows // BLOCK
    tiles_per_seq = seq_rows // ROW_TILE

    meta = jnp.broadcast_to(meta_tokens[None].astype(x.dtype), (batch, N_META, D_MODEL))
    hp = jnp.concatenate([jnp.zeros((batch, META_PAD, D_MODEL), x.dtype), meta, x], axis=1)
    hp = hp.reshape(batch * seq_rows, D_MODEL)

    ang = _rope_angles(seq_rows, ROT_DIM // 2, ROPE_THETA)
    cos8, sin8 = jnp.cos(ang), jnp.sin(ang)
    rest = ATT_HEAD_DIM - ROT_DIM
    ones = jnp.ones((seq_rows, rest), F32)
    zeros = jnp.zeros((seq_rows, rest), F32)
    zero8 = jnp.zeros_like(sin8)
    att_cos = jnp.tile(jnp.concatenate([cos8, cos8, ones], axis=1), (1, 2))
    att_sin_lo = jnp.tile(jnp.concatenate([-sin8, zero8, zeros], axis=1), (1, 2))
    att_sin_hi = jnp.tile(jnp.concatenate([zero8, sin8, zeros], axis=1), (1, 2))
    ang = _rope_angles(seq_rows, RET_QK_DIM // 2, RET_THETA)
    ret_cos, ret_sin = jnp.cos(ang), jnp.sin(ang)

    row_vec = lambda v: v.reshape(1, -1).astype(F32)
    two_heads = lambda v: jnp.tile(v.reshape(1, -1).astype(F32), (1, 2))

    xr, gate, q, k, v = _inproj0(hp, row_vec(mix_norm_ab[0]), ab_w_in[0].astype(BF16),
                                 two_heads(q_norm[0]), two_heads(k_norm[0]),
                                 att_cos, att_sin_lo, att_sin_hi, tiles_per_seq)
    y_att = _attention(attn_sinks[0].astype(F32), q, k, v, batch, nblk)
    w_gates = jnp.concatenate([_block_diag(lru_w_a[0]), _block_diag(lru_w_i[0])], axis=1)
    b_gates = jnp.concatenate([lru_b_a[0].reshape(1, -1), lru_b_i[0].reshape(1, -1)], axis=1)
    y_rnn = _lru(xr, gate, lru_conv_w[0], row_vec(lru_conv_b[0]), w_gates.astype(BF16),
                 b_gates.astype(F32), row_vec(lru_lambda[0]), batch, seq_rows)
    h = _outproj_ffn(hp, [y_rnn, y_att], ab_w_out[0].astype(BF16), row_vec(ffn_norm[0]),
                     ffn_w_gu[0].astype(BF16), ffn_w_down[0].astype(BF16), "l0_outproj_ffn")

    rq, rk, rv, rg = _inproj1(h, row_vec(mix_norm_ret[0]), ret_w_in[0].astype(BF16),
                              ret_cos, ret_sin, tiles_per_seq)
    y_ret = _retention(rq, rk, rv, rg, batch, nblk)
    h = _outproj_ffn(h, [y_ret], ret_w_out[0].astype(BF16), row_vec(ffn_norm[1]),
                     ffn_w_gu[1].astype(BF16), ffn_w_down[1].astype(BF16), "l1_outproj_ffn")

    return h.reshape(batch, seq_rows, D_MODEL)[:, BLOCK:]
```

```python
import functools
import itertools
import math

import jax
import jax.numpy as jnp
from jax import lax
from jax.experimental import pallas as pl
from jax.experimental.pallas import tpu as pltpu

F32 = jnp.float32
BF16 = jnp.bfloat16

D_MODEL = 1024
N_META = 16
BLOCK = 128
META_PAD = BLOCK - N_META
RMS_EPS = 1e-6
NEG_INF = -1e30

LRU_WIDTH = 512
LRU_HEADS = 8
LRU_BLOCK_W = 64
CONV_W = 4
LRU_C = 8.0

ATT_HEADS = 8
ATT_KV_HEADS = 2
ATT_GROUP = ATT_HEADS // ATT_KV_HEADS
ATT_HEAD_DIM = 64
ROPE_THETA = 500000.0
ROT_DIM = 16
Q_WIDTH = 512
KV_WIDTH = 128
AB_IN_WIDTH = 2 * LRU_WIDTH + Q_WIDTH + 2 * KV_WIDTH

RET_HEADS = 4
RET_QK_DIM = 256
RET_V_DIM = 512
RET_THETA = 10000.0
RET_LOG_G = tuple(math.log1p(-(2.0 ** (-5.0 - h))) for h in range(RET_HEADS))

D_FF = 2816

LANES = 128
SUBLANES = 8
ROW_TILE = 640
FFN_CHUNK = 256
DOWN_CHUNK = 256


def _rms_scale(x):
    return lax.rsqrt(jnp.mean(x * x, axis=-1, keepdims=True) + RMS_EPS)


def _silu(x):
    return x * jax.nn.sigmoid(x)


def _gelu_tanh(x):
    return 0.5 * x * (1.0 + jnp.tanh(0.7978845608028654 * (x + 0.044715 * (x * x * x))))


def _const_spec(shape):
    zeros = (0,) * len(shape)
    return pl.BlockSpec(shape, lambda *_: zeros, pipeline_mode=pl.Buffered(1))


def _inproj0_kernel(h_ref, gain_ref, w_ref, qg_ref, kg_ref, c_ref, s1_ref, s2_ref,
                    xr_ref, gate_ref, q_ref, k_ref, v_ref):
    h = h_ref[...]
    xn = (h * _rms_scale(h) * gain_ref[...]).astype(BF16)
    y = jnp.dot(xn, w_ref[...], preferred_element_type=F32)
    xr_ref[...] = y[:, :LRU_WIDTH]
    gate_ref[...] = y[:, LRU_WIDTH:2 * LRU_WIDTH]

    rows = h.shape[0]
    cos = c_ref[...]
    sin_lo = s1_ref[...]
    sin_hi = s2_ref[...]
    first_head = lax.broadcasted_iota(jnp.int32, (rows, LANES), 1) < ATT_HEAD_DIM

    def norm_rope(x, gain, scale):
        sq = x * x
        s_a = jnp.sum(jnp.where(first_head, sq, 0.0), axis=-1, keepdims=True)
        s_b = jnp.sum(jnp.where(first_head, 0.0, sq), axis=-1, keepdims=True)
        inv = jnp.where(first_head,
                        lax.rsqrt(s_a * (1.0 / ATT_HEAD_DIM) + RMS_EPS),
                        lax.rsqrt(s_b * (1.0 / ATT_HEAD_DIM) + RMS_EPS))
        xg = x * gain
        rot = (xg * cos + pltpu.roll(xg, LANES - ROT_DIM // 2, 1) * sin_lo
               + pltpu.roll(xg, ROT_DIM // 2, 1) * sin_hi)
        return rot * (inv * scale)

    q0 = 2 * LRU_WIDTH
    for j in range(Q_WIDTH // LANES):
        xq = y[:, q0 + j * LANES:q0 + (j + 1) * LANES]
        q_ref[:, j * LANES:(j + 1) * LANES] = norm_rope(
            xq, qg_ref[...], ATT_HEAD_DIM ** -0.5).astype(BF16)
    k0 = q0 + Q_WIDTH
    k_ref[...] = norm_rope(y[:, k0:k0 + KV_WIDTH], kg_ref[...], 1.0).astype(BF16)
    v_ref[...] = y[:, k0 + KV_WIDTH:k0 + 2 * KV_WIDTH].astype(BF16)


def _inproj0(hp, gain, w_in, q_gain, k_gain, cos_t, sin_lo_t, sin_hi_t, tiles_per_seq):
    rows = hp.shape[0]
    tm = ROW_TILE
    row = lambda w: pl.BlockSpec((tm, w), lambda i: (i, 0))
    tab = pl.BlockSpec((tm, LANES), lambda i: (i % tiles_per_seq, 0))
    return pl.pallas_call(
        _inproj0_kernel,
        grid=(rows // tm,),
        in_specs=[row(D_MODEL), _const_spec((1, D_MODEL)), _const_spec((D_MODEL, AB_IN_WIDTH)),
                  _const_spec((1, LANES)), _const_spec((1, LANES)), tab, tab, tab],
        out_specs=[row(LRU_WIDTH), row(LRU_WIDTH), row(Q_WIDTH), row(KV_WIDTH), row(KV_WIDTH)],
        out_shape=[jax.ShapeDtypeStruct((rows, LRU_WIDTH), F32),
                   jax.ShapeDtypeStruct((rows, LRU_WIDTH), F32),
                   jax.ShapeDtypeStruct((rows, Q_WIDTH), BF16),
                   jax.ShapeDtypeStruct((rows, KV_WIDTH), BF16),
                   jax.ShapeDtypeStruct((rows, KV_WIDTH), BF16)],
        compiler_params=pltpu.CompilerParams(dimension_semantics=("arbitrary",)),
        name="l0_inproj",
    )(hp, gain, w_in, q_gain, k_gain, cos_t, sin_lo_t, sin_hi_t)


def _attn_group(n, sinks, q, kc, kp, km, vc, vp, vm):
    rows = ATT_GROUP * BLOCK
    i = lax.broadcasted_iota(jnp.int32, (rows, BLOCK), 0) & (BLOCK - 1)
    j = lax.broadcasted_iota(jnp.int32, (rows, BLOCK), 1)
    causal = j <= i
    win_ok = n >= jnp.where(causal, 1, 2)
    meta_ok = (j >= META_PAD) & (n >= jnp.where(causal, 0, 1))
    head_of_row = lax.broadcasted_iota(jnp.int32, (rows, 1), 0) // BLOCK
    contract_last = (((1,), (1,)), ((), ()))

    qs = jnp.concatenate([q[:, a * ATT_HEAD_DIM:(a + 1) * ATT_HEAD_DIM]
                          for a in range(ATT_GROUP)], axis=0)
    s_c = lax.dot_general(qs, kc, contract_last, preferred_element_type=F32)
    s_p = lax.dot_general(qs, kp, contract_last, preferred_element_type=F32)
    s_m = lax.dot_general(qs, km, contract_last, preferred_element_type=F32)
    s_w = jnp.where(win_ok, jnp.where(causal, s_c, s_p), NEG_INF)
    s_m = jnp.where(meta_ok, s_m, NEG_INF)
    sink = jnp.zeros((rows, 1), F32)
    for a in range(ATT_GROUP):
        sink = jnp.where(head_of_row == a, sinks[a], sink)
    m = jnp.maximum(jnp.maximum(jnp.max(s_w, axis=-1, keepdims=True),
                                jnp.max(s_m, axis=-1, keepdims=True)), sink)
    p_w = jnp.exp(s_w - m)
    p_m = jnp.exp(s_m - m)
    den = (jnp.sum(p_w, axis=-1, keepdims=True) + jnp.sum(p_m, axis=-1, keepdims=True)
           + jnp.exp(sink - m))
    p_c = jnp.where(causal, p_w, 0.0).astype(BF16)
    p_p = jnp.where(causal, 0.0, p_w).astype(BF16)
    o = (jnp.dot(p_c, vc, preferred_element_type=F32)
         + jnp.dot(p_p, vp, preferred_element_type=F32)
         + jnp.dot(p_m.astype(BF16), vm, preferred_element_type=F32))
    o = o / den
    return jnp.concatenate([o[a * BLOCK:(a + 1) * BLOCK] for a in range(ATT_GROUP)],
                           axis=1).astype(BF16)


def _lru_block(n, x, gate, cw, cb, wg_ref, bg, lam, xbuf, hcar):
    tl = x.shape[0]
    xbuf[SUBLANES:SUBLANES + tl, :] = x
    xc = x * cw[CONV_W - 1:CONV_W] + cb
    for d in range(1, CONV_W):
        xc = xc + xbuf[SUBLANES - d:SUBLANES - d + tl, :] * cw[CONV_W - 1 - d:CONV_W - d]
    xbuf[0:SUBLANES, :] = x[tl - SUBLANES:tl]

    ga = jnp.dot(xc.astype(BF16), wg_ref[...], preferred_element_type=F32) + bg
    r = jax.nn.sigmoid(ga[:, :LRU_WIDTH])
    gi = jax.nn.sigmoid(ga[:, LRU_WIDTH:])
    z = -lam
    softplus = jnp.maximum(z, 0.0) + jnp.log1p(jnp.exp(-jnp.abs(z)))
    log_a = (-LRU_C * softplus) * r
    a = jnp.exp(log_a)
    mult = jnp.sqrt(jnp.tanh(-log_a) * (a * a + 1.0))
    row = lax.broadcasted_iota(jnp.int32, (tl, 1), 0)
    t = n * tl + row
    mult = jnp.where(t == META_PAD, 1.0, mult)
    b = jnp.where(t < META_PAD, 0.0, mult * gi * xc)

    d = 1
    while d < tl:
        keep = row >= d
        b = jnp.where(keep, a * pltpu.roll(b, d, 0), 0.0) + b
        a = jnp.where(keep, a * pltpu.roll(a, d, 0), a)
        d *= 2
    h = b + a * hcar[0:1, :]
    hcar[...] = jnp.broadcast_to(h[tl - 1:tl], hcar.shape)
    return (_gelu_tanh(gate) * h).astype(BF16)


def _ffn(h1, gain, wgu_ref, wd_ref):
    xn = (h1 * _rms_scale(h1) * gain).astype(BF16)
    acts = []
    for c in range(D_FF // FFN_CHUNK):
        lo = c * FFN_CHUNK
        g = jnp.dot(xn, wgu_ref[:, lo:lo + FFN_CHUNK], preferred_element_type=F32)
        u = jnp.dot(xn, wgu_ref[:, D_FF + lo:D_FF + lo + FFN_CHUNK], preferred_element_type=F32)
        acts.append((_silu(g) * u).astype(BF16))
    act = jnp.concatenate(acts, axis=1)
    return h1 + jnp.dot(act, wd_ref[...], preferred_element_type=F32)


def _l0_mix_ffn_kernel(sink_ref, h_ref, xr_ref, gate_ref, q_ref, k_ref, v_ref,
                       cw_ref, cb_ref, wg_ref, bg_ref, lam_ref, wo_ref, gain_ref, wgu_ref, wd_ref,
                       o_ref, ybuf, xbuf, hcar, kprev, vprev, kmeta, vmeta,
                       *, tiles_per_seq, n_tiles):
    s = pl.program_id(0)
    tile_in_seq = jnp.minimum(s, n_tiles - 1) % tiles_per_seq
    blocks_per_tile = h_ref.shape[0] // BLOCK
    last = slice((blocks_per_tile - 1) * BLOCK, blocks_per_tile * BLOCK)

    @pl.when(s == 0)
    def _():
        ybuf[...] = jnp.zeros_like(ybuf)

    @pl.when(tile_in_seq == 0)
    def _():
        xbuf[0:SUBLANES, :] = jnp.zeros((SUBLANES, LRU_WIDTH), F32)
        hcar[...] = jnp.zeros_like(hcar)
        kmeta[...] = k_ref[0:BLOCK, :]
        vmeta[...] = v_ref[0:BLOCK, :]
        kprev[...] = k_ref[0:BLOCK, :]
        vprev[...] = v_ref[0:BLOCK, :]

    h1 = h_ref[...] + jnp.dot(ybuf[...], wo_ref[...], preferred_element_type=F32)
    xn = (h1 * _rms_scale(h1) * gain_ref[...]).astype(BF16)

    cw = cw_ref[...]
    cb = cb_ref[...]
    bg = bg_ref[...]
    lam = lam_ref[...]

    def lru_step(blk):
        n = tile_in_seq * blocks_per_tile + blk
        rows = slice(blk * BLOCK, (blk + 1) * BLOCK)
        ybuf[rows, 0:LRU_WIDTH] = _lru_block(n, xr_ref[rows, :], gate_ref[rows, :], cw, cb,
                                             wg_ref, bg, lam, xbuf, hcar)

    def attn_step(blk, g):
        n = tile_in_seq * blocks_per_tile + blk
        rows = slice(blk * BLOCK, (blk + 1) * BLOCK)
        prev_rows = slice((blk - 1) * BLOCK, blk * BLOCK)
        lanes = slice(g * ATT_HEAD_DIM, (g + 1) * ATT_HEAD_DIM)
        width = ATT_GROUP * ATT_HEAD_DIM
        kp = kprev[:, lanes] if blk == 0 else k_ref[prev_rows, lanes]
        vp = vprev[:, lanes] if blk == 0 else v_ref[prev_rows, lanes]
        sinks = [sink_ref[g * ATT_GROUP + a] for a in range(ATT_GROUP)]
        ybuf[rows, LRU_WIDTH + g * width:LRU_WIDTH + (g + 1) * width] = _attn_group(
            n, sinks, q_ref[rows, g * width:(g + 1) * width], k_ref[rows, lanes], kp,
            kmeta[:, lanes], v_ref[rows, lanes], vp, vmeta[:, lanes])

    acts = []

    def gate_up_step(c):
        lo = c * FFN_CHUNK
        g = jnp.dot(xn, wgu_ref[:, lo:lo + FFN_CHUNK], preferred_element_type=F32)
        u = jnp.dot(xn, wgu_ref[:, D_FF + lo:D_FF + lo + FFN_CHUNK], preferred_element_type=F32)
        acts.append((_silu(g) * u).astype(BF16))

    def down_step(c):
        cols = slice(c * DOWN_CHUNK, (c + 1) * DOWN_CHUNK)
        if len(acts) > 1:
            acts[:] = [jnp.concatenate(acts, axis=1)]
        o_ref[:, cols] = h1[:, cols] + jnp.dot(acts[0], wd_ref[:, cols],
                                               preferred_element_type=F32)

    ffn_steps = ([functools.partial(gate_up_step, c) for c in range(D_FF // FFN_CHUNK)]
                 + [functools.partial(down_step, c) for c in range(D_MODEL // DOWN_CHUNK)])
    mix_steps = []
    for blk in range(blocks_per_tile):
        mix_steps.append(functools.partial(lru_step, blk))
        mix_steps.extend(functools.partial(attn_step, blk, g) for g in range(ATT_KV_HEADS))
    for ffn_step, mix_step in itertools.zip_longest(ffn_steps, mix_steps):
        if ffn_step is not None:
            ffn_step()
        if mix_step is not None:
            mix_step()
    kprev[...] = k_ref[last, :]
    vprev[...] = v_ref[last, :]


def _l0_mix_ffn(sinks, hp, xr, gate, q, k, v, conv_w, conv_b, w_gates, b_gates, lam,
                w_out, gain, w_gu, w_down, tiles_per_seq):
    rows = hp.shape[0]
    tm = ROW_TILE
    n_tiles = rows // tm
    mix = lambda w: pl.BlockSpec((tm, w), lambda s: (jnp.minimum(s, n_tiles - 1), 0))
    ffn = lambda w: pl.BlockSpec((tm, w), lambda s: (jnp.maximum(s - 1, 0), 0))
    return pl.pallas_call(
        functools.partial(_l0_mix_ffn_kernel, tiles_per_seq=tiles_per_seq, n_tiles=n_tiles),
        grid=(n_tiles + 1,),
        in_specs=[pl.BlockSpec(memory_space=pltpu.SMEM), ffn(D_MODEL),
                  mix(LRU_WIDTH), mix(LRU_WIDTH), mix(Q_WIDTH), mix(KV_WIDTH), mix(KV_WIDTH),
                  _const_spec((CONV_W, LRU_WIDTH)), _const_spec((1, LRU_WIDTH)),
                  _const_spec((LRU_WIDTH, 2 * LRU_WIDTH)), _const_spec((1, 2 * LRU_WIDTH)),
                  _const_spec((1, LRU_WIDTH)), _const_spec(w_out.shape), _const_spec((1, D_MODEL)),
                  _const_spec(w_gu.shape), _const_spec(w_down.shape)],
        out_specs=ffn(D_MODEL),
        out_shape=jax.ShapeDtypeStruct(hp.shape, F32),
        scratch_shapes=[pltpu.VMEM((tm, LRU_WIDTH + Q_WIDTH), BF16),
                        pltpu.VMEM((SUBLANES + BLOCK, LRU_WIDTH), F32),
                        pltpu.VMEM((SUBLANES, LRU_WIDTH), F32),
                        pltpu.VMEM((BLOCK, KV_WIDTH), BF16), pltpu.VMEM((BLOCK, KV_WIDTH), BF16),
                        pltpu.VMEM((BLOCK, KV_WIDTH), BF16), pltpu.VMEM((BLOCK, KV_WIDTH), BF16)],
        compiler_params=pltpu.CompilerParams(dimension_semantics=("arbitrary",)),
        name="l0_mix_ffn",
    )(sinks, hp, xr, gate, q, k, v, conv_w, conv_b, w_gates, b_gates, lam, w_out, gain, w_gu,
      w_down)


def _outproj_ffn_kernel(*refs, n_mix):
    h_ref = refs[0]
    y_refs = refs[1:1 + n_mix]
    wo_ref, gain_ref, wgu_ref, wd_ref, o_ref = refs[1 + n_mix:]
    h1 = h_ref[...]
    k0 = 0
    for y_ref in y_refs:
        width = y_ref.shape[1]
        h1 = h1 + jnp.dot(y_ref[...], wo_ref[k0:k0 + width, :], preferred_element_type=F32)
        k0 += width
    o_ref[...] = _ffn(h1, gain_ref[...], wgu_ref, wd_ref)


def _outproj_ffn(h, ys, w_out, gain, w_gu, w_down, name):
    rows = h.shape[0]
    tm = ROW_TILE
    row = lambda w: pl.BlockSpec((tm, w), lambda i: (i, 0))
    return pl.pallas_call(
        functools.partial(_outproj_ffn_kernel, n_mix=len(ys)),
        grid=(rows // tm,),
        in_specs=[row(D_MODEL)] + [row(y.shape[1]) for y in ys]
        + [_const_spec(w_out.shape), _const_spec((1, D_MODEL)), _const_spec(w_gu.shape),
           _const_spec(w_down.shape)],
        out_specs=row(D_MODEL),
        out_shape=jax.ShapeDtypeStruct(h.shape, F32),
        compiler_params=pltpu.CompilerParams(dimension_semantics=("arbitrary",)),
        name=name,
    )(h, *ys, w_out, gain, w_gu, w_down)


def _inproj1_kernel(h_ref, gain_ref, w_ref, cos_ref, sin_ref, q_ref, k_ref, v_ref, g_ref):
    h = h_ref[...]
    xn = (h * _rms_scale(h) * gain_ref[...]).astype(BF16)
    cos = cos_ref[...]
    sin = sin_ref[...]
    half = RET_QK_DIM // 2

    def rope_heads(col0, out_ref, scale):
        for hd in range(RET_HEADS):
            lo = hd * RET_QK_DIM
            y = jnp.dot(xn, w_ref[:, col0 + lo:col0 + lo + RET_QK_DIM],
                        preferred_element_type=F32)
            x1 = y[:, :half]
            x2 = y[:, half:]
            out_ref[:, lo:lo + half] = ((x1 * cos - x2 * sin) * scale).astype(BF16)
            out_ref[:, lo + half:lo + RET_QK_DIM] = ((x2 * cos + x1 * sin) * scale).astype(BF16)

    rope_heads(0, q_ref, 1.0)
    rope_heads(D_MODEL, k_ref, RET_QK_DIM ** -0.5)
    for hd in range(RET_HEADS):
        lo = hd * RET_V_DIM
        v = jnp.dot(xn, w_ref[:, 2 * D_MODEL + lo:2 * D_MODEL + lo + RET_V_DIM],
                    preferred_element_type=F32)
        v_ref[:, lo:lo + RET_V_DIM] = v.astype(BF16)
        g = jnp.dot(xn, w_ref[:, 4 * D_MODEL + lo:4 * D_MODEL + lo + RET_V_DIM],
                    preferred_element_type=F32)
        g_ref[:, lo:lo + RET_V_DIM] = _silu(g).astype(BF16)


def _inproj1(h, gain, w_in, cos_t, sin_t, tiles_per_seq):
    rows = h.shape[0]
    tm = ROW_TILE
    row = lambda w: pl.BlockSpec((tm, w), lambda i: (i, 0))
    tab = pl.BlockSpec((tm, RET_QK_DIM // 2), lambda i: (i % tiles_per_seq, 0))
    return pl.pallas_call(
        _inproj1_kernel,
        grid=(rows // tm,),
        in_specs=[row(D_MODEL), _const_spec((1, D_MODEL)), _const_spec(w_in.shape), tab, tab],
        out_specs=[row(D_MODEL), row(D_MODEL), row(2 * D_MODEL), row(2 * D_MODEL)],
        out_shape=[jax.ShapeDtypeStruct((rows, D_MODEL), BF16),
                   jax.ShapeDtypeStruct((rows, D_MODEL), BF16),
                   jax.ShapeDtypeStruct((rows, 2 * D_MODEL), BF16),
                   jax.ShapeDtypeStruct((rows, 2 * D_MODEL), BF16)],
        compiler_params=pltpu.CompilerParams(dimension_semantics=("arbitrary",)),
        name="l1_inproj",
    )(h, gain, w_in, cos_t, sin_t)


def _retention_kernel(q_ref, k_ref, v_ref, g_ref, o_ref, state):
    @pl.when(pl.program_id(1) == 0)
    def _():
        state[...] = jnp.zeros_like(state)

    ii = lax.broadcasted_iota(jnp.int32, (BLOCK, BLOCK), 0)
    jj = lax.broadcasted_iota(jnp.int32, (BLOCK, BLOCK), 1)
    diff = (ii - jj).astype(F32)
    idx = lax.broadcasted_iota(jnp.int32, (BLOCK, 1), 0).astype(F32)
    contract_last = (((1,), (1,)), ((), ()))
    contract_rows = (((0,), (0,)), ((), ()))

    for hd in range(RET_HEADS):
        log_g = RET_LOG_G[hd]
        decay_intra = jnp.where(diff >= 0.0, jnp.exp(jnp.maximum(diff, 0.0) * log_g), 0.0)
        decay_q = jnp.exp((idx + 1.0) * log_g)
        decay_k = jnp.exp((BLOCK - 1.0 - idx) * log_g)
        decay_chunk = math.exp(BLOCK * log_g)
        q = q_ref[:, hd * RET_QK_DIM:(hd + 1) * RET_QK_DIM]
        k = k_ref[:, hd * RET_QK_DIM:(hd + 1) * RET_QK_DIM]
        v = v_ref[:, hd * RET_V_DIM:(hd + 1) * RET_V_DIM]
        qk = lax.dot_general(q, k, contract_last, preferred_element_type=F32) * decay_intra
        st = state[hd]
        o = (jnp.dot(qk.astype(BF16), v, preferred_element_type=F32)
             + jnp.dot(q, st.astype(BF16), preferred_element_type=F32) * decay_q)
        kd = (k.astype(F32) * decay_k).astype(BF16)
        state[hd] = decay_chunk * st + lax.dot_general(kd, v, contract_rows,
                                                       preferred_element_type=F32)
        gate = g_ref[:, hd * RET_V_DIM:(hd + 1) * RET_V_DIM].astype(F32)
        o_ref[:, hd * RET_V_DIM:(hd + 1) * RET_V_DIM] = (o * _rms_scale(o) * gate).astype(BF16)


def _retention(q, k, v, g, batch, nblk):
    row = lambda w: pl.BlockSpec((BLOCK, w), lambda b, n: (b * nblk + n, 0))
    return pl.pallas_call(
        _retention_kernel,
        grid=(batch, nblk),
        in_specs=[row(D_MODEL), row(D_MODEL), row(2 * D_MODEL), row(2 * D_MODEL)],
        out_specs=row(2 * D_MODEL),
        out_shape=jax.ShapeDtypeStruct(v.shape, BF16),
        scratch_shapes=[pltpu.VMEM((RET_HEADS, RET_QK_DIM, RET_V_DIM), F32)],
        compiler_params=pltpu.CompilerParams(dimension_semantics=("arbitrary", "arbitrary")),
        name="l1_retention",
    )(q, k, v, g)


def _rope_angles(n_rows, half, theta):
    inv_freq = jnp.power(jnp.asarray(theta, F32), -jnp.arange(half, dtype=F32) / half)
    pos = (jnp.arange(n_rows, dtype=jnp.int32) - META_PAD).astype(F32)
    return pos[:, None] * inv_freq[None, :]


def _block_diag(w):
    heads, wi, wo = w.shape
    eye = jnp.eye(heads, dtype=w.dtype)
    return (eye[:, None, :, None] * w[:, :, None, :]).reshape(heads * wi, heads * wo)


def kernel(x, meta_tokens, mix_norm_ab, ab_w_in, lru_conv_w, lru_conv_b, lru_w_a, lru_b_a, lru_w_i, lru_b_i, lru_lambda, q_norm, k_norm, attn_sinks, ab_w_out, mix_norm_ret, ret_w_in, ret_w_out, ffn_norm, ffn_w_gu, ffn_w_down):
    batch, seq, _ = x.shape
    seq_rows = META_PAD + N_META + seq
    nblk = seq_rows // BLOCK
    tiles_per_seq = seq_rows // ROW_TILE

    meta = jnp.broadcast_to(meta_tokens[None].astype(x.dtype), (batch, N_META, D_MODEL))
    hp = jnp.concatenate([jnp.zeros((batch, META_PAD, D_MODEL), x.dtype), meta, x], axis=1)
    hp = hp.reshape(batch * seq_rows, D_MODEL)

    ang = _rope_angles(seq_rows, ROT_DIM // 2, ROPE_THETA)
    cos8, sin8 = jnp.cos(ang), jnp.sin(ang)
    rest = ATT_HEAD_DIM - ROT_DIM
    ones = jnp.ones((seq_rows, rest), F32)
    zeros = jnp.zeros((seq_rows, rest), F32)
    zero8 = jnp.zeros_like(sin8)
    att_cos = jnp.tile(jnp.concatenate([cos8, cos8, ones], axis=1), (1, 2))
    att_sin_lo = jnp.tile(jnp.concatenate([-sin8, zero8, zeros], axis=1), (1, 2))
    att_sin_hi = jnp.tile(jnp.concatenate([zero8, sin8, zeros], axis=1), (1, 2))
    ang = _rope_angles(seq_rows, RET_QK_DIM // 2, RET_THETA)
    ret_cos, ret_sin = jnp.cos(ang), jnp.sin(ang)

    row_vec = lambda v: v.reshape(1, -1).astype(F32)
    two_heads = lambda v: jnp.tile(v.reshape(1, -1).astype(F32), (1, 2))

    xr, gate, q, k, v = _inproj0(hp, row_vec(mix_norm_ab[0]), ab_w_in[0].astype(BF16),
                                 two_heads(q_norm[0]), two_heads(k_norm[0]),
                                 att_cos, att_sin_lo, att_sin_hi, tiles_per_seq)
    w_gates = jnp.concatenate([_block_diag(lru_w_a[0]), _block_diag(lru_w_i[0])], axis=1)
    b_gates = jnp.concatenate([lru_b_a[0].reshape(1, -1), lru_b_i[0].reshape(1, -1)], axis=1)
    h = _l0_mix_ffn(attn_sinks[0].astype(F32), hp, xr, gate, q, k, v, lru_conv_w[0],
                    row_vec(lru_conv_b[0]), w_gates.astype(BF16), b_gates.astype(F32),
                    row_vec(lru_lambda[0]), ab_w_out[0].astype(BF16), row_vec(ffn_norm[0]),
                    ffn_w_gu[0].astype(BF16), ffn_w_down[0].astype(BF16), tiles_per_seq)

    rq, rk, rv, rg = _inproj1(h, row_vec(mix_norm_ret[0]), ret_w_in[0].astype(BF16),
                              ret_cos, ret_sin, tiles_per_seq)
    y_ret = _retention(rq, rk, rv, rg, batch, nblk)
    h = _outproj_ffn(h, [y_ret], ret_w_out[0].astype(BF16), row_vec(ffn_norm[1]),
                     ffn_w_gu[1].astype(BF16), ffn_w_down[1].astype(BF16), "l1_outproj_ffn")

    return h.reshape(batch, seq_rows, D_MODEL)[:, BLOCK:]
```

```python
import functools
import itertools
import math

import jax
import jax.numpy as jnp
from jax import lax
from jax.experimental import pallas as pl
from jax.experimental.pallas import tpu as pltpu

F32 = jnp.float32
BF16 = jnp.bfloat16

D_MODEL = 1024
N_META = 16
BLOCK = 128
META_PAD = BLOCK - N_META
RMS_EPS = 1e-6
NEG_INF = -1e30

LRU_WIDTH = 512
LRU_HEADS = 8
LRU_BLOCK_W = 64
CONV_W = 4
LRU_C = 8.0

ATT_HEADS = 8
ATT_KV_HEADS = 2
ATT_GROUP = ATT_HEADS // ATT_KV_HEADS
ATT_HEAD_DIM = 64
ROPE_THETA = 500000.0
ROT_DIM = 16
Q_WIDTH = 512
KV_WIDTH = 128
AB_IN_WIDTH = 2 * LRU_WIDTH + Q_WIDTH + 2 * KV_WIDTH

RET_HEADS = 4
RET_QK_DIM = 256
RET_V_DIM = 512
RET_THETA = 10000.0
RET_LOG_G = tuple(math.log1p(-(2.0 ** (-5.0 - h))) for h in range(RET_HEADS))

D_FF = 2816

LANES = 128
SUBLANES = 8
ROW_TILE = 640
OUT_TILE = 512
INPROJ_CHUNK = 256
CAST_STEPS = 16
FFN_CHUNK = 256
DOWN_CHUNK = 256


def _rms_scale(x):
    return lax.rsqrt(jnp.mean(x * x, axis=-1, keepdims=True) + RMS_EPS)


def _silu(x):
    return x * jax.nn.sigmoid(x)


def _gelu_tanh(x):
    return 0.5 * x * (1.0 + jnp.tanh(0.7978845608028654 * (x + 0.044715 * (x * x * x))))


def _const_spec(shape):
    zeros = (0,) * len(shape)
    return pl.BlockSpec(shape, lambda *_: zeros, pipeline_mode=pl.Buffered(1))


def _padded_rows(x_ref, meta_ref, first):
    xw = x_ref[...]
    tm = xw.shape[0]
    return jnp.concatenate([jnp.where(first, meta_ref[...], xw[0:BLOCK]),
                            jnp.where(first, xw[0:tm - BLOCK], xw[BLOCK:tm])], axis=0)


def _token_window_spec(tm, tiles_per_seq, seq, tile_of_step):
    def index(s):
        t = tile_of_step(s)
        start = jnp.maximum((t % tiles_per_seq) * tm - BLOCK, 0)
        return (pl.multiple_of((t // tiles_per_seq) * seq + start, BLOCK), 0)
    return pl.BlockSpec((pl.Element(tm), pl.Element(D_MODEL)), index)


def _rope_offsets(freq_ref, cos_ref, sin_ref):
    r = lax.broadcasted_iota(jnp.int32, cos_ref.shape, 0).astype(F32)
    ang = r * freq_ref[...]
    cos_ref[...] = jnp.cos(ang)
    sin_ref[...] = jnp.sin(ang)


def _rope_tables(base_pos, freq_ref, cos_ref, sin_ref):
    ang = base_pos.astype(F32) * freq_ref[...]
    cb = jnp.cos(ang)
    sb = jnp.sin(ang)
    oc = cos_ref[...]
    os_ = sin_ref[...]
    return cb * oc - sb * os_, sb * oc + cb * os_


def _cast_blocks(cast_refs):
    n = len(cast_refs) // 2
    for src, dst in zip(cast_refs[:n], cast_refs[n:]):
        dst[...] = src[...].astype(BF16)


def _cast_specs(weights):
    ins, outs, shapes = [], [], []
    step = lambda s: jnp.minimum(s, CAST_STEPS - 1)
    for layer, w in weights:
        _, rows, cols = w.shape
        blk = rows // CAST_STEPS
        ins.append(pl.BlockSpec((None, blk, cols), lambda s, layer=layer: (layer, step(s), 0)))
        outs.append(pl.BlockSpec((blk, cols), lambda s: (step(s), 0)))
        shapes.append(jax.ShapeDtypeStruct((rows, cols), BF16))
    return ins, outs, shapes


def _inproj0_kernel(*refs, tiles_per_seq, n_cast):
    (x_ref, meta_ref, gain_ref, w_ref, qg_ref, kg_ref, freq_ref) = refs[:7]
    cast_in = refs[7:7 + n_cast]
    xr_ref, gate_ref, q_ref, k_ref, v_ref = refs[7 + n_cast:12 + n_cast]
    cast_out = refs[12 + n_cast:12 + 2 * n_cast]
    ocos, osin = refs[12 + 2 * n_cast:]
    step = pl.program_id(0)
    tile_in_seq = step % tiles_per_seq

    @pl.when(step == 0)
    def _():
        _rope_offsets(freq_ref, ocos, osin)

    _cast_blocks(cast_in + cast_out)

    h = _padded_rows(x_ref, meta_ref, tile_in_seq == 0)
    rows = h.shape[0]
    xn = (h * _rms_scale(h) * gain_ref[...]).astype(BF16)

    q0 = 2 * LRU_WIDTH
    k0 = q0 + Q_WIDTH
    y = jnp.dot(xn, w_ref[:, q0:], preferred_element_type=F32)
    v_ref[...] = y[:, k0 - q0 + KV_WIDTH:].astype(BF16)

    cos, sin = _rope_tables(tile_in_seq * rows - META_PAD, freq_ref, ocos, osin)
    lane = lax.broadcasted_iota(jnp.int32, (1, LANES), 1) & (ATT_HEAD_DIM - 1)
    half = ROT_DIM // 2
    sin_lo = sin * jnp.where(lane < half, -1.0, 0.0)
    sin_hi = sin * jnp.where((lane >= half) & (lane < ROT_DIM), 1.0, 0.0)
    hi = lax.broadcasted_iota(jnp.int32, (2 * LANES, LANES), 0) & (LANES - 1)
    hj = lax.broadcasted_iota(jnp.int32, (2 * LANES, LANES), 1)
    head_mean = jnp.where(hi // ATT_HEAD_DIM == hj // ATT_HEAD_DIM,
                          1.0 / ATT_HEAD_DIM, 0.0).astype(BF16)

    def norm_rope(x, gain, scale):
        sq = x * x
        sq_hi = sq.astype(BF16)
        sq_lo = (sq - sq_hi.astype(F32)).astype(BF16)
        ms = jnp.dot(jnp.concatenate([sq_hi, sq_lo], axis=1), head_mean,
                     preferred_element_type=F32)
        xg = x * gain
        rot = (xg * cos + pltpu.roll(xg, LANES - half, 1) * sin_lo
               + pltpu.roll(xg, half, 1) * sin_hi)
        return rot * (lax.rsqrt(ms + RMS_EPS) * scale)

    def qk_step(j):
        if j < Q_WIDTH // LANES:
            q_ref[:, j * LANES:(j + 1) * LANES] = norm_rope(
                y[:, j * LANES:(j + 1) * LANES], qg_ref[...], ATT_HEAD_DIM ** -0.5).astype(BF16)
        else:
            k_ref[...] = norm_rope(y[:, k0 - q0:k0 - q0 + KV_WIDTH], kg_ref[...],
                                   1.0).astype(BF16)

    def proj_step(c):
        cols = slice(c * INPROJ_CHUNK, (c + 1) * INPROJ_CHUNK)
        out_ref = xr_ref if c < LRU_WIDTH // INPROJ_CHUNK else gate_ref
        dst = slice((c * INPROJ_CHUNK) % LRU_WIDTH, (c * INPROJ_CHUNK) % LRU_WIDTH + INPROJ_CHUNK)
        out_ref[:, dst] = jnp.dot(xn, w_ref[:, cols], preferred_element_type=F32)

    proj_steps = [functools.partial(proj_step, c) for c in range(2 * LRU_WIDTH // INPROJ_CHUNK)]
    qk_steps = [functools.partial(qk_step, j) for j in range(Q_WIDTH // LANES + 1)]
    for proj, qk in itertools.zip_longest(proj_steps, qk_steps):
        if proj is not None:
            proj()
        if qk is not None:
            qk()


def _inproj0(x2d, meta_blk, gain, w_in, q_gain, k_gain, freq, cast_weights, batch,
             tiles_per_seq, seq):
    tm = ROW_TILE
    rows = batch * tiles_per_seq * tm
    row = lambda w: pl.BlockSpec((tm, w), lambda i: (i, 0))
    cast_in, cast_out, cast_shapes = _cast_specs(cast_weights)
    return pl.pallas_call(
        functools.partial(_inproj0_kernel, tiles_per_seq=tiles_per_seq, n_cast=len(cast_in)),
        grid=(rows // tm,),
        in_specs=[_token_window_spec(tm, tiles_per_seq, seq, lambda s: s),
                  _const_spec((BLOCK, D_MODEL)), _const_spec((1, D_MODEL)),
                  _const_spec((D_MODEL, AB_IN_WIDTH)), _const_spec((1, LANES)),
                  _const_spec((1, LANES)), _const_spec((1, LANES))] + cast_in,
        out_specs=[row(LRU_WIDTH), row(LRU_WIDTH), row(Q_WIDTH), row(KV_WIDTH), row(KV_WIDTH)]
        + cast_out,
        out_shape=[jax.ShapeDtypeStruct((rows, LRU_WIDTH), F32),
                   jax.ShapeDtypeStruct((rows, LRU_WIDTH), F32),
                   jax.ShapeDtypeStruct((rows, Q_WIDTH), BF16),
                   jax.ShapeDtypeStruct((rows, KV_WIDTH), BF16),
                   jax.ShapeDtypeStruct((rows, KV_WIDTH), BF16)] + cast_shapes,
        scratch_shapes=[pltpu.VMEM((tm, LANES), F32), pltpu.VMEM((tm, LANES), F32)],
        compiler_params=pltpu.CompilerParams(dimension_semantics=("arbitrary",)),
        name="l0_inproj",
    )(x2d, meta_blk, gain, w_in, q_gain, k_gain, freq, *[w for _, w in cast_weights])


def _attn_group(n, sinks, q, kc, kp, km, vc, vp, vm):
    rows = ATT_GROUP * BLOCK
    i = lax.broadcasted_iota(jnp.int32, (rows, BLOCK), 0) & (BLOCK - 1)
    j = lax.broadcasted_iota(jnp.int32, (rows, BLOCK), 1)
    causal = j <= i
    win_ok = n >= jnp.where(causal, 1, 2)
    meta_ok = (j >= META_PAD) & (n >= jnp.where(causal, 0, 1))
    head_of_row = lax.broadcasted_iota(jnp.int32, (rows, 1), 0) // BLOCK
    contract_last = (((1,), (1,)), ((), ()))

    qs = jnp.concatenate([q[:, a * ATT_HEAD_DIM:(a + 1) * ATT_HEAD_DIM]
                          for a in range(ATT_GROUP)], axis=0)
    s_c = lax.dot_general(qs, kc, contract_last, preferred_element_type=F32)
    s_p = lax.dot_general(qs, kp, contract_last, preferred_element_type=F32)
    s_m = lax.dot_general(qs, km, contract_last, preferred_element_type=F32)
    s_w = jnp.where(win_ok, jnp.where(causal, s_c, s_p), NEG_INF)
    s_m = jnp.where(meta_ok, s_m, NEG_INF)
    sink = jnp.zeros((rows, 1), F32)
    for a in range(ATT_GROUP):
        sink = jnp.where(head_of_row == a, sinks[a], sink)
    m = jnp.maximum(jnp.maximum(jnp.max(s_w, axis=-1, keepdims=True),
                                jnp.max(s_m, axis=-1, keepdims=True)), sink)
    p_w = jnp.exp(s_w - m)
    p_m = jnp.exp(s_m - m)
    den = (jnp.sum(p_w, axis=-1, keepdims=True) + jnp.sum(p_m, axis=-1, keepdims=True)
           + jnp.exp(sink - m))
    p_c = jnp.where(causal, p_w, 0.0).astype(BF16)
    p_p = jnp.where(causal, 0.0, p_w).astype(BF16)
    o = (jnp.dot(p_c, vc, preferred_element_type=F32)
         + jnp.dot(p_p, vp, preferred_element_type=F32)
         + jnp.dot(p_m.astype(BF16), vm, preferred_element_type=F32))
    o = o / den
    return jnp.concatenate([o[a * BLOCK:(a + 1) * BLOCK] for a in range(ATT_GROUP)],
                           axis=1).astype(BF16)


def _lru_block(n, x, gate, cw, cb, wg_ref, bg, lam, xbuf, hcar):
    tl = x.shape[0]
    xbuf[SUBLANES:SUBLANES + tl, :] = x
    xc = x * cw[CONV_W - 1:CONV_W] + cb
    for d in range(1, CONV_W):
        xc = xc + xbuf[SUBLANES - d:SUBLANES - d + tl, :] * cw[CONV_W - 1 - d:CONV_W - d]
    xbuf[0:SUBLANES, :] = x[tl - SUBLANES:tl]

    ga = jnp.dot(xc.astype(BF16), wg_ref[...], preferred_element_type=F32) + bg
    r = jax.nn.sigmoid(ga[:, :LRU_WIDTH])
    gi = jax.nn.sigmoid(ga[:, LRU_WIDTH:])
    z = -lam
    softplus = jnp.maximum(z, 0.0) + jnp.log1p(jnp.exp(-jnp.abs(z)))
    log_a = (-LRU_C * softplus) * r
    a = jnp.exp(log_a)
    mult = jnp.sqrt(jnp.tanh(-log_a) * (a * a + 1.0))
    row = lax.broadcasted_iota(jnp.int32, (tl, 1), 0)
    t = n * tl + row
    mult = jnp.where(t == META_PAD, 1.0, mult)
    b = jnp.where(t < META_PAD, 0.0, mult * gi * xc)

    d = 1
    while d < tl:
        keep = row >= d
        b = jnp.where(keep, a * pltpu.roll(b, d, 0), 0.0) + b
        a = jnp.where(keep, a * pltpu.roll(a, d, 0), a)
        d *= 2
    h = b + a * hcar[0:1, :]
    hcar[...] = jnp.broadcast_to(h[tl - 1:tl], hcar.shape)
    return (_gelu_tanh(gate) * h).astype(BF16)


def _ffn(h1, gain, wgu_ref, wd_ref):
    xn = (h1 * _rms_scale(h1) * gain).astype(BF16)
    acts = []
    for c in range(D_FF // FFN_CHUNK):
        lo = c * FFN_CHUNK
        g = jnp.dot(xn, wgu_ref[:, lo:lo + FFN_CHUNK], preferred_element_type=F32)
        u = jnp.dot(xn, wgu_ref[:, D_FF + lo:D_FF + lo + FFN_CHUNK], preferred_element_type=F32)
        acts.append((_silu(g) * u).astype(BF16))
    act = jnp.concatenate(acts, axis=1)
    return h1 + jnp.dot(act, wd_ref[...], preferred_element_type=F32)


def _l0_mix_ffn_kernel(*refs, tiles_per_seq, n_tiles, n_cast):
    (sink_ref, x_ref, meta_ref, xr_ref, gate_ref, q_ref, k_ref, v_ref, cw_ref, cb_ref, wg_ref,
     bg_ref, lam_ref, wo_ref, gain_ref, wgu_ref, wd_ref) = refs[:17]
    cast_in = refs[17:17 + n_cast]
    o_ref = refs[17 + n_cast]
    cast_out = refs[18 + n_cast:18 + 2 * n_cast]
    ybuf, xbuf, hcar, kprev, vprev, kmeta, vmeta = refs[18 + 2 * n_cast:]
    s = pl.program_id(0)
    tile_in_seq = jnp.minimum(s, n_tiles - 1) % tiles_per_seq
    blocks_per_tile = x_ref.shape[0] // BLOCK
    last = slice((blocks_per_tile - 1) * BLOCK, blocks_per_tile * BLOCK)
    _cast_blocks(cast_in + cast_out)

    @pl.when(s == 0)
    def _():
        ybuf[...] = jnp.zeros_like(ybuf)

    @pl.when(tile_in_seq == 0)
    def _():
        xbuf[0:SUBLANES, :] = jnp.zeros((SUBLANES, LRU_WIDTH), F32)
        hcar[...] = jnp.zeros_like(hcar)
        kmeta[...] = k_ref[0:BLOCK, :]
        vmeta[...] = v_ref[0:BLOCK, :]
        kprev[...] = k_ref[0:BLOCK, :]
        vprev[...] = v_ref[0:BLOCK, :]

    ffn_first = jnp.maximum(s - 1, 0) % tiles_per_seq == 0
    h1 = (_padded_rows(x_ref, meta_ref, ffn_first)
          + jnp.dot(ybuf[...], wo_ref[...], preferred_element_type=F32))
    xn = (h1 * _rms_scale(h1) * gain_ref[...]).astype(BF16)

    cw = cw_ref[...]
    cb = cb_ref[...]
    bg = bg_ref[...]
    lam = lam_ref[...]

    def lru_step(blk):
        n = tile_in_seq * blocks_per_tile + blk
        rows = slice(blk * BLOCK, (blk + 1) * BLOCK)
        ybuf[rows, 0:LRU_WIDTH] = _lru_block(n, xr_ref[rows, :], gate_ref[rows, :], cw, cb,
                                             wg_ref, bg, lam, xbuf, hcar)

    def attn_step(blk, g):
        n = tile_in_seq * blocks_per_tile + blk
        rows = slice(blk * BLOCK, (blk + 1) * BLOCK)
        prev_rows = slice((blk - 1) * BLOCK, blk * BLOCK)
        lanes = slice(g * ATT_HEAD_DIM, (g + 1) * ATT_HEAD_DIM)
        width = ATT_GROUP * ATT_HEAD_DIM
        kp = kprev[:, lanes] if blk == 0 else k_ref[prev_rows, lanes]
        vp = vprev[:, lanes] if blk == 0 else v_ref[prev_rows, lanes]
        sinks = [sink_ref[g * ATT_GROUP + a] for a in range(ATT_GROUP)]
        ybuf[rows, LRU_WIDTH + g * width:LRU_WIDTH + (g + 1) * width] = _attn_group(
            n, sinks, q_ref[rows, g * width:(g + 1) * width], k_ref[rows, lanes], kp,
            kmeta[:, lanes], v_ref[rows, lanes], vp, vmeta[:, lanes])

    acts = []

    def gate_up_step(c):
        lo = c * FFN_CHUNK
        g = jnp.dot(xn, wgu_ref[:, lo:lo + FFN_CHUNK], preferred_element_type=F32)
        u = jnp.dot(xn, wgu_ref[:, D_FF + lo:D_FF + lo + FFN_CHUNK], preferred_element_type=F32)
        acts.append((_silu(g) * u).astype(BF16))

    def down_step(c):
        cols = slice(c * DOWN_CHUNK, (c + 1) * DOWN_CHUNK)
        if len(acts) > 1:
            acts[:] = [jnp.concatenate(acts, axis=1)]
        o_ref[:, cols] = h1[:, cols] + jnp.dot(acts[0], wd_ref[:, cols],
                                               preferred_element_type=F32)

    ffn_steps = ([functools.partial(gate_up_step, c) for c in range(D_FF // FFN_CHUNK)]
                 + [functools.partial(down_step, c) for c in range(D_MODEL // DOWN_CHUNK)])
    mix_steps = []
    for blk in range(blocks_per_tile):
        mix_steps.append(functools.partial(lru_step, blk))
        mix_steps.extend(functools.partial(attn_step, blk, g) for g in range(ATT_KV_HEADS))
    for ffn_step, mix_step in itertools.zip_longest(ffn_steps, mix_steps):
        if ffn_step is not None:
            ffn_step()
        if mix_step is not None:
            mix_step()
    kprev[...] = k_ref[last, :]
    vprev[...] = v_ref[last, :]


def _l0_mix_ffn(sinks, x2d, meta_blk, xr, gate, q, k, v, conv_w, conv_b, w_gates, b_gates, lam,
                w_out, gain, w_gu, w_down, cast_weights, tiles_per_seq, seq):
    rows = xr.shape[0]
    tm = ROW_TILE
    n_tiles = rows // tm
    mix = lambda w: pl.BlockSpec((tm, w), lambda s: (jnp.minimum(s, n_tiles - 1), 0))
    cast_in, cast_out, cast_shapes = _cast_specs(cast_weights)
    return pl.pallas_call(
        functools.partial(_l0_mix_ffn_kernel, tiles_per_seq=tiles_per_seq, n_tiles=n_tiles,
                          n_cast=len(cast_in)),
        grid=(n_tiles + 1,),
        in_specs=[pl.BlockSpec(memory_space=pltpu.SMEM),
                  _token_window_spec(tm, tiles_per_seq, seq, lambda s: jnp.maximum(s - 1, 0)),
                  _const_spec((BLOCK, D_MODEL)),
                  mix(LRU_WIDTH), mix(LRU_WIDTH), mix(Q_WIDTH), mix(KV_WIDTH), mix(KV_WIDTH),
                  _const_spec((CONV_W, LRU_WIDTH)), _const_spec((1, LRU_WIDTH)),
                  _const_spec((LRU_WIDTH, 2 * LRU_WIDTH)), _const_spec((1, 2 * LRU_WIDTH)),
                  _const_spec((1, LRU_WIDTH)), _const_spec(w_out.shape), _const_spec((1, D_MODEL)),
                  _const_spec(w_gu.shape), _const_spec(w_down.shape)] + cast_in,
        out_specs=[pl.BlockSpec((tm, D_MODEL), lambda s: (jnp.maximum(s - 1, 0), 0))] + cast_out,
        out_shape=[jax.ShapeDtypeStruct((rows, D_MODEL), F32)] + cast_shapes,
        scratch_shapes=[pltpu.VMEM((tm, LRU_WIDTH + Q_WIDTH), BF16),
                        pltpu.VMEM((SUBLANES + BLOCK, LRU_WIDTH), F32),
                        pltpu.VMEM((SUBLANES, LRU_WIDTH), F32),
                        pltpu.VMEM((BLOCK, KV_WIDTH), BF16), pltpu.VMEM((BLOCK, KV_WIDTH), BF16),
                        pltpu.VMEM((BLOCK, KV_WIDTH), BF16), pltpu.VMEM((BLOCK, KV_WIDTH), BF16)],
        compiler_params=pltpu.CompilerParams(dimension_semantics=("arbitrary",)),
        name="l0_mix_ffn",
    )(sinks, x2d, meta_blk, xr, gate, q, k, v, conv_w, conv_b, w_gates, b_gates, lam, w_out, gain,
      w_gu, w_down, *[w for _, w in cast_weights])


def _outproj_ffn_kernel(h_ref, y_ref, wo_ref, gain_ref, wgu_ref, wd_ref, o_ref):
    h1 = h_ref[...] + jnp.dot(y_ref[...], wo_ref[...], preferred_element_type=F32)
    o_ref[...] = _ffn(h1, gain_ref[...], wgu_ref, wd_ref)


def _final_outproj_ffn(h, y, w_out, gain, w_gu, w_down, batch, seq_rows, seq):
    tm = OUT_TILE
    tiles = seq // tm
    win = lambda w: pl.BlockSpec((pl.Element(tm), pl.Element(w)),
                                 lambda b, j: (pl.multiple_of(b * seq_rows + BLOCK + j * tm, BLOCK),
                                               0))
    return pl.pallas_call(
        _outproj_ffn_kernel,
        grid=(batch, tiles),
        in_specs=[win(D_MODEL), win(y.shape[1]), _const_spec(w_out.shape),
                  _const_spec((1, D_MODEL)), _const_spec(w_gu.shape), _const_spec(w_down.shape)],
        out_specs=pl.BlockSpec((tm, D_MODEL), lambda b, j: (b * tiles + j, 0)),
        out_shape=jax.ShapeDtypeStruct((batch * seq, D_MODEL), F32),
        compiler_params=pltpu.CompilerParams(dimension_semantics=("arbitrary", "arbitrary")),
        name="l1_outproj_ffn",
    )(h, y, w_out, gain, w_gu, w_down)


def _inproj1_kernel(h_ref, gain_ref, w_ref, freq_ref, q_ref, k_ref, v_ref, g_ref, ocos, osin,
                    *, tiles_per_seq):
    step = pl.program_id(0)

    @pl.when(step == 0)
    def _():
        _rope_offsets(freq_ref, ocos, osin)

    h = h_ref[...]
    xn = (h * _rms_scale(h) * gain_ref[...]).astype(BF16)
    cos, sin = _rope_tables((step % tiles_per_seq) * h.shape[0] - META_PAD, freq_ref, ocos, osin)
    half = RET_QK_DIM // 2

    def rope_heads(col0, out_ref, scale):
        for hd in range(RET_HEADS):
            lo = hd * RET_QK_DIM
            y = jnp.dot(xn, w_ref[:, col0 + lo:col0 + lo + RET_QK_DIM],
                        preferred_element_type=F32)
            x1 = y[:, :half]
            x2 = y[:, half:]
            out_ref[:, lo:lo + half] = ((x1 * cos - x2 * sin) * scale).astype(BF16)
            out_ref[:, lo + half:lo + RET_QK_DIM] = ((x2 * cos + x1 * sin) * scale).astype(BF16)

    rope_heads(0, q_ref, 1.0)
    rope_heads(D_MODEL, k_ref, RET_QK_DIM ** -0.5)
    for hd in range(RET_HEADS):
        lo = hd * RET_V_DIM
        v = jnp.dot(xn, w_ref[:, 2 * D_MODEL + lo:2 * D_MODEL + lo + RET_V_DIM],
                    preferred_element_type=F32)
        v_ref[:, lo:lo + RET_V_DIM] = v.astype(BF16)
        g = jnp.dot(xn, w_ref[:, 4 * D_MODEL + lo:4 * D_MODEL + lo + RET_V_DIM],
                    preferred_element_type=F32)
        g_ref[:, lo:lo + RET_V_DIM] = _silu(g).astype(BF16)


def _inproj1(h, gain, w_in, freq, tiles_per_seq):
    rows = h.shape[0]
    tm = ROW_TILE
    half = RET_QK_DIM // 2
    row = lambda w: pl.BlockSpec((tm, w), lambda i: (i, 0))
    return pl.pallas_call(
        functools.partial(_inproj1_kernel, tiles_per_seq=tiles_per_seq),
        grid=(rows // tm,),
        in_specs=[row(D_MODEL), _const_spec((1, D_MODEL)), _const_spec(w_in.shape),
                  _const_spec((1, half))],
        out_specs=[row(D_MODEL), row(D_MODEL), row(2 * D_MODEL), row(2 * D_MODEL)],
        out_shape=[jax.ShapeDtypeStruct((rows, D_MODEL), BF16),
                   jax.ShapeDtypeStruct((rows, D_MODEL), BF16),
                   jax.ShapeDtypeStruct((rows, 2 * D_MODEL), BF16),
                   jax.ShapeDtypeStruct((rows, 2 * D_MODEL), BF16)],
        scratch_shapes=[pltpu.VMEM((tm, half), F32), pltpu.VMEM((tm, half), F32)],
        compiler_params=pltpu.CompilerParams(dimension_semantics=("arbitrary",)),
        name="l1_inproj",
    )(h, gain, w_in, freq)


def _retention_kernel(q_ref, k_ref, v_ref, g_ref, o_ref, state):
    @pl.when(pl.program_id(1) == 0)
    def _():
        state[...] = jnp.zeros_like(state)

    ii = lax.broadcasted_iota(jnp.int32, (BLOCK, BLOCK), 0)
    jj = lax.broadcasted_iota(jnp.int32, (BLOCK, BLOCK), 1)
    diff = (ii - jj).astype(F32)
    idx = lax.broadcasted_iota(jnp.int32, (BLOCK, 1), 0).astype(F32)
    contract_last = (((1,), (1,)), ((), ()))
    contract_rows = (((0,), (0,)), ((), ()))

    for hd in range(RET_HEADS):
        log_g = RET_LOG_G[hd]
        decay_intra = jnp.where(diff >= 0.0, jnp.exp(jnp.maximum(diff, 0.0) * log_g), 0.0)
        decay_q = jnp.exp((idx + 1.0) * log_g)
        decay_k = jnp.exp((BLOCK - 1.0 - idx) * log_g)
        decay_chunk = math.exp(BLOCK * log_g)
        q = q_ref[:, hd * RET_QK_DIM:(hd + 1) * RET_QK_DIM]
        k = k_ref[:, hd * RET_QK_DIM:(hd + 1) * RET_QK_DIM]
        v = v_ref[:, hd * RET_V_DIM:(hd + 1) * RET_V_DIM]
        qk = lax.dot_general(q, k, contract_last, preferred_element_type=F32) * decay_intra
        st = state[hd]
        o = (jnp.dot(qk.astype(BF16), v, preferred_element_type=F32)
             + jnp.dot(q, st.astype(BF16), preferred_element_type=F32) * decay_q)
        kd = (k.astype(F32) * decay_k).astype(BF16)
        state[hd] = decay_chunk * st + lax.dot_general(kd, v, contract_rows,
                                                       preferred_element_type=F32)
        gate = g_ref[:, hd * RET_V_DIM:(hd + 1) * RET_V_DIM].astype(F32)
        o_ref[:, hd * RET_V_DIM:(hd + 1) * RET_V_DIM] = (o * _rms_scale(o) * gate).astype(BF16)


def _retention(q, k, v, g, batch, nblk):
    row = lambda w: pl.BlockSpec((BLOCK, w), lambda b, n: (b * nblk + n, 0))
    return pl.pallas_call(
        _retention_kernel,
        grid=(batch, nblk),
        in_specs=[row(D_MODEL), row(D_MODEL), row(2 * D_MODEL), row(2 * D_MODEL)],
        out_specs=row(2 * D_MODEL),
        out_shape=jax.ShapeDtypeStruct(v.shape, BF16),
        scratch_shapes=[pltpu.VMEM((RET_HEADS, RET_QK_DIM, RET_V_DIM), F32)],
        compiler_params=pltpu.CompilerParams(dimension_semantics=("arbitrary", "arbitrary")),
        name="l1_retention",
    )(q, k, v, g)


def _inv_freq(half, theta):
    return jnp.power(jnp.asarray(theta, F32), -jnp.arange(half, dtype=F32) / half)


def _block_diag(w):
    heads, wi, wo = w.shape
    eye = jnp.eye(heads, dtype=w.dtype)
    return (eye[:, None, :, None] * w[:, :, None, :]).reshape(heads * wi, heads * wo)


def kernel(x, meta_tokens, mix_norm_ab, ab_w_in, lru_conv_w, lru_conv_b, lru_w_a, lru_b_a, lru_w_i, lru_b_i, lru_lambda, q_norm, k_norm, attn_sinks, ab_w_out, mix_norm_ret, ret_w_in, ret_w_out, ffn_norm, ffn_w_gu, ffn_w_down):
    batch, seq, _ = x.shape
    seq_rows = META_PAD + N_META + seq
    nblk = seq_rows // BLOCK
    tiles_per_seq = seq_rows // ROW_TILE

    x2d = x.reshape(batch * seq, D_MODEL)
    meta_blk = jnp.concatenate([jnp.zeros((META_PAD, D_MODEL), x.dtype),
                                meta_tokens.astype(x.dtype)], axis=0)

    f_att = _inv_freq(ROT_DIM // 2, ROPE_THETA)
    f_att = jnp.concatenate([f_att, f_att, jnp.zeros((ATT_HEAD_DIM - ROT_DIM,), F32)])
    f_att = jnp.tile(f_att, LANES // ATT_HEAD_DIM).reshape(1, LANES)
    f_ret = _inv_freq(RET_QK_DIM // 2, RET_THETA).reshape(1, RET_QK_DIM // 2)

    row_vec = lambda v: v.reshape(1, -1).astype(F32)
    two_heads = lambda v: jnp.tile(v.reshape(1, -1).astype(F32), (1, 2))

    xr, gate, q, k, v, w_gu0, w_down0, w_in1, w_out1, w_gu1, w_down1 = _inproj0(
        x2d, meta_blk, row_vec(mix_norm_ab[0]), ab_w_in[0].astype(BF16), two_heads(q_norm[0]),
        two_heads(k_norm[0]), f_att,
        [(0, ffn_w_gu), (0, ffn_w_down), (0, ret_w_in), (0, ret_w_out), (1, ffn_w_gu),
         (1, ffn_w_down)], batch, tiles_per_seq, seq)
    w_gates = jnp.concatenate([_block_diag(lru_w_a[0]), _block_diag(lru_w_i[0])], axis=1)
    b_gates = jnp.concatenate([lru_b_a[0].reshape(1, -1), lru_b_i[0].reshape(1, -1)], axis=1)
    (h,) = _l0_mix_ffn(
        attn_sinks[0].astype(F32), x2d, meta_blk, xr, gate, q, k, v, lru_conv_w[0],
        row_vec(lru_conv_b[0]), w_gates.astype(BF16), b_gates.astype(F32), row_vec(lru_lambda[0]),
        ab_w_out[0].astype(BF16), row_vec(ffn_norm[0]), w_gu0, w_down0, [], tiles_per_seq, seq)

    rq, rk, rv, rg = _inproj1(h, row_vec(mix_norm_ret[0]), w_in1, f_ret, tiles_per_seq)
    y_ret = _retention(rq, rk, rv, rg, batch, nblk)
    out = _final_outproj_ffn(h, y_ret, w_out1, row_vec(ffn_norm[1]), w_gu1, w_down1,
                             batch, seq_rows, seq)
    return out.reshape(batch, seq, D_MODEL)
```

```python
import functools
import itertools
import math

import jax
import jax.numpy as jnp
from jax import lax
from jax.experimental import pallas as pl
from jax.experimental.pallas import tpu as pltpu

F32 = jnp.float32
BF16 = jnp.bfloat16

D_MODEL = 1024
N_META = 16
BLOCK = 128
META_PAD = BLOCK - N_META
RMS_EPS = 1e-6
NEG_INF = -1e30

LRU_WIDTH = 512
LRU_HEADS = 8
LRU_BLOCK_W = 64
CONV_W = 4
LRU_C = 8.0

ATT_HEADS = 8
ATT_KV_HEADS = 2
ATT_GROUP = ATT_HEADS // ATT_KV_HEADS
ATT_HEAD_DIM = 64
ROPE_THETA = 500000.0
ROT_DIM = 16
Q_WIDTH = 512
KV_WIDTH = 128
AB_IN_WIDTH = 2 * LRU_WIDTH + Q_WIDTH + 2 * KV_WIDTH

RET_HEADS = 4
RET_QK_DIM = 256
RET_V_DIM = 512
RET_THETA = 10000.0
RET_LOG_G = tuple(math.log1p(-(2.0 ** (-5.0 - h))) for h in range(RET_HEADS))

D_FF = 2816

LANES = 128
SUBLANES = 8
ROW_TILE = 640
OUT_TILE = 512
INPROJ_CHUNK = 256
CAST_STEPS = 16
FFN_CHUNK = 256
DOWN_CHUNK = 256


def _rms_scale(x):
    return lax.rsqrt(jnp.mean(x * x, axis=-1, keepdims=True) + RMS_EPS)


def _silu(x):
    return x * jax.nn.sigmoid(x)


def _gelu_tanh(x):
    return 0.5 * x * (1.0 + jnp.tanh(0.7978845608028654 * (x + 0.044715 * (x * x * x))))


def _const_spec(shape):
    zeros = (0,) * len(shape)
    return pl.BlockSpec(shape, lambda *_: zeros, pipeline_mode=pl.Buffered(1))


def _padded_rows(x_ref, meta_ref, first):
    xw = x_ref[...]
    tm = xw.shape[0]
    return jnp.concatenate([jnp.where(first, meta_ref[...], xw[0:BLOCK]),
                            jnp.where(first, xw[0:tm - BLOCK], xw[BLOCK:tm])], axis=0)


def _token_window_spec(tm, tiles_per_seq, seq, tile_of_step):
    def index(s):
        t = tile_of_step(s)
        start = jnp.maximum((t % tiles_per_seq) * tm - BLOCK, 0)
        return (pl.multiple_of((t // tiles_per_seq) * seq + start, BLOCK), 0)
    return pl.BlockSpec((pl.Element(tm), pl.Element(D_MODEL)), index)


def _rope_offsets(freq_ref, cos_ref, sin_ref):
    r = lax.broadcasted_iota(jnp.int32, cos_ref.shape, 0).astype(F32)
    ang = r * freq_ref[...]
    cos_ref[...] = jnp.cos(ang)
    sin_ref[...] = jnp.sin(ang)


def _rope_tables(base_pos, freq_ref, cos_ref, sin_ref):
    ang = base_pos.astype(F32) * freq_ref[...]
    cb = jnp.cos(ang)
    sb = jnp.sin(ang)
    oc = cos_ref[...]
    os_ = sin_ref[...]
    return cb * oc - sb * os_, sb * oc + cb * os_


def _cast_blocks(cast_refs):
    n = len(cast_refs) // 2
    for src, dst in zip(cast_refs[:n], cast_refs[n:]):
        dst[...] = src[...].astype(BF16)


def _cast_specs(weights):
    ins, outs, shapes = [], [], []
    step = lambda s: jnp.minimum(s, CAST_STEPS - 1)
    for layer, w in weights:
        _, rows, cols = w.shape
        blk = rows // CAST_STEPS
        ins.append(pl.BlockSpec((None, blk, cols), lambda s, layer=layer: (layer, step(s), 0)))
        outs.append(pl.BlockSpec((blk, cols), lambda s: (step(s), 0)))
        shapes.append(jax.ShapeDtypeStruct((rows, cols), BF16))
    return ins, outs, shapes


def _inproj0_kernel(*refs, tiles_per_seq, n_cast):
    (x_ref, meta_ref, gain_ref, w_ref, qg_ref, kg_ref, freq_ref) = refs[:7]
    cast_in = refs[7:7 + n_cast]
    xr_ref, gate_ref, q_ref, k_ref, v_ref = refs[7 + n_cast:12 + n_cast]
    cast_out = refs[12 + n_cast:12 + 2 * n_cast]
    ocos, osin = refs[12 + 2 * n_cast:]
    step = pl.program_id(0)
    tile_in_seq = step % tiles_per_seq

    @pl.when(step == 0)
    def _():
        _rope_offsets(freq_ref, ocos, osin)

    _cast_blocks(cast_in + cast_out)

    h = _padded_rows(x_ref, meta_ref, tile_in_seq == 0)
    rows = h.shape[0]
    xn = (h * _rms_scale(h) * gain_ref[...]).astype(BF16)

    q0 = 2 * LRU_WIDTH
    k0 = q0 + Q_WIDTH
    y = jnp.dot(xn, w_ref[:, q0:], preferred_element_type=F32)
    v_ref[...] = y[:, k0 - q0 + KV_WIDTH:].astype(BF16)

    cos, sin = _rope_tables(tile_in_seq * rows - META_PAD, freq_ref, ocos, osin)
    lane = lax.broadcasted_iota(jnp.int32, (1, LANES), 1) & (ATT_HEAD_DIM - 1)
    half = ROT_DIM // 2
    sin_lo = sin * jnp.where(lane < half, -1.0, 0.0)
    sin_hi = sin * jnp.where((lane >= half) & (lane < ROT_DIM), 1.0, 0.0)
    hi = lax.broadcasted_iota(jnp.int32, (2 * LANES, LANES), 0) & (LANES - 1)
    hj = lax.broadcasted_iota(jnp.int32, (2 * LANES, LANES), 1)
    head_mean = jnp.where(hi // ATT_HEAD_DIM == hj // ATT_HEAD_DIM,
                          1.0 / ATT_HEAD_DIM, 0.0).astype(BF16)

    def norm_rope(x, gain, scale):
        sq = x * x
        sq_hi = sq.astype(BF16)
        sq_lo = (sq - sq_hi.astype(F32)).astype(BF16)
        ms = jnp.dot(jnp.concatenate([sq_hi, sq_lo], axis=1), head_mean,
                     preferred_element_type=F32)
        xg = x * gain
        rot = (xg * cos + pltpu.roll(xg, LANES - half, 1) * sin_lo
               + pltpu.roll(xg, half, 1) * sin_hi)
        return rot * (lax.rsqrt(ms + RMS_EPS) * scale)

    def qk_step(j):
        if j < Q_WIDTH // LANES:
            q_ref[:, j * LANES:(j + 1) * LANES] = norm_rope(
                y[:, j * LANES:(j + 1) * LANES], qg_ref[...], ATT_HEAD_DIM ** -0.5).astype(BF16)
        else:
            k_ref[...] = norm_rope(y[:, k0 - q0:k0 - q0 + KV_WIDTH], kg_ref[...],
                                   1.0).astype(BF16)

    def proj_step(c):
        cols = slice(c * INPROJ_CHUNK, (c + 1) * INPROJ_CHUNK)
        out_ref = xr_ref if c < LRU_WIDTH // INPROJ_CHUNK else gate_ref
        dst = slice((c * INPROJ_CHUNK) % LRU_WIDTH, (c * INPROJ_CHUNK) % LRU_WIDTH + INPROJ_CHUNK)
        out_ref[:, dst] = jnp.dot(xn, w_ref[:, cols], preferred_element_type=F32)

    proj_steps = [functools.partial(proj_step, c) for c in range(2 * LRU_WIDTH // INPROJ_CHUNK)]
    qk_steps = [functools.partial(qk_step, j) for j in range(Q_WIDTH // LANES + 1)]
    for proj, qk in itertools.zip_longest(proj_steps, qk_steps):
        if proj is not None:
            proj()
        if qk is not None:
            qk()


def _inproj0(x2d, meta_blk, gain, w_in, q_gain, k_gain, freq, cast_weights, batch,
             tiles_per_seq, seq):
    tm = ROW_TILE
    rows = batch * tiles_per_seq * tm
    row = lambda w: pl.BlockSpec((tm, w), lambda i: (i, 0))
    cast_in, cast_out, cast_shapes = _cast_specs(cast_weights)
    return pl.pallas_call(
        functools.partial(_inproj0_kernel, tiles_per_seq=tiles_per_seq, n_cast=len(cast_in)),
        grid=(rows // tm,),
        in_specs=[_token_window_spec(tm, tiles_per_seq, seq, lambda s: s),
                  _const_spec((BLOCK, D_MODEL)), _const_spec((1, D_MODEL)),
                  _const_spec((D_MODEL, AB_IN_WIDTH)), _const_spec((1, LANES)),
                  _const_spec((1, LANES)), _const_spec((1, LANES))] + cast_in,
        out_specs=[row(LRU_WIDTH), row(LRU_WIDTH), row(Q_WIDTH), row(KV_WIDTH), row(KV_WIDTH)]
        + cast_out,
        out_shape=[jax.ShapeDtypeStruct((rows, LRU_WIDTH), F32),
                   jax.ShapeDtypeStruct((rows, LRU_WIDTH), F32),
                   jax.ShapeDtypeStruct((rows, Q_WIDTH), BF16),
                   jax.ShapeDtypeStruct((rows, KV_WIDTH), BF16),
                   jax.ShapeDtypeStruct((rows, KV_WIDTH), BF16)] + cast_shapes,
        scratch_shapes=[pltpu.VMEM((tm, LANES), F32), pltpu.VMEM((tm, LANES), F32)],
        compiler_params=pltpu.CompilerParams(dimension_semantics=("arbitrary",)),
        name="l0_inproj",
    )(x2d, meta_blk, gain, w_in, q_gain, k_gain, freq, *[w for _, w in cast_weights])


def _attn_group(n, sinks, q, kc, kp, km, vc, vp, vm):
    rows = ATT_GROUP * BLOCK
    i = lax.broadcasted_iota(jnp.int32, (rows, BLOCK), 0) & (BLOCK - 1)
    j = lax.broadcasted_iota(jnp.int32, (rows, BLOCK), 1)
    causal = j <= i
    win_ok = n >= jnp.where(causal, 1, 2)
    meta_ok = (j >= META_PAD) & (n >= jnp.where(causal, 0, 1))
    head_of_row = lax.broadcasted_iota(jnp.int32, (rows, 1), 0) // BLOCK
    contract_last = (((1,), (1,)), ((), ()))

    qs = jnp.concatenate([q[:, a * ATT_HEAD_DIM:(a + 1) * ATT_HEAD_DIM]
                          for a in range(ATT_GROUP)], axis=0)
    s_c = lax.dot_general(qs, kc, contract_last, preferred_element_type=F32)
    s_p = lax.dot_general(qs, kp, contract_last, preferred_element_type=F32)
    s_m = lax.dot_general(qs, km, contract_last, preferred_element_type=F32)
    s_w = jnp.where(win_ok, jnp.where(causal, s_c, s_p), NEG_INF)
    s_m = jnp.where(meta_ok, s_m, NEG_INF)
    sink = jnp.zeros((rows, 1), F32)
    for a in range(ATT_GROUP):
        sink = jnp.where(head_of_row == a, sinks[a], sink)
    m = jnp.maximum(jnp.maximum(jnp.max(s_w, axis=-1, keepdims=True),
                                jnp.max(s_m, axis=-1, keepdims=True)), sink)
    p_w = jnp.exp(s_w - m)
    p_m = jnp.exp(s_m - m)
    den = (jnp.sum(p_w, axis=-1, keepdims=True) + jnp.sum(p_m, axis=-1, keepdims=True)
           + jnp.exp(sink - m))
    p_c = jnp.where(causal, p_w, 0.0).astype(BF16)
    p_p = jnp.where(causal, 0.0, p_w).astype(BF16)
    o = (jnp.dot(p_c, vc, preferred_element_type=F32)
         + jnp.dot(p_p, vp, preferred_element_type=F32)
         + jnp.dot(p_m.astype(BF16), vm, preferred_element_type=F32))
    o = o / den
    return jnp.concatenate([o[a * BLOCK:(a + 1) * BLOCK] for a in range(ATT_GROUP)],
                           axis=1).astype(BF16)


def _lru_block(n, x, gate, cw, cb, wg_ref, bg, lam, xbuf, hcar):
    tl = x.shape[0]
    xbuf[SUBLANES:SUBLANES + tl, :] = x
    xc = x * cw[CONV_W - 1:CONV_W] + cb
    for d in range(1, CONV_W):
        xc = xc + xbuf[SUBLANES - d:SUBLANES - d + tl, :] * cw[CONV_W - 1 - d:CONV_W - d]
    xbuf[0:SUBLANES, :] = x[tl - SUBLANES:tl]

    ga = jnp.dot(xc.astype(BF16), wg_ref[...], preferred_element_type=F32) + bg
    r = jax.nn.sigmoid(ga[:, :LRU_WIDTH])
    gi = jax.nn.sigmoid(ga[:, LRU_WIDTH:])
    z = -lam
    softplus = jnp.maximum(z, 0.0) + jnp.log1p(jnp.exp(-jnp.abs(z)))
    log_a = (-LRU_C * softplus) * r
    a = jnp.exp(log_a)
    mult = jnp.sqrt(jnp.tanh(-log_a) * (a * a + 1.0))
    row = lax.broadcasted_iota(jnp.int32, (tl, 1), 0)
    t = n * tl + row
    mult = jnp.where(t == META_PAD, 1.0, mult)
    b = jnp.where(t < META_PAD, 0.0, mult * gi * xc)

    d = 1
    while d < tl:
        keep = row >= d
        b = jnp.where(keep, a * pltpu.roll(b, d, 0), 0.0) + b
        a = jnp.where(keep, a * pltpu.roll(a, d, 0), a)
        d *= 2
    h = b + a * hcar[0:1, :]
    hcar[...] = jnp.broadcast_to(h[tl - 1:tl], hcar.shape)
    return (_gelu_tanh(gate) * h).astype(BF16)


def _ffn(h1, gain, wgu_ref, wd_ref):
    xn = (h1 * _rms_scale(h1) * gain).astype(BF16)
    acts = []
    for c in range(D_FF // FFN_CHUNK):
        lo = c * FFN_CHUNK
        g = jnp.dot(xn, wgu_ref[:, lo:lo + FFN_CHUNK], preferred_element_type=F32)
        u = jnp.dot(xn, wgu_ref[:, D_FF + lo:D_FF + lo + FFN_CHUNK], preferred_element_type=F32)
        acts.append((_silu(g) * u).astype(BF16))
    act = jnp.concatenate(acts, axis=1)
    return h1 + jnp.dot(act, wd_ref[...], preferred_element_type=F32)


def _l0_mix_ffn_kernel(*refs, tiles_per_seq, n_tiles, n_cast):
    (sink_ref, x_ref, meta_ref, xr_ref, gate_ref, q_ref, k_ref, v_ref, cw_ref, cb_ref, wg_ref,
     bg_ref, lam_ref, wo_ref, gain_ref, wgu_ref, wd_ref) = refs[:17]
    cast_in = refs[17:17 + n_cast]
    o_ref = refs[17 + n_cast]
    cast_out = refs[18 + n_cast:18 + 2 * n_cast]
    ybuf, xbuf, hcar, kprev, vprev, kmeta, vmeta = refs[18 + 2 * n_cast:]
    s = pl.program_id(0)
    tile_in_seq = jnp.minimum(s, n_tiles - 1) % tiles_per_seq
    blocks_per_tile = x_ref.shape[0] // BLOCK
    last = slice((blocks_per_tile - 1) * BLOCK, blocks_per_tile * BLOCK)
    _cast_blocks(cast_in + cast_out)

    @pl.when(s == 0)
    def _():
        ybuf[...] = jnp.zeros_like(ybuf)

    @pl.when(tile_in_seq == 0)
    def _():
        xbuf[0:SUBLANES, :] = jnp.zeros((SUBLANES, LRU_WIDTH), F32)
        hcar[...] = jnp.zeros_like(hcar)
        kmeta[...] = k_ref[0:BLOCK, :]
        vmeta[...] = v_ref[0:BLOCK, :]
        kprev[...] = k_ref[0:BLOCK, :]
        vprev[...] = v_ref[0:BLOCK, :]

    ffn_first = jnp.maximum(s - 1, 0) % tiles_per_seq == 0
    h1 = (_padded_rows(x_ref, meta_ref, ffn_first)
          + jnp.dot(ybuf[...], wo_ref[...], preferred_element_type=F32))
    xn = (h1 * _rms_scale(h1) * gain_ref[...]).astype(BF16)

    cw = cw_ref[...]
    cb = cb_ref[...]
    bg = bg_ref[...]
    lam = lam_ref[...]

    def lru_step(blk):
        n = tile_in_seq * blocks_per_tile + blk
        rows = slice(blk * BLOCK, (blk + 1) * BLOCK)
        ybuf[rows, 0:LRU_WIDTH] = _lru_block(n, xr_ref[rows, :], gate_ref[rows, :], cw, cb,
                                             wg_ref, bg, lam, xbuf, hcar)

    def attn_step(blk, g):
        n = tile_in_seq * blocks_per_tile + blk
        rows = slice(blk * BLOCK, (blk + 1) * BLOCK)
        prev_rows = slice((blk - 1) * BLOCK, blk * BLOCK)
        lanes = slice(g * ATT_HEAD_DIM, (g + 1) * ATT_HEAD_DIM)
        width = ATT_GROUP * ATT_HEAD_DIM
        kp = kprev[:, lanes] if blk == 0 else k_ref[prev_rows, lanes]
        vp = vprev[:, lanes] if blk == 0 else v_ref[prev_rows, lanes]
        sinks = [sink_ref[g * ATT_GROUP + a] for a in range(ATT_GROUP)]
        ybuf[rows, LRU_WIDTH + g * width:LRU_WIDTH + (g + 1) * width] = _attn_group(
            n, sinks, q_ref[rows, g * width:(g + 1) * width], k_ref[rows, lanes], kp,
            kmeta[:, lanes], v_ref[rows, lanes], vp, vmeta[:, lanes])

    acts = []

    def gate_up_step(c):
        lo = c * FFN_CHUNK
        g = jnp.dot(xn, wgu_ref[:, lo:lo + FFN_CHUNK], preferred_element_type=F32)
        u = jnp.dot(xn, wgu_ref[:, D_FF + lo:D_FF + lo + FFN_CHUNK], preferred_element_type=F32)
        acts.append((_silu(g) * u).astype(BF16))

    def down_step(c):
        cols = slice(c * DOWN_CHUNK, (c + 1) * DOWN_CHUNK)
        if len(acts) > 1:
            acts[:] = [jnp.concatenate(acts, axis=1)]
        o_ref[:, cols] = h1[:, cols] + jnp.dot(acts[0], wd_ref[:, cols],
                                               preferred_element_type=F32)

    ffn_steps = ([functools.partial(gate_up_step, c) for c in range(D_FF // FFN_CHUNK)]
                 + [functools.partial(down_step, c) for c in range(D_MODEL // DOWN_CHUNK)])
    mix_steps = []
    for blk in range(blocks_per_tile):
        mix_steps.append(functools.partial(lru_step, blk))
        mix_steps.extend(functools.partial(attn_step, blk, g) for g in range(ATT_KV_HEADS))
    for ffn_step, mix_step in itertools.zip_longest(ffn_steps, mix_steps):
        if ffn_step is not None:
            ffn_step()
        if mix_step is not None:
            mix_step()
    kprev[...] = k_ref[last, :]
    vprev[...] = v_ref[last, :]


def _l0_mix_ffn(sinks, x2d, meta_blk, xr, gate, q, k, v, conv_w, conv_b, w_gates, b_gates, lam,
                w_out, gain, w_gu, w_down, cast_weights, tiles_per_seq, seq):
    rows = xr.shape[0]
    tm = ROW_TILE
    n_tiles = rows // tm
    mix = lambda w: pl.BlockSpec((tm, w), lambda s: (jnp.minimum(s, n_tiles - 1), 0))
    cast_in, cast_out, cast_shapes = _cast_specs(cast_weights)
    return pl.pallas_call(
        functools.partial(_l0_mix_ffn_kernel, tiles_per_seq=tiles_per_seq, n_tiles=n_tiles,
                          n_cast=len(cast_in)),
        grid=(n_tiles + 1,),
        in_specs=[pl.BlockSpec(memory_space=pltpu.SMEM),
                  _token_window_spec(tm, tiles_per_seq, seq, lambda s: jnp.maximum(s - 1, 0)),
                  _const_spec((BLOCK, D_MODEL)),
                  mix(LRU_WIDTH), mix(LRU_WIDTH), mix(Q_WIDTH), mix(KV_WIDTH), mix(KV_WIDTH),
                  _const_spec((CONV_W, LRU_WIDTH)), _const_spec((1, LRU_WIDTH)),
                  _const_spec((LRU_WIDTH, 2 * LRU_WIDTH)), _const_spec((1, 2 * LRU_WIDTH)),
                  _const_spec((1, LRU_WIDTH)), _const_spec(w_out.shape), _const_spec((1, D_MODEL)),
                  _const_spec(w_gu.shape), _const_spec(w_down.shape)] + cast_in,
        out_specs=[pl.BlockSpec((tm, D_MODEL), lambda s: (jnp.maximum(s - 1, 0), 0))] + cast_out,
        out_shape=[jax.ShapeDtypeStruct((rows, D_MODEL), F32)] + cast_shapes,
        scratch_shapes=[pltpu.VMEM((tm, LRU_WIDTH + Q_WIDTH), BF16),
                        pltpu.VMEM((SUBLANES + BLOCK, LRU_WIDTH), F32),
                        pltpu.VMEM((SUBLANES, LRU_WIDTH), F32),
                        pltpu.VMEM((BLOCK, KV_WIDTH), BF16), pltpu.VMEM((BLOCK, KV_WIDTH), BF16),
                        pltpu.VMEM((BLOCK, KV_WIDTH), BF16), pltpu.VMEM((BLOCK, KV_WIDTH), BF16)],
        compiler_params=pltpu.CompilerParams(dimension_semantics=("arbitrary",)),
        name="l0_mix_ffn",
    )(sinks, x2d, meta_blk, xr, gate, q, k, v, conv_w, conv_b, w_gates, b_gates, lam, w_out, gain,
      w_gu, w_down, *[w for _, w in cast_weights])


def _outproj_ffn_kernel(h_ref, y_ref, wo_ref, gain_ref, wgu_ref, wd_ref, o_ref):
    h1 = h_ref[...] + jnp.dot(y_ref[...], wo_ref[...], preferred_element_type=F32)
    o_ref[...] = _ffn(h1, gain_ref[...], wgu_ref, wd_ref)


def _final_outproj_ffn(h, y, w_out, gain, w_gu, w_down, batch, seq_rows, seq):
    tm = OUT_TILE
    tiles = seq // tm
    win = lambda w: pl.BlockSpec((pl.Element(tm), pl.Element(w)),
                                 lambda b, j: (pl.multiple_of(b * seq_rows + BLOCK + j * tm, BLOCK),
                                               0))
    return pl.pallas_call(
        _outproj_ffn_kernel,
        grid=(batch, tiles),
        in_specs=[win(D_MODEL), win(y.shape[1]), _const_spec(w_out.shape),
                  _const_spec((1, D_MODEL)), _const_spec(w_gu.shape), _const_spec(w_down.shape)],
        out_specs=pl.BlockSpec((tm, D_MODEL), lambda b, j: (b * tiles + j, 0)),
        out_shape=jax.ShapeDtypeStruct((batch * seq, D_MODEL), F32),
        compiler_params=pltpu.CompilerParams(dimension_semantics=("arbitrary", "arbitrary")),
        name="l1_outproj_ffn",
    )(h, y, w_out, gain, w_gu, w_down)


_Q0, _K0, _V0, _G0 = 0, D_MODEL, 2 * D_MODEL, 4 * D_MODEL


def _retention_head(hd, q, k, v, gate, state):
    ii = lax.broadcasted_iota(jnp.int32, (BLOCK, BLOCK), 0)
    jj = lax.broadcasted_iota(jnp.int32, (BLOCK, BLOCK), 1)
    diff = (ii - jj).astype(F32)
    idx = lax.broadcasted_iota(jnp.int32, (BLOCK, 1), 0).astype(F32)
    log_g = RET_LOG_G[hd]
    decay_intra = jnp.where(diff >= 0.0, jnp.exp(jnp.maximum(diff, 0.0) * log_g), 0.0)
    decay_q = jnp.exp((idx + 1.0) * log_g)
    decay_k = jnp.exp((BLOCK - 1.0 - idx) * log_g)
    decay_chunk = math.exp(BLOCK * log_g)
    qk = lax.dot_general(q, k, (((1,), (1,)), ((), ())),
                         preferred_element_type=F32) * decay_intra
    st = state[hd]
    o = (jnp.dot(qk.astype(BF16), v, preferred_element_type=F32)
         + jnp.dot(q, st.astype(BF16), preferred_element_type=F32) * decay_q)
    kd = (k.astype(F32) * decay_k).astype(BF16)
    state[hd] = decay_chunk * st + lax.dot_general(kd, v, (((0,), (0,)), ((), ())),
                                                   preferred_element_type=F32)
    return (o * _rms_scale(o) * gate.astype(F32)).astype(BF16)


def _l1_mix_kernel(h_ref, gain_ref, w_ref, freq_ref, y_ref, qkvg, state, ocos, osin,
                   *, tiles_per_seq, n_tiles):
    s = pl.program_id(0)
    proj_slot = s % 2
    ret_slot = 1 - proj_slot
    proj_tile_in_seq = jnp.minimum(s, n_tiles - 1) % tiles_per_seq
    ret_tile_in_seq = jnp.maximum(s - 1, 0) % tiles_per_seq
    tm = h_ref.shape[0]

    @pl.when(s == 0)
    def _():
        _rope_offsets(freq_ref, ocos, osin)
        qkvg[1] = jnp.zeros(qkvg.shape[1:], BF16)

    @pl.when(ret_tile_in_seq == 0)
    def _():
        state[...] = jnp.zeros_like(state)

    h = h_ref[...]
    xn = (h * _rms_scale(h) * gain_ref[...]).astype(BF16)
    cos, sin = _rope_tables(proj_tile_in_seq * tm - META_PAD, freq_ref, ocos, osin)
    half = RET_QK_DIM // 2

    def rope_step(col0, hd, scale):
        lo = col0 + hd * RET_QK_DIM
        y = jnp.dot(xn, w_ref[:, lo:lo + RET_QK_DIM], preferred_element_type=F32)
        x1 = y[:, :half]
        x2 = y[:, half:]
        qkvg[proj_slot, :, lo:lo + half] = ((x1 * cos - x2 * sin) * scale).astype(BF16)
        qkvg[proj_slot, :, lo + half:lo + RET_QK_DIM] = ((x2 * cos + x1 * sin)
                                                         * scale).astype(BF16)

    def value_step(col0, hd, act):
        lo = col0 + hd * RET_V_DIM
        y = jnp.dot(xn, w_ref[:, lo:lo + RET_V_DIM], preferred_element_type=F32)
        qkvg[proj_slot, :, lo:lo + RET_V_DIM] = act(y).astype(BF16)

    def retention_step(c, hd):
        rows = slice(c * BLOCK, (c + 1) * BLOCK)
        qcols = slice(hd * RET_QK_DIM, (hd + 1) * RET_QK_DIM)
        vcols = slice(hd * RET_V_DIM, (hd + 1) * RET_V_DIM)
        y_ref[rows, vcols] = _retention_head(
            hd,
            qkvg[ret_slot, rows, _Q0 + qcols.start:_Q0 + qcols.stop],
            qkvg[ret_slot, rows, _K0 + qcols.start:_K0 + qcols.stop],
            qkvg[ret_slot, rows, _V0 + vcols.start:_V0 + vcols.stop],
            qkvg[ret_slot, rows, _G0 + vcols.start:_G0 + vcols.stop], state)

    proj_steps = []
    for hd in range(RET_HEADS):
        proj_steps.append(functools.partial(rope_step, _Q0, hd, 1.0))
        proj_steps.append(functools.partial(rope_step, _K0, hd, RET_QK_DIM ** -0.5))
        proj_steps.append(functools.partial(value_step, _V0, hd, lambda y: y))
        proj_steps.append(functools.partial(value_step, _G0, hd, _silu))
    ret_steps = [functools.partial(retention_step, c, hd)
                 for c in range(tm // BLOCK) for hd in range(RET_HEADS)]
    for proj, ret in itertools.zip_longest(proj_steps, ret_steps):
        if proj is not None:
            proj()
        if ret is not None:
            ret()


def _l1_mix(h, gain, w_in, freq, tiles_per_seq):
    rows = h.shape[0]
    tm = ROW_TILE
    n_tiles = rows // tm
    half = RET_QK_DIM // 2
    return pl.pallas_call(
        functools.partial(_l1_mix_kernel, tiles_per_seq=tiles_per_seq, n_tiles=n_tiles),
        grid=(n_tiles + 1,),
        in_specs=[pl.BlockSpec((tm, D_MODEL), lambda s: (jnp.minimum(s, n_tiles - 1), 0)),
                  _const_spec((1, D_MODEL)), _const_spec(w_in.shape), _const_spec((1, half))],
        out_specs=pl.BlockSpec((tm, 2 * D_MODEL), lambda s: (jnp.maximum(s - 1, 0), 0)),
        out_shape=jax.ShapeDtypeStruct((rows, 2 * D_MODEL), BF16),
        scratch_shapes=[pltpu.VMEM((2, tm, 6 * D_MODEL), BF16),
                        pltpu.VMEM((RET_HEADS, RET_QK_DIM, RET_V_DIM), F32),
                        pltpu.VMEM((tm, half), F32), pltpu.VMEM((tm, half), F32)],
        compiler_params=pltpu.CompilerParams(dimension_semantics=("arbitrary",)),
        name="l1_mix",
    )(h, gain, w_in, freq)


def _inv_freq(half, theta):
    return jnp.power(jnp.asarray(theta, F32), -jnp.arange(half, dtype=F32) / half)


def _block_diag(w):
    heads, wi, wo = w.shape
    eye = jnp.eye(heads, dtype=w.dtype)
    return (eye[:, None, :, None] * w[:, :, None, :]).reshape(heads * wi, heads * wo)


def kernel(x, meta_tokens, mix_norm_ab, ab_w_in, lru_conv_w, lru_conv_b, lru_w_a, lru_b_a, lru_w_i, lru_b_i, lru_lambda, q_norm, k_norm, attn_sinks, ab_w_out, mix_norm_ret, ret_w_in, ret_w_out, ffn_norm, ffn_w_gu, ffn_w_down):
    batch, seq, _ = x.shape
    seq_rows = META_PAD + N_META + seq
    tiles_per_seq = seq_rows // ROW_TILE

    x2d = x.reshape(batch * seq, D_MODEL)
    meta_blk = jnp.concatenate([jnp.zeros((META_PAD, D_MODEL), x.dtype),
                                meta_tokens.astype(x.dtype)], axis=0)

    f_att = _inv_freq(ROT_DIM // 2, ROPE_THETA)
    f_att = jnp.concatenate([f_att, f_att, jnp.zeros((ATT_HEAD_DIM - ROT_DIM,), F32)])
    f_att = jnp.tile(f_att, LANES // ATT_HEAD_DIM).reshape(1, LANES)
    f_ret = _inv_freq(RET_QK_DIM // 2, RET_THETA).reshape(1, RET_QK_DIM // 2)

    row_vec = lambda v: v.reshape(1, -1).astype(F32)
    two_heads = lambda v: jnp.tile(v.reshape(1, -1).astype(F32), (1, 2))

    xr, gate, q, k, v, w_gu0, w_down0, w_in1, w_out1, w_gu1, w_down1 = _inproj0(
        x2d, meta_blk, row_vec(mix_norm_ab[0]), ab_w_in[0].astype(BF16), two_heads(q_norm[0]),
        two_heads(k_norm[0]), f_att,
        [(0, ffn_w_gu), (0, ffn_w_down), (0, ret_w_in), (0, ret_w_out), (1, ffn_w_gu),
         (1, ffn_w_down)], batch, tiles_per_seq, seq)
    w_gates = jnp.concatenate([_block_diag(lru_w_a[0]), _block_diag(lru_w_i[0])], axis=1)
    b_gates = jnp.concatenate([lru_b_a[0].reshape(1, -1), lru_b_i[0].reshape(1, -1)], axis=1)
    (h,) = _l0_mix_ffn(
        attn_sinks[0].astype(F32), x2d, meta_blk, xr, gate, q, k, v, lru_conv_w[0],
        row_vec(lru_conv_b[0]), w_gates.astype(BF16), b_gates.astype(F32), row_vec(lru_lambda[0]),
        ab_w_out[0].astype(BF16), row_vec(ffn_norm[0]), w_gu0, w_down0, [], tiles_per_seq, seq)

    y_ret = _l1_mix(h, row_vec(mix_norm_ret[0]), w_in1, f_ret, tiles_per_seq)
    out = _final_outproj_ffn(h, y_ret, w_out1, row_vec(ffn_norm[1]), w_gu1, w_down1,
                             batch, seq_rows, seq)
    return out.reshape(batch, seq, D_MODEL)
```

```python
import functools
import itertools
import math

import jax
import jax.numpy as jnp
from jax import lax
from jax.experimental import pallas as pl
from jax.experimental.pallas import tpu as pltpu

F32 = jnp.float32
BF16 = jnp.bfloat16

D_MODEL = 1024
N_META = 16
BLOCK = 128
META_PAD = BLOCK - N_META
RMS_EPS = 1e-6
NEG_INF = -1e30

LRU_WIDTH = 512
LRU_HEADS = 8
LRU_BLOCK_W = 64
CONV_W = 4
LRU_C = 8.0

ATT_HEADS = 8
ATT_KV_HEADS = 2
ATT_GROUP = ATT_HEADS // ATT_KV_HEADS
ATT_HEAD_DIM = 64
ROPE_THETA = 500000.0
ROT_DIM = 16
Q_WIDTH = 512
KV_WIDTH = 128
AB_IN_WIDTH = 2 * LRU_WIDTH + Q_WIDTH + 2 * KV_WIDTH

RET_HEADS = 4
RET_QK_DIM = 256
RET_V_DIM = 512
RET_THETA = 10000.0
RET_LOG_G = tuple(math.log1p(-(2.0 ** (-5.0 - h))) for h in range(RET_HEADS))

D_FF = 2816

LANES = 128
SUBLANES = 8
ROW_TILE = 640
OUT_TILE = 512
GATE_TILE = 256
INPROJ_CHUNK = 256
CAST_STEPS = 16
FFN_CHUNK = 256
DOWN_CHUNK = 256


def _rms_scale(x):
    return lax.rsqrt(jnp.mean(x * x, axis=-1, keepdims=True) + RMS_EPS)


def _silu(x):
    return x * jax.nn.sigmoid(x)


def _gelu_tanh(x):
    return 0.5 * x * (1.0 + jnp.tanh(0.7978845608028654 * (x + 0.044715 * (x * x * x))))


def _interleave(primary, secondary):
    out, done = [], 0
    for i, step in enumerate(primary):
        out.append(step)
        upto = ((i + 1) * len(secondary)) // len(primary)
        out.extend(secondary[done:upto])
        done = upto
    return out + list(secondary[done:])


def _const_spec(shape):
    zeros = (0,) * len(shape)
    return pl.BlockSpec(shape, lambda *_: zeros, pipeline_mode=pl.Buffered(1))


def _padded_rows(x_ref, meta_ref, first):
    xw = x_ref[...]
    tm = xw.shape[0]
    return jnp.concatenate([jnp.where(first, meta_ref[...], xw[0:BLOCK]),
                            jnp.where(first, xw[0:tm - BLOCK], xw[BLOCK:tm])], axis=0)


def _token_window_spec(tm, tiles_per_seq, seq, tile_of_step):
    def index(s):
        t = tile_of_step(s)
        start = jnp.maximum((t % tiles_per_seq) * tm - BLOCK, 0)
        return (pl.multiple_of((t // tiles_per_seq) * seq + start, BLOCK), 0)
    return pl.BlockSpec((pl.Element(tm), pl.Element(D_MODEL)), index)


def _rope_offsets(freq_ref, cos_ref, sin_ref):
    r = lax.broadcasted_iota(jnp.int32, cos_ref.shape, 0).astype(F32)
    ang = r * freq_ref[...]
    cos_ref[...] = jnp.cos(ang)
    sin_ref[...] = jnp.sin(ang)


def _rope_tables(base_pos, freq_ref, cos_ref, sin_ref):
    ang = base_pos.astype(F32) * freq_ref[...]
    cb = jnp.cos(ang)
    sb = jnp.sin(ang)
    oc = cos_ref[...]
    os_ = sin_ref[...]
    return cb * oc - sb * os_, sb * oc + cb * os_


def _cast_blocks(cast_refs):
    n = len(cast_refs) // 2
    for src, dst in zip(cast_refs[:n], cast_refs[n:]):
        dst[...] = src[...].astype(BF16)


def _cast_specs(weights):
    ins, outs, shapes = [], [], []
    step = lambda s: jnp.minimum(s, CAST_STEPS - 1)
    for layer, w in weights:
        _, rows, cols = w.shape
        blk = rows // CAST_STEPS
        ins.append(pl.BlockSpec((None, blk, cols), lambda s, layer=layer: (layer, step(s), 0)))
        outs.append(pl.BlockSpec((blk, cols), lambda s: (step(s), 0)))
        shapes.append(jax.ShapeDtypeStruct((rows, cols), BF16))
    return ins, outs, shapes


def _inproj0_kernel(*refs, tiles_per_seq, n_cast):
    (x_ref, meta_ref, gain_ref, w_ref, qg_ref, kg_ref, freq_ref) = refs[:7]
    cast_in = refs[7:7 + n_cast]
    xr_ref, gate_ref, q_ref, k_ref, v_ref = refs[7 + n_cast:12 + n_cast]
    cast_out = refs[12 + n_cast:12 + 2 * n_cast]
    ocos, osin = refs[12 + 2 * n_cast:]
    step = pl.program_id(0)
    tile_in_seq = step % tiles_per_seq

    @pl.when(step == 0)
    def _():
        _rope_offsets(freq_ref, ocos, osin)

    _cast_blocks(cast_in + cast_out)

    h = _padded_rows(x_ref, meta_ref, tile_in_seq == 0)
    rows = h.shape[0]
    xn = (h * _rms_scale(h) * gain_ref[...]).astype(BF16)

    q0 = 2 * LRU_WIDTH
    k0 = q0 + Q_WIDTH
    y = jnp.dot(xn, w_ref[:, q0:], preferred_element_type=F32)
    v_ref[...] = y[:, k0 - q0 + KV_WIDTH:].astype(BF16)

    cos, sin = _rope_tables(tile_in_seq * rows - META_PAD, freq_ref, ocos, osin)
    lane = lax.broadcasted_iota(jnp.int32, (1, LANES), 1) & (ATT_HEAD_DIM - 1)
    half = ROT_DIM // 2
    sin_lo = sin * jnp.where(lane < half, -1.0, 0.0)
    sin_hi = sin * jnp.where((lane >= half) & (lane < ROT_DIM), 1.0, 0.0)
    hi = lax.broadcasted_iota(jnp.int32, (2 * LANES, LANES), 0) & (LANES - 1)
    hj = lax.broadcasted_iota(jnp.int32, (2 * LANES, LANES), 1)
    head_mean = jnp.where(hi // ATT_HEAD_DIM == hj // ATT_HEAD_DIM,
                          1.0 / ATT_HEAD_DIM, 0.0).astype(BF16)

    def norm_rope(x, gain, scale):
        sq = x * x
        sq_hi = sq.astype(BF16)
        sq_lo = (sq - sq_hi.astype(F32)).astype(BF16)
        ms = jnp.dot(jnp.concatenate([sq_hi, sq_lo], axis=1), head_mean,
                     preferred_element_type=F32)
        xg = x * gain
        rot = (xg * cos + pltpu.roll(xg, LANES - half, 1) * sin_lo
               + pltpu.roll(xg, half, 1) * sin_hi)
        return rot * (lax.rsqrt(ms + RMS_EPS) * scale)

    def qk_step(j):
        if j < Q_WIDTH // LANES:
            q_ref[:, j * LANES:(j + 1) * LANES] = norm_rope(
                y[:, j * LANES:(j + 1) * LANES], qg_ref[...], ATT_HEAD_DIM ** -0.5).astype(BF16)
        else:
            k_ref[...] = norm_rope(y[:, k0 - q0:k0 - q0 + KV_WIDTH], kg_ref[...],
                                   1.0).astype(BF16)

    def proj_step(c):
        cols = slice(c * INPROJ_CHUNK, (c + 1) * INPROJ_CHUNK)
        out_ref = xr_ref if c < LRU_WIDTH // INPROJ_CHUNK else gate_ref
        dst = slice((c * INPROJ_CHUNK) % LRU_WIDTH, (c * INPROJ_CHUNK) % LRU_WIDTH + INPROJ_CHUNK)
        out_ref[:, dst] = jnp.dot(xn, w_ref[:, cols], preferred_element_type=F32)

    proj_steps = [functools.partial(proj_step, c) for c in range(2 * LRU_WIDTH // INPROJ_CHUNK)]
    qk_steps = [functools.partial(qk_step, j) for j in range(Q_WIDTH // LANES + 1)]
    for proj, qk in itertools.zip_longest(proj_steps, qk_steps):
        if proj is not None:
            proj()
        if qk is not None:
            qk()


def _inproj0(x2d, meta_blk, gain, w_in, q_gain, k_gain, freq, cast_weights, batch,
             tiles_per_seq, seq):
    tm = ROW_TILE
    rows = batch * tiles_per_seq * tm
    row = lambda w: pl.BlockSpec((tm, w), lambda i: (i, 0))
    cast_in, cast_out, cast_shapes = _cast_specs(cast_weights)
    return pl.pallas_call(
        functools.partial(_inproj0_kernel, tiles_per_seq=tiles_per_seq, n_cast=len(cast_in)),
        grid=(rows // tm,),
        in_specs=[_token_window_spec(tm, tiles_per_seq, seq, lambda s: s),
                  _const_spec((BLOCK, D_MODEL)), _const_spec((1, D_MODEL)),
                  _const_spec((D_MODEL, AB_IN_WIDTH)), _const_spec((1, LANES)),
                  _const_spec((1, LANES)), _const_spec((1, LANES))] + cast_in,
        out_specs=[row(LRU_WIDTH), row(LRU_WIDTH), row(Q_WIDTH), row(KV_WIDTH), row(KV_WIDTH)]
        + cast_out,
        out_shape=[jax.ShapeDtypeStruct((rows, LRU_WIDTH), F32),
                   jax.ShapeDtypeStruct((rows, LRU_WIDTH), F32),
                   jax.ShapeDtypeStruct((rows, Q_WIDTH), BF16),
                   jax.ShapeDtypeStruct((rows, KV_WIDTH), BF16),
                   jax.ShapeDtypeStruct((rows, KV_WIDTH), BF16)] + cast_shapes,
        scratch_shapes=[pltpu.VMEM((tm, LANES), F32), pltpu.VMEM((tm, LANES), F32)],
        compiler_params=pltpu.CompilerParams(dimension_semantics=("arbitrary",)),
        name="l0_inproj",
    )(x2d, meta_blk, gain, w_in, q_gain, k_gain, freq, *[w for _, w in cast_weights])


def _attn_group_t(n, sinks, q, kc, kp, km, vc, vp, vm):
    j = lax.broadcasted_iota(jnp.int32, (BLOCK, BLOCK), 0)
    i = lax.broadcasted_iota(jnp.int32, (BLOCK, BLOCK), 1)
    causal = j <= i
    win_ok = n >= jnp.where(causal, 1, 2)
    meta_ok = (j >= META_PAD) & (n >= jnp.where(causal, 0, 1))
    contract_last = (((1,), (1,)), ((), ()))
    contract_rows = (((0,), (0,)), ((), ()))

    qs = jnp.concatenate([q[:, a * ATT_HEAD_DIM:(a + 1) * ATT_HEAD_DIM]
                          for a in range(ATT_GROUP)], axis=0)
    s_c = lax.dot_general(kc, qs, contract_last, preferred_element_type=F32)
    s_p = lax.dot_general(kp, qs, contract_last, preferred_element_type=F32)
    s_m = lax.dot_general(km, qs, contract_last, preferred_element_type=F32)
    p_c, p_p, p_m, inv_den = [], [], [], []
    for a in range(ATT_GROUP):
        head = slice(a * BLOCK, (a + 1) * BLOCK)
        sw = jnp.where(win_ok, jnp.where(causal, s_c[:, head], s_p[:, head]), NEG_INF)
        sm = jnp.where(meta_ok, s_m[:, head], NEG_INF)
        m = jnp.maximum(jnp.maximum(jnp.max(sw, axis=0, keepdims=True),
                                    jnp.max(sm, axis=0, keepdims=True)), sinks[a])
        pw = jnp.exp(sw - m)
        pm = jnp.exp(sm - m)
        den = (jnp.sum(pw, axis=0, keepdims=True) + jnp.sum(pm, axis=0, keepdims=True)
               + jnp.exp(sinks[a] - m))
        inv_den.append(1.0 / den)
        p_c.append(jnp.where(causal, pw, 0.0).astype(BF16))
        p_p.append(jnp.where(causal, 0.0, pw).astype(BF16))
        p_m.append(pm.astype(BF16))
    lanes = lambda parts: jnp.concatenate(parts, axis=1)
    o_t = (lax.dot_general(vc, lanes(p_c), contract_rows, preferred_element_type=F32)
           + lax.dot_general(vp, lanes(p_p), contract_rows, preferred_element_type=F32)
           + lax.dot_general(vm, lanes(p_m), contract_rows, preferred_element_type=F32))
    return o_t * lanes(inv_den)


def _attn_block(n, sink_ref, q, kc, kp, km, vc, vp, vm):
    width = ATT_GROUP * ATT_HEAD_DIM
    o_t = []
    for g in range(ATT_KV_HEADS):
        lanes = slice(g * ATT_HEAD_DIM, (g + 1) * ATT_HEAD_DIM)
        sinks = [sink_ref[g * ATT_GROUP + a] for a in range(ATT_GROUP)]
        o_t.append(_attn_group_t(n, sinks, q[:, g * width:(g + 1) * width], kc[:, lanes],
                                 kp[:, lanes], km[:, lanes], vc[:, lanes], vp[:, lanes],
                                 vm[:, lanes]))
    o_t = jnp.concatenate(o_t, axis=0)
    return jnp.concatenate([o_t[:, a * BLOCK:(a + 1) * BLOCK].T for a in range(ATT_GROUP)],
                           axis=1).astype(BF16)


def _lru_block(n, xr_ref, gate_ref, rows, cw, cb, wg_ref, bg, lam, xbuf, hcar, y_ref):
    tl = BLOCK
    row = lax.broadcasted_iota(jnp.int32, (tl, 1), 0)
    t = n * tl + row
    z = -lam
    softplus = jnp.maximum(z, 0.0) + jnp.log1p(jnp.exp(-jnp.abs(z)))
    for lo in range(0, LRU_WIDTH, GATE_TILE):
        cols = slice(lo, lo + GATE_TILE)
        x = xr_ref[rows, cols]
        xbuf[SUBLANES:SUBLANES + tl, cols] = x
        xc = x * cw[CONV_W - 1:CONV_W, cols] + cb[:, cols]
        for d in range(1, CONV_W):
            xc = xc + (xbuf[SUBLANES - d:SUBLANES - d + tl, cols]
                       * cw[CONV_W - 1 - d:CONV_W - d, cols])
        xbuf[0:SUBLANES, cols] = x[tl - SUBLANES:tl]
        xcb = xc.astype(BF16)
        ga_r = jnp.dot(xcb, wg_ref[cols, cols], preferred_element_type=F32)
        ga_i = jnp.dot(xcb, wg_ref[cols, LRU_WIDTH + lo:LRU_WIDTH + lo + GATE_TILE],
                       preferred_element_type=F32)
        for sub in range(0, GATE_TILE, LANES):
            grp = slice(lo + sub, lo + sub + LANES)
            r = jax.nn.sigmoid(ga_r[:, sub:sub + LANES] + bg[:, grp])
            gi = jax.nn.sigmoid(ga_i[:, sub:sub + LANES]
                                + bg[:, LRU_WIDTH + lo + sub:LRU_WIDTH + lo + sub + LANES])
            log_a = (-LRU_C * softplus[:, grp]) * r
            a = jnp.exp(log_a)
            mult = jnp.sqrt(jnp.tanh(-log_a) * (a * a + 1.0))
            mult = jnp.where(t == META_PAD, 1.0, mult)
            b = jnp.where(t < META_PAD, 0.0, mult * gi * xc[:, sub:sub + LANES])

            d = 1
            while d < SUBLANES:
                keep = row >= d
                b = jnp.where(keep, a * pltpu.roll(b, d, 0), 0.0) + b
                a = jnp.where(keep, a * pltpu.roll(a, d, 0), a)
                d *= 2
            while d < tl:
                b = jnp.concatenate([b[:d], a[d:] * b[:tl - d] + b[d:]], axis=0)
                a = jnp.concatenate([a[:d], a[d:] * a[:tl - d]], axis=0)
                d *= 2
            h = b + a * hcar[0:1, grp]
            hcar[:, grp] = jnp.broadcast_to(h[tl - 1:tl], (SUBLANES, LANES))
            y_ref[rows, grp] = (_gelu_tanh(gate_ref[rows, grp]) * h).astype(BF16)


def _ffn(h1, gain, wgu_ref, wd_ref):
    xn = (h1 * _rms_scale(h1) * gain).astype(BF16)
    acts = []
    for c in range(D_FF // FFN_CHUNK):
        lo = c * FFN_CHUNK
        g = jnp.dot(xn, wgu_ref[:, lo:lo + FFN_CHUNK], preferred_element_type=F32)
        u = jnp.dot(xn, wgu_ref[:, D_FF + lo:D_FF + lo + FFN_CHUNK], preferred_element_type=F32)
        acts.append((_silu(g) * u).astype(BF16))
    act = jnp.concatenate(acts, axis=1)
    return h1 + jnp.dot(act, wd_ref[...], preferred_element_type=F32)


def _l0_mix_ffn_kernel(*refs, tiles_per_seq, n_tiles, n_cast):
    (sink_ref, x_ref, meta_ref, xr_ref, gate_ref, q_ref, k_ref, v_ref, cw_ref, cb_ref, wg_ref,
     bg_ref, lam_ref, wo_ref, gain_ref, wgu_ref, wd_ref) = refs[:17]
    cast_in = refs[17:17 + n_cast]
    o_ref = refs[17 + n_cast]
    cast_out = refs[18 + n_cast:18 + 2 * n_cast]
    ybuf, xbuf, hcar, kprev, vprev, kmeta, vmeta = refs[18 + 2 * n_cast:]
    s = pl.program_id(0)
    tile_in_seq = jnp.minimum(s, n_tiles - 1) % tiles_per_seq
    blocks_per_tile = x_ref.shape[0] // BLOCK
    last = slice((blocks_per_tile - 1) * BLOCK, blocks_per_tile * BLOCK)
    _cast_blocks(cast_in + cast_out)

    @pl.when(s == 0)
    def _():
        ybuf[...] = jnp.zeros_like(ybuf)

    @pl.when(tile_in_seq == 0)
    def _():
        xbuf[0:SUBLANES, :] = jnp.zeros((SUBLANES, LRU_WIDTH), F32)
        hcar[...] = jnp.zeros_like(hcar)
        kmeta[...] = k_ref[0:BLOCK, :]
        vmeta[...] = v_ref[0:BLOCK, :]
        kprev[...] = k_ref[0:BLOCK, :]
        vprev[...] = v_ref[0:BLOCK, :]

    ffn_first = jnp.maximum(s - 1, 0) % tiles_per_seq == 0
    h1 = (_padded_rows(x_ref, meta_ref, ffn_first)
          + jnp.dot(ybuf[...], wo_ref[...], preferred_element_type=F32))
    xn = (h1 * _rms_scale(h1) * gain_ref[...]).astype(BF16)

    cw = cw_ref[...]
    cb = cb_ref[...]
    bg = bg_ref[...]
    lam = lam_ref[...]

    def lru_step(blk):
        n = tile_in_seq * blocks_per_tile + blk
        rows = slice(blk * BLOCK, (blk + 1) * BLOCK)
        _lru_block(n, xr_ref, gate_ref, rows, cw, cb, wg_ref, bg, lam, xbuf, hcar, ybuf)

    def attn_step(blk):
        n = tile_in_seq * blocks_per_tile + blk
        rows = slice(blk * BLOCK, (blk + 1) * BLOCK)
        prev_rows = slice((blk - 1) * BLOCK, blk * BLOCK)
        kp = kprev[...] if blk == 0 else k_ref[prev_rows, :]
        vp = vprev[...] if blk == 0 else v_ref[prev_rows, :]
        ybuf[rows, LRU_WIDTH:LRU_WIDTH + Q_WIDTH] = _attn_block(
            n, sink_ref, q_ref[rows, :], k_ref[rows, :], kp, kmeta[...], v_ref[rows, :], vp,
            vmeta[...])

    acts = []

    def gate_up_step(c):
        lo = c * FFN_CHUNK
        g = jnp.dot(xn, wgu_ref[:, lo:lo + FFN_CHUNK], preferred_element_type=F32)
        u = jnp.dot(xn, wgu_ref[:, D_FF + lo:D_FF + lo + FFN_CHUNK], preferred_element_type=F32)
        acts.append((_silu(g) * u).astype(BF16))

    def down_step(c):
        cols = slice(c * DOWN_CHUNK, (c + 1) * DOWN_CHUNK)
        if len(acts) > 1:
            acts[:] = [jnp.concatenate(acts, axis=1)]
        o_ref[:, cols] = h1[:, cols] + jnp.dot(acts[0], wd_ref[:, cols],
                                               preferred_element_type=F32)

    ffn_steps = ([functools.partial(gate_up_step, c) for c in range(D_FF // FFN_CHUNK)]
                 + [functools.partial(down_step, c) for c in range(D_MODEL // DOWN_CHUNK)])
    mix_steps = []
    for blk in range(blocks_per_tile):
        mix_steps.append(functools.partial(lru_step, blk))
        mix_steps.append(functools.partial(attn_step, blk))
    for step in _interleave(ffn_steps, mix_steps):
        step()
    kprev[...] = k_ref[last, :]
    vprev[...] = v_ref[last, :]


def _l0_mix_ffn(sinks, x2d, meta_blk, xr, gate, q, k, v, conv_w, conv_b, w_gates, b_gates, lam,
                w_out, gain, w_gu, w_down, cast_weights, tiles_per_seq, seq):
    rows = xr.shape[0]
    tm = ROW_TILE
    n_tiles = rows // tm
    mix = lambda w: pl.BlockSpec((tm, w), lambda s: (jnp.minimum(s, n_tiles - 1), 0))
    cast_in, cast_out, cast_shapes = _cast_specs(cast_weights)
    return pl.pallas_call(
        functools.partial(_l0_mix_ffn_kernel, tiles_per_seq=tiles_per_seq, n_tiles=n_tiles,
                          n_cast=len(cast_in)),
        grid=(n_tiles + 1,),
        in_specs=[pl.BlockSpec(memory_space=pltpu.SMEM),
                  _token_window_spec(tm, tiles_per_seq, seq, lambda s: jnp.maximum(s - 1, 0)),
                  _const_spec((BLOCK, D_MODEL)),
                  mix(LRU_WIDTH), mix(LRU_WIDTH), mix(Q_WIDTH), mix(KV_WIDTH), mix(KV_WIDTH),
                  _const_spec((CONV_W, LRU_WIDTH)), _const_spec((1, LRU_WIDTH)),
                  _const_spec((LRU_WIDTH, 2 * LRU_WIDTH)), _const_spec((1, 2 * LRU_WIDTH)),
                  _const_spec((1, LRU_WIDTH)), _const_spec(w_out.shape), _const_spec((1, D_MODEL)),
                  _const_spec(w_gu.shape), _const_spec(w_down.shape)] + cast_in,
        out_specs=[pl.BlockSpec((tm, D_MODEL), lambda s: (jnp.maximum(s - 1, 0), 0))] + cast_out,
        out_shape=[jax.ShapeDtypeStruct((rows, D_MODEL), F32)] + cast_shapes,
        scratch_shapes=[pltpu.VMEM((tm, LRU_WIDTH + Q_WIDTH), BF16),
                        pltpu.VMEM((SUBLANES + BLOCK, LRU_WIDTH), F32),
                        pltpu.VMEM((SUBLANES, LRU_WIDTH), F32),
                        pltpu.VMEM((BLOCK, KV_WIDTH), BF16), pltpu.VMEM((BLOCK, KV_WIDTH), BF16),
                        pltpu.VMEM((BLOCK, KV_WIDTH), BF16), pltpu.VMEM((BLOCK, KV_WIDTH), BF16)],
        compiler_params=pltpu.CompilerParams(dimension_semantics=("arbitrary",)),
        name="l0_mix_ffn",
    )(sinks, x2d, meta_blk, xr, gate, q, k, v, conv_w, conv_b, w_gates, b_gates, lam, w_out, gain,
      w_gu, w_down, *[w for _, w in cast_weights])


def _outproj_ffn_kernel(h_ref, y_ref, wo_ref, gain_ref, wgu_ref, wd_ref, o_ref):
    h1 = h_ref[...] + jnp.dot(y_ref[...], wo_ref[...], preferred_element_type=F32)
    o_ref[...] = _ffn(h1, gain_ref[...], wgu_ref, wd_ref)


def _final_outproj_ffn(h, y, w_out, gain, w_gu, w_down, batch, seq_rows, seq):
    tm = OUT_TILE
    tiles = seq // tm
    win = lambda w: pl.BlockSpec((pl.Element(tm), pl.Element(w)),
                                 lambda b, j: (pl.multiple_of(b * seq_rows + BLOCK + j * tm, BLOCK),
                                               0))
    return pl.pallas_call(
        _outproj_ffn_kernel,
        grid=(batch, tiles),
        in_specs=[win(D_MODEL), win(y.shape[1]), _const_spec(w_out.shape),
                  _const_spec((1, D_MODEL)), _const_spec(w_gu.shape), _const_spec(w_down.shape)],
        out_specs=pl.BlockSpec((tm, D_MODEL), lambda b, j: (b * tiles + j, 0)),
        out_shape=jax.ShapeDtypeStruct((batch * seq, D_MODEL), F32),
        compiler_params=pltpu.CompilerParams(dimension_semantics=("arbitrary", "arbitrary")),
        name="l1_outproj_ffn",
    )(h, y, w_out, gain, w_gu, w_down)


_Q0, _K0, _V0, _G0 = 0, D_MODEL, 2 * D_MODEL, 4 * D_MODEL


def _retention_decays(hd):
    ii = lax.broadcasted_iota(jnp.int32, (BLOCK, BLOCK), 0)
    jj = lax.broadcasted_iota(jnp.int32, (BLOCK, BLOCK), 1)
    diff = (ii - jj).astype(F32)
    idx = lax.broadcasted_iota(jnp.int32, (BLOCK, 1), 0).astype(F32)
    log_g = RET_LOG_G[hd]
    decay_intra = jnp.where(diff >= 0.0, jnp.exp(jnp.maximum(diff, 0.0) * log_g), 0.0)
    return (decay_intra, jnp.exp((idx + 1.0) * log_g), jnp.exp((BLOCK - 1.0 - idx) * log_g),
            math.exp(BLOCK * log_g))


def _l1_mix_kernel(h_ref, gain_ref, w_ref, freq_ref, y_ref, qkvg, state, ocos, osin,
                   *, tiles_per_seq, n_tiles):
    s = pl.program_id(0)
    proj_slot = s % 2
    ret_slot = 1 - proj_slot
    proj_tile_in_seq = jnp.minimum(s, n_tiles - 1) % tiles_per_seq
    ret_tile_in_seq = jnp.maximum(s - 1, 0) % tiles_per_seq
    tm = h_ref.shape[0]

    @pl.when(s == 0)
    def _():
        _rope_offsets(freq_ref, ocos, osin)
        qkvg[1] = jnp.zeros(qkvg.shape[1:], BF16)

    @pl.when(ret_tile_in_seq == 0)
    def _():
        state[...] = jnp.zeros_like(state)

    h = h_ref[...]
    xn = (h * _rms_scale(h) * gain_ref[...]).astype(BF16)
    cos, sin = _rope_tables(proj_tile_in_seq * tm - META_PAD, freq_ref, ocos, osin)
    half = RET_QK_DIM // 2

    def rope_step(col0, hd, scale):
        lo = col0 + hd * RET_QK_DIM
        y = jnp.dot(xn, w_ref[:, lo:lo + RET_QK_DIM], preferred_element_type=F32)
        x1 = y[:, :half]
        x2 = y[:, half:]
        qkvg[proj_slot, :, lo:lo + half] = ((x1 * cos - x2 * sin) * scale).astype(BF16)
        qkvg[proj_slot, :, lo + half:lo + RET_QK_DIM] = ((x2 * cos + x1 * sin)
                                                         * scale).astype(BF16)

    def value_step(col0, hd, act):
        lo = col0 + hd * RET_V_DIM
        y = jnp.dot(xn, w_ref[:, lo:lo + RET_V_DIM], preferred_element_type=F32)
        qkvg[proj_slot, :, lo:lo + RET_V_DIM] = act(y).astype(BF16)

    def ret_operand(c, hd, col0, width):
        return qkvg[ret_slot, c * BLOCK:(c + 1) * BLOCK, col0 + hd * width:col0 + (hd + 1) * width]

    heads = range(RET_HEADS)
    decays = [_retention_decays(hd) for hd in heads]
    live = {}

    def prep_step(c):
        live["kdt", c] = [
            (ret_operand(c, hd, _K0, RET_QK_DIM).astype(F32) * decays[hd][2]).T.astype(BF16)
            for hd in heads]

    def scores_step(c):
        live["qk", c] = [lax.dot_general(ret_operand(c, hd, _Q0, RET_QK_DIM),
                                         ret_operand(c, hd, _K0, RET_QK_DIM),
                                         (((1,), (1,)), ((), ())), preferred_element_type=F32)
                         for hd in heads]
        kdt = live.pop(("kdt", c))
        live["kv", c] = [jnp.dot(kdt[hd], ret_operand(c, hd, _V0, RET_V_DIM),
                                 preferred_element_type=F32) for hd in heads]

    def decay_step(c):
        qk = live.pop(("qk", c))
        live["qkd", c] = [(qk[hd] * decays[hd][0]).astype(BF16) for hd in heads]
        live["stb", c] = [state[hd].astype(BF16) for hd in heads]

    def output_step(c):
        qkd = live.pop(("qkd", c))
        stb = live.pop(("stb", c))
        live["o", c] = [
            jnp.dot(qkd[hd], ret_operand(c, hd, _V0, RET_V_DIM), preferred_element_type=F32)
            + jnp.dot(ret_operand(c, hd, _Q0, RET_QK_DIM), stb[hd],
                      preferred_element_type=F32) * decays[hd][1]
            for hd in heads]

    def post_step(c):
        kv = live.pop(("kv", c))
        for hd, o in enumerate(live.pop(("o", c))):
            state[hd] = decays[hd][3] * state[hd] + kv[hd]
            gate = ret_operand(c, hd, _G0, RET_V_DIM).astype(F32)
            y_ref[c * BLOCK:(c + 1) * BLOCK, hd * RET_V_DIM:(hd + 1) * RET_V_DIM] = (
                o * _rms_scale(o) * gate).astype(BF16)

    proj_steps = []
    for hd in heads:
        proj_steps.append([functools.partial(rope_step, _Q0, hd, 1.0),
                           functools.partial(rope_step, _K0, hd, RET_QK_DIM ** -0.5)])
        proj_steps.append([functools.partial(value_step, _V0, hd, lambda y: y)])
        proj_steps.append([functools.partial(value_step, _G0, hd, _silu)])
    n_chunks = tm // BLOCK
    ret_steps = []
    for c in range(n_chunks):
        ret_steps.append([functools.partial(scores_step, c), functools.partial(decay_step, c)]
                         + ([functools.partial(prep_step, c + 1)] if c + 1 < n_chunks else []))
        ret_steps.append([functools.partial(output_step, c), functools.partial(post_step, c)])
    prep_step(0)
    for proj, ret in itertools.zip_longest(proj_steps, ret_steps, fillvalue=()):
        for step in (*proj, *ret):
            step()


def _l1_mix(h, gain, w_in, freq, tiles_per_seq):
    rows = h.shape[0]
    tm = ROW_TILE
    n_tiles = rows // tm
    half = RET_QK_DIM // 2
    return pl.pallas_call(
        functools.partial(_l1_mix_kernel, tiles_per_seq=tiles_per_seq, n_tiles=n_tiles),
        grid=(n_tiles + 1,),
        in_specs=[pl.BlockSpec((tm, D_MODEL), lambda s: (jnp.minimum(s, n_tiles - 1), 0)),
                  _const_spec((1, D_MODEL)), _const_spec(w_in.shape), _const_spec((1, half))],
        out_specs=pl.BlockSpec((tm, 2 * D_MODEL), lambda s: (jnp.maximum(s - 1, 0), 0)),
        out_shape=jax.ShapeDtypeStruct((rows, 2 * D_MODEL), BF16),
        scratch_shapes=[pltpu.VMEM((2, tm, 6 * D_MODEL), BF16),
                        pltpu.VMEM((RET_HEADS, RET_QK_DIM, RET_V_DIM), F32),
                        pltpu.VMEM((tm, half), F32), pltpu.VMEM((tm, half), F32)],
        compiler_params=pltpu.CompilerParams(dimension_semantics=("arbitrary",)),
        name="l1_mix",
    )(h, gain, w_in, freq)


def _inv_freq(half, theta):
    return jnp.power(jnp.asarray(theta, F32), -jnp.arange(half, dtype=F32) / half)


def _block_diag(w):
    heads, wi, wo = w.shape
    eye = jnp.eye(heads, dtype=w.dtype)
    return (eye[:, None, :, None] * w[:, :, None, :]).reshape(heads * wi, heads * wo)


def kernel(x, meta_tokens, mix_norm_ab, ab_w_in, lru_conv_w, lru_conv_b, lru_w_a, lru_b_a, lru_w_i, lru_b_i, lru_lambda, q_norm, k_norm, attn_sinks, ab_w_out, mix_norm_ret, ret_w_in, ret_w_out, ffn_norm, ffn_w_gu, ffn_w_down):
    batch, seq, _ = x.shape
    seq_rows = META_PAD + N_META + seq
    tiles_per_seq = seq_rows // ROW_TILE

    x2d = x.reshape(batch * seq, D_MODEL)
    meta_blk = jnp.concatenate([jnp.zeros((META_PAD, D_MODEL), x.dtype),
                                meta_tokens.astype(x.dtype)], axis=0)

    f_att = _inv_freq(ROT_DIM // 2, ROPE_THETA)
    f_att = jnp.concatenate([f_att, f_att, jnp.zeros((ATT_HEAD_DIM - ROT_DIM,), F32)])
    f_att = jnp.tile(f_att, LANES // ATT_HEAD_DIM).reshape(1, LANES)
    f_ret = _inv_freq(RET_QK_DIM // 2, RET_THETA).reshape(1, RET_QK_DIM // 2)

    row_vec = lambda v: v.reshape(1, -1).astype(F32)
    two_heads = lambda v: jnp.tile(v.reshape(1, -1).astype(F32), (1, 2))

    xr, gate, q, k, v, w_gu0, w_down0, w_in1, w_out1, w_gu1, w_down1 = _inproj0(
        x2d, meta_blk, row_vec(mix_norm_ab[0]), ab_w_in[0].astype(BF16), two_heads(q_norm[0]),
        two_heads(k_norm[0]), f_att,
        [(0, ffn_w_gu), (0, ffn_w_down), (0, ret_w_in), (0, ret_w_out), (1, ffn_w_gu),
         (1, ffn_w_down)], batch, tiles_per_seq, seq)
    w_gates = jnp.concatenate([_block_diag(lru_w_a[0]), _block_diag(lru_w_i[0])], axis=1)
    b_gates = jnp.concatenate([lru_b_a[0].reshape(1, -1), lru_b_i[0].reshape(1, -1)], axis=1)
    w_att = ab_w_out[0][LRU_WIDTH:].reshape(ATT_KV_HEADS, ATT_GROUP, ATT_HEAD_DIM, D_MODEL)
    w_att = w_att.transpose(1, 0, 2, 3).reshape(Q_WIDTH, D_MODEL)
    w_out0 = jnp.concatenate([ab_w_out[0][:LRU_WIDTH], w_att], axis=0).astype(BF16)
    (h,) = _l0_mix_ffn(
        attn_sinks[0].astype(F32), x2d, meta_blk, xr, gate, q, k, v, lru_conv_w[0],
        row_vec(lru_conv_b[0]), w_gates.astype(BF16), b_gates.astype(F32), row_vec(lru_lambda[0]),
        w_out0, row_vec(ffn_norm[0]), w_gu0, w_down0, [], tiles_per_seq, seq)

    y_ret = _l1_mix(h, row_vec(mix_norm_ret[0]), w_in1, f_ret, tiles_per_seq)
    out = _final_outproj_ffn(h, y_ret, w_out1, row_vec(ffn_norm[1]), w_gu1, w_down1,
                             batch, seq_rows, seq)
    return out.reshape(batch, seq, D_MODEL)
```

```python
import functools
import itertools
import math

import jax
import jax.numpy as jnp
from jax import lax
from jax.experimental import pallas as pl
from jax.experimental.pallas import tpu as pltpu

F32 = jnp.float32
BF16 = jnp.bfloat16

D_MODEL = 1024
N_META = 16
BLOCK = 128
META_PAD = BLOCK - N_META
RMS_EPS = 1e-6
NEG_INF = -1e30

LRU_WIDTH = 512
LRU_HEADS = 8
LRU_BLOCK_W = 64
CONV_W = 4
LRU_C = 8.0

ATT_HEADS = 8
ATT_KV_HEADS = 2
ATT_GROUP = ATT_HEADS // ATT_KV_HEADS
ATT_HEAD_DIM = 64
ROPE_THETA = 500000.0
ROT_DIM = 16
Q_WIDTH = 512
KV_WIDTH = 128
AB_IN_WIDTH = 2 * LRU_WIDTH + Q_WIDTH + 2 * KV_WIDTH

RET_HEADS = 4
RET_QK_DIM = 256
RET_V_DIM = 512
RET_THETA = 10000.0
RET_LOG_G = tuple(math.log1p(-(2.0 ** (-5.0 - h))) for h in range(RET_HEADS))

D_FF = 2816

LANES = 128
SUBLANES = 8
ROW_TILE = 640
OUT_TILE = 512
SQRT_GUARD = 1e-30
GATE_TILE = 256
INPROJ_CHUNK = 256
CAST_STEPS = 16
FFN_CHUNK = 256
DOWN_CHUNK = 256


def _rms_scale(x):
    return lax.rsqrt(jnp.mean(x * x, axis=-1, keepdims=True) + RMS_EPS)


def _sigmoid(x):
    return 0.5 * jnp.tanh(0.5 * x) + 0.5


def _silu(x):
    half = 0.5 * x
    return half + half * jnp.tanh(half)


def _gelu_tanh(x):
    half = 0.5 * x
    return half + half * jnp.tanh(0.7978845608028654 * (x + 0.044715 * (x * x * x)))


def _interleave(primary, secondary):
    out, done = [], 0
    for i, step in enumerate(primary):
        out.append(step)
        upto = ((i + 1) * len(secondary)) // len(primary)
        out.extend(secondary[done:upto])
        done = upto
    return out + list(secondary[done:])


def _const_spec(shape):
    zeros = (0,) * len(shape)
    return pl.BlockSpec(shape, lambda *_: zeros, pipeline_mode=pl.Buffered(1))


def _padded_rows(x_ref, meta_ref, first):
    xw = x_ref[...]
    tm = xw.shape[0]
    return jnp.concatenate([jnp.where(first, meta_ref[...], xw[0:BLOCK]),
                            jnp.where(first, xw[0:tm - BLOCK], xw[BLOCK:tm])], axis=0)


def _token_window_spec(tm, tiles_per_seq, seq, tile_of_step):
    def index(s):
        t = tile_of_step(s)
        start = jnp.maximum((t % tiles_per_seq) * tm - BLOCK, 0)
        return (pl.multiple_of((t // tiles_per_seq) * seq + start, BLOCK), 0)
    return pl.BlockSpec((pl.Element(tm), pl.Element(D_MODEL)), index)


def _rope_offsets(freq_ref, cos_ref, sin_ref):
    r = lax.broadcasted_iota(jnp.int32, cos_ref.shape, 0).astype(F32)
    ang = r * freq_ref[...]
    cos_ref[...] = jnp.cos(ang)
    sin_ref[...] = jnp.sin(ang)


def _rope_tables(base_pos, freq_ref, cos_ref, sin_ref):
    ang = base_pos.astype(F32) * freq_ref[...]
    cb = jnp.cos(ang)
    sb = jnp.sin(ang)
    oc = cos_ref[...]
    os_ = sin_ref[...]
    return cb * oc - sb * os_, sb * oc + cb * os_


def _cast_blocks(cast_refs):
    n = len(cast_refs) // 2
    for src, dst in zip(cast_refs[:n], cast_refs[n:]):
        dst[...] = src[...].astype(BF16)


def _cast_specs(weights):
    ins, outs, shapes = [], [], []
    step = lambda s: jnp.minimum(s, CAST_STEPS - 1)
    for layer, w in weights:
        _, rows, cols = w.shape
        blk = rows // CAST_STEPS
        ins.append(pl.BlockSpec((None, blk, cols), lambda s, layer=layer: (layer, step(s), 0)))
        outs.append(pl.BlockSpec((blk, cols), lambda s: (step(s), 0)))
        shapes.append(jax.ShapeDtypeStruct((rows, cols), BF16))
    return ins, outs, shapes


def _inproj0_kernel(*refs, tiles_per_seq, n_cast):
    (x_ref, meta_ref, gain_ref, w_ref, qg_ref, kg_ref, freq_ref) = refs[:7]
    cast_in = refs[7:7 + n_cast]
    xr_ref, gate_ref, q_ref, k_ref, v_ref = refs[7 + n_cast:12 + n_cast]
    cast_out = refs[12 + n_cast:12 + 2 * n_cast]
    ocos, osin = refs[12 + 2 * n_cast:]
    step = pl.program_id(0)
    tile_in_seq = step % tiles_per_seq

    @pl.when(step == 0)
    def _():
        _rope_offsets(freq_ref, ocos, osin)

    _cast_blocks(cast_in + cast_out)

    h = _padded_rows(x_ref, meta_ref, tile_in_seq == 0)
    rows = h.shape[0]
    xn = (h * _rms_scale(h) * gain_ref[...]).astype(BF16)

    q0 = 2 * LRU_WIDTH
    k0 = q0 + Q_WIDTH
    y = jnp.dot(xn, w_ref[:, q0:], preferred_element_type=F32)
    v_ref[...] = y[:, k0 - q0 + KV_WIDTH:].astype(BF16)

    cos, sin = _rope_tables(tile_in_seq * rows - META_PAD, freq_ref, ocos, osin)
    lane = lax.broadcasted_iota(jnp.int32, (1, LANES), 1) & (ATT_HEAD_DIM - 1)
    half = ROT_DIM // 2
    sin_lo = sin * jnp.where(lane < half, -1.0, 0.0)
    sin_hi = sin * jnp.where((lane >= half) & (lane < ROT_DIM), 1.0, 0.0)
    hi = lax.broadcasted_iota(jnp.int32, (2 * LANES, LANES), 0) & (LANES - 1)
    hj = lax.broadcasted_iota(jnp.int32, (2 * LANES, LANES), 1)
    head_mean = jnp.where(hi // ATT_HEAD_DIM == hj // ATT_HEAD_DIM,
                          1.0 / ATT_HEAD_DIM, 0.0).astype(BF16)

    def norm_rope(x, gain, scale):
        sq = x * x
        sq_hi = sq.astype(BF16)
        sq_lo = (sq - sq_hi.astype(F32)).astype(BF16)
        ms = jnp.dot(jnp.concatenate([sq_hi, sq_lo], axis=1), head_mean,
                     preferred_element_type=F32)
        xg = x * gain
        rot = (xg * cos + pltpu.roll(xg, LANES - half, 1) * sin_lo
               + pltpu.roll(xg, half, 1) * sin_hi)
        return rot * (lax.rsqrt(ms + RMS_EPS) * scale)

    def qk_step(j):
        if j < Q_WIDTH // LANES:
            q_ref[:, j * LANES:(j + 1) * LANES] = norm_rope(
                y[:, j * LANES:(j + 1) * LANES], qg_ref[...], ATT_HEAD_DIM ** -0.5).astype(BF16)
        else:
            k_ref[...] = norm_rope(y[:, k0 - q0:k0 - q0 + KV_WIDTH], kg_ref[...],
                                   1.0).astype(BF16)

    def proj_step(c):
        cols = slice(c * INPROJ_CHUNK, (c + 1) * INPROJ_CHUNK)
        out_ref = xr_ref if c < LRU_WIDTH // INPROJ_CHUNK else gate_ref
        dst = slice((c * INPROJ_CHUNK) % LRU_WIDTH, (c * INPROJ_CHUNK) % LRU_WIDTH + INPROJ_CHUNK)
        out_ref[:, dst] = jnp.dot(xn, w_ref[:, cols], preferred_element_type=F32)

    proj_steps = [functools.partial(proj_step, c) for c in range(2 * LRU_WIDTH // INPROJ_CHUNK)]
    qk_steps = [functools.partial(qk_step, j) for j in range(Q_WIDTH // LANES + 1)]
    for proj, qk in itertools.zip_longest(proj_steps, qk_steps):
        if proj is not None:
            proj()
        if qk is not None:
            qk()


def _inproj0(x2d, meta_blk, gain, w_in, q_gain, k_gain, freq, cast_weights, batch,
             tiles_per_seq, seq):
    tm = ROW_TILE
    rows = batch * tiles_per_seq * tm
    row = lambda w: pl.BlockSpec((tm, w), lambda i: (i, 0))
    cast_in, cast_out, cast_shapes = _cast_specs(cast_weights)
    return pl.pallas_call(
        functools.partial(_inproj0_kernel, tiles_per_seq=tiles_per_seq, n_cast=len(cast_in)),
        grid=(rows // tm,),
        in_specs=[_token_window_spec(tm, tiles_per_seq, seq, lambda s: s),
                  _const_spec((BLOCK, D_MODEL)), _const_spec((1, D_MODEL)),
                  _const_spec((D_MODEL, AB_IN_WIDTH)), _const_spec((1, LANES)),
                  _const_spec((1, LANES)), _const_spec((1, LANES))] + cast_in,
        out_specs=[row(LRU_WIDTH), row(LRU_WIDTH), row(Q_WIDTH), row(KV_WIDTH), row(KV_WIDTH)]
        + cast_out,
        out_shape=[jax.ShapeDtypeStruct((rows, LRU_WIDTH), F32),
                   jax.ShapeDtypeStruct((rows, LRU_WIDTH), F32),
                   jax.ShapeDtypeStruct((rows, Q_WIDTH), BF16),
                   jax.ShapeDtypeStruct((rows, KV_WIDTH), BF16),
                   jax.ShapeDtypeStruct((rows, KV_WIDTH), BF16)] + cast_shapes,
        scratch_shapes=[pltpu.VMEM((tm, LANES), F32), pltpu.VMEM((tm, LANES), F32)],
        compiler_params=pltpu.CompilerParams(dimension_semantics=("arbitrary",)),
        name="l0_inproj",
    )(x2d, meta_blk, gain, w_in, q_gain, k_gain, freq, *[w for _, w in cast_weights])


def _attn_probs(n, sinks, q, kc, kp, km):
    j = lax.broadcasted_iota(jnp.int32, (BLOCK, BLOCK), 0)
    i = lax.broadcasted_iota(jnp.int32, (BLOCK, BLOCK), 1)
    causal = j <= i
    win_ok = n >= jnp.where(causal, 1, 2)
    meta_ok = (j >= META_PAD) & (n >= jnp.where(causal, 0, 1))
    contract_last = (((1,), (1,)), ((), ()))

    qs = jnp.concatenate([q[:, a * ATT_HEAD_DIM:(a + 1) * ATT_HEAD_DIM]
                          for a in range(ATT_GROUP)], axis=0)
    s_c = lax.dot_general(kc, qs, contract_last, preferred_element_type=F32)
    s_p = lax.dot_general(kp, qs, contract_last, preferred_element_type=F32)
    s_m = lax.dot_general(km, qs, contract_last, preferred_element_type=F32)
    p_c, p_p, p_m, inv_den = [], [], [], []
    for a in range(ATT_GROUP):
        head = slice(a * BLOCK, (a + 1) * BLOCK)
        sw = jnp.where(win_ok, jnp.where(causal, s_c[:, head], s_p[:, head]), NEG_INF)
        sm = jnp.where(meta_ok, s_m[:, head], NEG_INF)
        m = jnp.maximum(jnp.maximum(jnp.max(sw, axis=0, keepdims=True),
                                    jnp.max(sm, axis=0, keepdims=True)), sinks[a])
        pw = jnp.exp(sw - m)
        pm = jnp.exp(sm - m)
        den = (jnp.sum(pw, axis=0, keepdims=True) + jnp.sum(pm, axis=0, keepdims=True)
               + jnp.exp(sinks[a] - m))
        inv_den.append(1.0 / den)
        p_c.append(jnp.where(causal, pw, 0.0).astype(BF16))
        p_p.append(jnp.where(causal, 0.0, pw).astype(BF16))
        p_m.append(pm.astype(BF16))
    lanes = lambda parts: jnp.concatenate(parts, axis=1)
    return lanes(p_c), lanes(p_p), lanes(p_m), lanes(inv_den)


def _attn_out_t(probs, vc, vp, vm):
    p_c, p_p, p_m, inv_den = probs
    contract_rows = (((0,), (0,)), ((), ()))
    o_t = (lax.dot_general(vc, p_c, contract_rows, preferred_element_type=F32)
           + lax.dot_general(vp, p_p, contract_rows, preferred_element_type=F32)
           + lax.dot_general(vm, p_m, contract_rows, preferred_element_type=F32))
    return o_t * inv_den


def _attn_untranspose(o_t):
    o_t = jnp.concatenate(o_t, axis=0)
    return jnp.concatenate([o_t[:, a * BLOCK:(a + 1) * BLOCK].T for a in range(ATT_GROUP)],
                           axis=1).astype(BF16)


def _lru_gates(lo, xr_ref, rows, cw, cb, wg_ref, xbuf):
    tl = BLOCK
    cols = slice(lo, lo + GATE_TILE)
    x = xr_ref[rows, cols]
    xbuf[SUBLANES:SUBLANES + tl, cols] = x
    xc = x * cw[CONV_W - 1:CONV_W, cols] + cb[:, cols]
    for d in range(1, CONV_W):
        xc = xc + (xbuf[SUBLANES - d:SUBLANES - d + tl, cols]
                   * cw[CONV_W - 1 - d:CONV_W - d, cols])
    xbuf[0:SUBLANES, cols] = x[tl - SUBLANES:tl]
    xcb = xc.astype(BF16)
    ga_r = jnp.dot(xcb, wg_ref[cols, cols], preferred_element_type=F32)
    ga_i = jnp.dot(xcb, wg_ref[cols, LRU_WIDTH + lo:LRU_WIDTH + lo + GATE_TILE],
                   preferred_element_type=F32)
    return xc, ga_r, ga_i


def _lru_scan(n, xc, ga_r, ga_i, grp, gate_ref, rows, bg, softplus, hcar, y_ref):
    tl = BLOCK
    row = lax.broadcasted_iota(jnp.int32, (tl, 1), 0)
    t = n * tl + row
    r = _sigmoid(ga_r + bg[:, grp])
    gi = _sigmoid(ga_i + bg[:, LRU_WIDTH + grp.start:LRU_WIDTH + grp.stop])
    log_a = (-LRU_C * softplus[:, grp]) * r
    a = jnp.exp(log_a)
    mult2 = jnp.tanh(-log_a) * (a * a + 1.0)
    mult = mult2 * lax.rsqrt(jnp.maximum(mult2, SQRT_GUARD))
    mult = jnp.where(t == META_PAD, 1.0, mult)
    b = jnp.where(t < META_PAD, 0.0, mult * gi * xc)

    d = 1
    while d < SUBLANES:
        keep = row >= d
        b = jnp.where(keep, a * pltpu.roll(b, d, 0), 0.0) + b
        a = jnp.where(keep, a * pltpu.roll(a, d, 0), a)
        d *= 2
    while d < tl:
        b = jnp.concatenate([b[:d], a[d:] * b[:tl - d] + b[d:]], axis=0)
        a = jnp.concatenate([a[:d], a[d:] * a[:tl - d]], axis=0)
        d *= 2
    h = b + a * hcar[0:1, grp]
    hcar[:, grp] = jnp.broadcast_to(h[tl - 1:tl], (SUBLANES, LANES))
    y_ref[rows, grp] = (_gelu_tanh(gate_ref[rows, grp]) * h).astype(BF16)


def _ffn(h1, gain, wgu_ref, wd_ref):
    xn = (h1 * _rms_scale(h1) * gain).astype(BF16)
    acts = []
    for c in range(D_FF // FFN_CHUNK):
        lo = c * FFN_CHUNK
        g = jnp.dot(xn, wgu_ref[:, lo:lo + FFN_CHUNK], preferred_element_type=F32)
        u = jnp.dot(xn, wgu_ref[:, D_FF + lo:D_FF + lo + FFN_CHUNK], preferred_element_type=F32)
        acts.append((_silu(g) * u).astype(BF16))
    act = jnp.concatenate(acts, axis=1)
    return h1 + jnp.dot(act, wd_ref[...], preferred_element_type=F32)


def _l0_mix_ffn_kernel(*refs, tiles_per_seq, n_tiles, n_cast):
    (sink_ref, x_ref, meta_ref, xr_ref, gate_ref, q_ref, k_ref, v_ref, cw_ref, cb_ref, wg_ref,
     bg_ref, lam_ref, wo_ref, gain_ref, wgu_ref, wd_ref) = refs[:17]
    cast_in = refs[17:17 + n_cast]
    o_ref = refs[17 + n_cast]
    cast_out = refs[18 + n_cast:18 + 2 * n_cast]
    ybuf, xbuf, hcar, kprev, vprev, kmeta, vmeta = refs[18 + 2 * n_cast:]
    s = pl.program_id(0)
    tile_in_seq = jnp.minimum(s, n_tiles - 1) % tiles_per_seq
    blocks_per_tile = x_ref.shape[0] // BLOCK
    last = slice((blocks_per_tile - 1) * BLOCK, blocks_per_tile * BLOCK)
    _cast_blocks(cast_in + cast_out)

    @pl.when(s == 0)
    def _():
        ybuf[...] = jnp.zeros_like(ybuf)

    @pl.when(tile_in_seq == 0)
    def _():
        xbuf[0:SUBLANES, :] = jnp.zeros((SUBLANES, LRU_WIDTH), F32)
        hcar[...] = jnp.zeros_like(hcar)
        kmeta[...] = k_ref[0:BLOCK, :]
        vmeta[...] = v_ref[0:BLOCK, :]
        kprev[...] = k_ref[0:BLOCK, :]
        vprev[...] = v_ref[0:BLOCK, :]

    ffn_first = jnp.maximum(s - 1, 0) % tiles_per_seq == 0
    h1 = (_padded_rows(x_ref, meta_ref, ffn_first)
          + jnp.dot(ybuf[...], wo_ref[...], preferred_element_type=F32))
    xn = (h1 * _rms_scale(h1) * gain_ref[...]).astype(BF16)

    cw = cw_ref[...]
    cb = cb_ref[...]
    bg = bg_ref[...]
    z = -lam_ref[...]
    softplus = jnp.maximum(z, 0.0) + jnp.log1p(jnp.exp(-jnp.abs(z)))

    live = {}

    def block_rows(blk):
        return tile_in_seq * blocks_per_tile + blk, slice(blk * BLOCK, (blk + 1) * BLOCK)

    def lru_gates_step(blk, lo):
        _, rows = block_rows(blk)
        live["lru", blk, lo] = _lru_gates(lo, xr_ref, rows, cw, cb, wg_ref, xbuf)

    def lru_scan_step(blk, lo, sub):
        n, rows = block_rows(blk)
        xc, ga_r, ga_i = live["lru", blk, lo]
        part = slice(sub, sub + LANES)
        _lru_scan(n, xc[:, part], ga_r[:, part], ga_i[:, part],
                  slice(lo + sub, lo + sub + LANES), gate_ref, rows, bg, softplus, hcar, ybuf)
        if sub + LANES == GATE_TILE:
            del live["lru", blk, lo]

    def kv_blocks(ref, prev_ref, meta_ref, blk, g):
        _, rows = block_rows(blk)
        lanes = slice(g * ATT_HEAD_DIM, (g + 1) * ATT_HEAD_DIM)
        prev = prev_ref[:, lanes] if blk == 0 else ref[(blk - 1) * BLOCK:blk * BLOCK, lanes]
        return ref[rows, lanes], prev, meta_ref[:, lanes]

    def attn_probs_step(blk, g):
        n, rows = block_rows(blk)
        width = ATT_GROUP * ATT_HEAD_DIM
        sinks = [sink_ref[g * ATT_GROUP + a] for a in range(ATT_GROUP)]
        live["probs", blk, g] = _attn_probs(n, sinks, q_ref[rows, g * width:(g + 1) * width],
                                            *kv_blocks(k_ref, kprev, kmeta, blk, g))

    def attn_out_step(blk, g):
        _, rows = block_rows(blk)
        live["out", blk, g] = _attn_out_t(live.pop(("probs", blk, g)),
                                          *kv_blocks(v_ref, vprev, vmeta, blk, g))
        if g + 1 == ATT_KV_HEADS:
            ybuf[rows, LRU_WIDTH:LRU_WIDTH + Q_WIDTH] = _attn_untranspose(
                [live.pop(("out", blk, h)) for h in range(ATT_KV_HEADS)])

    acts = []

    def gate_step(c):
        lo = c * FFN_CHUNK
        live["g", c] = jnp.dot(xn, wgu_ref[:, lo:lo + FFN_CHUNK], preferred_element_type=F32)

    def up_step(c):
        lo = c * FFN_CHUNK
        u = jnp.dot(xn, wgu_ref[:, D_FF + lo:D_FF + lo + FFN_CHUNK], preferred_element_type=F32)
        acts.append((_silu(live.pop(("g", c))) * u).astype(BF16))

    def down_step(c):
        cols = slice(c * DOWN_CHUNK, (c + 1) * DOWN_CHUNK)
        if len(acts) > 1:
            acts[:] = [jnp.concatenate(acts, axis=1)]
        o_ref[:, cols] = h1[:, cols] + jnp.dot(acts[0], wd_ref[:, cols],
                                               preferred_element_type=F32)

    ffn_steps = []
    for c in range(D_FF // FFN_CHUNK):
        ffn_steps += [functools.partial(gate_step, c), functools.partial(up_step, c)]
    ffn_steps += [functools.partial(down_step, c) for c in range(D_MODEL // DOWN_CHUNK)]
    mix_steps = []
    for blk in range(blocks_per_tile):
        for lo in range(0, LRU_WIDTH, GATE_TILE):
            mix_steps.append(functools.partial(lru_gates_step, blk, lo))
            mix_steps += [functools.partial(lru_scan_step, blk, lo, sub)
                          for sub in range(0, GATE_TILE, LANES)]
        mix_steps += [functools.partial(attn_probs_step, blk, g) for g in range(ATT_KV_HEADS)]
        mix_steps += [functools.partial(attn_out_step, blk, g) for g in range(ATT_KV_HEADS)]
    for step in _interleave(ffn_steps, mix_steps):
        step()
    kprev[...] = k_ref[last, :]
    vprev[...] = v_ref[last, :]


def _l0_mix_ffn(sinks, x2d, meta_blk, xr, gate, q, k, v, conv_w, conv_b, w_gates, b_gates, lam,
                w_out, gain, w_gu, w_down, cast_weights, tiles_per_seq, seq):
    rows = xr.shape[0]
    tm = ROW_TILE
    n_tiles = rows // tm
    mix = lambda w: pl.BlockSpec((tm, w), lambda s: (jnp.minimum(s, n_tiles - 1), 0))
    cast_in, cast_out, cast_shapes = _cast_specs(cast_weights)
    return pl.pallas_call(
        functools.partial(_l0_mix_ffn_kernel, tiles_per_seq=tiles_per_seq, n_tiles=n_tiles,
                          n_cast=len(cast_in)),
        grid=(n_tiles + 1,),
        in_specs=[pl.BlockSpec(memory_space=pltpu.SMEM),
                  _token_window_spec(tm, tiles_per_seq, seq, lambda s: jnp.maximum(s - 1, 0)),
                  _const_spec((BLOCK, D_MODEL)),
                  mix(LRU_WIDTH), mix(LRU_WIDTH), mix(Q_WIDTH), mix(KV_WIDTH), mix(KV_WIDTH),
                  _const_spec((CONV_W, LRU_WIDTH)), _const_spec((1, LRU_WIDTH)),
                  _const_spec((LRU_WIDTH, 2 * LRU_WIDTH)), _const_spec((1, 2 * LRU_WIDTH)),
                  _const_spec((1, LRU_WIDTH)), _const_spec(w_out.shape), _const_spec((1, D_MODEL)),
                  _const_spec(w_gu.shape), _const_spec(w_down.shape)] + cast_in,
        out_specs=[pl.BlockSpec((tm, D_MODEL), lambda s: (jnp.maximum(s - 1, 0), 0))] + cast_out,
        out_shape=[jax.ShapeDtypeStruct((rows, D_MODEL), F32)] + cast_shapes,
        scratch_shapes=[pltpu.VMEM((tm, LRU_WIDTH + Q_WIDTH), BF16),
                        pltpu.VMEM((SUBLANES + BLOCK, LRU_WIDTH), F32),
                        pltpu.VMEM((SUBLANES, LRU_WIDTH), F32),
                        pltpu.VMEM((BLOCK, KV_WIDTH), BF16), pltpu.VMEM((BLOCK, KV_WIDTH), BF16),
                        pltpu.VMEM((BLOCK, KV_WIDTH), BF16), pltpu.VMEM((BLOCK, KV_WIDTH), BF16)],
        compiler_params=pltpu.CompilerParams(dimension_semantics=("arbitrary",)),
        name="l0_mix_ffn",
    )(sinks, x2d, meta_blk, xr, gate, q, k, v, conv_w, conv_b, w_gates, b_gates, lam, w_out, gain,
      w_gu, w_down, *[w for _, w in cast_weights])


def _outproj_ffn_kernel(h_ref, y_ref, wo_ref, gain_ref, wgu_ref, wd_ref, o_ref):
    h1 = h_ref[...] + jnp.dot(y_ref[...], wo_ref[...], preferred_element_type=F32)
    o_ref[...] = _ffn(h1, gain_ref[...], wgu_ref, wd_ref)


def _final_outproj_ffn(h, y, w_out, gain, w_gu, w_down, batch, seq_rows, seq):
    tm = OUT_TILE
    tiles = seq // tm
    win = lambda w: pl.BlockSpec((pl.Element(tm), pl.Element(w)),
                                 lambda b, j: (pl.multiple_of(b * seq_rows + BLOCK + j * tm, BLOCK),
                                               0))
    return pl.pallas_call(
        _outproj_ffn_kernel,
        grid=(batch, tiles),
        in_specs=[win(D_MODEL), win(y.shape[1]), _const_spec(w_out.shape),
                  _const_spec((1, D_MODEL)), _const_spec(w_gu.shape), _const_spec(w_down.shape)],
        out_specs=pl.BlockSpec((tm, D_MODEL), lambda b, j: (b * tiles + j, 0)),
        out_shape=jax.ShapeDtypeStruct((batch * seq, D_MODEL), F32),
        compiler_params=pltpu.CompilerParams(dimension_semantics=("arbitrary", "arbitrary")),
        name="l1_outproj_ffn",
    )(h, y, w_out, gain, w_gu, w_down)


_Q0, _K0, _V0, _G0 = 0, D_MODEL, 2 * D_MODEL, 4 * D_MODEL


def _retention_decays(hd):
    ii = lax.broadcasted_iota(jnp.int32, (BLOCK, BLOCK), 0)
    jj = lax.broadcasted_iota(jnp.int32, (BLOCK, BLOCK), 1)
    diff = (ii - jj).astype(F32)
    idx = lax.broadcasted_iota(jnp.int32, (BLOCK, 1), 0).astype(F32)
    log_g = RET_LOG_G[hd]
    decay_intra = jnp.where(diff >= 0.0, jnp.exp(jnp.maximum(diff, 0.0) * log_g), 0.0)
    return (decay_intra, jnp.exp((idx + 1.0) * log_g), jnp.exp((BLOCK - 1.0 - idx) * log_g),
            math.exp(BLOCK * log_g))


def _l1_mix_kernel(h_ref, gain_ref, w_ref, freq_ref, y_ref, qkvg, state, ocos, osin,
                   *, tiles_per_seq, n_tiles):
    s = pl.program_id(0)
    proj_slot = s % 2
    ret_slot = 1 - proj_slot
    proj_tile_in_seq = jnp.minimum(s, n_tiles - 1) % tiles_per_seq
    ret_tile_in_seq = jnp.maximum(s - 1, 0) % tiles_per_seq
    tm = h_ref.shape[0]

    @pl.when(s == 0)
    def _():
        _rope_offsets(freq_ref, ocos, osin)
        qkvg[1] = jnp.zeros(qkvg.shape[1:], BF16)

    @pl.when(ret_tile_in_seq == 0)
    def _():
        state[...] = jnp.zeros_like(state)

    h = h_ref[...]
    xn = (h * _rms_scale(h) * gain_ref[...]).astype(BF16)
    cos, sin = _rope_tables(proj_tile_in_seq * tm - META_PAD, freq_ref, ocos, osin)
    half = RET_QK_DIM // 2

    def rope_step(col0, hd, scale):
        lo = col0 + hd * RET_QK_DIM
        y = jnp.dot(xn, w_ref[:, lo:lo + RET_QK_DIM], preferred_element_type=F32)
        x1 = y[:, :half]
        x2 = y[:, half:]
        qkvg[proj_slot, :, lo:lo + half] = ((x1 * cos - x2 * sin) * scale).astype(BF16)
        qkvg[proj_slot, :, lo + half:lo + RET_QK_DIM] = ((x2 * cos + x1 * sin)
                                                         * scale).astype(BF16)

    def value_step(col0, hd, act):
        lo = col0 + hd * RET_V_DIM
        y = jnp.dot(xn, w_ref[:, lo:lo + RET_V_DIM], preferred_element_type=F32)
        qkvg[proj_slot, :, lo:lo + RET_V_DIM] = act(y).astype(BF16)

    def ret_operand(c, hd, col0, width):
        return qkvg[ret_slot, c * BLOCK:(c + 1) * BLOCK, col0 + hd * width:col0 + (hd + 1) * width]

    heads = range(RET_HEADS)
    decays = [_retention_decays(hd) for hd in heads]
    live = {}

    def prep_step(c):
        live["kdt", c] = [
            (ret_operand(c, hd, _K0, RET_QK_DIM).astype(F32) * decays[hd][2]).T.astype(BF16)
            for hd in heads]

    def scores_step(c):
        live["qk", c] = [lax.dot_general(ret_operand(c, hd, _Q0, RET_QK_DIM),
                                         ret_operand(c, hd, _K0, RET_QK_DIM),
                                         (((1,), (1,)), ((), ())), preferred_element_type=F32)
                         for hd in heads]
        kdt = live.pop(("kdt", c))
        live["kv", c] = [jnp.dot(kdt[hd], ret_operand(c, hd, _V0, RET_V_DIM),
                                 preferred_element_type=F32) for hd in heads]

    def decay_step(c):
        qk = live.pop(("qk", c))
        live["qkd", c] = [(qk[hd] * decays[hd][0]).astype(BF16) for hd in heads]
        live["stb", c] = [state[hd].astype(BF16) for hd in heads]

    def output_step(c):
        qkd = live.pop(("qkd", c))
        stb = live.pop(("stb", c))
        live["o", c] = [
            jnp.dot(qkd[hd], ret_operand(c, hd, _V0, RET_V_DIM), preferred_element_type=F32)
            + jnp.dot(ret_operand(c, hd, _Q0, RET_QK_DIM), stb[hd],
                      preferred_element_type=F32) * decays[hd][1]
            for hd in heads]

    def post_step(c):
        kv = live.pop(("kv", c))
        for hd, o in enumerate(live.pop(("o", c))):
            state[hd] = decays[hd][3] * state[hd] + kv[hd]
            gate = ret_operand(c, hd, _G0, RET_V_DIM).astype(F32)
            y_ref[c * BLOCK:(c + 1) * BLOCK, hd * RET_V_DIM:(hd + 1) * RET_V_DIM] = (
                o * _rms_scale(o) * gate).astype(BF16)

    proj_steps = []
    for hd in heads:
        proj_steps.append([functools.partial(rope_step, _Q0, hd, 1.0),
                           functools.partial(rope_step, _K0, hd, RET_QK_DIM ** -0.5)])
        proj_steps.append([functools.partial(value_step, _V0, hd, lambda y: y)])
        proj_steps.append([functools.partial(value_step, _G0, hd, _silu)])
    n_chunks = tm // BLOCK
    ret_steps = []
    for c in range(n_chunks):
        ret_steps.append([functools.partial(scores_step, c), functools.partial(decay_step, c)]
                         + ([functools.partial(prep_step, c + 1)] if c + 1 < n_chunks else []))
        ret_steps.append([functools.partial(output_step, c), functools.partial(post_step, c)])
    prep_step(0)
    for proj, ret in itertools.zip_longest(proj_steps, ret_steps, fillvalue=()):
        for step in (*proj, *ret):
            step()


def _l1_mix(h, gain, w_in, freq, tiles_per_seq):
    rows = h.shape[0]
    tm = ROW_TILE
    n_tiles = rows // tm
    half = RET_QK_DIM // 2
    return pl.pallas_call(
        functools.partial(_l1_mix_kernel, tiles_per_seq=tiles_per_seq, n_tiles=n_tiles),
        grid=(n_tiles + 1,),
        in_specs=[pl.BlockSpec((tm, D_MODEL), lambda s: (jnp.minimum(s, n_tiles - 1), 0)),
                  _const_spec((1, D_MODEL)), _const_spec(w_in.shape), _const_spec((1, half))],
        out_specs=pl.BlockSpec((tm, 2 * D_MODEL), lambda s: (jnp.maximum(s - 1, 0), 0)),
        out_shape=jax.ShapeDtypeStruct((rows, 2 * D_MODEL), BF16),
        scratch_shapes=[pltpu.VMEM((2, tm, 6 * D_MODEL), BF16),
                        pltpu.VMEM((RET_HEADS, RET_QK_DIM, RET_V_DIM), F32),
                        pltpu.VMEM((tm, half), F32), pltpu.VMEM((tm, half), F32)],
        compiler_params=pltpu.CompilerParams(dimension_semantics=("arbitrary",)),
        name="l1_mix",
    )(h, gain, w_in, freq)


def _inv_freq(half, theta):
    return jnp.power(jnp.asarray(theta, F32), -jnp.arange(half, dtype=F32) / half)


def _block_diag(w):
    heads, wi, wo = w.shape
    eye = jnp.eye(heads, dtype=w.dtype)
    return (eye[:, None, :, None] * w[:, :, None, :]).reshape(heads * wi, heads * wo)


def kernel(x, meta_tokens, mix_norm_ab, ab_w_in, lru_conv_w, lru_conv_b, lru_w_a, lru_b_a, lru_w_i, lru_b_i, lru_lambda, q_norm, k_norm, attn_sinks, ab_w_out, mix_norm_ret, ret_w_in, ret_w_out, ffn_norm, ffn_w_gu, ffn_w_down):
    batch, seq, _ = x.shape
    seq_rows = META_PAD + N_META + seq
    tiles_per_seq = seq_rows // ROW_TILE

    x2d = x.reshape(batch * seq, D_MODEL)
    meta_blk = jnp.concatenate([jnp.zeros((META_PAD, D_MODEL), x.dtype),
                                meta_tokens.astype(x.dtype)], axis=0)

    f_att = _inv_freq(ROT_DIM // 2, ROPE_THETA)
    f_att = jnp.concatenate([f_att, f_att, jnp.zeros((ATT_HEAD_DIM - ROT_DIM,), F32)])
    f_att = jnp.tile(f_att, LANES // ATT_HEAD_DIM).reshape(1, LANES)
    f_ret = _inv_freq(RET_QK_DIM // 2, RET_THETA).reshape(1, RET_QK_DIM // 2)

    row_vec = lambda v: v.reshape(1, -1).astype(F32)
    two_heads = lambda v: jnp.tile(v.reshape(1, -1).astype(F32), (1, 2))

    xr, gate, q, k, v, w_gu0, w_down0, w_in1, w_out1, w_gu1, w_down1 = _inproj0(
        x2d, meta_blk, row_vec(mix_norm_ab[0]), ab_w_in[0].astype(BF16), two_heads(q_norm[0]),
        two_heads(k_norm[0]), f_att,
        [(0, ffn_w_gu), (0, ffn_w_down), (0, ret_w_in), (0, ret_w_out), (1, ffn_w_gu),
         (1, ffn_w_down)], batch, tiles_per_seq, seq)
    w_gates = jnp.concatenate([_block_diag(lru_w_a[0]), _block_diag(lru_w_i[0])], axis=1)
    b_gates = jnp.concatenate([lru_b_a[0].reshape(1, -1), lru_b_i[0].reshape(1, -1)], axis=1)
    w_att = ab_w_out[0][LRU_WIDTH:].reshape(ATT_KV_HEADS, ATT_GROUP, ATT_HEAD_DIM, D_MODEL)
    w_att = w_att.transpose(1, 0, 2, 3).reshape(Q_WIDTH, D_MODEL)
    w_out0 = jnp.concatenate([ab_w_out[0][:LRU_WIDTH], w_att], axis=0).astype(BF16)
    (h,) = _l0_mix_ffn(
        attn_sinks[0].astype(F32), x2d, meta_blk, xr, gate, q, k, v, lru_conv_w[0],
        row_vec(lru_conv_b[0]), w_gates.astype(BF16), b_gates.astype(F32), row_vec(lru_lambda[0]),
        w_out0, row_vec(ffn_norm[0]), w_gu0, w_down0, [], tiles_per_seq, seq)

    y_ret = _l1_mix(h, row_vec(mix_norm_ret[0]), w_in1, f_ret, tiles_per_seq)
    out = _final_outproj_ffn(h, y_ret, w_out1, row_vec(ffn_norm[1]), w_gu1, w_down1,
                             batch, seq_rows, seq)
    return out.reshape(batch, seq, D_MODEL)
```

```python
import functools
import itertools
import math

import jax
import jax.numpy as jnp
from jax import lax
from jax.experimental import pallas as pl
from jax.experimental.pallas import tpu as pltpu

F32 = jnp.float32
BF16 = jnp.bfloat16

D_MODEL = 1024
N_META = 16
BLOCK = 128
META_PAD = BLOCK - N_META
RMS_EPS = 1e-6
NEG_INF = -1e30

LRU_WIDTH = 512
LRU_HEADS = 8
LRU_BLOCK_W = 64
CONV_W = 4
LRU_C = 8.0

ATT_HEADS = 8
ATT_KV_HEADS = 2
ATT_GROUP = ATT_HEADS // ATT_KV_HEADS
ATT_HEAD_DIM = 64
ROPE_THETA = 500000.0
ROT_DIM = 16
Q_WIDTH = 512
KV_WIDTH = 128
AB_IN_WIDTH = 2 * LRU_WIDTH + Q_WIDTH + 2 * KV_WIDTH

RET_HEADS = 4
RET_QK_DIM = 256
RET_V_DIM = 512
RET_THETA = 10000.0
RET_LOG_G = tuple(math.log1p(-(2.0 ** (-5.0 - h))) for h in range(RET_HEADS))

D_FF = 2816

LANES = 128
SUBLANES = 8
ROW_TILE = 640
OUT_TILE = 512
SQRT_GUARD = 1e-30
GATE_TILE = 256
INPROJ_CHUNK = 256
PROJ_CHUNK = 256
CAST_STEPS = 16
FFN_CHUNK = 256
DOWN_CHUNK = 256


def _rms_scale(x):
    return lax.rsqrt(jnp.mean(x * x, axis=-1, keepdims=True) + RMS_EPS)


def _sigmoid(x):
    return 0.5 * jnp.tanh(0.5 * x) + 0.5


def _silu(x):
    half = 0.5 * x
    return half + half * jnp.tanh(half)


def _gelu_tanh(x):
    half = 0.5 * x
    return half + half * jnp.tanh(0.7978845608028654 * (x + 0.044715 * (x * x * x)))


def _interleave(primary, secondary):
    out, done = [], 0
    for i, step in enumerate(primary):
        out.append(step)
        upto = ((i + 1) * len(secondary)) // len(primary)
        out.extend(secondary[done:upto])
        done = upto
    return out + list(secondary[done:])


def _const_spec(shape):
    zeros = (0,) * len(shape)
    return pl.BlockSpec(shape, lambda *_: zeros, pipeline_mode=pl.Buffered(1))


def _padded_rows(x_ref, meta_ref, first):
    xw = x_ref[...]
    tm = xw.shape[0]
    return jnp.concatenate([jnp.where(first, meta_ref[...], xw[0:BLOCK]),
                            jnp.where(first, xw[0:tm - BLOCK], xw[BLOCK:tm])], axis=0)


def _token_window_spec(tm, tiles_per_seq, seq, tile_of_step):
    def index(s):
        t = tile_of_step(s)
        start = jnp.maximum((t % tiles_per_seq) * tm - BLOCK, 0)
        return (pl.multiple_of((t // tiles_per_seq) * seq + start, BLOCK), 0)
    return pl.BlockSpec((pl.Element(tm), pl.Element(D_MODEL)), index)


def _rope_offsets(freq_ref, cos_ref, sin_ref):
    r = lax.broadcasted_iota(jnp.int32, cos_ref.shape, 0).astype(F32)
    ang = r * freq_ref[...]
    cos_ref[...] = jnp.cos(ang)
    sin_ref[...] = jnp.sin(ang)


def _rope_tables(base_pos, freq_ref, cos_ref, sin_ref):
    ang = base_pos.astype(F32) * freq_ref[...]
    cb = jnp.cos(ang)
    sb = jnp.sin(ang)
    oc = cos_ref[...]
    os_ = sin_ref[...]
    return cb * oc - sb * os_, sb * oc + cb * os_


def _cast_blocks(cast_refs):
    n = len(cast_refs) // 2
    for src, dst in zip(cast_refs[:n], cast_refs[n:]):
        dst[...] = src[...].astype(BF16)


def _cast_specs(weights):
    ins, outs, shapes = [], [], []
    step = lambda s: jnp.minimum(s, CAST_STEPS - 1)
    for layer, w in weights:
        _, rows, cols = w.shape
        blk = rows // CAST_STEPS
        ins.append(pl.BlockSpec((None, blk, cols), lambda s, layer=layer: (layer, step(s), 0)))
        outs.append(pl.BlockSpec((blk, cols), lambda s: (step(s), 0)))
        shapes.append(jax.ShapeDtypeStruct((rows, cols), BF16))
    return ins, outs, shapes


def _inproj0_kernel(*refs, tiles_per_seq, n_cast):
    (x_ref, meta_ref, gain_ref, w_ref, qg_ref, kg_ref, freq_ref) = refs[:7]
    cast_in = refs[7:7 + n_cast]
    xr_ref, gate_ref, q_ref, k_ref, v_ref = refs[7 + n_cast:12 + n_cast]
    cast_out = refs[12 + n_cast:12 + 2 * n_cast]
    ocos, osin = refs[12 + 2 * n_cast:]
    step = pl.program_id(0)
    tile_in_seq = step % tiles_per_seq

    @pl.when(step == 0)
    def _():
        _rope_offsets(freq_ref, ocos, osin)

    _cast_blocks(cast_in + cast_out)

    h = _padded_rows(x_ref, meta_ref, tile_in_seq == 0)
    rows = h.shape[0]
    xn = (h * _rms_scale(h) * gain_ref[...]).astype(BF16)

    q0 = 2 * LRU_WIDTH
    k0 = q0 + Q_WIDTH
    chunk_dot = lambda lo: jnp.dot(xn, w_ref[:, lo:lo + INPROJ_CHUNK], preferred_element_type=F32)
    y_q = [chunk_dot(q0 + lo) for lo in range(0, Q_WIDTH, INPROJ_CHUNK)]
    y_kv = chunk_dot(k0)
    v_ref[...] = y_kv[:, KV_WIDTH:].astype(BF16)

    cos, sin = _rope_tables(tile_in_seq * rows - META_PAD, freq_ref, ocos, osin)
    lane = lax.broadcasted_iota(jnp.int32, (1, LANES), 1) & (ATT_HEAD_DIM - 1)
    half = ROT_DIM // 2
    sin_lo = sin * jnp.where(lane < half, -1.0, 0.0)
    sin_hi = sin * jnp.where((lane >= half) & (lane < ROT_DIM), 1.0, 0.0)
    hi = lax.broadcasted_iota(jnp.int32, (2 * LANES, LANES), 0) & (LANES - 1)
    hj = lax.broadcasted_iota(jnp.int32, (2 * LANES, LANES), 1)
    head_mean = jnp.where(hi // ATT_HEAD_DIM == hj // ATT_HEAD_DIM,
                          1.0 / ATT_HEAD_DIM, 0.0).astype(BF16)

    def norm_rope(x, gain, scale):
        sq = x * x
        sq_hi = sq.astype(BF16)
        sq_lo = (sq - sq_hi.astype(F32)).astype(BF16)
        ms = jnp.dot(jnp.concatenate([sq_hi, sq_lo], axis=1), head_mean,
                     preferred_element_type=F32)
        xg = x * gain
        rot = (xg * cos + pltpu.roll(xg, LANES - half, 1) * sin_lo
               + pltpu.roll(xg, half, 1) * sin_hi)
        return rot * (lax.rsqrt(ms + RMS_EPS) * scale)

    def qk_step(j):
        if j < Q_WIDTH // LANES:
            lo = (j * LANES) % INPROJ_CHUNK
            q_ref[:, j * LANES:(j + 1) * LANES] = norm_rope(
                y_q[j * LANES // INPROJ_CHUNK][:, lo:lo + LANES], qg_ref[...],
                ATT_HEAD_DIM ** -0.5).astype(BF16)
        else:
            k_ref[...] = norm_rope(y_kv[:, :KV_WIDTH], kg_ref[...], 1.0).astype(BF16)

    def proj_step(c):
        out_ref = xr_ref if c < LRU_WIDTH // INPROJ_CHUNK else gate_ref
        dst = slice((c * INPROJ_CHUNK) % LRU_WIDTH, (c * INPROJ_CHUNK) % LRU_WIDTH + INPROJ_CHUNK)
        out_ref[:, dst] = chunk_dot(c * INPROJ_CHUNK)

    proj_steps = [functools.partial(proj_step, c) for c in range(2 * LRU_WIDTH // INPROJ_CHUNK)]
    qk_steps = [functools.partial(qk_step, j) for j in range(Q_WIDTH // LANES + 1)]
    for step in _interleave(proj_steps, qk_steps):
        step()


def _inproj0(x2d, meta_blk, gain, w_in, q_gain, k_gain, freq, cast_weights, batch,
             tiles_per_seq, seq):
    tm = ROW_TILE
    rows = batch * tiles_per_seq * tm
    row = lambda w: pl.BlockSpec((tm, w), lambda i: (i, 0))
    cast_in, cast_out, cast_shapes = _cast_specs(cast_weights)
    return pl.pallas_call(
        functools.partial(_inproj0_kernel, tiles_per_seq=tiles_per_seq, n_cast=len(cast_in)),
        grid=(rows // tm,),
        in_specs=[_token_window_spec(tm, tiles_per_seq, seq, lambda s: s),
                  _const_spec((BLOCK, D_MODEL)), _const_spec((1, D_MODEL)),
                  _const_spec((D_MODEL, AB_IN_WIDTH)), _const_spec((1, LANES)),
                  _const_spec((1, LANES)), _const_spec((1, LANES))] + cast_in,
        out_specs=[row(LRU_WIDTH), row(LRU_WIDTH), row(Q_WIDTH), row(KV_WIDTH), row(KV_WIDTH)]
        + cast_out,
        out_shape=[jax.ShapeDtypeStruct((rows, LRU_WIDTH), F32),
                   jax.ShapeDtypeStruct((rows, LRU_WIDTH), F32),
                   jax.ShapeDtypeStruct((rows, Q_WIDTH), BF16),
                   jax.ShapeDtypeStruct((rows, KV_WIDTH), BF16),
                   jax.ShapeDtypeStruct((rows, KV_WIDTH), BF16)] + cast_shapes,
        scratch_shapes=[pltpu.VMEM((tm, LANES), F32), pltpu.VMEM((tm, LANES), F32)],
        compiler_params=pltpu.CompilerParams(dimension_semantics=("arbitrary",)),
        name="l0_inproj",
    )(x2d, meta_blk, gain, w_in, q_gain, k_gain, freq, *[w for _, w in cast_weights])


def _attn_probs(n, sinks, q, kc, kp, km):
    j = lax.broadcasted_iota(jnp.int32, (BLOCK, BLOCK), 0)
    i = lax.broadcasted_iota(jnp.int32, (BLOCK, BLOCK), 1)
    causal = j <= i
    win_ok = n >= jnp.where(causal, 1, 2)
    meta_ok = (j >= META_PAD) & (n >= jnp.where(causal, 0, 1))
    contract_last = (((1,), (1,)), ((), ()))

    qs = jnp.concatenate([q[:, a * ATT_HEAD_DIM:(a + 1) * ATT_HEAD_DIM]
                          for a in range(ATT_GROUP)], axis=0)
    s_c = lax.dot_general(kc, qs, contract_last, preferred_element_type=F32)
    s_p = lax.dot_general(kp, qs, contract_last, preferred_element_type=F32)
    s_m = lax.dot_general(km, qs, contract_last, preferred_element_type=F32)
    p_c, p_p, p_m, inv_den = [], [], [], []
    for a in range(ATT_GROUP):
        head = slice(a * BLOCK, (a + 1) * BLOCK)
        sw = jnp.where(win_ok, jnp.where(causal, s_c[:, head], s_p[:, head]), NEG_INF)
        sm = jnp.where(meta_ok, s_m[:, head], NEG_INF)
        m = jnp.maximum(jnp.maximum(jnp.max(sw, axis=0, keepdims=True),
                                    jnp.max(sm, axis=0, keepdims=True)), sinks[a])
        pw = jnp.exp(sw - m)
        pm = jnp.exp(sm - m)
        den = (jnp.sum(pw, axis=0, keepdims=True) + jnp.sum(pm, axis=0, keepdims=True)
               + jnp.exp(sinks[a] - m))
        inv_den.append(1.0 / den)
        p_c.append(jnp.where(causal, pw, 0.0).astype(BF16))
        p_p.append(jnp.where(causal, 0.0, pw).astype(BF16))
        p_m.append(pm.astype(BF16))
    lanes = lambda parts: jnp.concatenate(parts, axis=1)
    return lanes(p_c), lanes(p_p), lanes(p_m), lanes(inv_den)


def _attn_out_t(probs, vc, vp, vm):
    p_c, p_p, p_m, inv_den = probs
    contract_rows = (((0,), (0,)), ((), ()))
    o_t = (lax.dot_general(vc, p_c, contract_rows, preferred_element_type=F32)
           + lax.dot_general(vp, p_p, contract_rows, preferred_element_type=F32)
           + lax.dot_general(vm, p_m, contract_rows, preferred_element_type=F32))
    return o_t * inv_den


def _attn_untranspose(o_t):
    o_t = jnp.concatenate(o_t, axis=0)
    return jnp.concatenate([o_t[:, a * BLOCK:(a + 1) * BLOCK].T for a in range(ATT_GROUP)],
                           axis=1).astype(BF16)


def _lru_gates(lo, xr_ref, rows, cw, cb, wg_ref, xbuf):
    tl = BLOCK
    cols = slice(lo, lo + GATE_TILE)
    x = xr_ref[rows, cols]
    xbuf[SUBLANES:SUBLANES + tl, cols] = x
    xc = x * cw[CONV_W - 1:CONV_W, cols] + cb[:, cols]
    for d in range(1, CONV_W):
        xc = xc + (xbuf[SUBLANES - d:SUBLANES - d + tl, cols]
                   * cw[CONV_W - 1 - d:CONV_W - d, cols])
    xbuf[0:SUBLANES, cols] = x[tl - SUBLANES:tl]
    xcb = xc.astype(BF16)
    ga_r = jnp.dot(xcb, wg_ref[cols, cols], preferred_element_type=F32)
    ga_i = jnp.dot(xcb, wg_ref[cols, LRU_WIDTH + lo:LRU_WIDTH + lo + GATE_TILE],
                   preferred_element_type=F32)
    return xc, ga_r, ga_i


def _lru_scan(n, xc, ga_r, ga_i, grp, gate_ref, rows, bg, softplus, hcar, y_ref):
    tl = BLOCK
    row = lax.broadcasted_iota(jnp.int32, (tl, 1), 0)
    t = n * tl + row
    r = _sigmoid(ga_r + bg[:, grp])
    gi = _sigmoid(ga_i + bg[:, LRU_WIDTH + grp.start:LRU_WIDTH + grp.stop])
    log_a = (-LRU_C * softplus[:, grp]) * r
    a = jnp.exp(log_a)
    mult2 = jnp.tanh(-log_a) * (a * a + 1.0)
    mult = mult2 * lax.rsqrt(jnp.maximum(mult2, SQRT_GUARD))
    mult = jnp.where(t == META_PAD, 1.0, mult)
    b = jnp.where(t < META_PAD, 0.0, mult * gi * xc)

    d = 1
    while d < SUBLANES:
        keep = row >= d
        b = jnp.where(keep, a * pltpu.roll(b, d, 0), 0.0) + b
        a = jnp.where(keep, a * pltpu.roll(a, d, 0), a)
        d *= 2
    while d < tl:
        b = jnp.concatenate([b[:d], a[d:] * b[:tl - d] + b[d:]], axis=0)
        a = jnp.concatenate([a[:d], a[d:] * a[:tl - d]], axis=0)
        d *= 2
    h = b + a * hcar[0:1, grp]
    hcar[:, grp] = jnp.broadcast_to(h[tl - 1:tl], (SUBLANES, LANES))
    y_ref[rows, grp] = (_gelu_tanh(gate_ref[rows, grp]) * h).astype(BF16)


def _ffn(h1, gain, wgu_ref, wd_ref):
    xn = (h1 * _rms_scale(h1) * gain).astype(BF16)
    acts = []
    for c in range(D_FF // FFN_CHUNK):
        lo = c * FFN_CHUNK
        g = jnp.dot(xn, wgu_ref[:, lo:lo + FFN_CHUNK], preferred_element_type=F32)
        u = jnp.dot(xn, wgu_ref[:, D_FF + lo:D_FF + lo + FFN_CHUNK], preferred_element_type=F32)
        acts.append((_silu(g) * u).astype(BF16))
    act = jnp.concatenate(acts, axis=1)
    return h1 + jnp.dot(act, wd_ref[...], preferred_element_type=F32)


def _l0_mix_ffn_kernel(*refs, tiles_per_seq, n_tiles, n_cast):
    (sink_ref, x_ref, meta_ref, xr_ref, gate_ref, q_ref, k_ref, v_ref, cw_ref, cb_ref, wg_ref,
     bg_ref, lam_ref, wo_ref, gain_ref, wgu_ref, wd_ref) = refs[:17]
    cast_in = refs[17:17 + n_cast]
    o_ref = refs[17 + n_cast]
    cast_out = refs[18 + n_cast:18 + 2 * n_cast]
    ybuf, xbuf, hcar, kprev, vprev, kmeta, vmeta = refs[18 + 2 * n_cast:]
    s = pl.program_id(0)
    tile_in_seq = jnp.minimum(s, n_tiles - 1) % tiles_per_seq
    blocks_per_tile = x_ref.shape[0] // BLOCK
    last = slice((blocks_per_tile - 1) * BLOCK, blocks_per_tile * BLOCK)
    _cast_blocks(cast_in + cast_out)

    @pl.when(s == 0)
    def _():
        ybuf[...] = jnp.zeros_like(ybuf)

    @pl.when(tile_in_seq == 0)
    def _():
        xbuf[0:SUBLANES, :] = jnp.zeros((SUBLANES, LRU_WIDTH), F32)
        hcar[...] = jnp.zeros_like(hcar)
        kmeta[...] = k_ref[0:BLOCK, :]
        vmeta[...] = v_ref[0:BLOCK, :]
        kprev[...] = k_ref[0:BLOCK, :]
        vprev[...] = v_ref[0:BLOCK, :]

    ffn_first = jnp.maximum(s - 1, 0) % tiles_per_seq == 0
    h1 = (_padded_rows(x_ref, meta_ref, ffn_first)
          + jnp.dot(ybuf[...], wo_ref[...], preferred_element_type=F32))
    xn = (h1 * _rms_scale(h1) * gain_ref[...]).astype(BF16)

    cw = cw_ref[...]
    cb = cb_ref[...]
    bg = bg_ref[...]
    z = -lam_ref[...]
    softplus = jnp.maximum(z, 0.0) + jnp.log1p(jnp.exp(-jnp.abs(z)))

    live = {}

    def block_rows(blk):
        return tile_in_seq * blocks_per_tile + blk, slice(blk * BLOCK, (blk + 1) * BLOCK)

    def lru_gates_step(blk, lo):
        _, rows = block_rows(blk)
        live["lru", blk, lo] = _lru_gates(lo, xr_ref, rows, cw, cb, wg_ref, xbuf)

    def lru_scan_step(blk, lo, sub):
        n, rows = block_rows(blk)
        xc, ga_r, ga_i = live["lru", blk, lo]
        part = slice(sub, sub + LANES)
        _lru_scan(n, xc[:, part], ga_r[:, part], ga_i[:, part],
                  slice(lo + sub, lo + sub + LANES), gate_ref, rows, bg, softplus, hcar, ybuf)
        if sub + LANES == GATE_TILE:
            del live["lru", blk, lo]

    def kv_blocks(ref, prev_ref, meta_ref, blk, g):
        _, rows = block_rows(blk)
        lanes = slice(g * ATT_HEAD_DIM, (g + 1) * ATT_HEAD_DIM)
        prev = prev_ref[:, lanes] if blk == 0 else ref[(blk - 1) * BLOCK:blk * BLOCK, lanes]
        return ref[rows, lanes], prev, meta_ref[:, lanes]

    def attn_probs_step(blk, g):
        n, rows = block_rows(blk)
        width = ATT_GROUP * ATT_HEAD_DIM
        sinks = [sink_ref[g * ATT_GROUP + a] for a in range(ATT_GROUP)]
        live["probs", blk, g] = _attn_probs(n, sinks, q_ref[rows, g * width:(g + 1) * width],
                                            *kv_blocks(k_ref, kprev, kmeta, blk, g))

    def attn_out_step(blk, g):
        _, rows = block_rows(blk)
        live["out", blk, g] = _attn_out_t(live.pop(("probs", blk, g)),
                                          *kv_blocks(v_ref, vprev, vmeta, blk, g))
        if g + 1 == ATT_KV_HEADS:
            ybuf[rows, LRU_WIDTH:LRU_WIDTH + Q_WIDTH] = _attn_untranspose(
                [live.pop(("out", blk, h)) for h in range(ATT_KV_HEADS)])

    acts = []

    def gate_step(c):
        lo = c * FFN_CHUNK
        live["g", c] = jnp.dot(xn, wgu_ref[:, lo:lo + FFN_CHUNK], preferred_element_type=F32)

    def up_step(c):
        lo = c * FFN_CHUNK
        u = jnp.dot(xn, wgu_ref[:, D_FF + lo:D_FF + lo + FFN_CHUNK], preferred_element_type=F32)
        acts.append((_silu(live.pop(("g", c))) * u).astype(BF16))

    def down_step(c):
        cols = slice(c * DOWN_CHUNK, (c + 1) * DOWN_CHUNK)
        if len(acts) > 1:
            acts[:] = [jnp.concatenate(acts, axis=1)]
        o_ref[:, cols] = h1[:, cols] + jnp.dot(acts[0], wd_ref[:, cols],
                                               preferred_element_type=F32)

    ffn_steps = []
    for c in range(D_FF // FFN_CHUNK):
        ffn_steps += [functools.partial(gate_step, c), functools.partial(up_step, c)]
    ffn_steps += [functools.partial(down_step, c) for c in range(D_MODEL // DOWN_CHUNK)]
    mix_steps = []
    for blk in range(blocks_per_tile):
        for lo in range(0, LRU_WIDTH, GATE_TILE):
            mix_steps.append(functools.partial(lru_gates_step, blk, lo))
            mix_steps += [functools.partial(lru_scan_step, blk, lo, sub)
                          for sub in range(0, GATE_TILE, LANES)]
        mix_steps += [functools.partial(attn_probs_step, blk, g) for g in range(ATT_KV_HEADS)]
        mix_steps += [functools.partial(attn_out_step, blk, g) for g in range(ATT_KV_HEADS)]
    for step in _interleave(ffn_steps, mix_steps):
        step()
    kprev[...] = k_ref[last, :]
    vprev[...] = v_ref[last, :]


def _l0_mix_ffn(sinks, x2d, meta_blk, xr, gate, q, k, v, conv_w, conv_b, w_gates, b_gates, lam,
                w_out, gain, w_gu, w_down, cast_weights, tiles_per_seq, seq):
    rows = xr.shape[0]
    tm = ROW_TILE
    n_tiles = rows // tm
    mix = lambda w: pl.BlockSpec((tm, w), lambda s: (jnp.minimum(s, n_tiles - 1), 0))
    cast_in, cast_out, cast_shapes = _cast_specs(cast_weights)
    return pl.pallas_call(
        functools.partial(_l0_mix_ffn_kernel, tiles_per_seq=tiles_per_seq, n_tiles=n_tiles,
                          n_cast=len(cast_in)),
        grid=(n_tiles + 1,),
        in_specs=[pl.BlockSpec(memory_space=pltpu.SMEM),
                  _token_window_spec(tm, tiles_per_seq, seq, lambda s: jnp.maximum(s - 1, 0)),
                  _const_spec((BLOCK, D_MODEL)),
                  mix(LRU_WIDTH), mix(LRU_WIDTH), mix(Q_WIDTH), mix(KV_WIDTH), mix(KV_WIDTH),
                  _const_spec((CONV_W, LRU_WIDTH)), _const_spec((1, LRU_WIDTH)),
                  _const_spec((LRU_WIDTH, 2 * LRU_WIDTH)), _const_spec((1, 2 * LRU_WIDTH)),
                  _const_spec((1, LRU_WIDTH)), _const_spec(w_out.shape), _const_spec((1, D_MODEL)),
                  _const_spec(w_gu.shape), _const_spec(w_down.shape)] + cast_in,
        out_specs=[pl.BlockSpec((tm, D_MODEL), lambda s: (jnp.maximum(s - 1, 0), 0))] + cast_out,
        out_shape=[jax.ShapeDtypeStruct((rows, D_MODEL), F32)] + cast_shapes,
        scratch_shapes=[pltpu.VMEM((tm, LRU_WIDTH + Q_WIDTH), BF16),
                        pltpu.VMEM((SUBLANES + BLOCK, LRU_WIDTH), F32),
                        pltpu.VMEM((SUBLANES, LRU_WIDTH), F32),
                        pltpu.VMEM((BLOCK, KV_WIDTH), BF16), pltpu.VMEM((BLOCK, KV_WIDTH), BF16),
                        pltpu.VMEM((BLOCK, KV_WIDTH), BF16), pltpu.VMEM((BLOCK, KV_WIDTH), BF16)],
        compiler_params=pltpu.CompilerParams(dimension_semantics=("arbitrary",)),
        name="l0_mix_ffn",
    )(sinks, x2d, meta_blk, xr, gate, q, k, v, conv_w, conv_b, w_gates, b_gates, lam, w_out, gain,
      w_gu, w_down, *[w for _, w in cast_weights])


def _outproj_ffn_kernel(h_ref, y_ref, wo_ref, gain_ref, wgu_ref, wd_ref, o_ref):
    h1 = h_ref[...] + jnp.dot(y_ref[...], wo_ref[...], preferred_element_type=F32)
    o_ref[...] = _ffn(h1, gain_ref[...], wgu_ref, wd_ref)


def _final_outproj_ffn(h, y, w_out, gain, w_gu, w_down, batch, seq_rows, seq):
    tm = OUT_TILE
    tiles = seq // tm
    win = lambda w: pl.BlockSpec((pl.Element(tm), pl.Element(w)),
                                 lambda b, j: (pl.multiple_of(b * seq_rows + BLOCK + j * tm, BLOCK),
                                               0))
    return pl.pallas_call(
        _outproj_ffn_kernel,
        grid=(batch, tiles),
        in_specs=[win(D_MODEL), win(y.shape[1]), _const_spec(w_out.shape),
                  _const_spec((1, D_MODEL)), _const_spec(w_gu.shape), _const_spec(w_down.shape)],
        out_specs=pl.BlockSpec((tm, D_MODEL), lambda b, j: (b * tiles + j, 0)),
        out_shape=jax.ShapeDtypeStruct((batch * seq, D_MODEL), F32),
        compiler_params=pltpu.CompilerParams(dimension_semantics=("arbitrary", "arbitrary")),
        name="l1_outproj_ffn",
    )(h, y, w_out, gain, w_gu, w_down)


_Q0, _K0, _V0, _G0 = 0, D_MODEL, 2 * D_MODEL, 4 * D_MODEL


def _retention_decays(hd):
    ii = lax.broadcasted_iota(jnp.int32, (BLOCK, BLOCK), 0)
    jj = lax.broadcasted_iota(jnp.int32, (BLOCK, BLOCK), 1)
    diff = (ii - jj).astype(F32)
    idx = lax.broadcasted_iota(jnp.int32, (BLOCK, 1), 0).astype(F32)
    log_g = RET_LOG_G[hd]
    decay_intra = jnp.where(diff >= 0.0, jnp.exp(jnp.maximum(diff, 0.0) * log_g), 0.0)
    return (decay_intra, jnp.exp((idx + 1.0) * log_g), jnp.exp((BLOCK - 1.0 - idx) * log_g),
            math.exp(BLOCK * log_g))


def _l1_mix_kernel(*refs, tiles_per_seq, n_tiles, n_cast):
    h_ref, gain_ref, w_ref, freq_ref = refs[:4]
    cast_in = refs[4:4 + n_cast]
    y_ref = refs[4 + n_cast]
    cast_out = refs[5 + n_cast:5 + 2 * n_cast]
    qkvg, state, ocos, osin = refs[5 + 2 * n_cast:]
    _cast_blocks(cast_in + cast_out)
    s = pl.program_id(0)
    proj_slot = s % 2
    ret_slot = 1 - proj_slot
    proj_tile_in_seq = jnp.minimum(s, n_tiles - 1) % tiles_per_seq
    ret_tile_in_seq = jnp.maximum(s - 1, 0) % tiles_per_seq
    tm = h_ref.shape[0]

    @pl.when(s == 0)
    def _():
        _rope_offsets(freq_ref, ocos, osin)
        qkvg[1] = jnp.zeros(qkvg.shape[1:], BF16)

    @pl.when(ret_tile_in_seq == 0)
    def _():
        state[...] = jnp.zeros_like(state)

    h = h_ref[...]
    xn = (h * _rms_scale(h) * gain_ref[...]).astype(BF16)
    cos, sin = _rope_tables(proj_tile_in_seq * tm - META_PAD, freq_ref, ocos, osin)
    half = RET_QK_DIM // 2

    def rope_step(col0, hd, scale):
        lo = col0 + hd * RET_QK_DIM
        y = jnp.dot(xn, w_ref[:, lo:lo + RET_QK_DIM], preferred_element_type=F32)
        x1 = y[:, :half]
        x2 = y[:, half:]
        qkvg[proj_slot, :, lo:lo + half] = ((x1 * cos - x2 * sin) * scale).astype(BF16)
        qkvg[proj_slot, :, lo + half:lo + RET_QK_DIM] = ((x2 * cos + x1 * sin)
                                                         * scale).astype(BF16)

    def value_step(lo, act):
        y = jnp.dot(xn, w_ref[:, lo:lo + PROJ_CHUNK], preferred_element_type=F32)
        qkvg[proj_slot, :, lo:lo + PROJ_CHUNK] = act(y).astype(BF16)

    def ret_operand(c, hd, col0, width):
        return qkvg[ret_slot, c * BLOCK:(c + 1) * BLOCK, col0 + hd * width:col0 + (hd + 1) * width]

    heads = range(RET_HEADS)
    decays = [_retention_decays(hd) for hd in heads]
    live = {}

    def prep_step(c, hd):
        live["kdt", c, hd] = (ret_operand(c, hd, _K0, RET_QK_DIM).astype(F32)
                              * decays[hd][2]).T.astype(BF16)

    def scores_step(c, hd):
        live["qk", c, hd] = lax.dot_general(
            ret_operand(c, hd, _Q0, RET_QK_DIM), ret_operand(c, hd, _K0, RET_QK_DIM),
            (((1,), (1,)), ((), ())), preferred_element_type=F32)
        live["kv", c, hd] = jnp.dot(live.pop(("kdt", c, hd)), ret_operand(c, hd, _V0, RET_V_DIM),
                                    preferred_element_type=F32)

    def decay_step(c, hd):
        live["qkd", c, hd] = (live.pop(("qk", c, hd)) * decays[hd][0]).astype(BF16)
        live["stb", c, hd] = state[hd].astype(BF16)

    def output_step(c, hd):
        live["o", c, hd] = (
            jnp.dot(live.pop(("qkd", c, hd)), ret_operand(c, hd, _V0, RET_V_DIM),
                    preferred_element_type=F32)
            + jnp.dot(ret_operand(c, hd, _Q0, RET_QK_DIM), live.pop(("stb", c, hd)),
                      preferred_element_type=F32) * decays[hd][1])

    def post_step(c, hd):
        o = live.pop(("o", c, hd))
        state[hd] = decays[hd][3] * state[hd] + live.pop(("kv", c, hd))
        gate = ret_operand(c, hd, _G0, RET_V_DIM).astype(F32)
        y_ref[c * BLOCK:(c + 1) * BLOCK, hd * RET_V_DIM:(hd + 1) * RET_V_DIM] = (
            o * _rms_scale(o) * gate).astype(BF16)

    def both(first, second):
        def step():
            first()
            second()
        return step

    proj_steps = []
    for hd in heads:
        proj_steps.append(both(functools.partial(rope_step, _Q0, hd, 1.0),
                               functools.partial(rope_step, _K0, hd, RET_QK_DIM ** -0.5)))
        for col0, act in ((_V0, lambda y: y), (_G0, _silu)):
            lo = col0 + hd * RET_V_DIM
            proj_steps.append(both(functools.partial(value_step, lo, act),
                                   functools.partial(value_step, lo + PROJ_CHUNK, act)))
    n_chunks = tm // BLOCK
    each_head = lambda step, c: [functools.partial(step, c, hd) for hd in heads]
    ret_steps = each_head(prep_step, 0)
    for c in range(n_chunks):
        ret_steps += each_head(scores_step, c)
        for hd in heads:
            ret_steps.append(functools.partial(decay_step, c, hd))
            if c + 1 < n_chunks:
                ret_steps.append(functools.partial(prep_step, c + 1, hd))
        ret_steps += each_head(output_step, c) + each_head(post_step, c)
    for step in _interleave(proj_steps, ret_steps):
        step()


def _l1_mix(h, gain, w_in, freq, cast_weights, tiles_per_seq):
    rows = h.shape[0]
    tm = ROW_TILE
    n_tiles = rows // tm
    half = RET_QK_DIM // 2
    cast_in, cast_out, cast_shapes = _cast_specs(cast_weights)
    return pl.pallas_call(
        functools.partial(_l1_mix_kernel, tiles_per_seq=tiles_per_seq, n_tiles=n_tiles,
                          n_cast=len(cast_in)),
        grid=(n_tiles + 1,),
        in_specs=[pl.BlockSpec((tm, D_MODEL), lambda s: (jnp.minimum(s, n_tiles - 1), 0)),
                  _const_spec((1, D_MODEL)), _const_spec(w_in.shape), _const_spec((1, half))]
        + cast_in,
        out_specs=[pl.BlockSpec((tm, 2 * D_MODEL), lambda s: (jnp.maximum(s - 1, 0), 0))]
        + cast_out,
        out_shape=[jax.ShapeDtypeStruct((rows, 2 * D_MODEL), BF16)] + cast_shapes,
        scratch_shapes=[pltpu.VMEM((2, tm, 6 * D_MODEL), BF16),
                        pltpu.VMEM((RET_HEADS, RET_QK_DIM, RET_V_DIM), F32),
                        pltpu.VMEM((tm, half), F32), pltpu.VMEM((tm, half), F32)],
        compiler_params=pltpu.CompilerParams(dimension_semantics=("arbitrary",)),
        name="l1_mix",
    )(h, gain, w_in, freq, *[w for _, w in cast_weights])


def _inv_freq(half, theta):
    return jnp.power(jnp.asarray(theta, F32), -jnp.arange(half, dtype=F32) / half)


def _block_diag(w):
    heads, wi, wo = w.shape
    eye = jnp.eye(heads, dtype=w.dtype)
    return (eye[:, None, :, None] * w[:, :, None, :]).reshape(heads * wi, heads * wo)


def kernel(x, meta_tokens, mix_norm_ab, ab_w_in, lru_conv_w, lru_conv_b, lru_w_a, lru_b_a, lru_w_i, lru_b_i, lru_lambda, q_norm, k_norm, attn_sinks, ab_w_out, mix_norm_ret, ret_w_in, ret_w_out, ffn_norm, ffn_w_gu, ffn_w_down):
    batch, seq, _ = x.shape
    seq_rows = META_PAD + N_META + seq
    tiles_per_seq = seq_rows // ROW_TILE

    x2d = x.reshape(batch * seq, D_MODEL)
    meta_blk = jnp.concatenate([jnp.zeros((META_PAD, D_MODEL), x.dtype),
                                meta_tokens.astype(x.dtype)], axis=0)

    f_att = _inv_freq(ROT_DIM // 2, ROPE_THETA)
    f_att = jnp.concatenate([f_att, f_att, jnp.zeros((ATT_HEAD_DIM - ROT_DIM,), F32)])
    f_att = jnp.tile(f_att, LANES // ATT_HEAD_DIM).reshape(1, LANES)
    f_ret = _inv_freq(RET_QK_DIM // 2, RET_THETA).reshape(1, RET_QK_DIM // 2)

    row_vec = lambda v: v.reshape(1, -1).astype(F32)
    two_heads = lambda v: jnp.tile(v.reshape(1, -1).astype(F32), (1, 2))

    xr, gate, q, k, v, w_gu0, w_down0 = _inproj0(
        x2d, meta_blk, row_vec(mix_norm_ab[0]), ab_w_in[0].astype(BF16), two_heads(q_norm[0]),
        two_heads(k_norm[0]), f_att, [(0, ffn_w_gu), (0, ffn_w_down)], batch, tiles_per_seq, seq)
    w_gates = jnp.concatenate([_block_diag(lru_w_a[0]), _block_diag(lru_w_i[0])], axis=1)
    b_gates = jnp.concatenate([lru_b_a[0].reshape(1, -1), lru_b_i[0].reshape(1, -1)], axis=1)
    w_att = ab_w_out[0][LRU_WIDTH:].reshape(ATT_KV_HEADS, ATT_GROUP, ATT_HEAD_DIM, D_MODEL)
    w_att = w_att.transpose(1, 0, 2, 3).reshape(Q_WIDTH, D_MODEL)
    w_out0 = jnp.concatenate([ab_w_out[0][:LRU_WIDTH], w_att], axis=0).astype(BF16)
    h, w_in1, w_out1 = _l0_mix_ffn(
        attn_sinks[0].astype(F32), x2d, meta_blk, xr, gate, q, k, v, lru_conv_w[0],
        row_vec(lru_conv_b[0]), w_gates.astype(BF16), b_gates.astype(F32), row_vec(lru_lambda[0]),
        w_out0, row_vec(ffn_norm[0]), w_gu0, w_down0, [(0, ret_w_in), (0, ret_w_out)],
        tiles_per_seq, seq)

    y_ret, w_gu1, w_down1 = _l1_mix(h, row_vec(mix_norm_ret[0]), w_in1, f_ret,
                                    [(1, ffn_w_gu), (1, ffn_w_down)], tiles_per_seq)
    out = _final_outproj_ffn(h, y_ret, w_out1, row_vec(ffn_norm[1]), w_gu1, w_down1,
                             batch, seq_rows, seq)
    return out.reshape(batch, seq, D_MODEL)
```

```python
import functools
import itertools
import math

import jax
import jax.numpy as jnp
from jax import lax
from jax.experimental import pallas as pl
from jax.experimental.pallas import tpu as pltpu

F32 = jnp.float32
BF16 = jnp.bfloat16

D_MODEL = 1024
N_META = 16
BLOCK = 128
META_PAD = BLOCK - N_META
RMS_EPS = 1e-6
NEG_INF = -1e30

LRU_WIDTH = 512
LRU_HEADS = 8
LRU_BLOCK_W = 64
CONV_W = 4
LRU_C = 8.0

ATT_HEADS = 8
ATT_KV_HEADS = 2
ATT_GROUP = ATT_HEADS // ATT_KV_HEADS
ATT_HEAD_DIM = 64
ROPE_THETA = 500000.0
ROT_DIM = 16
Q_WIDTH = 512
KV_WIDTH = 128
AB_IN_WIDTH = 2 * LRU_WIDTH + Q_WIDTH + 2 * KV_WIDTH

RET_HEADS = 4
RET_QK_DIM = 256
RET_V_DIM = 512
RET_THETA = 10000.0
RET_LOG_G = tuple(math.log1p(-(2.0 ** (-5.0 - h))) for h in range(RET_HEADS))

D_FF = 2816

LANES = 128
SUBLANES = 8
ROW_TILE = 640
OUT_TILE = 512
SQRT_GUARD = 1e-30
GATE_TILE = 256
INPROJ_CHUNK = 256
PROJ_CHUNK = 256
CAST_STEPS = 16
FFN_CHUNK = 256
DOWN_CHUNK = 256


def _rms_scale(x):
    return lax.rsqrt(jnp.mean(x * x, axis=-1, keepdims=True) + RMS_EPS)


def _sigmoid(x):
    return 0.5 * jnp.tanh(0.5 * x) + 0.5


def _silu(x):
    half = 0.5 * x
    return half + half * jnp.tanh(half)


def _gelu_tanh(x):
    half = 0.5 * x
    return half + half * jnp.tanh(0.7978845608028654 * (x + 0.044715 * (x * x * x)))


def _interleave(primary, secondary):
    out, done = [], 0
    for i, step in enumerate(primary):
        out.append(step)
        upto = ((i + 1) * len(secondary)) // len(primary)
        out.extend(secondary[done:upto])
        done = upto
    return out + list(secondary[done:])


def _const_spec(shape):
    zeros = (0,) * len(shape)
    return pl.BlockSpec(shape, lambda *_: zeros, pipeline_mode=pl.Buffered(1))


def _padded_rows(x_ref, meta_ref, first):
    xw = x_ref[...]
    tm = xw.shape[0]
    return jnp.concatenate([jnp.where(first, meta_ref[...], xw[0:BLOCK]),
                            jnp.where(first, xw[0:tm - BLOCK], xw[BLOCK:tm])], axis=0)


def _token_window_spec(tm, tiles_per_seq, seq, tile_of_step):
    def index(s):
        t = tile_of_step(s)
        start = jnp.maximum((t % tiles_per_seq) * tm - BLOCK, 0)
        return (pl.multiple_of((t // tiles_per_seq) * seq + start, BLOCK), 0)
    return pl.BlockSpec((pl.Element(tm), pl.Element(D_MODEL)), index)


def _rope_offsets(freq_ref, cos_ref, sin_ref):
    r = lax.broadcasted_iota(jnp.int32, cos_ref.shape, 0).astype(F32)
    ang = r * freq_ref[...]
    cos_ref[...] = jnp.cos(ang)
    sin_ref[...] = jnp.sin(ang)


def _rope_tables(base_pos, freq_ref, cos_ref, sin_ref):
    ang = base_pos.astype(F32) * freq_ref[...]
    cb = jnp.cos(ang)
    sb = jnp.sin(ang)
    oc = cos_ref[...]
    os_ = sin_ref[...]
    return cb * oc - sb * os_, sb * oc + cb * os_


def _cast_blocks(cast_refs):
    n = len(cast_refs) // 2
    for src, dst in zip(cast_refs[:n], cast_refs[n:]):
        dst[...] = src[...].astype(BF16)


def _cast_specs(weights):
    ins, outs, shapes = [], [], []
    step = lambda s: jnp.minimum(s, CAST_STEPS - 1)
    for layer, w in weights:
        _, rows, cols = w.shape
        blk = rows // CAST_STEPS
        ins.append(pl.BlockSpec((None, blk, cols), lambda s, layer=layer: (layer, step(s), 0)))
        outs.append(pl.BlockSpec((blk, cols), lambda s: (step(s), 0)))
        shapes.append(jax.ShapeDtypeStruct((rows, cols), BF16))
    return ins, outs, shapes


def _inproj0_kernel(*refs, tiles_per_seq, n_cast):
    (x_ref, meta_ref, gain_ref, w_ref, qg_ref, kg_ref, freq_ref) = refs[:7]
    cast_in = refs[7:7 + n_cast]
    xr_ref, gate_ref, q_ref, k_ref, v_ref = refs[7 + n_cast:12 + n_cast]
    cast_out = refs[12 + n_cast:12 + 2 * n_cast]
    ocos, osin = refs[12 + 2 * n_cast:]
    step = pl.program_id(0)
    tile_in_seq = step % tiles_per_seq

    @pl.when(step == 0)
    def _():
        _rope_offsets(freq_ref, ocos, osin)

    _cast_blocks(cast_in + cast_out)

    h = _padded_rows(x_ref, meta_ref, tile_in_seq == 0)
    rows = h.shape[0]
    xn = (h * _rms_scale(h) * gain_ref[...]).astype(BF16)

    q0 = 2 * LRU_WIDTH
    k0 = q0 + Q_WIDTH
    chunk_dot = lambda lo: jnp.dot(xn, w_ref[:, lo:lo + INPROJ_CHUNK], preferred_element_type=F32)
    y_q = [chunk_dot(q0 + lo) for lo in range(0, Q_WIDTH, INPROJ_CHUNK)]
    y_kv = chunk_dot(k0)
    v_ref[...] = y_kv[:, KV_WIDTH:].astype(BF16)

    cos, sin = _rope_tables(tile_in_seq * rows - META_PAD, freq_ref, ocos, osin)
    lane = lax.broadcasted_iota(jnp.int32, (1, LANES), 1) & (ATT_HEAD_DIM - 1)
    half = ROT_DIM // 2
    sin_lo = sin * jnp.where(lane < half, -1.0, 0.0)
    sin_hi = sin * jnp.where((lane >= half) & (lane < ROT_DIM), 1.0, 0.0)
    hi = lax.broadcasted_iota(jnp.int32, (2 * LANES, LANES), 0) & (LANES - 1)
    hj = lax.broadcasted_iota(jnp.int32, (2 * LANES, LANES), 1)
    head_mean = jnp.where(hi // ATT_HEAD_DIM == hj // ATT_HEAD_DIM,
                          1.0 / ATT_HEAD_DIM, 0.0).astype(BF16)

    def norm_rope(x, gain, scale):
        sq = x * x
        sq_hi = sq.astype(BF16)
        sq_lo = (sq - sq_hi.astype(F32)).astype(BF16)
        ms = jnp.dot(jnp.concatenate([sq_hi, sq_lo], axis=1), head_mean,
                     preferred_element_type=F32)
        xg = x * gain
        rot = (xg * cos + pltpu.roll(xg, LANES - half, 1) * sin_lo
               + pltpu.roll(xg, half, 1) * sin_hi)
        return rot * (lax.rsqrt(ms + RMS_EPS) * scale)

    def qk_step(j):
        if j < Q_WIDTH // LANES:
            lo = (j * LANES) % INPROJ_CHUNK
            q_ref[:, j * LANES:(j + 1) * LANES] = norm_rope(
                y_q[j * LANES // INPROJ_CHUNK][:, lo:lo + LANES], qg_ref[...],
                ATT_HEAD_DIM ** -0.5).astype(BF16)
        else:
            k_ref[...] = norm_rope(y_kv[:, :KV_WIDTH], kg_ref[...], 1.0).astype(BF16)

    def proj_step(c):
        out_ref = xr_ref if c < LRU_WIDTH // INPROJ_CHUNK else gate_ref
        dst = slice((c * INPROJ_CHUNK) % LRU_WIDTH, (c * INPROJ_CHUNK) % LRU_WIDTH + INPROJ_CHUNK)
        out_ref[:, dst] = chunk_dot(c * INPROJ_CHUNK)

    proj_steps = [functools.partial(proj_step, c) for c in range(2 * LRU_WIDTH // INPROJ_CHUNK)]
    qk_steps = [functools.partial(qk_step, j) for j in range(Q_WIDTH // LANES + 1)]
    for step in _interleave(proj_steps, qk_steps):
        step()


def _inproj0(x2d, meta_blk, gain, w_in, q_gain, k_gain, freq, cast_weights, batch,
             tiles_per_seq, seq):
    tm = ROW_TILE
    rows = batch * tiles_per_seq * tm
    row = lambda w: pl.BlockSpec((tm, w), lambda i: (i, 0))
    cast_in, cast_out, cast_shapes = _cast_specs(cast_weights)
    return pl.pallas_call(
        functools.partial(_inproj0_kernel, tiles_per_seq=tiles_per_seq, n_cast=len(cast_in)),
        grid=(rows // tm,),
        in_specs=[_token_window_spec(tm, tiles_per_seq, seq, lambda s: s),
                  _const_spec((BLOCK, D_MODEL)), _const_spec((1, D_MODEL)),
                  _const_spec((D_MODEL, AB_IN_WIDTH)), _const_spec((1, LANES)),
                  _const_spec((1, LANES)), _const_spec((1, LANES))] + cast_in,
        out_specs=[row(LRU_WIDTH), row(LRU_WIDTH), row(Q_WIDTH), row(KV_WIDTH), row(KV_WIDTH)]
        + cast_out,
        out_shape=[jax.ShapeDtypeStruct((rows, LRU_WIDTH), F32),
                   jax.ShapeDtypeStruct((rows, LRU_WIDTH), F32),
                   jax.ShapeDtypeStruct((rows, Q_WIDTH), BF16),
                   jax.ShapeDtypeStruct((rows, KV_WIDTH), BF16),
                   jax.ShapeDtypeStruct((rows, KV_WIDTH), BF16)] + cast_shapes,
        scratch_shapes=[pltpu.VMEM((tm, LANES), F32), pltpu.VMEM((tm, LANES), F32)],
        compiler_params=pltpu.CompilerParams(dimension_semantics=("arbitrary",)),
        name="l0_inproj",
    )(x2d, meta_blk, gain, w_in, q_gain, k_gain, freq, *[w for _, w in cast_weights])


def _attn_probs(n, sinks, q, kc, kp, km):
    j = lax.broadcasted_iota(jnp.int32, (BLOCK, BLOCK), 0)
    i = lax.broadcasted_iota(jnp.int32, (BLOCK, BLOCK), 1)
    causal = j <= i
    win_ok = n >= jnp.where(causal, 1, 2)
    meta_ok = (j >= META_PAD) & (n >= jnp.where(causal, 0, 1))
    contract_last = (((1,), (1,)), ((), ()))

    qs = jnp.concatenate([q[:, a * ATT_HEAD_DIM:(a + 1) * ATT_HEAD_DIM]
                          for a in range(ATT_GROUP)], axis=0)
    s_c = lax.dot_general(kc, qs, contract_last, preferred_element_type=F32)
    s_p = lax.dot_general(kp, qs, contract_last, preferred_element_type=F32)
    s_m = lax.dot_general(km, qs, contract_last, preferred_element_type=F32)
    p_c, p_p, p_m, inv_den = [], [], [], []
    for a in range(ATT_GROUP):
        head = slice(a * BLOCK, (a + 1) * BLOCK)
        sw = jnp.where(win_ok, jnp.where(causal, s_c[:, head], s_p[:, head]), NEG_INF)
        sm = jnp.where(meta_ok, s_m[:, head], NEG_INF)
        m = jnp.maximum(jnp.maximum(jnp.max(sw, axis=0, keepdims=True),
                                    jnp.max(sm, axis=0, keepdims=True)), sinks[a])
        pw = jnp.exp(sw - m)
        pm = jnp.exp(sm - m)
        den = (jnp.sum(pw, axis=0, keepdims=True) + jnp.sum(pm, axis=0, keepdims=True)
               + jnp.exp(sinks[a] - m))
        inv_den.append(1.0 / den)
        p_c.append(jnp.where(causal, pw, 0.0).astype(BF16))
        p_p.append(jnp.where(causal, 0.0, pw).astype(BF16))
        p_m.append(pm.astype(BF16))
    lanes = lambda parts: jnp.concatenate(parts, axis=1)
    return lanes(p_c), lanes(p_p), lanes(p_m), lanes(inv_den)


def _attn_out_t(probs, vc, vp, vm):
    p_c, p_p, p_m, inv_den = probs
    contract_rows = (((0,), (0,)), ((), ()))
    o_t = (lax.dot_general(vc, p_c, contract_rows, preferred_element_type=F32)
           + lax.dot_general(vp, p_p, contract_rows, preferred_element_type=F32)
           + lax.dot_general(vm, p_m, contract_rows, preferred_element_type=F32))
    return o_t * inv_den


def _attn_untranspose(o_t):
    o_t = jnp.concatenate(o_t, axis=0)
    return jnp.concatenate([o_t[:, a * BLOCK:(a + 1) * BLOCK].T for a in range(ATT_GROUP)],
                           axis=1).astype(BF16)


def _lru_gates(lo, xr_ref, rows, cw, cb, wg_ref, xbuf):
    tl = BLOCK
    cols = slice(lo, lo + GATE_TILE)
    x = xr_ref[rows, cols]
    xbuf[SUBLANES:SUBLANES + tl, cols] = x
    xc = x * cw[CONV_W - 1:CONV_W, cols] + cb[:, cols]
    for d in range(1, CONV_W):
        xc = xc + (xbuf[SUBLANES - d:SUBLANES - d + tl, cols]
                   * cw[CONV_W - 1 - d:CONV_W - d, cols])
    xbuf[0:SUBLANES, cols] = x[tl - SUBLANES:tl]
    xcb = xc.astype(BF16)
    ga_r = jnp.dot(xcb, wg_ref[cols, cols], preferred_element_type=F32)
    ga_i = jnp.dot(xcb, wg_ref[cols, LRU_WIDTH + lo:LRU_WIDTH + lo + GATE_TILE],
                   preferred_element_type=F32)
    return xc, ga_r, ga_i


def _lru_scan(n, xc, ga_r, ga_i, grp, gate_ref, rows, bg, softplus, hcar, y_ref):
    tl = BLOCK
    row = lax.broadcasted_iota(jnp.int32, (tl, 1), 0)
    t = n * tl + row
    r = _sigmoid(ga_r + bg[:, grp])
    gi = _sigmoid(ga_i + bg[:, LRU_WIDTH + grp.start:LRU_WIDTH + grp.stop])
    log_a = (-LRU_C * softplus[:, grp]) * r
    a = jnp.exp(log_a)
    mult2 = jnp.tanh(-log_a) * (a * a + 1.0)
    mult = mult2 * lax.rsqrt(jnp.maximum(mult2, SQRT_GUARD))
    mult = jnp.where(t == META_PAD, 1.0, mult)
    b = jnp.where(t < META_PAD, 0.0, mult * gi * xc)

    d = 1
    while d < SUBLANES:
        keep = row >= d
        b = jnp.where(keep, a * pltpu.roll(b, d, 0), 0.0) + b
        a = jnp.where(keep, a * pltpu.roll(a, d, 0), a)
        d *= 2
    while d < tl:
        b = jnp.concatenate([b[:d], a[d:] * b[:tl - d] + b[d:]], axis=0)
        a = jnp.concatenate([a[:d], a[d:] * a[:tl - d]], axis=0)
        d *= 2
    h = b + a * hcar[0:1, grp]
    hcar[:, grp] = jnp.broadcast_to(h[tl - 1:tl], (SUBLANES, LANES))
    y_ref[rows, grp] = (_gelu_tanh(gate_ref[rows, grp]) * h).astype(BF16)


def _ffn(h1, gain, wgu_ref, wd_ref):
    xn = (h1 * _rms_scale(h1) * gain).astype(BF16)
    acts = []
    for c in range(D_FF // FFN_CHUNK):
        lo = c * FFN_CHUNK
        g = jnp.dot(xn, wgu_ref[:, lo:lo + FFN_CHUNK], preferred_element_type=F32)
        u = jnp.dot(xn, wgu_ref[:, D_FF + lo:D_FF + lo + FFN_CHUNK], preferred_element_type=F32)
        acts.append((_silu(g) * u).astype(BF16))
    act = jnp.concatenate(acts, axis=1)
    return h1 + jnp.dot(act, wd_ref[...], preferred_element_type=F32)


def _l0_mix_ffn_kernel(*refs, tiles_per_seq, n_tiles, n_cast):
    (sink_ref, x_ref, meta_ref, xr_ref, gate_ref, q_ref, k_ref, v_ref, cw_ref, cb_ref, wg_ref,
     bg_ref, lam_ref, wo_ref, gain_ref, wgu_ref, wd_ref) = refs[:17]
    cast_in = refs[17:17 + n_cast]
    o_ref = refs[17 + n_cast]
    cast_out = refs[18 + n_cast:18 + 2 * n_cast]
    ybuf, xbuf, hcar, kprev, vprev, kmeta, vmeta = refs[18 + 2 * n_cast:]
    s = pl.program_id(0)
    tile_in_seq = jnp.minimum(s, n_tiles - 1) % tiles_per_seq
    blocks_per_tile = x_ref.shape[0] // BLOCK
    last = slice((blocks_per_tile - 1) * BLOCK, blocks_per_tile * BLOCK)
    _cast_blocks(cast_in + cast_out)

    @pl.when(s == 0)
    def _():
        ybuf[...] = jnp.zeros_like(ybuf)

    @pl.when(tile_in_seq == 0)
    def _():
        xbuf[0:SUBLANES, :] = jnp.zeros((SUBLANES, LRU_WIDTH), F32)
        hcar[...] = jnp.zeros_like(hcar)
        kmeta[...] = k_ref[0:BLOCK, :]
        vmeta[...] = v_ref[0:BLOCK, :]
        kprev[...] = k_ref[0:BLOCK, :]
        vprev[...] = v_ref[0:BLOCK, :]

    ffn_first = jnp.maximum(s - 1, 0) % tiles_per_seq == 0
    h1 = (_padded_rows(x_ref, meta_ref, ffn_first)
          + jnp.dot(ybuf[...], wo_ref[...], preferred_element_type=F32))
    xn = (h1 * _rms_scale(h1) * gain_ref[...]).astype(BF16)

    cw = cw_ref[...]
    cb = cb_ref[...]
    bg = bg_ref[...]
    z = -lam_ref[...]
    softplus = jnp.maximum(z, 0.0) + jnp.log1p(jnp.exp(-jnp.abs(z)))

    live = {}

    def block_rows(blk):
        return tile_in_seq * blocks_per_tile + blk, slice(blk * BLOCK, (blk + 1) * BLOCK)

    def lru_gates_step(blk, lo):
        _, rows = block_rows(blk)
        live["lru", blk, lo] = _lru_gates(lo, xr_ref, rows, cw, cb, wg_ref, xbuf)

    def lru_scan_step(blk, lo, sub):
        n, rows = block_rows(blk)
        xc, ga_r, ga_i = live["lru", blk, lo]
        part = slice(sub, sub + LANES)
        _lru_scan(n, xc[:, part], ga_r[:, part], ga_i[:, part],
                  slice(lo + sub, lo + sub + LANES), gate_ref, rows, bg, softplus, hcar, ybuf)
        if sub + LANES == GATE_TILE:
            del live["lru", blk, lo]

    def kv_blocks(ref, prev_ref, meta_ref, blk, g):
        _, rows = block_rows(blk)
        lanes = slice(g * ATT_HEAD_DIM, (g + 1) * ATT_HEAD_DIM)
        prev = prev_ref[:, lanes] if blk == 0 else ref[(blk - 1) * BLOCK:blk * BLOCK, lanes]
        return ref[rows, lanes], prev, meta_ref[:, lanes]

    def attn_probs_step(blk, g):
        n, rows = block_rows(blk)
        width = ATT_GROUP * ATT_HEAD_DIM
        sinks = [sink_ref[g * ATT_GROUP + a] for a in range(ATT_GROUP)]
        live["probs", blk, g] = _attn_probs(n, sinks, q_ref[rows, g * width:(g + 1) * width],
                                            *kv_blocks(k_ref, kprev, kmeta, blk, g))

    def attn_out_step(blk, g):
        _, rows = block_rows(blk)
        live["out", blk, g] = _attn_out_t(live.pop(("probs", blk, g)),
                                          *kv_blocks(v_ref, vprev, vmeta, blk, g))
        if g + 1 == ATT_KV_HEADS:
            ybuf[rows, LRU_WIDTH:LRU_WIDTH + Q_WIDTH] = _attn_untranspose(
                [live.pop(("out", blk, h)) for h in range(ATT_KV_HEADS)])

    acts = []

    def gate_step(c):
        lo = c * FFN_CHUNK
        live["g", c] = jnp.dot(xn, wgu_ref[:, lo:lo + FFN_CHUNK], preferred_element_type=F32)

    def up_step(c):
        lo = c * FFN_CHUNK
        u = jnp.dot(xn, wgu_ref[:, D_FF + lo:D_FF + lo + FFN_CHUNK], preferred_element_type=F32)
        acts.append((_silu(live.pop(("g", c))) * u).astype(BF16))

    def down_step(c):
        cols = slice(c * DOWN_CHUNK, (c + 1) * DOWN_CHUNK)
        if len(acts) > 1:
            acts[:] = [jnp.concatenate(acts, axis=1)]
        o_ref[:, cols] = h1[:, cols] + jnp.dot(acts[0], wd_ref[:, cols],
                                               preferred_element_type=F32)

    ffn_steps = []
    for c in range(D_FF // FFN_CHUNK):
        ffn_steps += [functools.partial(gate_step, c), functools.partial(up_step, c)]
    ffn_steps += [functools.partial(down_step, c) for c in range(D_MODEL // DOWN_CHUNK)]
    mix_steps = []
    for blk in range(blocks_per_tile):
        for lo in range(0, LRU_WIDTH, GATE_TILE):
            mix_steps.append(functools.partial(lru_gates_step, blk, lo))
            mix_steps += [functools.partial(lru_scan_step, blk, lo, sub)
                          for sub in range(0, GATE_TILE, LANES)]
        mix_steps += [functools.partial(attn_probs_step, blk, g) for g in range(ATT_KV_HEADS)]
        mix_steps += [functools.partial(attn_out_step, blk, g) for g in range(ATT_KV_HEADS)]
    for step in _interleave(ffn_steps, mix_steps):
        step()
    kprev[...] = k_ref[last, :]
    vprev[...] = v_ref[last, :]


def _l0_mix_ffn(sinks, x2d, meta_blk, xr, gate, q, k, v, conv_w, conv_b, w_gates, b_gates, lam,
                w_out, gain, w_gu, w_down, cast_weights, tiles_per_seq, seq):
    rows = xr.shape[0]
    tm = ROW_TILE
    n_tiles = rows // tm
    mix = lambda w: pl.BlockSpec((tm, w), lambda s: (jnp.minimum(s, n_tiles - 1), 0))
    cast_in, cast_out, cast_shapes = _cast_specs(cast_weights)
    return pl.pallas_call(
        functools.partial(_l0_mix_ffn_kernel, tiles_per_seq=tiles_per_seq, n_tiles=n_tiles,
                          n_cast=len(cast_in)),
        grid=(n_tiles + 1,),
        in_specs=[pl.BlockSpec(memory_space=pltpu.SMEM),
                  _token_window_spec(tm, tiles_per_seq, seq, lambda s: jnp.maximum(s - 1, 0)),
                  _const_spec((BLOCK, D_MODEL)),
                  mix(LRU_WIDTH), mix(LRU_WIDTH), mix(Q_WIDTH), mix(KV_WIDTH), mix(KV_WIDTH),
                  _const_spec((CONV_W, LRU_WIDTH)), _const_spec((1, LRU_WIDTH)),
                  _const_spec((LRU_WIDTH, 2 * LRU_WIDTH)), _const_spec((1, 2 * LRU_WIDTH)),
                  _const_spec((1, LRU_WIDTH)), _const_spec(w_out.shape), _const_spec((1, D_MODEL)),
                  _const_spec(w_gu.shape), _const_spec(w_down.shape)] + cast_in,
        out_specs=[pl.BlockSpec((tm, D_MODEL), lambda s: (jnp.maximum(s - 1, 0), 0))] + cast_out,
        out_shape=[jax.ShapeDtypeStruct((rows, D_MODEL), F32)] + cast_shapes,
        scratch_shapes=[pltpu.VMEM((tm, LRU_WIDTH + Q_WIDTH), BF16),
                        pltpu.VMEM((SUBLANES + BLOCK, LRU_WIDTH), F32),
                        pltpu.VMEM((SUBLANES, LRU_WIDTH), F32),
                        pltpu.VMEM((BLOCK, KV_WIDTH), BF16), pltpu.VMEM((BLOCK, KV_WIDTH), BF16),
                        pltpu.VMEM((BLOCK, KV_WIDTH), BF16), pltpu.VMEM((BLOCK, KV_WIDTH), BF16)],
        compiler_params=pltpu.CompilerParams(dimension_semantics=("arbitrary",)),
        name="l0_mix_ffn",
    )(sinks, x2d, meta_blk, xr, gate, q, k, v, conv_w, conv_b, w_gates, b_gates, lam, w_out, gain,
      w_gu, w_down, *[w for _, w in cast_weights])


def _outproj_ffn_kernel(h_ref, y_ref, wo_ref, gain_ref, wgu_ref, wd_ref, o_ref):
    h1 = h_ref[...] + jnp.dot(y_ref[...], wo_ref[...], preferred_element_type=F32)
    o_ref[...] = _ffn(h1, gain_ref[...], wgu_ref, wd_ref)


def _final_outproj_ffn(h, y, w_out, gain, w_gu, w_down, batch, seq_rows, seq):
    tm = OUT_TILE
    tiles = seq // tm
    win = lambda w: pl.BlockSpec((pl.Element(tm), pl.Element(w)),
                                 lambda b, j: (pl.multiple_of(b * seq_rows + BLOCK + j * tm, BLOCK),
                                               0))
    return pl.pallas_call(
        _outproj_ffn_kernel,
        grid=(batch, tiles),
        in_specs=[win(D_MODEL), win(y.shape[1]), _const_spec(w_out.shape),
                  _const_spec((1, D_MODEL)), _const_spec(w_gu.shape), _const_spec(w_down.shape)],
        out_specs=pl.BlockSpec((tm, D_MODEL), lambda b, j: (b * tiles + j, 0)),
        out_shape=jax.ShapeDtypeStruct((batch * seq, D_MODEL), F32),
        compiler_params=pltpu.CompilerParams(dimension_semantics=("arbitrary", "arbitrary")),
        name="l1_outproj_ffn",
    )(h, y, w_out, gain, w_gu, w_down)


_Q0, _K0, _V0, _G0 = 0, D_MODEL, 2 * D_MODEL, 4 * D_MODEL


def _retention_decays(hd):
    ii = lax.broadcasted_iota(jnp.int32, (BLOCK, BLOCK), 0)
    jj = lax.broadcasted_iota(jnp.int32, (BLOCK, BLOCK), 1)
    diff = (ii - jj).astype(F32)
    idx = lax.broadcasted_iota(jnp.int32, (BLOCK, 1), 0).astype(F32)
    log_g = RET_LOG_G[hd]
    decay_intra = jnp.where(diff >= 0.0, jnp.exp(jnp.maximum(diff, 0.0) * log_g), 0.0)
    return (decay_intra, jnp.exp((idx + 1.0) * log_g), jnp.exp((BLOCK - 1.0 - idx) * log_g),
            math.exp(BLOCK * log_g))


def _l1_mix_kernel(*refs, tiles_per_seq, n_tiles, n_cast):
    h_ref, gain_ref, w_ref, freq_ref = refs[:4]
    cast_in = refs[4:4 + n_cast]
    y_ref = refs[4 + n_cast]
    cast_out = refs[5 + n_cast:5 + 2 * n_cast]
    qkvg, state, ocos, osin = refs[5 + 2 * n_cast:]
    _cast_blocks(cast_in + cast_out)
    s = pl.program_id(0)
    proj_slot = s % 2
    ret_slot = 1 - proj_slot
    proj_tile_in_seq = jnp.minimum(s, n_tiles - 1) % tiles_per_seq
    ret_tile_in_seq = jnp.maximum(s - 1, 0) % tiles_per_seq
    tm = h_ref.shape[0]

    @pl.when(s == 0)
    def _():
        _rope_offsets(freq_ref, ocos, osin)
        qkvg[1] = jnp.zeros(qkvg.shape[1:], BF16)

    @pl.when(ret_tile_in_seq == 0)
    def _():
        state[...] = jnp.zeros_like(state)

    h = h_ref[...]
    xn = (h * _rms_scale(h) * gain_ref[...]).astype(BF16)
    cos, sin = _rope_tables(proj_tile_in_seq * tm - META_PAD, freq_ref, ocos, osin)
    half = RET_QK_DIM // 2

    def rope_step(col0, hd, scale):
        lo = col0 + hd * RET_QK_DIM
        y = jnp.dot(xn, w_ref[:, lo:lo + RET_QK_DIM], preferred_element_type=F32)
        x1 = y[:, :half]
        x2 = y[:, half:]
        qkvg[proj_slot, :, lo:lo + half] = ((x1 * cos - x2 * sin) * scale).astype(BF16)
        qkvg[proj_slot, :, lo + half:lo + RET_QK_DIM] = ((x2 * cos + x1 * sin)
                                                         * scale).astype(BF16)

    def value_step(lo, act):
        y = jnp.dot(xn, w_ref[:, lo:lo + PROJ_CHUNK], preferred_element_type=F32)
        qkvg[proj_slot, :, lo:lo + PROJ_CHUNK] = act(y).astype(BF16)

    def ret_operand(c, hd, col0, width):
        return qkvg[ret_slot, c * BLOCK:(c + 1) * BLOCK, col0 + hd * width:col0 + (hd + 1) * width]

    heads = range(RET_HEADS)
    decays = [_retention_decays(hd) for hd in heads]
    live = {}

    def prep_step(c, hd):
        live["kdt", c, hd] = (ret_operand(c, hd, _K0, RET_QK_DIM).astype(F32)
                              * decays[hd][2]).T.astype(BF16)

    def scores_step(c, hd):
        live["qk", c, hd] = lax.dot_general(
            ret_operand(c, hd, _Q0, RET_QK_DIM), ret_operand(c, hd, _K0, RET_QK_DIM),
            (((1,), (1,)), ((), ())), preferred_element_type=F32)
        live["kv", c, hd] = jnp.dot(live.pop(("kdt", c, hd)), ret_operand(c, hd, _V0, RET_V_DIM),
                                    preferred_element_type=F32)

    def decay_step(c, hd):
        live["qkd", c, hd] = (live.pop(("qk", c, hd)) * decays[hd][0]).astype(BF16)
        live["stb", c, hd] = state[hd].astype(BF16)

    def output_step(c, hd):
        live["o", c, hd] = (
            jnp.dot(live.pop(("qkd", c, hd)), ret_operand(c, hd, _V0, RET_V_DIM),
                    preferred_element_type=F32)
            + jnp.dot(ret_operand(c, hd, _Q0, RET_QK_DIM), live.pop(("stb", c, hd)),
                      preferred_element_type=F32) * decays[hd][1])

    def post_step(c, hd):
        o = live.pop(("o", c, hd))
        state[hd] = decays[hd][3] * state[hd] + live.pop(("kv", c, hd))
        gate = ret_operand(c, hd, _G0, RET_V_DIM).astype(F32)
        y_ref[c * BLOCK:(c + 1) * BLOCK, hd * RET_V_DIM:(hd + 1) * RET_V_DIM] = (
            o * _rms_scale(o) * gate).astype(BF16)

    def both(first, second):
        def step():
            first()
            second()
        return step

    proj_steps = []
    for hd in heads:
        proj_steps.append(both(functools.partial(rope_step, _Q0, hd, 1.0),
                               functools.partial(rope_step, _K0, hd, RET_QK_DIM ** -0.5)))
        for col0, act in ((_V0, lambda y: y), (_G0, _silu)):
            lo = col0 + hd * RET_V_DIM
            proj_steps.append(both(functools.partial(value_step, lo, act),
                                   functools.partial(value_step, lo + PROJ_CHUNK, act)))
    n_chunks = tm // BLOCK
    each_head = lambda step, c: [functools.partial(step, c, hd) for hd in heads]
    ret_steps = each_head(prep_step, 0)
    for c in range(n_chunks):
        ret_steps += each_head(scores_step, c)
        for hd in heads:
            ret_steps.append(functools.partial(decay_step, c, hd))
            if c + 1 < n_chunks:
                ret_steps.append(functools.partial(prep_step, c + 1, hd))
        ret_steps += each_head(output_step, c) + each_head(post_step, c)
    for step in _interleave(proj_steps, ret_steps):
        step()


def _l1_mix(h, gain, w_in, freq, cast_weights, tiles_per_seq):
    rows = h.shape[0]
    tm = ROW_TILE
    n_tiles = rows // tm
    half = RET_QK_DIM // 2
    cast_in, cast_out, cast_shapes = _cast_specs(cast_weights)
    return pl.pallas_call(
        functools.partial(_l1_mix_kernel, tiles_per_seq=tiles_per_seq, n_tiles=n_tiles,
                          n_cast=len(cast_in)),
        grid=(n_tiles + 1,),
        in_specs=[pl.BlockSpec((tm, D_MODEL), lambda s: (jnp.minimum(s, n_tiles - 1), 0)),
                  _const_spec((1, D_MODEL)), _const_spec(w_in.shape), _const_spec((1, half))]
        + cast_in,
        out_specs=[pl.BlockSpec((tm, 2 * D_MODEL), lambda s: (jnp.maximum(s - 1, 0), 0))]
        + cast_out,
        out_shape=[jax.ShapeDtypeStruct((rows, 2 * D_MODEL), BF16)] + cast_shapes,
        scratch_shapes=[pltpu.VMEM((2, tm, 6 * D_MODEL), BF16),
                        pltpu.VMEM((RET_HEADS, RET_QK_DIM, RET_V_DIM), F32),
                        pltpu.VMEM((tm, half), F32), pltpu.VMEM((tm, half), F32)],
        compiler_params=pltpu.CompilerParams(dimension_semantics=("arbitrary",)),
        name="l1_mix",
    )(h, gain, w_in, freq, *[w for _, w in cast_weights])


def _inv_freq(half, theta):
    return jnp.power(jnp.asarray(theta, F32), -jnp.arange(half, dtype=F32) / half)


def _block_diag(w):
    heads, wi, wo = w.shape
    eye = jnp.eye(heads, dtype=w.dtype)
    return (eye[:, None, :, None] * w[:, :, None, :]).reshape(heads * wi, heads * wo)


def kernel(x, meta_tokens, mix_norm_ab, ab_w_in, lru_conv_w, lru_conv_b, lru_w_a, lru_b_a, lru_w_i, lru_b_i, lru_lambda, q_norm, k_norm, attn_sinks, ab_w_out, mix_norm_ret, ret_w_in, ret_w_out, ffn_norm, ffn_w_gu, ffn_w_down):
    batch, seq, _ = x.shape
    seq_rows = META_PAD + N_META + seq
    tiles_per_seq = seq_rows // ROW_TILE

    x2d = x.reshape(batch * seq, D_MODEL)
    meta_blk = jnp.concatenate([jnp.zeros((META_PAD, D_MODEL), x.dtype),
                                meta_tokens.astype(x.dtype)], axis=0)

    f_att = _inv_freq(ROT_DIM // 2, ROPE_THETA)
    f_att = jnp.concatenate([f_att, f_att, jnp.zeros((ATT_HEAD_DIM - ROT_DIM,), F32)])
    f_att = jnp.tile(f_att, LANES // ATT_HEAD_DIM).reshape(1, LANES)
    f_ret = _inv_freq(RET_QK_DIM // 2, RET_THETA).reshape(1, RET_QK_DIM // 2)

    row_vec = lambda v: v.reshape(1, -1).astype(F32)
    two_heads = lambda v: jnp.tile(v.reshape(1, -1).astype(F32), (1, 2))

    xr, gate, q, k, v, w_gu0, w_down0, w_gu1, w_down1 = _inproj0(
        x2d, meta_blk, row_vec(mix_norm_ab[0]), ab_w_in[0].astype(BF16), two_heads(q_norm[0]),
        two_heads(k_norm[0]), f_att,
        [(0, ffn_w_gu), (0, ffn_w_down), (1, ffn_w_gu), (1, ffn_w_down)],
        batch, tiles_per_seq, seq)
    w_gates = jnp.concatenate([_block_diag(lru_w_a[0]), _block_diag(lru_w_i[0])], axis=1)
    b_gates = jnp.concatenate([lru_b_a[0].reshape(1, -1), lru_b_i[0].reshape(1, -1)], axis=1)
    w_att = ab_w_out[0][LRU_WIDTH:].reshape(ATT_KV_HEADS, ATT_GROUP, ATT_HEAD_DIM, D_MODEL)
    w_att = w_att.transpose(1, 0, 2, 3).reshape(Q_WIDTH, D_MODEL)
    w_out0 = jnp.concatenate([ab_w_out[0][:LRU_WIDTH], w_att], axis=0).astype(BF16)
    h, w_in1, w_out1 = _l0_mix_ffn(
        attn_sinks[0].astype(F32), x2d, meta_blk, xr, gate, q, k, v, lru_conv_w[0],
        row_vec(lru_conv_b[0]), w_gates.astype(BF16), b_gates.astype(F32), row_vec(lru_lambda[0]),
        w_out0, row_vec(ffn_norm[0]), w_gu0, w_down0, [(0, ret_w_in), (0, ret_w_out)],
        tiles_per_seq, seq)

    (y_ret,) = _l1_mix(h, row_vec(mix_norm_ret[0]), w_in1, f_ret, [], tiles_per_seq)
    out = _final_outproj_ffn(h, y_ret, w_out1, row_vec(ffn_norm[1]), w_gu1, w_down1,
                             batch, seq_rows, seq)
    return out.reshape(batch, seq, D_MODEL)
```

```python
import functools
import itertools
import math

import jax
import jax.numpy as jnp
from jax import lax
from jax.experimental import pallas as pl
from jax.experimental.pallas import tpu as pltpu

F32 = jnp.float32
BF16 = jnp.bfloat16

D_MODEL = 1024
N_META = 16
BLOCK = 128
META_PAD = BLOCK - N_META
RMS_EPS = 1e-6
NEG_INF = -1e30

LRU_WIDTH = 512
LRU_HEADS = 8
LRU_BLOCK_W = 64
CONV_W = 4
LRU_C = 8.0

ATT_HEADS = 8
ATT_KV_HEADS = 2
ATT_GROUP = ATT_HEADS // ATT_KV_HEADS
ATT_HEAD_DIM = 64
ROPE_THETA = 500000.0
ROT_DIM = 16
Q_WIDTH = 512
KV_WIDTH = 128
AB_IN_WIDTH = 2 * LRU_WIDTH + Q_WIDTH + 2 * KV_WIDTH

RET_HEADS = 4
RET_QK_DIM = 256
RET_V_DIM = 512
RET_THETA = 10000.0
RET_LOG_G = tuple(math.log1p(-(2.0 ** (-5.0 - h))) for h in range(RET_HEADS))

D_FF = 2816

LANES = 128
SUBLANES = 8
ROW_TILE = 640
OUT_TILE = 512
SQRT_GUARD = 1e-30
GATE_TILE = 256
INPROJ_CHUNK = 256
CAST_STEPS = 16
FFN_CHUNK = 256
DOWN_CHUNK = 256


def _rms_scale(x):
    return lax.rsqrt(jnp.mean(x * x, axis=-1, keepdims=True) + RMS_EPS)


def _sigmoid(x):
    return 0.5 * jnp.tanh(0.5 * x) + 0.5


def _silu(x):
    half = 0.5 * x
    return half + half * jnp.tanh(half)


def _gelu_tanh(x):
    half = 0.5 * x
    return half + half * jnp.tanh(0.7978845608028654 * (x + 0.044715 * (x * x * x)))


def _interleave(primary, secondary):
    out, done = [], 0
    for i, step in enumerate(primary):
        out.append(step)
        upto = ((i + 1) * len(secondary)) // len(primary)
        out.extend(secondary[done:upto])
        done = upto
    return out + list(secondary[done:])


def _const_spec(shape):
    zeros = (0,) * len(shape)
    return pl.BlockSpec(shape, lambda *_: zeros, pipeline_mode=pl.Buffered(1))


def _padded_rows(x_ref, meta_ref, first):
    xw = x_ref[...]
    tm = xw.shape[0]
    return jnp.concatenate([jnp.where(first, meta_ref[...], xw[0:BLOCK]),
                            jnp.where(first, xw[0:tm - BLOCK], xw[BLOCK:tm])], axis=0)


def _token_window_spec(tm, tiles_per_seq, seq, tile_of_step):
    def index(s):
        t = tile_of_step(s)
        start = jnp.maximum((t % tiles_per_seq) * tm - BLOCK, 0)
        return (pl.multiple_of((t // tiles_per_seq) * seq + start, BLOCK), 0)
    return pl.BlockSpec((pl.Element(tm), pl.Element(D_MODEL)), index)


def _rope_offsets(freq_ref, cos_ref, sin_ref):
    r = lax.broadcasted_iota(jnp.int32, cos_ref.shape, 0).astype(F32)
    ang = r * freq_ref[...]
    cos_ref[...] = jnp.cos(ang)
    sin_ref[...] = jnp.sin(ang)


def _rope_tables(base_pos, freq_ref, cos_ref, sin_ref):
    ang = base_pos.astype(F32) * freq_ref[...]
    cb = jnp.cos(ang)
    sb = jnp.sin(ang)
    oc = cos_ref[...]
    os_ = sin_ref[...]
    return cb * oc - sb * os_, sb * oc + cb * os_


def _cast_blocks(cast_refs):
    n = len(cast_refs) // 2
    for src, dst in zip(cast_refs[:n], cast_refs[n:]):
        dst[...] = src[...].astype(BF16)


def _cast_specs(weights):
    ins, outs, shapes = [], [], []
    step = lambda s: jnp.minimum(s, CAST_STEPS - 1)
    for layer, w in weights:
        _, rows, cols = w.shape
        blk = rows // CAST_STEPS
        ins.append(pl.BlockSpec((None, blk, cols), lambda s, layer=layer: (layer, step(s), 0)))
        outs.append(pl.BlockSpec((blk, cols), lambda s: (step(s), 0)))
        shapes.append(jax.ShapeDtypeStruct((rows, cols), BF16))
    return ins, outs, shapes


def _inproj0_kernel(*refs, tiles_per_seq, n_cast):
    (x_ref, meta_ref, gain_ref, w_ref, qg_ref, kg_ref, freq_ref) = refs[:7]
    cast_in = refs[7:7 + n_cast]
    xr_ref, gate_ref, q_ref, k_ref, v_ref = refs[7 + n_cast:12 + n_cast]
    cast_out = refs[12 + n_cast:12 + 2 * n_cast]
    ocos, osin = refs[12 + 2 * n_cast:]
    step = pl.program_id(0)
    tile_in_seq = step % tiles_per_seq

    @pl.when(step == 0)
    def _():
        _rope_offsets(freq_ref, ocos, osin)

    _cast_blocks(cast_in + cast_out)

    h = _padded_rows(x_ref, meta_ref, tile_in_seq == 0)
    rows = h.shape[0]
    xn = (h * _rms_scale(h) * gain_ref[...]).astype(BF16)

    q0 = 2 * LRU_WIDTH
    k0 = q0 + Q_WIDTH
    chunk_dot = lambda lo: jnp.dot(xn, w_ref[:, lo:lo + INPROJ_CHUNK], preferred_element_type=F32)
    y_q = [chunk_dot(q0 + lo) for lo in range(0, Q_WIDTH, INPROJ_CHUNK)]
    y_kv = chunk_dot(k0)
    v_ref[...] = y_kv[:, KV_WIDTH:].astype(BF16)

    cos, sin = _rope_tables(tile_in_seq * rows - META_PAD, freq_ref, ocos, osin)
    lane = lax.broadcasted_iota(jnp.int32, (1, LANES), 1) & (ATT_HEAD_DIM - 1)
    half = ROT_DIM // 2
    sin_lo = sin * jnp.where(lane < half, -1.0, 0.0)
    sin_hi = sin * jnp.where((lane >= half) & (lane < ROT_DIM), 1.0, 0.0)
    hi = lax.broadcasted_iota(jnp.int32, (2 * LANES, LANES), 0) & (LANES - 1)
    hj = lax.broadcasted_iota(jnp.int32, (2 * LANES, LANES), 1)
    head_mean = jnp.where(hi // ATT_HEAD_DIM == hj // ATT_HEAD_DIM,
                          1.0 / ATT_HEAD_DIM, 0.0).astype(BF16)

    def norm_rope(x, gain, scale):
        sq = x * x
        sq_hi = sq.astype(BF16)
        sq_lo = (sq - sq_hi.astype(F32)).astype(BF16)
        ms = jnp.dot(jnp.concatenate([sq_hi, sq_lo], axis=1), head_mean,
                     preferred_element_type=F32)
        xg = x * gain
        rot = (xg * cos + pltpu.roll(xg, LANES - half, 1) * sin_lo
               + pltpu.roll(xg, half, 1) * sin_hi)
        return rot * (lax.rsqrt(ms + RMS_EPS) * scale)

    def qk_step(j):
        if j < Q_WIDTH // LANES:
            lo = (j * LANES) % INPROJ_CHUNK
            q_ref[:, j * LANES:(j + 1) * LANES] = norm_rope(
                y_q[j * LANES // INPROJ_CHUNK][:, lo:lo + LANES], qg_ref[...],
                ATT_HEAD_DIM ** -0.5).astype(BF16)
        else:
            k_ref[...] = norm_rope(y_kv[:, :KV_WIDTH], kg_ref[...], 1.0).astype(BF16)

    def proj_step(c):
        out_ref = xr_ref if c < LRU_WIDTH // INPROJ_CHUNK else gate_ref
        dst = slice((c * INPROJ_CHUNK) % LRU_WIDTH, (c * INPROJ_CHUNK) % LRU_WIDTH + INPROJ_CHUNK)
        out_ref[:, dst] = chunk_dot(c * INPROJ_CHUNK)

    proj_steps = [functools.partial(proj_step, c) for c in range(2 * LRU_WIDTH // INPROJ_CHUNK)]
    qk_steps = [functools.partial(qk_step, j) for j in range(Q_WIDTH // LANES + 1)]
    for step in _interleave(proj_steps, qk_steps):
        step()


def _inproj0(x2d, meta_blk, gain, w_in, q_gain, k_gain, freq, cast_weights, batch,
             tiles_per_seq, seq):
    tm = ROW_TILE
    rows = batch * tiles_per_seq * tm
    row = lambda w: pl.BlockSpec((tm, w), lambda i: (i, 0))
    cast_in, cast_out, cast_shapes = _cast_specs(cast_weights)
    return pl.pallas_call(
        functools.partial(_inproj0_kernel, tiles_per_seq=tiles_per_seq, n_cast=len(cast_in)),
        grid=(rows // tm,),
        in_specs=[_token_window_spec(tm, tiles_per_seq, seq, lambda s: s),
                  _const_spec((BLOCK, D_MODEL)), _const_spec((1, D_MODEL)),
                  _const_spec((D_MODEL, AB_IN_WIDTH)), _const_spec((1, LANES)),
                  _const_spec((1, LANES)), _const_spec((1, LANES))] + cast_in,
        out_specs=[row(LRU_WIDTH), row(LRU_WIDTH), row(Q_WIDTH), row(KV_WIDTH), row(KV_WIDTH)]
        + cast_out,
        out_shape=[jax.ShapeDtypeStruct((rows, LRU_WIDTH), F32),
                   jax.ShapeDtypeStruct((rows, LRU_WIDTH), F32),
                   jax.ShapeDtypeStruct((rows, Q_WIDTH), BF16),
                   jax.ShapeDtypeStruct((rows, KV_WIDTH), BF16),
                   jax.ShapeDtypeStruct((rows, KV_WIDTH), BF16)] + cast_shapes,
        scratch_shapes=[pltpu.VMEM((tm, LANES), F32), pltpu.VMEM((tm, LANES), F32)],
        compiler_params=pltpu.CompilerParams(dimension_semantics=("arbitrary",)),
        name="l0_inproj",
    )(x2d, meta_blk, gain, w_in, q_gain, k_gain, freq, *[w for _, w in cast_weights])


def _attn_probs(n, sinks, q, kc, kp, km):
    j = lax.broadcasted_iota(jnp.int32, (BLOCK, BLOCK), 0)
    i = lax.broadcasted_iota(jnp.int32, (BLOCK, BLOCK), 1)
    causal = j <= i
    win_ok = n >= jnp.where(causal, 1, 2)
    meta_ok = (j >= META_PAD) & (n >= jnp.where(causal, 0, 1))
    contract_last = (((1,), (1,)), ((), ()))

    qs = jnp.concatenate([q[:, a * ATT_HEAD_DIM:(a + 1) * ATT_HEAD_DIM]
                          for a in range(ATT_GROUP)], axis=0)
    s_c = lax.dot_general(kc, qs, contract_last, preferred_element_type=F32)
    s_p = lax.dot_general(kp, qs, contract_last, preferred_element_type=F32)
    s_m = lax.dot_general(km, qs, contract_last, preferred_element_type=F32)
    p_c, p_p, p_m, inv_den = [], [], [], []
    for a in range(ATT_GROUP):
        head = slice(a * BLOCK, (a + 1) * BLOCK)
        sw = jnp.where(win_ok, jnp.where(causal, s_c[:, head], s_p[:, head]), NEG_INF)
        sm = jnp.where(meta_ok, s_m[:, head], NEG_INF)
        m = jnp.maximum(jnp.maximum(jnp.max(sw, axis=0, keepdims=True),
                                    jnp.max(sm, axis=0, keepdims=True)), sinks[a])
        pw = jnp.exp(sw - m)
        pm = jnp.exp(sm - m)
        den = (jnp.sum(pw, axis=0, keepdims=True) + jnp.sum(pm, axis=0, keepdims=True)
               + jnp.exp(sinks[a] - m))
        inv_den.append(1.0 / den)
        p_c.append(jnp.where(causal, pw, 0.0).astype(BF16))
        p_p.append(jnp.where(causal, 0.0, pw).astype(BF16))
        p_m.append(pm.astype(BF16))
    lanes = lambda parts: jnp.concatenate(parts, axis=1)
    return lanes(p_c), lanes(p_p), lanes(p_m), lanes(inv_den)


def _attn_out_t(probs, vc, vp, vm):
    p_c, p_p, p_m, inv_den = probs
    contract_rows = (((0,), (0,)), ((), ()))
    o_t = (lax.dot_general(vc, p_c, contract_rows, preferred_element_type=F32)
           + lax.dot_general(vp, p_p, contract_rows, preferred_element_type=F32)
           + lax.dot_general(vm, p_m, contract_rows, preferred_element_type=F32))
    return o_t * inv_den


def _attn_untranspose(o_t):
    o_t = jnp.concatenate(o_t, axis=0)
    return jnp.concatenate([o_t[:, a * BLOCK:(a + 1) * BLOCK].T for a in range(ATT_GROUP)],
                           axis=1).astype(BF16)


def _lru_gates(lo, xr_ref, rows, cw, cb, wg_ref, xbuf):
    tl = BLOCK
    cols = slice(lo, lo + GATE_TILE)
    x = xr_ref[rows, cols]
    xbuf[SUBLANES:SUBLANES + tl, cols] = x
    xc = x * cw[CONV_W - 1:CONV_W, cols] + cb[:, cols]
    for d in range(1, CONV_W):
        xc = xc + (xbuf[SUBLANES - d:SUBLANES - d + tl, cols]
                   * cw[CONV_W - 1 - d:CONV_W - d, cols])
    xbuf[0:SUBLANES, cols] = x[tl - SUBLANES:tl]
    xcb = xc.astype(BF16)
    ga_r = jnp.dot(xcb, wg_ref[cols, cols], preferred_element_type=F32)
    ga_i = jnp.dot(xcb, wg_ref[cols, LRU_WIDTH + lo:LRU_WIDTH + lo + GATE_TILE],
                   preferred_element_type=F32)
    return xc, ga_r, ga_i


def _lru_scan(n, xc, ga_r, ga_i, grp, gate_ref, rows, bg, softplus, hcar, y_ref):
    tl = BLOCK
    row = lax.broadcasted_iota(jnp.int32, (tl, 1), 0)
    t = n * tl + row
    r = _sigmoid(ga_r + bg[:, grp])
    gi = _sigmoid(ga_i + bg[:, LRU_WIDTH + grp.start:LRU_WIDTH + grp.stop])
    log_a = (-LRU_C * softplus[:, grp]) * r
    a = jnp.exp(log_a)
    mult2 = jnp.tanh(-log_a) * (a * a + 1.0)
    mult = mult2 * lax.rsqrt(jnp.maximum(mult2, SQRT_GUARD))
    mult = jnp.where(t == META_PAD, 1.0, mult)
    b = jnp.where(t < META_PAD, 0.0, mult * gi * xc)

    d = 1
    while d < SUBLANES:
        keep = row >= d
        b = jnp.where(keep, a * pltpu.roll(b, d, 0), 0.0) + b
        a = jnp.where(keep, a * pltpu.roll(a, d, 0), a)
        d *= 2
    while d < tl:
        b = jnp.concatenate([b[:d], a[d:] * b[:tl - d] + b[d:]], axis=0)
        a = jnp.concatenate([a[:d], a[d:] * a[:tl - d]], axis=0)
        d *= 2
    h = b + a * hcar[0:1, grp]
    hcar[:, grp] = jnp.broadcast_to(h[tl - 1:tl], (SUBLANES, LANES))
    y_ref[rows, grp] = (_gelu_tanh(gate_ref[rows, grp]) * h).astype(BF16)


def _ffn(h1, gain, wgu_ref, wd_ref):
    xn = (h1 * _rms_scale(h1) * gain).astype(BF16)
    acts = []
    for c in range(D_FF // FFN_CHUNK):
        lo = c * FFN_CHUNK
        g = jnp.dot(xn, wgu_ref[:, lo:lo + FFN_CHUNK], preferred_element_type=F32)
        u = jnp.dot(xn, wgu_ref[:, D_FF + lo:D_FF + lo + FFN_CHUNK], preferred_element_type=F32)
        acts.append((_silu(g) * u).astype(BF16))
    act = jnp.concatenate(acts, axis=1)
    return h1 + jnp.dot(act, wd_ref[...], preferred_element_type=F32)


def _l0_mix_ffn_kernel(*refs, tiles_per_seq, n_tiles, n_cast):
    (sink_ref, x_ref, meta_ref, xr_ref, gate_ref, q_ref, k_ref, v_ref, cw_ref, cb_ref, wg_ref,
     bg_ref, lam_ref, wo_ref, gain_ref, wgu_ref, wd_ref) = refs[:17]
    cast_in = refs[17:17 + n_cast]
    o_ref = refs[17 + n_cast]
    cast_out = refs[18 + n_cast:18 + 2 * n_cast]
    ybuf, xbuf, hcar, kprev, vprev, kmeta, vmeta = refs[18 + 2 * n_cast:]
    s = pl.program_id(0)
    tile_in_seq = jnp.minimum(s, n_tiles - 1) % tiles_per_seq
    blocks_per_tile = x_ref.shape[0] // BLOCK
    last = slice((blocks_per_tile - 1) * BLOCK, blocks_per_tile * BLOCK)
    _cast_blocks(cast_in + cast_out)

    @pl.when(s == 0)
    def _():
        ybuf[...] = jnp.zeros_like(ybuf)

    @pl.when(tile_in_seq == 0)
    def _():
        xbuf[0:SUBLANES, :] = jnp.zeros((SUBLANES, LRU_WIDTH), F32)
        hcar[...] = jnp.zeros_like(hcar)
        kmeta[...] = k_ref[0:BLOCK, :]
        vmeta[...] = v_ref[0:BLOCK, :]
        kprev[...] = k_ref[0:BLOCK, :]
        vprev[...] = v_ref[0:BLOCK, :]

    ffn_first = jnp.maximum(s - 1, 0) % tiles_per_seq == 0
    h1 = (_padded_rows(x_ref, meta_ref, ffn_first)
          + jnp.dot(ybuf[...], wo_ref[...], preferred_element_type=F32))
    xn = (h1 * _rms_scale(h1) * gain_ref[...]).astype(BF16)

    cw = cw_ref[...]
    cb = cb_ref[...]
    bg = bg_ref[...]
    z = -lam_ref[...]
    softplus = jnp.maximum(z, 0.0) + jnp.log1p(jnp.exp(-jnp.abs(z)))

    live = {}

    def block_rows(blk):
        return tile_in_seq * blocks_per_tile + blk, slice(blk * BLOCK, (blk + 1) * BLOCK)

    def lru_gates_step(blk, lo):
        _, rows = block_rows(blk)
        live["lru", blk, lo] = _lru_gates(lo, xr_ref, rows, cw, cb, wg_ref, xbuf)

    def lru_scan_step(blk, lo, sub):
        n, rows = block_rows(blk)
        xc, ga_r, ga_i = live["lru", blk, lo]
        part = slice(sub, sub + LANES)
        _lru_scan(n, xc[:, part], ga_r[:, part], ga_i[:, part],
                  slice(lo + sub, lo + sub + LANES), gate_ref, rows, bg, softplus, hcar, ybuf)
        if sub + LANES == GATE_TILE:
            del live["lru", blk, lo]

    def kv_blocks(ref, prev_ref, meta_ref, blk, g):
        _, rows = block_rows(blk)
        lanes = slice(g * ATT_HEAD_DIM, (g + 1) * ATT_HEAD_DIM)
        prev = prev_ref[:, lanes] if blk == 0 else ref[(blk - 1) * BLOCK:blk * BLOCK, lanes]
        return ref[rows, lanes], prev, meta_ref[:, lanes]

    def attn_probs_step(blk, g):
        n, rows = block_rows(blk)
        width = ATT_GROUP * ATT_HEAD_DIM
        sinks = [sink_ref[g * ATT_GROUP + a] for a in range(ATT_GROUP)]
        live["probs", blk, g] = _attn_probs(n, sinks, q_ref[rows, g * width:(g + 1) * width],
                                            *kv_blocks(k_ref, kprev, kmeta, blk, g))

    def attn_out_step(blk, g):
        _, rows = block_rows(blk)
        live["out", blk, g] = _attn_out_t(live.pop(("probs", blk, g)),
                                          *kv_blocks(v_ref, vprev, vmeta, blk, g))
        if g + 1 == ATT_KV_HEADS:
            ybuf[rows, LRU_WIDTH:LRU_WIDTH + Q_WIDTH] = _attn_untranspose(
                [live.pop(("out", blk, h)) for h in range(ATT_KV_HEADS)])

    acts = []

    def gate_step(c):
        lo = c * FFN_CHUNK
        live["g", c] = jnp.dot(xn, wgu_ref[:, lo:lo + FFN_CHUNK], preferred_element_type=F32)

    def up_step(c):
        lo = c * FFN_CHUNK
        u = jnp.dot(xn, wgu_ref[:, D_FF + lo:D_FF + lo + FFN_CHUNK], preferred_element_type=F32)
        acts.append((_silu(live.pop(("g", c))) * u).astype(BF16))

    def down_step(c):
        cols = slice(c * DOWN_CHUNK, (c + 1) * DOWN_CHUNK)
        if len(acts) > 1:
            acts[:] = [jnp.concatenate(acts, axis=1)]
        o_ref[:, cols] = h1[:, cols] + jnp.dot(acts[0], wd_ref[:, cols],
                                               preferred_element_type=F32)

    ffn_steps = []
    for c in range(D_FF // FFN_CHUNK):
        ffn_steps += [functools.partial(gate_step, c), functools.partial(up_step, c)]
    ffn_steps += [functools.partial(down_step, c) for c in range(D_MODEL // DOWN_CHUNK)]
    mix_steps = []
    for blk in range(blocks_per_tile):
        for lo in range(0, LRU_WIDTH, GATE_TILE):
            mix_steps.append(functools.partial(lru_gates_step, blk, lo))
            mix_steps += [functools.partial(lru_scan_step, blk, lo, sub)
                          for sub in range(0, GATE_TILE, LANES)]
        mix_steps += [functools.partial(attn_probs_step, blk, g) for g in range(ATT_KV_HEADS)]
        mix_steps += [functools.partial(attn_out_step, blk, g) for g in range(ATT_KV_HEADS)]
    for step in _interleave(ffn_steps, mix_steps):
        step()
    kprev[...] = k_ref[last, :]
    vprev[...] = v_ref[last, :]


def _l0_mix_ffn(sinks, x2d, meta_blk, xr, gate, q, k, v, conv_w, conv_b, w_gates, b_gates, lam,
                w_out, gain, w_gu, w_down, cast_weights, tiles_per_seq, seq):
    rows = xr.shape[0]
    tm = ROW_TILE
    n_tiles = rows // tm
    mix = lambda w: pl.BlockSpec((tm, w), lambda s: (jnp.minimum(s, n_tiles - 1), 0))
    cast_in, cast_out, cast_shapes = _cast_specs(cast_weights)
    return pl.pallas_call(
        functools.partial(_l0_mix_ffn_kernel, tiles_per_seq=tiles_per_seq, n_tiles=n_tiles,
                          n_cast=len(cast_in)),
        grid=(n_tiles + 1,),
        in_specs=[pl.BlockSpec(memory_space=pltpu.SMEM),
                  _token_window_spec(tm, tiles_per_seq, seq, lambda s: jnp.maximum(s - 1, 0)),
                  _const_spec((BLOCK, D_MODEL)),
                  mix(LRU_WIDTH), mix(LRU_WIDTH), mix(Q_WIDTH), mix(KV_WIDTH), mix(KV_WIDTH),
                  _const_spec((CONV_W, LRU_WIDTH)), _const_spec((1, LRU_WIDTH)),
                  _const_spec((LRU_WIDTH, 2 * LRU_WIDTH)), _const_spec((1, 2 * LRU_WIDTH)),
                  _const_spec((1, LRU_WIDTH)), _const_spec(w_out.shape), _const_spec((1, D_MODEL)),
                  _const_spec(w_gu.shape), _const_spec(w_down.shape)] + cast_in,
        out_specs=[pl.BlockSpec((tm, D_MODEL), lambda s: (jnp.maximum(s - 1, 0), 0))] + cast_out,
        out_shape=[jax.ShapeDtypeStruct((rows, D_MODEL), F32)] + cast_shapes,
        scratch_shapes=[pltpu.VMEM((tm, LRU_WIDTH + Q_WIDTH), BF16),
                        pltpu.VMEM((SUBLANES + BLOCK, LRU_WIDTH), F32),
                        pltpu.VMEM((SUBLANES, LRU_WIDTH), F32),
                        pltpu.VMEM((BLOCK, KV_WIDTH), BF16), pltpu.VMEM((BLOCK, KV_WIDTH), BF16),
                        pltpu.VMEM((BLOCK, KV_WIDTH), BF16), pltpu.VMEM((BLOCK, KV_WIDTH), BF16)],
        compiler_params=pltpu.CompilerParams(dimension_semantics=("arbitrary",)),
        name="l0_mix_ffn",
    )(sinks, x2d, meta_blk, xr, gate, q, k, v, conv_w, conv_b, w_gates, b_gates, lam, w_out, gain,
      w_gu, w_down, *[w for _, w in cast_weights])


def _outproj_ffn_kernel(h_ref, y_ref, wo_ref, gain_ref, wgu_ref, wd_ref, o_ref):
    h1 = h_ref[...] + jnp.dot(y_ref[...], wo_ref[...], preferred_element_type=F32)
    o_ref[...] = _ffn(h1, gain_ref[...], wgu_ref, wd_ref)


def _final_outproj_ffn(h, y, w_out, gain, w_gu, w_down, batch, seq_rows, seq):
    tm = OUT_TILE
    tiles = seq // tm
    win = lambda w: pl.BlockSpec((pl.Element(tm), pl.Element(w)),
                                 lambda b, j: (pl.multiple_of(b * seq_rows + BLOCK + j * tm, BLOCK),
                                               0))
    return pl.pallas_call(
        _outproj_ffn_kernel,
        grid=(batch, tiles),
        in_specs=[win(D_MODEL), win(y.shape[1]), _const_spec(w_out.shape),
                  _const_spec((1, D_MODEL)), _const_spec(w_gu.shape), _const_spec(w_down.shape)],
        out_specs=pl.BlockSpec((tm, D_MODEL), lambda b, j: (b * tiles + j, 0)),
        out_shape=jax.ShapeDtypeStruct((batch * seq, D_MODEL), F32),
        compiler_params=pltpu.CompilerParams(dimension_semantics=("arbitrary", "arbitrary")),
        name="l1_outproj_ffn",
    )(h, y, w_out, gain, w_gu, w_down)


_Q0, _K0, _V0, _G0 = 0, D_MODEL, 2 * D_MODEL, 4 * D_MODEL


def _retention_decays(hd):
    ii = lax.broadcasted_iota(jnp.int32, (BLOCK, BLOCK), 0)
    jj = lax.broadcasted_iota(jnp.int32, (BLOCK, BLOCK), 1)
    diff = (ii - jj).astype(F32)
    idx = lax.broadcasted_iota(jnp.int32, (BLOCK, 1), 0).astype(F32)
    log_g = RET_LOG_G[hd]
    decay_intra = jnp.where(diff >= 0.0, jnp.exp(jnp.maximum(diff, 0.0) * log_g), 0.0)
    return (decay_intra, jnp.exp((idx + 1.0) * log_g), jnp.exp((BLOCK - 1.0 - idx) * log_g),
            math.exp(BLOCK * log_g))


def _l1_mix_kernel(*refs, tiles_per_seq, n_tiles, n_cast):
    h_ref, gain_ref, w_ref, freq_ref = refs[:4]
    cast_in = refs[4:4 + n_cast]
    y_ref = refs[4 + n_cast]
    cast_out = refs[5 + n_cast:5 + 2 * n_cast]
    qkvg, state, ocos, osin = refs[5 + 2 * n_cast:]
    _cast_blocks(cast_in + cast_out)
    s = pl.program_id(0)
    proj_slot = s % 2
    ret_slot = 1 - proj_slot
    proj_tile_in_seq = jnp.minimum(s, n_tiles - 1) % tiles_per_seq
    ret_tile_in_seq = jnp.maximum(s - 1, 0) % tiles_per_seq
    tm = h_ref.shape[0]

    @pl.when(s == 0)
    def _():
        _rope_offsets(freq_ref, ocos, osin)
        qkvg[1] = jnp.zeros(qkvg.shape[1:], BF16)

    @pl.when(ret_tile_in_seq == 0)
    def _():
        state[...] = jnp.zeros_like(state)

    h = h_ref[...]
    xn = (h * _rms_scale(h) * gain_ref[...]).astype(BF16)
    cos, sin = _rope_tables(proj_tile_in_seq * tm - META_PAD, freq_ref, ocos, osin)
    half = RET_QK_DIM // 2

    def rope_step(col0, hd, scale):
        lo = col0 + hd * RET_QK_DIM
        y = jnp.dot(xn, w_ref[:, lo:lo + RET_QK_DIM], preferred_element_type=F32)
        x1 = y[:, :half]
        x2 = y[:, half:]
        qkvg[proj_slot, :, lo:lo + half] = ((x1 * cos - x2 * sin) * scale).astype(BF16)
        qkvg[proj_slot, :, lo + half:lo + RET_QK_DIM] = ((x2 * cos + x1 * sin)
                                                         * scale).astype(BF16)

    def value_step(col0, hd, act):
        lo = col0 + hd * RET_V_DIM
        y = jnp.dot(xn, w_ref[:, lo:lo + RET_V_DIM], preferred_element_type=F32)
        qkvg[proj_slot, :, lo:lo + RET_V_DIM] = act(y).astype(BF16)

    def ret_operand(c, hd, col0, width):
        return qkvg[ret_slot, c * BLOCK:(c + 1) * BLOCK, col0 + hd * width:col0 + (hd + 1) * width]

    heads = range(RET_HEADS)
    decays = [_retention_decays(hd) for hd in heads]
    live = {}

    def prep_step(c, hd):
        live["kdt", c, hd] = (ret_operand(c, hd, _K0, RET_QK_DIM).astype(F32)
                              * decays[hd][2]).T.astype(BF16)

    def qk_step(c, hd):
        live["qk", c, hd] = lax.dot_general(
            ret_operand(c, hd, _Q0, RET_QK_DIM), ret_operand(c, hd, _K0, RET_QK_DIM),
            (((1,), (1,)), ((), ())), preferred_element_type=F32)

    def kv_step(c, hd):
        live["kv", c, hd] = jnp.dot(live.pop(("kdt", c, hd)), ret_operand(c, hd, _V0, RET_V_DIM),
                                    preferred_element_type=F32)

    def decay_step(c, hd):
        live["qkd", c, hd] = (live.pop(("qk", c, hd)) * decays[hd][0]).astype(BF16)
        live["stb", c, hd] = state[hd].astype(BF16)

    def output_step(c, hd):
        live["o", c, hd] = (
            jnp.dot(live.pop(("qkd", c, hd)), ret_operand(c, hd, _V0, RET_V_DIM),
                    preferred_element_type=F32)
            + jnp.dot(ret_operand(c, hd, _Q0, RET_QK_DIM), live.pop(("stb", c, hd)),
                      preferred_element_type=F32) * decays[hd][1])

    def post_step(c, hd):
        o = live.pop(("o", c, hd))
        state[hd] = decays[hd][3] * state[hd] + live.pop(("kv", c, hd))
        gate = ret_operand(c, hd, _G0, RET_V_DIM).astype(F32)
        y_ref[c * BLOCK:(c + 1) * BLOCK, hd * RET_V_DIM:(hd + 1) * RET_V_DIM] = (
            o * _rms_scale(o) * gate).astype(BF16)

    proj_steps = []
    for hd in heads:
        proj_steps.append([functools.partial(rope_step, _Q0, hd, 1.0),
                           functools.partial(rope_step, _K0, hd, RET_QK_DIM ** -0.5)])
        proj_steps.append([functools.partial(value_step, _V0, hd, lambda y: y)])
        proj_steps.append([functools.partial(value_step, _G0, hd, _silu)])
    n_chunks = tm // BLOCK
    each_head = lambda step, c: [functools.partial(step, c, hd) for hd in heads]
    ret_steps = []
    for c in range(n_chunks):
        ret_steps.append(each_head(qk_step, c) + each_head(kv_step, c) + each_head(decay_step, c)
                         + (each_head(prep_step, c + 1) if c + 1 < n_chunks else []))
        ret_steps.append(each_head(output_step, c) + each_head(post_step, c))
    for step in each_head(prep_step, 0):
        step()
    for proj, ret in itertools.zip_longest(proj_steps, ret_steps, fillvalue=()):
        for step in (*proj, *ret):
            step()


def _l1_mix(h, gain, w_in, freq, cast_weights, tiles_per_seq):
    rows = h.shape[0]
    tm = ROW_TILE
    n_tiles = rows // tm
    half = RET_QK_DIM // 2
    cast_in, cast_out, cast_shapes = _cast_specs(cast_weights)
    return pl.pallas_call(
        functools.partial(_l1_mix_kernel, tiles_per_seq=tiles_per_seq, n_tiles=n_tiles,
                          n_cast=len(cast_in)),
        grid=(n_tiles + 1,),
        in_specs=[pl.BlockSpec((tm, D_MODEL), lambda s: (jnp.minimum(s, n_tiles - 1), 0)),
                  _const_spec((1, D_MODEL)), _const_spec(w_in.shape), _const_spec((1, half))]
        + cast_in,
        out_specs=[pl.BlockSpec((tm, 2 * D_MODEL), lambda s: (jnp.maximum(s - 1, 0), 0))]
        + cast_out,
        out_shape=[jax.ShapeDtypeStruct((rows, 2 * D_MODEL), BF16)] + cast_shapes,
        scratch_shapes=[pltpu.VMEM((2, tm, 6 * D_MODEL), BF16),
                        pltpu.VMEM((RET_HEADS, RET_QK_DIM, RET_V_DIM), F32),
                        pltpu.VMEM((tm, half), F32), pltpu.VMEM((tm, half), F32)],
        compiler_params=pltpu.CompilerParams(dimension_semantics=("arbitrary",)),
        name="l1_mix",
    )(h, gain, w_in, freq, *[w for _, w in cast_weights])


def _inv_freq(half, theta):
    return jnp.power(jnp.asarray(theta, F32), -jnp.arange(half, dtype=F32) / half)


def _block_diag(w):
    heads, wi, wo = w.shape
    eye = jnp.eye(heads, dtype=w.dtype)
    return (eye[:, None, :, None] * w[:, :, None, :]).reshape(heads * wi, heads * wo)


def kernel(x, meta_tokens, mix_norm_ab, ab_w_in, lru_conv_w, lru_conv_b, lru_w_a, lru_b_a, lru_w_i, lru_b_i, lru_lambda, q_norm, k_norm, attn_sinks, ab_w_out, mix_norm_ret, ret_w_in, ret_w_out, ffn_norm, ffn_w_gu, ffn_w_down):
    batch, seq, _ = x.shape
    seq_rows = META_PAD + N_META + seq
    tiles_per_seq = seq_rows // ROW_TILE

    x2d = x.reshape(batch * seq, D_MODEL)
    meta_blk = jnp.concatenate([jnp.zeros((META_PAD, D_MODEL), x.dtype),
                                meta_tokens.astype(x.dtype)], axis=0)

    f_att = _inv_freq(ROT_DIM // 2, ROPE_THETA)
    f_att = jnp.concatenate([f_att, f_att, jnp.zeros((ATT_HEAD_DIM - ROT_DIM,), F32)])
    f_att = jnp.tile(f_att, LANES // ATT_HEAD_DIM).reshape(1, LANES)
    f_ret = _inv_freq(RET_QK_DIM // 2, RET_THETA).reshape(1, RET_QK_DIM // 2)

    row_vec = lambda v: v.reshape(1, -1).astype(F32)
    two_heads = lambda v: jnp.tile(v.reshape(1, -1).astype(F32), (1, 2))

    xr, gate, q, k, v, w_gu0, w_down0 = _inproj0(
        x2d, meta_blk, row_vec(mix_norm_ab[0]), ab_w_in[0].astype(BF16), two_heads(q_norm[0]),
        two_heads(k_norm[0]), f_att, [(0, ffn_w_gu), (0, ffn_w_down)], batch, tiles_per_seq, seq)
    w_gates = jnp.concatenate([_block_diag(lru_w_a[0]), _block_diag(lru_w_i[0])], axis=1)
    b_gates = jnp.concatenate([lru_b_a[0].reshape(1, -1), lru_b_i[0].reshape(1, -1)], axis=1)
    w_att = ab_w_out[0][LRU_WIDTH:].reshape(ATT_KV_HEADS, ATT_GROUP, ATT_HEAD_DIM, D_MODEL)
    w_att = w_att.transpose(1, 0, 2, 3).reshape(Q_WIDTH, D_MODEL)
    w_out0 = jnp.concatenate([ab_w_out[0][:LRU_WIDTH], w_att], axis=0).astype(BF16)
    h, w_in1, w_out1 = _l0_mix_ffn(
        attn_sinks[0].astype(F32), x2d, meta_blk, xr, gate, q, k, v, lru_conv_w[0],
        row_vec(lru_conv_b[0]), w_gates.astype(BF16), b_gates.astype(F32), row_vec(lru_lambda[0]),
        w_out0, row_vec(ffn_norm[0]), w_gu0, w_down0, [(0, ret_w_in), (0, ret_w_out)],
        tiles_per_seq, seq)

    y_ret, w_gu1, w_down1 = _l1_mix(h, row_vec(mix_norm_ret[0]), w_in1, f_ret,
                                    [(1, ffn_w_gu), (1, ffn_w_down)], tiles_per_seq)
    out = _final_outproj_ffn(h, y_ret, w_out1, row_vec(ffn_norm[1]), w_gu1, w_down1,
                             batch, seq_rows, seq)
    return out.reshape(batch, seq, D_MODEL)
```

```python
import functools
import itertools
import math

import jax
import jax.numpy as jnp
from jax import lax
from jax.experimental import pallas as pl
from jax.experimental.pallas import tpu as pltpu

F32 = jnp.float32
BF16 = jnp.bfloat16

D_MODEL = 1024
N_META = 16
BLOCK = 128
META_PAD = BLOCK - N_META
RMS_EPS = 1e-6
NEG_INF = -1e30

LRU_WIDTH = 512
LRU_HEADS = 8
LRU_BLOCK_W = 64
CONV_W = 4
LRU_C = 8.0

ATT_HEADS = 8
ATT_KV_HEADS = 2
ATT_GROUP = ATT_HEADS // ATT_KV_HEADS
ATT_HEAD_DIM = 64
ROPE_THETA = 500000.0
ROT_DIM = 16
Q_WIDTH = 512
KV_WIDTH = 128
AB_IN_WIDTH = 2 * LRU_WIDTH + Q_WIDTH + 2 * KV_WIDTH

RET_HEADS = 4
RET_QK_DIM = 256
RET_V_DIM = 512
RET_THETA = 10000.0
RET_LOG_G = tuple(math.log1p(-(2.0 ** (-5.0 - h))) for h in range(RET_HEADS))

D_FF = 2816

LANES = 128
SUBLANES = 8
ROW_TILE = 640
OUT_TILE = 512
SQRT_GUARD = 1.1754944e-38
GATE_TILE = 256
INPROJ_CHUNK = 256
CAST_STEPS = 16
FFN_CHUNK = 256
DOWN_CHUNK = 256


def _rms_scale(x):
    return lax.rsqrt(jnp.mean(x * x, axis=-1, keepdims=True) + RMS_EPS)


def _sigmoid(x):
    return 0.5 * jnp.tanh(0.5 * x) + 0.5


def _silu(x):
    half = 0.5 * x
    return half + half * jnp.tanh(half)


def _gelu_tanh(x):
    half = 0.5 * x
    return half + half * jnp.tanh(0.7978845608028654 * (x + 0.044715 * (x * x * x)))


def _interleave(primary, secondary):
    out, done = [], 0
    for i, step in enumerate(primary):
        out.append(step)
        upto = ((i + 1) * len(secondary)) // len(primary)
        out.extend(secondary[done:upto])
        done = upto
    return out + list(secondary[done:])


def _const_spec(shape):
    zeros = (0,) * len(shape)
    return pl.BlockSpec(shape, lambda *_: zeros, pipeline_mode=pl.Buffered(1))


def _padded_rows(x_ref, meta_ref, first):
    xw = x_ref[...]
    tm = xw.shape[0]
    return jnp.concatenate([jnp.where(first, meta_ref[...], xw[0:BLOCK]),
                            jnp.where(first, xw[0:tm - BLOCK], xw[BLOCK:tm])], axis=0)


def _token_window_spec(tm, tiles_per_seq, seq, tile_of_step):
    def index(s):
        t = tile_of_step(s)
        start = jnp.maximum((t % tiles_per_seq) * tm - BLOCK, 0)
        return (pl.multiple_of((t // tiles_per_seq) * seq + start, BLOCK), 0)
    return pl.BlockSpec((pl.Element(tm), pl.Element(D_MODEL)), index)


def _rope_offsets(freq_ref, cos_ref, sin_ref):
    r = lax.broadcasted_iota(jnp.int32, cos_ref.shape, 0).astype(F32)
    ang = r * freq_ref[...]
    cos_ref[...] = jnp.cos(ang)
    sin_ref[...] = jnp.sin(ang)


def _rope_tables(base_pos, freq_ref, cos_ref, sin_ref):
    ang = base_pos.astype(F32) * freq_ref[...]
    cb = jnp.cos(ang)
    sb = jnp.sin(ang)
    oc = cos_ref[...]
    os_ = sin_ref[...]
    return cb * oc - sb * os_, sb * oc + cb * os_


def _cast_blocks(cast_refs):
    n = len(cast_refs) // 2
    for src, dst in zip(cast_refs[:n], cast_refs[n:]):
        dst[...] = src[...].astype(BF16)


def _cast_specs(weights):
    ins, outs, shapes = [], [], []
    step = lambda s: jnp.minimum(s, CAST_STEPS - 1)
    for layer, w in weights:
        _, rows, cols = w.shape
        blk = rows // CAST_STEPS
        ins.append(pl.BlockSpec((None, blk, cols), lambda s, layer=layer: (layer, step(s), 0)))
        outs.append(pl.BlockSpec((blk, cols), lambda s: (step(s), 0)))
        shapes.append(jax.ShapeDtypeStruct((rows, cols), BF16))
    return ins, outs, shapes


def _inproj0_kernel(*refs, tiles_per_seq, n_cast):
    (x_ref, meta_ref, gain_ref, w_ref, qg_ref, kg_ref, freq_ref) = refs[:7]
    cast_in = refs[7:7 + n_cast]
    xr_ref, gate_ref, q_ref, k_ref, v_ref = refs[7 + n_cast:12 + n_cast]
    cast_out = refs[12 + n_cast:12 + 2 * n_cast]
    ocos, osin = refs[12 + 2 * n_cast:]
    step = pl.program_id(0)
    tile_in_seq = step % tiles_per_seq

    @pl.when(step == 0)
    def _():
        _rope_offsets(freq_ref, ocos, osin)

    _cast_blocks(cast_in + cast_out)

    h = _padded_rows(x_ref, meta_ref, tile_in_seq == 0)
    rows = h.shape[0]
    xn = (h * _rms_scale(h) * gain_ref[...]).astype(BF16)

    q0 = 2 * LRU_WIDTH
    k0 = q0 + Q_WIDTH
    chunk_dot = lambda lo: jnp.dot(xn, w_ref[:, lo:lo + INPROJ_CHUNK], preferred_element_type=F32)
    y_q = [chunk_dot(q0 + lo) for lo in range(0, Q_WIDTH, INPROJ_CHUNK)]
    y_kv = chunk_dot(k0)
    v_ref[...] = y_kv[:, KV_WIDTH:].astype(BF16)

    cos, sin = _rope_tables(tile_in_seq * rows - META_PAD, freq_ref, ocos, osin)
    lane = lax.broadcasted_iota(jnp.int32, (1, LANES), 1) & (ATT_HEAD_DIM - 1)
    half = ROT_DIM // 2
    sin_lo = sin * jnp.where(lane < half, -1.0, 0.0)
    sin_hi = sin * jnp.where((lane >= half) & (lane < ROT_DIM), 1.0, 0.0)
    hi = lax.broadcasted_iota(jnp.int32, (2 * LANES, LANES), 0) & (LANES - 1)
    hj = lax.broadcasted_iota(jnp.int32, (2 * LANES, LANES), 1)
    head_mean = jnp.where(hi // ATT_HEAD_DIM == hj // ATT_HEAD_DIM,
                          1.0 / ATT_HEAD_DIM, 0.0).astype(BF16)

    def norm_rope(x, gain, scale):
        sq = x * x
        sq_hi = sq.astype(BF16)
        sq_lo = (sq - sq_hi.astype(F32)).astype(BF16)
        ms = jnp.dot(jnp.concatenate([sq_hi, sq_lo], axis=1), head_mean,
                     preferred_element_type=F32)
        xg = x * gain
        rot = (xg * cos + pltpu.roll(xg, LANES - half, 1) * sin_lo
               + pltpu.roll(xg, half, 1) * sin_hi)
        return rot * (lax.rsqrt(ms + RMS_EPS) * scale)

    def qk_step(j):
        if j < Q_WIDTH // LANES:
            lo = (j * LANES) % INPROJ_CHUNK
            q_ref[:, j * LANES:(j + 1) * LANES] = norm_rope(
                y_q[j * LANES // INPROJ_CHUNK][:, lo:lo + LANES], qg_ref[...],
                ATT_HEAD_DIM ** -0.5).astype(BF16)
        else:
            k_ref[...] = norm_rope(y_kv[:, :KV_WIDTH], kg_ref[...], 1.0).astype(BF16)

    def proj_step(c):
        out_ref = xr_ref if c < LRU_WIDTH // INPROJ_CHUNK else gate_ref
        dst = slice((c * INPROJ_CHUNK) % LRU_WIDTH, (c * INPROJ_CHUNK) % LRU_WIDTH + INPROJ_CHUNK)
        out_ref[:, dst] = chunk_dot(c * INPROJ_CHUNK)

    proj_steps = [functools.partial(proj_step, c) for c in range(2 * LRU_WIDTH // INPROJ_CHUNK)]
    qk_steps = [functools.partial(qk_step, j) for j in range(Q_WIDTH // LANES + 1)]
    for step in _interleave(proj_steps, qk_steps):
        step()


def _inproj0(x2d, meta_blk, gain, w_in, q_gain, k_gain, freq, cast_weights, batch,
             tiles_per_seq, seq):
    tm = ROW_TILE
    rows = batch * tiles_per_seq * tm
    row = lambda w: pl.BlockSpec((tm, w), lambda i: (i, 0))
    cast_in, cast_out, cast_shapes = _cast_specs(cast_weights)
    return pl.pallas_call(
        functools.partial(_inproj0_kernel, tiles_per_seq=tiles_per_seq, n_cast=len(cast_in)),
        grid=(rows // tm,),
        in_specs=[_token_window_spec(tm, tiles_per_seq, seq, lambda s: s),
                  _const_spec((BLOCK, D_MODEL)), _const_spec((1, D_MODEL)),
                  _const_spec((D_MODEL, AB_IN_WIDTH)), _const_spec((1, LANES)),
                  _const_spec((1, LANES)), _const_spec((1, LANES))] + cast_in,
        out_specs=[row(LRU_WIDTH), row(LRU_WIDTH), row(Q_WIDTH), row(KV_WIDTH), row(KV_WIDTH)]
        + cast_out,
        out_shape=[jax.ShapeDtypeStruct((rows, LRU_WIDTH), F32),
                   jax.ShapeDtypeStruct((rows, LRU_WIDTH), F32),
                   jax.ShapeDtypeStruct((rows, Q_WIDTH), BF16),
                   jax.ShapeDtypeStruct((rows, KV_WIDTH), BF16),
                   jax.ShapeDtypeStruct((rows, KV_WIDTH), BF16)] + cast_shapes,
        scratch_shapes=[pltpu.VMEM((tm, LANES), F32), pltpu.VMEM((tm, LANES), F32)],
        compiler_params=pltpu.CompilerParams(dimension_semantics=("arbitrary",)),
        name="l0_inproj",
    )(x2d, meta_blk, gain, w_in, q_gain, k_gain, freq, *[w for _, w in cast_weights])


def _attn_probs(n, sinks, q, kc, kp, km):
    j = lax.broadcasted_iota(jnp.int32, (BLOCK, BLOCK), 0)
    i = lax.broadcasted_iota(jnp.int32, (BLOCK, BLOCK), 1)
    causal = j <= i
    win_ok = n >= jnp.where(causal, 1, 2)
    meta_ok = (j >= META_PAD) & (n >= jnp.where(causal, 0, 1))
    contract_last = (((1,), (1,)), ((), ()))

    qs = jnp.concatenate([q[:, a * ATT_HEAD_DIM:(a + 1) * ATT_HEAD_DIM]
                          for a in range(ATT_GROUP)], axis=0)
    s_all = lax.dot_general(jnp.concatenate([kc, kp, km], axis=0), qs, contract_last,
                            preferred_element_type=F32)
    s_c, s_p, s_m = s_all[:BLOCK], s_all[BLOCK:2 * BLOCK], s_all[2 * BLOCK:]
    p_c, p_p, p_m, inv_den = [], [], [], []
    for a in range(ATT_GROUP):
        head = slice(a * BLOCK, (a + 1) * BLOCK)
        sw = jnp.where(win_ok, jnp.where(causal, s_c[:, head], s_p[:, head]), NEG_INF)
        sm = jnp.where(meta_ok, s_m[:, head], NEG_INF)
        m = jnp.maximum(jnp.maximum(jnp.max(sw, axis=0, keepdims=True),
                                    jnp.max(sm, axis=0, keepdims=True)), sinks[a])
        pw = jnp.exp(sw - m)
        pm = jnp.exp(sm - m)
        den = (jnp.sum(pw, axis=0, keepdims=True) + jnp.sum(pm, axis=0, keepdims=True)
               + jnp.exp(sinks[a] - m))
        inv_den.append(1.0 / den)
        p_c.append(jnp.where(causal, pw, 0.0).astype(BF16))
        p_p.append(jnp.where(causal, 0.0, pw).astype(BF16))
        p_m.append(pm.astype(BF16))
    lanes = lambda parts: jnp.concatenate(parts, axis=1)
    return lanes(p_c), lanes(p_p), lanes(p_m), lanes(inv_den)


def _attn_out_t(probs, vc, vp, vm):
    p_c, p_p, p_m, inv_den = probs
    contract_rows = (((0,), (0,)), ((), ()))
    o_t = lax.dot_general(jnp.concatenate([vc, vp, vm], axis=0),
                          jnp.concatenate([p_c, p_p, p_m], axis=0), contract_rows,
                          preferred_element_type=F32)
    return o_t * inv_den


def _attn_untranspose(o_t):
    o_t = jnp.concatenate(o_t, axis=0)
    return jnp.concatenate([o_t[:, a * BLOCK:(a + 1) * BLOCK].T for a in range(ATT_GROUP)],
                           axis=1).astype(BF16)


def _lru_gates(lo, xr_ref, rows, cw, cb, wg_ref, xbuf):
    tl = BLOCK
    cols = slice(lo, lo + GATE_TILE)
    x = xr_ref[rows, cols]
    xbuf[SUBLANES:SUBLANES + tl, cols] = x
    xc = x * cw[CONV_W - 1:CONV_W, cols] + cb[:, cols]
    for d in range(1, CONV_W):
        xc = xc + (xbuf[SUBLANES - d:SUBLANES - d + tl, cols]
                   * cw[CONV_W - 1 - d:CONV_W - d, cols])
    xbuf[0:SUBLANES, cols] = x[tl - SUBLANES:tl]
    xcb = xc.astype(BF16)
    ga_r = jnp.dot(xcb, wg_ref[cols, cols], preferred_element_type=F32)
    ga_i = jnp.dot(xcb, wg_ref[cols, LRU_WIDTH + lo:LRU_WIDTH + lo + GATE_TILE],
                   preferred_element_type=F32)
    return xc, ga_r, ga_i


def _lru_scan(n, xc, ga_r, ga_i, grp, gate_ref, rows, bg, softplus, hcar, y_ref):
    tl = BLOCK
    row = lax.broadcasted_iota(jnp.int32, (tl, 1), 0)
    t = n * tl + row
    r = _sigmoid(ga_r + bg[:, grp])
    gi = _sigmoid(ga_i + bg[:, LRU_WIDTH + grp.start:LRU_WIDTH + grp.stop])
    log_a = (-LRU_C * softplus[:, grp]) * r
    a = jnp.exp(log_a)
    mult2 = jnp.tanh(-log_a) * (a * a + 1.0)
    mult = mult2 * lax.rsqrt(jnp.maximum(mult2, SQRT_GUARD))
    mult = jnp.where(t == META_PAD, 1.0, mult)
    b = jnp.where(t < META_PAD, 0.0, mult * gi * xc)

    d = 1
    while d < SUBLANES:
        keep = row >= d
        b = jnp.where(keep, a * pltpu.roll(b, d, 0), 0.0) + b
        a = jnp.where(keep, a * pltpu.roll(a, d, 0), a)
        d *= 2
    while d < tl:
        b = jnp.concatenate([b[:d], a[d:] * b[:tl - d] + b[d:]], axis=0)
        a = jnp.concatenate([a[:d], a[d:] * a[:tl - d]], axis=0)
        d *= 2
    h = b + a * hcar[0:1, grp]
    hcar[:, grp] = jnp.broadcast_to(h[tl - 1:tl], (SUBLANES, LANES))
    y_ref[rows, grp] = (_gelu_tanh(gate_ref[rows, grp]) * h).astype(BF16)


def _ffn(h1, gain, wgu_ref, wd_ref):
    xn = (h1 * _rms_scale(h1) * gain).astype(BF16)
    acts = []
    for c in range(D_FF // FFN_CHUNK):
        lo = c * FFN_CHUNK
        g = jnp.dot(xn, wgu_ref[:, lo:lo + FFN_CHUNK], preferred_element_type=F32)
        u = jnp.dot(xn, wgu_ref[:, D_FF + lo:D_FF + lo + FFN_CHUNK], preferred_element_type=F32)
        acts.append((_silu(g) * u).astype(BF16))
    act = jnp.concatenate(acts, axis=1)
    return h1 + jnp.dot(act, wd_ref[...], preferred_element_type=F32)


def _l0_mix_ffn_kernel(*refs, tiles_per_seq, n_tiles, n_cast):
    (sink_ref, x_ref, meta_ref, xr_ref, gate_ref, q_ref, k_ref, v_ref, cw_ref, cb_ref, wg_ref,
     bg_ref, lam_ref, wo_ref, gain_ref, wgu_ref, wd_ref) = refs[:17]
    cast_in = refs[17:17 + n_cast]
    o_ref = refs[17 + n_cast]
    cast_out = refs[18 + n_cast:18 + 2 * n_cast]
    ybuf, xbuf, hcar, kprev, vprev, kmeta, vmeta = refs[18 + 2 * n_cast:]
    s = pl.program_id(0)
    tile_in_seq = jnp.minimum(s, n_tiles - 1) % tiles_per_seq
    blocks_per_tile = x_ref.shape[0] // BLOCK
    last = slice((blocks_per_tile - 1) * BLOCK, blocks_per_tile * BLOCK)
    _cast_blocks(cast_in + cast_out)

    @pl.when(s == 0)
    def _():
        ybuf[...] = jnp.zeros_like(ybuf)

    @pl.when(tile_in_seq == 0)
    def _():
        xbuf[0:SUBLANES, :] = jnp.zeros((SUBLANES, LRU_WIDTH), F32)
        hcar[...] = jnp.zeros_like(hcar)
        kmeta[...] = k_ref[0:BLOCK, :]
        vmeta[...] = v_ref[0:BLOCK, :]
        kprev[...] = k_ref[0:BLOCK, :]
        vprev[...] = v_ref[0:BLOCK, :]

    ffn_first = jnp.maximum(s - 1, 0) % tiles_per_seq == 0
    h1 = (_padded_rows(x_ref, meta_ref, ffn_first)
          + jnp.dot(ybuf[...], wo_ref[...], preferred_element_type=F32))
    xn = (h1 * _rms_scale(h1) * gain_ref[...]).astype(BF16)

    cw = cw_ref[...]
    cb = cb_ref[...]
    bg = bg_ref[...]
    z = -lam_ref[...]
    softplus = jnp.maximum(z, 0.0) + jnp.log1p(jnp.exp(-jnp.abs(z)))

    live = {}

    def block_rows(blk):
        return tile_in_seq * blocks_per_tile + blk, slice(blk * BLOCK, (blk + 1) * BLOCK)

    def lru_gates_step(blk, lo):
        _, rows = block_rows(blk)
        live["lru", blk, lo] = _lru_gates(lo, xr_ref, rows, cw, cb, wg_ref, xbuf)

    def lru_scan_step(blk, lo, sub):
        n, rows = block_rows(blk)
        xc, ga_r, ga_i = live["lru", blk, lo]
        part = slice(sub, sub + LANES)
        _lru_scan(n, xc[:, part], ga_r[:, part], ga_i[:, part],
                  slice(lo + sub, lo + sub + LANES), gate_ref, rows, bg, softplus, hcar, ybuf)
        if sub + LANES == GATE_TILE:
            del live["lru", blk, lo]

    def kv_blocks(ref, prev_ref, meta_ref, blk, g):
        _, rows = block_rows(blk)
        lanes = slice(g * ATT_HEAD_DIM, (g + 1) * ATT_HEAD_DIM)
        prev = prev_ref[:, lanes] if blk == 0 else ref[(blk - 1) * BLOCK:blk * BLOCK, lanes]
        return ref[rows, lanes], prev, meta_ref[:, lanes]

    def attn_probs_step(blk, g):
        n, rows = block_rows(blk)
        width = ATT_GROUP * ATT_HEAD_DIM
        sinks = [sink_ref[g * ATT_GROUP + a] for a in range(ATT_GROUP)]
        live["probs", blk, g] = _attn_probs(n, sinks, q_ref[rows, g * width:(g + 1) * width],
                                            *kv_blocks(k_ref, kprev, kmeta, blk, g))

    def attn_out_step(blk, g):
        _, rows = block_rows(blk)
        live["out", blk, g] = _attn_out_t(live.pop(("probs", blk, g)),
                                          *kv_blocks(v_ref, vprev, vmeta, blk, g))
        if g + 1 == ATT_KV_HEADS:
            ybuf[rows, LRU_WIDTH:LRU_WIDTH + Q_WIDTH] = _attn_untranspose(
                [live.pop(("out", blk, h)) for h in range(ATT_KV_HEADS)])

    acts = []

    def gate_step(c):
        lo = c * FFN_CHUNK
        live["g", c] = jnp.dot(xn, wgu_ref[:, lo:lo + FFN_CHUNK], preferred_element_type=F32)

    def up_step(c):
        lo = c * FFN_CHUNK
        u = jnp.dot(xn, wgu_ref[:, D_FF + lo:D_FF + lo + FFN_CHUNK], preferred_element_type=F32)
        acts.append((_silu(live.pop(("g", c))) * u).astype(BF16))

    def down_step(c):
        cols = slice(c * DOWN_CHUNK, (c + 1) * DOWN_CHUNK)
        if len(acts) > 1:
            acts[:] = [jnp.concatenate(acts, axis=1)]
        o_ref[:, cols] = h1[:, cols] + jnp.dot(acts[0], wd_ref[:, cols],
                                               preferred_element_type=F32)

    ffn_steps = []
    for c in range(D_FF // FFN_CHUNK):
        ffn_steps += [functools.partial(gate_step, c), functools.partial(up_step, c)]
    ffn_steps += [functools.partial(down_step, c) for c in range(D_MODEL // DOWN_CHUNK)]
    mix_steps = []
    for blk in range(blocks_per_tile):
        for lo in range(0, LRU_WIDTH, GATE_TILE):
            mix_steps.append(functools.partial(lru_gates_step, blk, lo))
            mix_steps += [functools.partial(lru_scan_step, blk, lo, sub)
                          for sub in range(0, GATE_TILE, LANES)]
        mix_steps += [functools.partial(attn_probs_step, blk, g) for g in range(ATT_KV_HEADS)]
        mix_steps += [functools.partial(attn_out_step, blk, g) for g in range(ATT_KV_HEADS)]
    for step in _interleave(ffn_steps, mix_steps):
        step()
    kprev[...] = k_ref[last, :]
    vprev[...] = v_ref[last, :]


def _l0_mix_ffn(sinks, x2d, meta_blk, xr, gate, q, k, v, conv_w, conv_b, w_gates, b_gates, lam,
                w_out, gain, w_gu, w_down, cast_weights, tiles_per_seq, seq):
    rows = xr.shape[0]
    tm = ROW_TILE
    n_tiles = rows // tm
    mix = lambda w: pl.BlockSpec((tm, w), lambda s: (jnp.minimum(s, n_tiles - 1), 0))
    cast_in, cast_out, cast_shapes = _cast_specs(cast_weights)
    return pl.pallas_call(
        functools.partial(_l0_mix_ffn_kernel, tiles_per_seq=tiles_per_seq, n_tiles=n_tiles,
                          n_cast=len(cast_in)),
        grid=(n_tiles + 1,),
        in_specs=[pl.BlockSpec(memory_space=pltpu.SMEM),
                  _token_window_spec(tm, tiles_per_seq, seq, lambda s: jnp.maximum(s - 1, 0)),
                  _const_spec((BLOCK, D_MODEL)),
                  mix(LRU_WIDTH), mix(LRU_WIDTH), mix(Q_WIDTH), mix(KV_WIDTH), mix(KV_WIDTH),
                  _const_spec((CONV_W, LRU_WIDTH)), _const_spec((1, LRU_WIDTH)),
                  _const_spec((LRU_WIDTH, 2 * LRU_WIDTH)), _const_spec((1, 2 * LRU_WIDTH)),
                  _const_spec((1, LRU_WIDTH)), _const_spec(w_out.shape), _const_spec((1, D_MODEL)),
                  _const_spec(w_gu.shape), _const_spec(w_down.shape)] + cast_in,
        out_specs=[pl.BlockSpec((tm, D_MODEL), lambda s: (jnp.maximum(s - 1, 0), 0))] + cast_out,
        out_shape=[jax.ShapeDtypeStruct((rows, D_MODEL), F32)] + cast_shapes,
        scratch_shapes=[pltpu.VMEM((tm, LRU_WIDTH + Q_WIDTH), BF16),
                        pltpu.VMEM((SUBLANES + BLOCK, LRU_WIDTH), F32),
                        pltpu.VMEM((SUBLANES, LRU_WIDTH), F32),
                        pltpu.VMEM((BLOCK, KV_WIDTH), BF16), pltpu.VMEM((BLOCK, KV_WIDTH), BF16),
                        pltpu.VMEM((BLOCK, KV_WIDTH), BF16), pltpu.VMEM((BLOCK, KV_WIDTH), BF16)],
        compiler_params=pltpu.CompilerParams(dimension_semantics=("arbitrary",)),
        name="l0_mix_ffn",
    )(sinks, x2d, meta_blk, xr, gate, q, k, v, conv_w, conv_b, w_gates, b_gates, lam, w_out, gain,
      w_gu, w_down, *[w for _, w in cast_weights])


def _outproj_ffn_kernel(h_ref, y_ref, wo_ref, gain_ref, wgu_ref, wd_ref, o_ref):
    h1 = h_ref[...] + jnp.dot(y_ref[...], wo_ref[...], preferred_element_type=F32)
    o_ref[...] = _ffn(h1, gain_ref[...], wgu_ref, wd_ref)


def _final_outproj_ffn(h, y, w_out, gain, w_gu, w_down, batch, seq_rows, seq):
    tm = OUT_TILE
    tiles = seq // tm
    win = lambda w: pl.BlockSpec((pl.Element(tm), pl.Element(w)),
                                 lambda b, j: (pl.multiple_of(b * seq_rows + BLOCK + j * tm, BLOCK),
                                               0))
    return pl.pallas_call(
        _outproj_ffn_kernel,
        grid=(batch, tiles),
        in_specs=[win(D_MODEL), win(y.shape[1]), _const_spec(w_out.shape),
                  _const_spec((1, D_MODEL)), _const_spec(w_gu.shape), _const_spec(w_down.shape)],
        out_specs=pl.BlockSpec((tm, D_MODEL), lambda b, j: (b * tiles + j, 0)),
        out_shape=jax.ShapeDtypeStruct((batch * seq, D_MODEL), F32),
        compiler_params=pltpu.CompilerParams(dimension_semantics=("arbitrary", "arbitrary")),
        name="l1_outproj_ffn",
    )(h, y, w_out, gain, w_gu, w_down)


_Q0, _K0, _V0, _G0 = 0, D_MODEL, 2 * D_MODEL, 4 * D_MODEL


def _retention_decays(hd):
    ii = lax.broadcasted_iota(jnp.int32, (BLOCK, BLOCK), 0)
    jj = lax.broadcasted_iota(jnp.int32, (BLOCK, BLOCK), 1)
    diff = (ii - jj).astype(F32)
    idx = lax.broadcasted_iota(jnp.int32, (BLOCK, 1), 0).astype(F32)
    log_g = RET_LOG_G[hd]
    decay_intra = jnp.where(diff >= 0.0, jnp.exp(jnp.maximum(diff, 0.0) * log_g), 0.0)
    return (decay_intra, jnp.exp((idx + 1.0) * log_g), jnp.exp((BLOCK - 1.0 - idx) * log_g),
            math.exp(BLOCK * log_g))


def _l1_mix_kernel(*refs, tiles_per_seq, n_tiles, n_cast):
    h_ref, gain_ref, w_ref, freq_ref = refs[:4]
    cast_in = refs[4:4 + n_cast]
    y_ref = refs[4 + n_cast]
    cast_out = refs[5 + n_cast:5 + 2 * n_cast]
    qkvg, state, ocos, osin = refs[5 + 2 * n_cast:]
    _cast_blocks(cast_in + cast_out)
    s = pl.program_id(0)
    proj_slot = s % 2
    ret_slot = 1 - proj_slot
    proj_tile_in_seq = jnp.minimum(s, n_tiles - 1) % tiles_per_seq
    ret_tile_in_seq = jnp.maximum(s - 1, 0) % tiles_per_seq
    tm = h_ref.shape[0]

    @pl.when(s == 0)
    def _():
        _rope_offsets(freq_ref, ocos, osin)
        qkvg[1] = jnp.zeros(qkvg.shape[1:], BF16)

    @pl.when(ret_tile_in_seq == 0)
    def _():
        state[...] = jnp.zeros_like(state)

    h = h_ref[...]
    xn = (h * _rms_scale(h) * gain_ref[...]).astype(BF16)
    cos, sin = _rope_tables(proj_tile_in_seq * tm - META_PAD, freq_ref, ocos, osin)
    half = RET_QK_DIM // 2

    def rope_step(col0, hd, scale):
        lo = col0 + hd * RET_QK_DIM
        y = jnp.dot(xn, w_ref[:, lo:lo + RET_QK_DIM], preferred_element_type=F32)
        x1 = y[:, :half]
        x2 = y[:, half:]
        qkvg[proj_slot, :, lo:lo + half] = ((x1 * cos - x2 * sin) * scale).astype(BF16)
        qkvg[proj_slot, :, lo + half:lo + RET_QK_DIM] = ((x2 * cos + x1 * sin)
                                                         * scale).astype(BF16)

    def value_step(col0, hd, act):
        lo = col0 + hd * RET_V_DIM
        y = jnp.dot(xn, w_ref[:, lo:lo + RET_V_DIM], preferred_element_type=F32)
        qkvg[proj_slot, :, lo:lo + RET_V_DIM] = act(y).astype(BF16)

    def ret_operand(c, hd, col0, width):
        return qkvg[ret_slot, c * BLOCK:(c + 1) * BLOCK, col0 + hd * width:col0 + (hd + 1) * width]

    heads = range(RET_HEADS)
    decays = [_retention_decays(hd) for hd in heads]
    live = {}

    def prep_step(c, hd):
        live["kdt", c, hd] = (ret_operand(c, hd, _K0, RET_QK_DIM).astype(F32)
                              * decays[hd][2]).T.astype(BF16)

    def qk_step(c, hd):
        live["qk", c, hd] = lax.dot_general(
            ret_operand(c, hd, _Q0, RET_QK_DIM), ret_operand(c, hd, _K0, RET_QK_DIM),
            (((1,), (1,)), ((), ())), preferred_element_type=F32)

    def decay_step(c, hd):
        qkd = (live.pop(("qk", c, hd)) * decays[hd][0]).astype(BF16)
        live["lhs", c, hd] = jnp.concatenate([qkd, live.pop(("kdt", c, hd))], axis=0)
        live["stb", c, hd] = state[hd].astype(BF16)

    def output_step(c, hd):
        both = jnp.dot(live.pop(("lhs", c, hd)), ret_operand(c, hd, _V0, RET_V_DIM),
                       preferred_element_type=F32)
        live["kv", c, hd] = both[BLOCK:]
        live["o", c, hd] = both[:BLOCK] + jnp.dot(
            ret_operand(c, hd, _Q0, RET_QK_DIM), live.pop(("stb", c, hd)),
            preferred_element_type=F32) * decays[hd][1]

    def post_step(c, hd):
        o = live.pop(("o", c, hd))
        state[hd] = decays[hd][3] * state[hd] + live.pop(("kv", c, hd))
        gate = ret_operand(c, hd, _G0, RET_V_DIM).astype(F32)
        y_ref[c * BLOCK:(c + 1) * BLOCK, hd * RET_V_DIM:(hd + 1) * RET_V_DIM] = (
            o * _rms_scale(o) * gate).astype(BF16)

    proj_steps = []
    for hd in heads:
        proj_steps.append([functools.partial(rope_step, _Q0, hd, 1.0),
                           functools.partial(rope_step, _K0, hd, RET_QK_DIM ** -0.5)])
        proj_steps.append([functools.partial(value_step, _V0, hd, lambda y: y)])
        proj_steps.append([functools.partial(value_step, _G0, hd, _silu)])
    n_chunks = tm // BLOCK
    each_head = lambda step, c: [functools.partial(step, c, hd) for hd in heads]
    ret_steps = []
    for c in range(n_chunks):
        ret_steps.append(each_head(qk_step, c) + each_head(decay_step, c)
                         + (each_head(prep_step, c + 1) if c + 1 < n_chunks else []))
        ret_steps.append(each_head(output_step, c) + each_head(post_step, c))
    for step in each_head(prep_step, 0):
        step()
    for proj, ret in itertools.zip_longest(proj_steps, ret_steps, fillvalue=()):
        for step in (*proj, *ret):
            step()


def _l1_mix(h, gain, w_in, freq, cast_weights, tiles_per_seq):
    rows = h.shape[0]
    tm = ROW_TILE
    n_tiles = rows // tm
    half = RET_QK_DIM // 2
    cast_in, cast_out, cast_shapes = _cast_specs(cast_weights)
    return pl.pallas_call(
        functools.partial(_l1_mix_kernel, tiles_per_seq=tiles_per_seq, n_tiles=n_tiles,
                          n_cast=len(cast_in)),
        grid=(n_tiles + 1,),
        in_specs=[pl.BlockSpec((tm, D_MODEL), lambda s: (jnp.minimum(s, n_tiles - 1), 0)),
                  _const_spec((1, D_MODEL)), _const_spec(w_in.shape), _const_spec((1, half))]
        + cast_in,
        out_specs=[pl.BlockSpec((tm, 2 * D_MODEL), lambda s: (jnp.maximum(s - 1, 0), 0))]
        + cast_out,
        out_shape=[jax.ShapeDtypeStruct((rows, 2 * D_MODEL), BF16)] + cast_shapes,
        scratch_shapes=[pltpu.VMEM((2, tm, 6 * D_MODEL), BF16),
                        pltpu.VMEM((RET_HEADS, RET_QK_DIM, RET_V_DIM), F32),
                        pltpu.VMEM((tm, half), F32), pltpu.VMEM((tm, half), F32)],
        compiler_params=pltpu.CompilerParams(dimension_semantics=("arbitrary",)),
        name="l1_mix",
    )(h, gain, w_in, freq, *[w for _, w in cast_weights])


def _inv_freq(half, theta):
    return jnp.power(jnp.asarray(theta, F32), -jnp.arange(half, dtype=F32) / half)


def _block_diag(w):
    heads, wi, wo = w.shape
    eye = jnp.eye(heads, dtype=w.dtype)
    return (eye[:, None, :, None] * w[:, :, None, :]).reshape(heads * wi, heads * wo)


def kernel(x, meta_tokens, mix_norm_ab, ab_w_in, lru_conv_w, lru_conv_b, lru_w_a, lru_b_a, lru_w_i, lru_b_i, lru_lambda, q_norm, k_norm, attn_sinks, ab_w_out, mix_norm_ret, ret_w_in, ret_w_out, ffn_norm, ffn_w_gu, ffn_w_down):
    batch, seq, _ = x.shape
    seq_rows = META_PAD + N_META + seq
    tiles_per_seq = seq_rows // ROW_TILE

    x2d = x.reshape(batch * seq, D_MODEL)
    meta_blk = jnp.concatenate([jnp.zeros((META_PAD, D_MODEL), x.dtype),
                                meta_tokens.astype(x.dtype)], axis=0)

    f_att = _inv_freq(ROT_DIM // 2, ROPE_THETA)
    f_att = jnp.concatenate([f_att, f_att, jnp.zeros((ATT_HEAD_DIM - ROT_DIM,), F32)])
    f_att = jnp.tile(f_att, LANES // ATT_HEAD_DIM).reshape(1, LANES)
    f_ret = _inv_freq(RET_QK_DIM // 2, RET_THETA).reshape(1, RET_QK_DIM // 2)

    row_vec = lambda v: v.reshape(1, -1).astype(F32)
    two_heads = lambda v: jnp.tile(v.reshape(1, -1).astype(F32), (1, 2))

    xr, gate, q, k, v, w_gu0, w_down0 = _inproj0(
        x2d, meta_blk, row_vec(mix_norm_ab[0]), ab_w_in[0].astype(BF16), two_heads(q_norm[0]),
        two_heads(k_norm[0]), f_att, [(0, ffn_w_gu), (0, ffn_w_down)], batch, tiles_per_seq, seq)
    w_gates = jnp.concatenate([_block_diag(lru_w_a[0]), _block_diag(lru_w_i[0])], axis=1)
    b_gates = jnp.concatenate([lru_b_a[0].reshape(1, -1), lru_b_i[0].reshape(1, -1)], axis=1)
    w_att = ab_w_out[0][LRU_WIDTH:].reshape(ATT_KV_HEADS, ATT_GROUP, ATT_HEAD_DIM, D_MODEL)
    w_att = w_att.transpose(1, 0, 2, 3).reshape(Q_WIDTH, D_MODEL)
    w_out0 = jnp.concatenate([ab_w_out[0][:LRU_WIDTH], w_att], axis=0).astype(BF16)
    h, w_in1, w_out1 = _l0_mix_ffn(
        attn_sinks[0].astype(F32), x2d, meta_blk, xr, gate, q, k, v, lru_conv_w[0],
        row_vec(lru_conv_b[0]), w_gates.astype(BF16), b_gates.astype(F32), row_vec(lru_lambda[0]),
        w_out0, row_vec(ffn_norm[0]), w_gu0, w_down0, [(0, ret_w_in), (0, ret_w_out)],
        tiles_per_seq, seq)

    y_ret, w_gu1, w_down1 = _l1_mix(h, row_vec(mix_norm_ret[0]), w_in1, f_ret,
                                    [(1, ffn_w_gu), (1, ffn_w_down)], tiles_per_seq)
    out = _final_outproj_ffn(h, y_ret, w_out1, row_vec(ffn_norm[1]), w_gu1, w_down1,
                             batch, seq_rows, seq)
    return out.reshape(batch, seq, D_MODEL)
```

```python
import functools
import itertools
import math

import jax
import jax.numpy as jnp
from jax import lax
from jax.experimental import pallas as pl
from jax.experimental.pallas import tpu as pltpu

F32 = jnp.float32
BF16 = jnp.bfloat16

D_MODEL = 1024
N_META = 16
BLOCK = 128
META_PAD = BLOCK - N_META
RMS_EPS = 1e-6
NEG_INF = -1e30

LRU_WIDTH = 512
LRU_HEADS = 8
LRU_BLOCK_W = 64
CONV_W = 4
LRU_C = 8.0

ATT_HEADS = 8
ATT_KV_HEADS = 2
ATT_GROUP = ATT_HEADS // ATT_KV_HEADS
ATT_HEAD_DIM = 64
ROPE_THETA = 500000.0
ROT_DIM = 16
Q_WIDTH = 512
KV_WIDTH = 128
AB_IN_WIDTH = 2 * LRU_WIDTH + Q_WIDTH + 2 * KV_WIDTH

RET_HEADS = 4
RET_QK_DIM = 256
RET_V_DIM = 512
RET_THETA = 10000.0
RET_LOG_G = tuple(math.log1p(-(2.0 ** (-5.0 - h))) for h in range(RET_HEADS))

D_FF = 2816

LANES = 128
SUBLANES = 8
ROW_TILE = 640
OUT_TILE = 512
SQRT_GUARD = 1.1754944e-38
GATE_TILE = 256
INPROJ_CHUNK = 256
CAST_STEPS = 16
FFN_CHUNK = 256
DOWN_CHUNK = 256


def _rms_scale(x):
    return lax.rsqrt(jnp.mean(x * x, axis=-1, keepdims=True) + RMS_EPS)


def _sigmoid(x):
    return 0.5 * jnp.tanh(0.5 * x) + 0.5


def _silu(x):
    half = 0.5 * x
    return half + half * jnp.tanh(half)


def _gelu_tanh(x):
    half = 0.5 * x
    return half + half * jnp.tanh(0.7978845608028654 * (x + 0.044715 * (x * x * x)))


def _interleave(primary, secondary):
    out, done = [], 0
    for i, step in enumerate(primary):
        out.append(step)
        upto = ((i + 1) * len(secondary)) // len(primary)
        out.extend(secondary[done:upto])
        done = upto
    return out + list(secondary[done:])


def _const_spec(shape):
    zeros = (0,) * len(shape)
    return pl.BlockSpec(shape, lambda *_: zeros, pipeline_mode=pl.Buffered(1))


def _padded_rows(x_ref, meta_ref, first):
    xw = x_ref[...]
    tm = xw.shape[0]
    return jnp.concatenate([jnp.where(first, meta_ref[...], xw[0:BLOCK]),
                            jnp.where(first, xw[0:tm - BLOCK], xw[BLOCK:tm])], axis=0)


def _token_window_spec(tm, tiles_per_seq, seq, tile_of_step):
    def index(s):
        t = tile_of_step(s)
        start = jnp.maximum((t % tiles_per_seq) * tm - BLOCK, 0)
        return (pl.multiple_of((t // tiles_per_seq) * seq + start, BLOCK), 0)
    return pl.BlockSpec((pl.Element(tm), pl.Element(D_MODEL)), index)


def _rope_offsets(freq_ref, cos_ref, sin_ref):
    r = lax.broadcasted_iota(jnp.int32, cos_ref.shape, 0).astype(F32)
    ang = r * freq_ref[...]
    cos_ref[...] = jnp.cos(ang)
    sin_ref[...] = jnp.sin(ang)


def _rope_tables(base_pos, freq_ref, cos_ref, sin_ref):
    ang = base_pos.astype(F32) * freq_ref[...]
    cb = jnp.cos(ang)
    sb = jnp.sin(ang)
    oc = cos_ref[...]
    os_ = sin_ref[...]
    return cb * oc - sb * os_, sb * oc + cb * os_


def _cast_blocks(cast_refs):
    n = len(cast_refs) // 2
    for src, dst in zip(cast_refs[:n], cast_refs[n:]):
        dst[...] = src[...].astype(BF16)


def _cast_specs(weights):
    ins, outs, shapes = [], [], []
    step = lambda s: jnp.minimum(s, CAST_STEPS - 1)
    for layer, w in weights:
        _, rows, cols = w.shape
        blk = rows // CAST_STEPS
        ins.append(pl.BlockSpec((None, blk, cols), lambda s, layer=layer: (layer, step(s), 0)))
        outs.append(pl.BlockSpec((blk, cols), lambda s: (step(s), 0)))
        shapes.append(jax.ShapeDtypeStruct((rows, cols), BF16))
    return ins, outs, shapes


def _inproj0_kernel(*refs, tiles_per_seq, n_cast):
    (x_ref, meta_ref, gain_ref, w_ref, qg_ref, kg_ref, freq_ref) = refs[:7]
    cast_in = refs[7:7 + n_cast]
    xr_ref, gate_ref, q_ref, k_ref, v_ref = refs[7 + n_cast:12 + n_cast]
    cast_out = refs[12 + n_cast:12 + 2 * n_cast]
    ocos, osin = refs[12 + 2 * n_cast:]
    step = pl.program_id(0)
    tile_in_seq = step % tiles_per_seq

    @pl.when(step == 0)
    def _():
        _rope_offsets(freq_ref, ocos, osin)

    _cast_blocks(cast_in + cast_out)

    h = _padded_rows(x_ref, meta_ref, tile_in_seq == 0)
    rows = h.shape[0]
    xn = (h * _rms_scale(h) * gain_ref[...]).astype(BF16)

    q0 = 2 * LRU_WIDTH
    k0 = q0 + Q_WIDTH
    chunk_dot = lambda lo: jnp.dot(xn, w_ref[:, lo:lo + INPROJ_CHUNK], preferred_element_type=F32)
    y_q = [chunk_dot(q0 + lo) for lo in range(0, Q_WIDTH, INPROJ_CHUNK)]
    y_kv = chunk_dot(k0)
    v_ref[...] = y_kv[:, KV_WIDTH:].astype(BF16)

    cos, sin = _rope_tables(tile_in_seq * rows - META_PAD, freq_ref, ocos, osin)
    lane = lax.broadcasted_iota(jnp.int32, (1, LANES), 1) & (ATT_HEAD_DIM - 1)
    half = ROT_DIM // 2
    sin_lo = sin * jnp.where(lane < half, -1.0, 0.0)
    sin_hi = sin * jnp.where((lane >= half) & (lane < ROT_DIM), 1.0, 0.0)
    hi = lax.broadcasted_iota(jnp.int32, (2 * LANES, LANES), 0) & (LANES - 1)
    hj = lax.broadcasted_iota(jnp.int32, (2 * LANES, LANES), 1)
    head_mean = jnp.where(hi // ATT_HEAD_DIM == hj // ATT_HEAD_DIM,
                          1.0 / ATT_HEAD_DIM, 0.0).astype(BF16)

    def norm_rope(x, gain, scale):
        sq = x * x
        sq_hi = sq.astype(BF16)
        sq_lo = (sq - sq_hi.astype(F32)).astype(BF16)
        ms = jnp.dot(jnp.concatenate([sq_hi, sq_lo], axis=1), head_mean,
                     preferred_element_type=F32)
        xg = x * gain
        rot = (xg * cos + pltpu.roll(xg, LANES - half, 1) * sin_lo
               + pltpu.roll(xg, half, 1) * sin_hi)
        return rot * (lax.rsqrt(ms + RMS_EPS) * scale)

    def qk_step(j):
        if j < Q_WIDTH // LANES:
            lo = (j * LANES) % INPROJ_CHUNK
            q_ref[:, j * LANES:(j + 1) * LANES] = norm_rope(
                y_q[j * LANES // INPROJ_CHUNK][:, lo:lo + LANES], qg_ref[...],
                ATT_HEAD_DIM ** -0.5).astype(BF16)
        else:
            k_ref[...] = norm_rope(y_kv[:, :KV_WIDTH], kg_ref[...], 1.0).astype(BF16)

    def proj_step(c):
        out_ref = xr_ref if c < LRU_WIDTH // INPROJ_CHUNK else gate_ref
        dst = slice((c * INPROJ_CHUNK) % LRU_WIDTH, (c * INPROJ_CHUNK) % LRU_WIDTH + INPROJ_CHUNK)
        out_ref[:, dst] = chunk_dot(c * INPROJ_CHUNK)

    proj_steps = [functools.partial(proj_step, c) for c in range(2 * LRU_WIDTH // INPROJ_CHUNK)]
    qk_steps = [functools.partial(qk_step, j) for j in range(Q_WIDTH // LANES + 1)]
    for step in _interleave(proj_steps, qk_steps):
        step()


def _inproj0(x2d, meta_blk, gain, w_in, q_gain, k_gain, freq, cast_weights, batch,
             tiles_per_seq, seq):
    tm = ROW_TILE
    rows = batch * tiles_per_seq * tm
    row = lambda w: pl.BlockSpec((tm, w), lambda i: (i, 0))
    cast_in, cast_out, cast_shapes = _cast_specs(cast_weights)
    return pl.pallas_call(
        functools.partial(_inproj0_kernel, tiles_per_seq=tiles_per_seq, n_cast=len(cast_in)),
        grid=(rows // tm,),
        in_specs=[_token_window_spec(tm, tiles_per_seq, seq, lambda s: s),
                  _const_spec((BLOCK, D_MODEL)), _const_spec((1, D_MODEL)),
                  _const_spec((D_MODEL, AB_IN_WIDTH)), _const_spec((1, LANES)),
                  _const_spec((1, LANES)), _const_spec((1, LANES))] + cast_in,
        out_specs=[row(LRU_WIDTH), row(LRU_WIDTH), row(Q_WIDTH), row(KV_WIDTH), row(KV_WIDTH)]
        + cast_out,
        out_shape=[jax.ShapeDtypeStruct((rows, LRU_WIDTH), F32),
                   jax.ShapeDtypeStruct((rows, LRU_WIDTH), F32),
                   jax.ShapeDtypeStruct((rows, Q_WIDTH), BF16),
                   jax.ShapeDtypeStruct((rows, KV_WIDTH), BF16),
                   jax.ShapeDtypeStruct((rows, KV_WIDTH), BF16)] + cast_shapes,
        scratch_shapes=[pltpu.VMEM((tm, LANES), F32), pltpu.VMEM((tm, LANES), F32)],
        compiler_params=pltpu.CompilerParams(dimension_semantics=("arbitrary",)),
        name="l0_inproj",
    )(x2d, meta_blk, gain, w_in, q_gain, k_gain, freq, *[w for _, w in cast_weights])


def _attn_probs(n, sinks, q, kc, kp, km):
    j = lax.broadcasted_iota(jnp.int32, (BLOCK, BLOCK), 0)
    i = lax.broadcasted_iota(jnp.int32, (BLOCK, BLOCK), 1)
    causal = j <= i
    win_ok = n >= jnp.where(causal, 1, 2)
    meta_ok = (j >= META_PAD) & (n >= jnp.where(causal, 0, 1))
    contract_last = (((1,), (1,)), ((), ()))

    qs = jnp.concatenate([q[:, a * ATT_HEAD_DIM:(a + 1) * ATT_HEAD_DIM]
                          for a in range(ATT_GROUP)], axis=0)
    s_c = lax.dot_general(kc, qs, contract_last, preferred_element_type=F32)
    s_p = lax.dot_general(kp, qs, contract_last, preferred_element_type=F32)
    s_m = lax.dot_general(km, qs, contract_last, preferred_element_type=F32)
    p_c, p_p, p_m, inv_den = [], [], [], []
    for a in range(ATT_GROUP):
        head = slice(a * BLOCK, (a + 1) * BLOCK)
        sw = jnp.where(win_ok, jnp.where(causal, s_c[:, head], s_p[:, head]), NEG_INF)
        sm = jnp.where(meta_ok, s_m[:, head], NEG_INF)
        m = jnp.maximum(jnp.maximum(jnp.max(sw, axis=0, keepdims=True),
                                    jnp.max(sm, axis=0, keepdims=True)), sinks[a])
        pw = jnp.exp(sw - m)
        pm = jnp.exp(sm - m)
        den = (jnp.sum(pw, axis=0, keepdims=True) + jnp.sum(pm, axis=0, keepdims=True)
               + jnp.exp(sinks[a] - m))
        inv_den.append(1.0 / den)
        p_c.append(jnp.where(causal, pw, 0.0).astype(BF16))
        p_p.append(jnp.where(causal, 0.0, pw).astype(BF16))
        p_m.append(pm.astype(BF16))
    lanes = lambda parts: jnp.concatenate(parts, axis=1)
    return lanes(p_c), lanes(p_p), lanes(p_m), lanes(inv_den)


def _attn_out_t(probs, vc, vp, vm):
    p_c, p_p, p_m, inv_den = probs
    contract_rows = (((0,), (0,)), ((), ()))
    o_t = (lax.dot_general(vc, p_c, contract_rows, preferred_element_type=F32)
           + lax.dot_general(vp, p_p, contract_rows, preferred_element_type=F32)
           + lax.dot_general(vm, p_m, contract_rows, preferred_element_type=F32))
    return o_t * inv_den


def _attn_untranspose(o_t):
    o_t = jnp.concatenate(o_t, axis=0)
    return jnp.concatenate([o_t[:, a * BLOCK:(a + 1) * BLOCK].T for a in range(ATT_GROUP)],
                           axis=1).astype(BF16)


def _lru_gates(lo, xr_ref, rows, cw, cb, wg_ref, xbuf):
    tl = BLOCK
    cols = slice(lo, lo + GATE_TILE)
    x = xr_ref[rows, cols]
    xbuf[SUBLANES:SUBLANES + tl, cols] = x
    xc = x * cw[CONV_W - 1:CONV_W, cols] + cb[:, cols]
    for d in range(1, CONV_W):
        xc = xc + (xbuf[SUBLANES - d:SUBLANES - d + tl, cols]
                   * cw[CONV_W - 1 - d:CONV_W - d, cols])
    xbuf[0:SUBLANES, cols] = x[tl - SUBLANES:tl]
    xcb = xc.astype(BF16)
    ga_r = jnp.dot(xcb, wg_ref[cols, cols], preferred_element_type=F32)
    ga_i = jnp.dot(xcb, wg_ref[cols, LRU_WIDTH + lo:LRU_WIDTH + lo + GATE_TILE],
                   preferred_element_type=F32)
    return xc, ga_r, ga_i


def _lru_scan(n, xc, ga_r, ga_i, grp, gate_ref, rows, bg, softplus, hcar, y_ref):
    tl = BLOCK
    row = lax.broadcasted_iota(jnp.int32, (tl, 1), 0)
    t = n * tl + row
    r = _sigmoid(ga_r + bg[:, grp])
    gi = _sigmoid(ga_i + bg[:, LRU_WIDTH + grp.start:LRU_WIDTH + grp.stop])
    log_a = (-LRU_C * softplus[:, grp]) * r
    a = jnp.exp(log_a)
    mult2 = jnp.tanh(-log_a) * (a * a + 1.0)
    mult = mult2 * lax.rsqrt(jnp.maximum(mult2, SQRT_GUARD))
    mult = jnp.where(t == META_PAD, 1.0, mult)
    b = jnp.where(t < META_PAD, 0.0, mult * gi * xc)

    d = 1
    while d < SUBLANES:
        keep = row >= d
        b = jnp.where(keep, a * pltpu.roll(b, d, 0), 0.0) + b
        a = jnp.where(keep, a * pltpu.roll(a, d, 0), a)
        d *= 2
    while d < tl:
        b = jnp.concatenate([b[:d], a[d:] * b[:tl - d] + b[d:]], axis=0)
        a = jnp.concatenate([a[:d], a[d:] * a[:tl - d]], axis=0)
        d *= 2
    h = b + a * hcar[0:1, grp]
    hcar[:, grp] = jnp.broadcast_to(h[tl - 1:tl], (SUBLANES, LANES))
    y_ref[rows, grp] = (_gelu_tanh(gate_ref[rows, grp]) * h).astype(BF16)


def _ffn(h1, gain, wgu_ref, wd_ref):
    xn = (h1 * _rms_scale(h1) * gain).astype(BF16)
    acts = []
    for c in range(D_FF // FFN_CHUNK):
        lo = c * FFN_CHUNK
        g = jnp.dot(xn, wgu_ref[:, lo:lo + FFN_CHUNK], preferred_element_type=F32)
        u = jnp.dot(xn, wgu_ref[:, D_FF + lo:D_FF + lo + FFN_CHUNK], preferred_element_type=F32)
        acts.append((_silu(g) * u).astype(BF16))
    act = jnp.concatenate(acts, axis=1)
    return h1 + jnp.dot(act, wd_ref[...], preferred_element_type=F32)


def _l0_mix_ffn_kernel(*refs, tiles_per_seq, n_tiles, n_cast):
    (sink_ref, x_ref, meta_ref, xr_ref, gate_ref, q_ref, k_ref, v_ref, cw_ref, cb_ref, wg_ref,
     bg_ref, lam_ref, wo_ref, gain_ref, wgu_ref, wd_ref) = refs[:17]
    cast_in = refs[17:17 + n_cast]
    o_ref = refs[17 + n_cast]
    cast_out = refs[18 + n_cast:18 + 2 * n_cast]
    ybuf, xbuf, hcar, kprev, vprev, kmeta, vmeta = refs[18 + 2 * n_cast:]
    s = pl.program_id(0)
    tile_in_seq = jnp.minimum(s, n_tiles - 1) % tiles_per_seq
    blocks_per_tile = x_ref.shape[0] // BLOCK
    last = slice((blocks_per_tile - 1) * BLOCK, blocks_per_tile * BLOCK)
    _cast_blocks(cast_in + cast_out)

    @pl.when(s == 0)
    def _():
        ybuf[...] = jnp.zeros_like(ybuf)

    @pl.when(tile_in_seq == 0)
    def _():
        xbuf[0:SUBLANES, :] = jnp.zeros((SUBLANES, LRU_WIDTH), F32)
        hcar[...] = jnp.zeros_like(hcar)
        kmeta[...] = k_ref[0:BLOCK, :]
        vmeta[...] = v_ref[0:BLOCK, :]
        kprev[...] = k_ref[0:BLOCK, :]
        vprev[...] = v_ref[0:BLOCK, :]

    ffn_first = jnp.maximum(s - 1, 0) % tiles_per_seq == 0
    h1 = (_padded_rows(x_ref, meta_ref, ffn_first)
          + jnp.dot(ybuf[...], wo_ref[...], preferred_element_type=F32))
    xn = (h1 * _rms_scale(h1) * gain_ref[...]).astype(BF16)

    cw = cw_ref[...]
    cb = cb_ref[...]
    bg = bg_ref[...]
    z = -lam_ref[...]
    softplus = jnp.maximum(z, 0.0) + jnp.log1p(jnp.exp(-jnp.abs(z)))

    live = {}

    def block_rows(blk):
        return tile_in_seq * blocks_per_tile + blk, slice(blk * BLOCK, (blk + 1) * BLOCK)

    def lru_gates_step(blk, lo):
        _, rows = block_rows(blk)
        live["lru", blk, lo] = _lru_gates(lo, xr_ref, rows, cw, cb, wg_ref, xbuf)

    def lru_scan_step(blk, lo, sub):
        n, rows = block_rows(blk)
        xc, ga_r, ga_i = live["lru", blk, lo]
        part = slice(sub, sub + LANES)
        _lru_scan(n, xc[:, part], ga_r[:, part], ga_i[:, part],
                  slice(lo + sub, lo + sub + LANES), gate_ref, rows, bg, softplus, hcar, ybuf)
        if sub + LANES == GATE_TILE:
            del live["lru", blk, lo]

    def kv_blocks(ref, prev_ref, meta_ref, blk, g):
        _, rows = block_rows(blk)
        lanes = slice(g * ATT_HEAD_DIM, (g + 1) * ATT_HEAD_DIM)
        prev = prev_ref[:, lanes] if blk == 0 else ref[(blk - 1) * BLOCK:blk * BLOCK, lanes]
        return ref[rows, lanes], prev, meta_ref[:, lanes]

    def attn_probs_step(blk, g):
        n, rows = block_rows(blk)
        width = ATT_GROUP * ATT_HEAD_DIM
        sinks = [sink_ref[g * ATT_GROUP + a] for a in range(ATT_GROUP)]
        live["probs", blk, g] = _attn_probs(n, sinks, q_ref[rows, g * width:(g + 1) * width],
                                            *kv_blocks(k_ref, kprev, kmeta, blk, g))

    def attn_out_step(blk, g):
        _, rows = block_rows(blk)
        live["out", blk, g] = _attn_out_t(live.pop(("probs", blk, g)),
                                          *kv_blocks(v_ref, vprev, vmeta, blk, g))
        if g + 1 == ATT_KV_HEADS:
            ybuf[rows, LRU_WIDTH:LRU_WIDTH + Q_WIDTH] = _attn_untranspose(
                [live.pop(("out", blk, h)) for h in range(ATT_KV_HEADS)])

    acts = []

    def gate_step(c):
        lo = c * FFN_CHUNK
        live["g", c] = jnp.dot(xn, wgu_ref[:, lo:lo + FFN_CHUNK], preferred_element_type=F32)

    def up_step(c):
        lo = c * FFN_CHUNK
        u = jnp.dot(xn, wgu_ref[:, D_FF + lo:D_FF + lo + FFN_CHUNK], preferred_element_type=F32)
        acts.append((_silu(live.pop(("g", c))) * u).astype(BF16))

    def down_step(c):
        cols = slice(c * DOWN_CHUNK, (c + 1) * DOWN_CHUNK)
        if len(acts) > 1:
            acts[:] = [jnp.concatenate(acts, axis=1)]
        o_ref[:, cols] = h1[:, cols] + jnp.dot(acts[0], wd_ref[:, cols],
                                               preferred_element_type=F32)

    ffn_steps = []
    for c in range(D_FF // FFN_CHUNK):
        ffn_steps += [functools.partial(gate_step, c), functools.partial(up_step, c)]
    ffn_steps += [functools.partial(down_step, c) for c in range(D_MODEL // DOWN_CHUNK)]
    mix_steps = []
    for blk in range(blocks_per_tile):
        for lo in range(0, LRU_WIDTH, GATE_TILE):
            mix_steps.append(functools.partial(lru_gates_step, blk, lo))
            mix_steps += [functools.partial(lru_scan_step, blk, lo, sub)
                          for sub in range(0, GATE_TILE, LANES)]
        mix_steps += [functools.partial(attn_probs_step, blk, g) for g in range(ATT_KV_HEADS)]
        mix_steps += [functools.partial(attn_out_step, blk, g) for g in range(ATT_KV_HEADS)]
    for step in _interleave(ffn_steps, mix_steps):
        step()
    kprev[...] = k_ref[last, :]
    vprev[...] = v_ref[last, :]


def _l0_mix_ffn(sinks, x2d, meta_blk, xr, gate, q, k, v, conv_w, conv_b, w_gates, b_gates, lam,
                w_out, gain, w_gu, w_down, cast_weights, tiles_per_seq, seq):
    rows = xr.shape[0]
    tm = ROW_TILE
    n_tiles = rows // tm
    mix = lambda w: pl.BlockSpec((tm, w), lambda s: (jnp.minimum(s, n_tiles - 1), 0))
    cast_in, cast_out, cast_shapes = _cast_specs(cast_weights)
    return pl.pallas_call(
        functools.partial(_l0_mix_ffn_kernel, tiles_per_seq=tiles_per_seq, n_tiles=n_tiles,
                          n_cast=len(cast_in)),
        grid=(n_tiles + 1,),
        in_specs=[pl.BlockSpec(memory_space=pltpu.SMEM),
                  _token_window_spec(tm, tiles_per_seq, seq, lambda s: jnp.maximum(s - 1, 0)),
                  _const_spec((BLOCK, D_MODEL)),
                  mix(LRU_WIDTH), mix(LRU_WIDTH), mix(Q_WIDTH), mix(KV_WIDTH), mix(KV_WIDTH),
                  _const_spec((CONV_W, LRU_WIDTH)), _const_spec((1, LRU_WIDTH)),
                  _const_spec((LRU_WIDTH, 2 * LRU_WIDTH)), _const_spec((1, 2 * LRU_WIDTH)),
                  _const_spec((1, LRU_WIDTH)), _const_spec(w_out.shape), _const_spec((1, D_MODEL)),
                  _const_spec(w_gu.shape), _const_spec(w_down.shape)] + cast_in,
        out_specs=[pl.BlockSpec((tm, D_MODEL), lambda s: (jnp.maximum(s - 1, 0), 0))] + cast_out,
        out_shape=[jax.ShapeDtypeStruct((rows, D_MODEL), F32)] + cast_shapes,
        scratch_shapes=[pltpu.VMEM((tm, LRU_WIDTH + Q_WIDTH), BF16),
                        pltpu.VMEM((SUBLANES + BLOCK, LRU_WIDTH), F32),
                        pltpu.VMEM((SUBLANES, LRU_WIDTH), F32),
                        pltpu.VMEM((BLOCK, KV_WIDTH), BF16), pltpu.VMEM((BLOCK, KV_WIDTH), BF16),
                        pltpu.VMEM((BLOCK, KV_WIDTH), BF16), pltpu.VMEM((BLOCK, KV_WIDTH), BF16)],
        compiler_params=pltpu.CompilerParams(dimension_semantics=("arbitrary",)),
        name="l0_mix_ffn",
    )(sinks, x2d, meta_blk, xr, gate, q, k, v, conv_w, conv_b, w_gates, b_gates, lam, w_out, gain,
      w_gu, w_down, *[w for _, w in cast_weights])


def _outproj_ffn_kernel(h_ref, y_ref, wo_ref, gain_ref, wgu_ref, wd_ref, o_ref):
    h1 = h_ref[...] + jnp.dot(y_ref[...], wo_ref[...], preferred_element_type=F32)
    o_ref[...] = _ffn(h1, gain_ref[...], wgu_ref, wd_ref)


def _final_outproj_ffn(h, y, w_out, gain, w_gu, w_down, batch, seq_rows, seq):
    tm = OUT_TILE
    tiles = seq // tm
    win = lambda w: pl.BlockSpec((pl.Element(tm), pl.Element(w)),
                                 lambda b, j: (pl.multiple_of(b * seq_rows + BLOCK + j * tm, BLOCK),
                                               0))
    return pl.pallas_call(
        _outproj_ffn_kernel,
        grid=(batch, tiles),
        in_specs=[win(D_MODEL), win(y.shape[1]), _const_spec(w_out.shape),
                  _const_spec((1, D_MODEL)), _const_spec(w_gu.shape), _const_spec(w_down.shape)],
        out_specs=pl.BlockSpec((tm, D_MODEL), lambda b, j: (b * tiles + j, 0)),
        out_shape=jax.ShapeDtypeStruct((batch * seq, D_MODEL), F32),
        compiler_params=pltpu.CompilerParams(dimension_semantics=("arbitrary", "arbitrary")),
        name="l1_outproj_ffn",
    )(h, y, w_out, gain, w_gu, w_down)


_Q0, _K0, _V0, _G0 = 0, D_MODEL, 2 * D_MODEL, 4 * D_MODEL


def _retention_decays(hd):
    ii = lax.broadcasted_iota(jnp.int32, (BLOCK, BLOCK), 0)
    jj = lax.broadcasted_iota(jnp.int32, (BLOCK, BLOCK), 1)
    diff = (ii - jj).astype(F32)
    idx = lax.broadcasted_iota(jnp.int32, (BLOCK, 1), 0).astype(F32)
    log_g = RET_LOG_G[hd]
    decay_intra = jnp.where(diff >= 0.0, jnp.exp(jnp.maximum(diff, 0.0) * log_g), 0.0)
    return (decay_intra, jnp.exp((idx + 1.0) * log_g), jnp.exp((BLOCK - 1.0 - idx) * log_g),
            math.exp(BLOCK * log_g))


def _l1_mix_kernel(*refs, tiles_per_seq, n_tiles, n_cast):
    h_ref, gain_ref, w_ref, freq_ref = refs[:4]
    cast_in = refs[4:4 + n_cast]
    y_ref = refs[4 + n_cast]
    cast_out = refs[5 + n_cast:5 + 2 * n_cast]
    qkvg, state, ocos, osin = refs[5 + 2 * n_cast:]
    _cast_blocks(cast_in + cast_out)
    s = pl.program_id(0)
    proj_slot = s % 2
    ret_slot = 1 - proj_slot
    proj_tile_in_seq = jnp.minimum(s, n_tiles - 1) % tiles_per_seq
    ret_tile_in_seq = jnp.maximum(s - 1, 0) % tiles_per_seq
    tm = h_ref.shape[0]

    @pl.when(s == 0)
    def _():
        _rope_offsets(freq_ref, ocos, osin)
        qkvg[1] = jnp.zeros(qkvg.shape[1:], BF16)

    @pl.when(ret_tile_in_seq == 0)
    def _():
        state[...] = jnp.zeros_like(state)

    h = h_ref[...]
    xn = (h * _rms_scale(h) * gain_ref[...]).astype(BF16)
    cos, sin = _rope_tables(proj_tile_in_seq * tm - META_PAD, freq_ref, ocos, osin)
    half = RET_QK_DIM // 2

    def rope_step(col0, hd, scale):
        lo = col0 + hd * RET_QK_DIM
        y = jnp.dot(xn, w_ref[:, lo:lo + RET_QK_DIM], preferred_element_type=F32)
        x1 = y[:, :half]
        x2 = y[:, half:]
        qkvg[proj_slot, :, lo:lo + half] = ((x1 * cos - x2 * sin) * scale).astype(BF16)
        qkvg[proj_slot, :, lo + half:lo + RET_QK_DIM] = ((x2 * cos + x1 * sin)
                                                         * scale).astype(BF16)

    def value_step(col0, hd, act):
        lo = col0 + hd * RET_V_DIM
        y = jnp.dot(xn, w_ref[:, lo:lo + RET_V_DIM], preferred_element_type=F32)
        qkvg[proj_slot, :, lo:lo + RET_V_DIM] = act(y).astype(BF16)

    def ret_operand(c, hd, col0, width):
        return qkvg[ret_slot, c * BLOCK:(c + 1) * BLOCK, col0 + hd * width:col0 + (hd + 1) * width]

    heads = range(RET_HEADS)
    decays = [_retention_decays(hd) for hd in heads]
    live = {}

    def prep_step(c, hd):
        live["kdt", c, hd] = (ret_operand(c, hd, _K0, RET_QK_DIM).astype(F32)
                              * decays[hd][2]).T.astype(BF16)

    def qk_step(c, hd):
        live["qk", c, hd] = lax.dot_general(
            ret_operand(c, hd, _Q0, RET_QK_DIM), ret_operand(c, hd, _K0, RET_QK_DIM),
            (((1,), (1,)), ((), ())), preferred_element_type=F32)

    def decay_step(c, hd):
        qkd = (live.pop(("qk", c, hd)) * decays[hd][0]).astype(BF16)
        live["lhs", c, hd] = jnp.concatenate([qkd, live.pop(("kdt", c, hd))], axis=0)
        live["stb", c, hd] = state[hd].astype(BF16)

    def output_step(c, hd):
        both = jnp.dot(live.pop(("lhs", c, hd)), ret_operand(c, hd, _V0, RET_V_DIM),
                       preferred_element_type=F32)
        live["kv", c, hd] = both[BLOCK:]
        live["o", c, hd] = both[:BLOCK] + jnp.dot(
            ret_operand(c, hd, _Q0, RET_QK_DIM), live.pop(("stb", c, hd)),
            preferred_element_type=F32) * decays[hd][1]

    def post_step(c, hd):
        o = live.pop(("o", c, hd))
        state[hd] = decays[hd][3] * state[hd] + live.pop(("kv", c, hd))
        gate = ret_operand(c, hd, _G0, RET_V_DIM).astype(F32)
        y_ref[c * BLOCK:(c + 1) * BLOCK, hd * RET_V_DIM:(hd + 1) * RET_V_DIM] = (
            o * _rms_scale(o) * gate).astype(BF16)

    proj_steps = []
    for hd in heads:
        proj_steps.append([functools.partial(rope_step, _Q0, hd, 1.0),
                           functools.partial(rope_step, _K0, hd, RET_QK_DIM ** -0.5)])
        proj_steps.append([functools.partial(value_step, _V0, hd, lambda y: y)])
        proj_steps.append([functools.partial(value_step, _G0, hd, _silu)])
    n_chunks = tm // BLOCK
    each_head = lambda step, c: [functools.partial(step, c, hd) for hd in heads]
    ret_steps = []
    for c in range(n_chunks):
        ret_steps.append(each_head(qk_step, c) + each_head(decay_step, c)
                         + (each_head(prep_step, c + 1) if c + 1 < n_chunks else []))
        ret_steps.append(each_head(output_step, c) + each_head(post_step, c))
    for step in each_head(prep_step, 0):
        step()
    for proj, ret in itertools.zip_longest(proj_steps, ret_steps, fillvalue=()):
        for step in (*proj, *ret):
            step()


def _l1_mix(h, gain, w_in, freq, cast_weights, tiles_per_seq):
    rows = h.shape[0]
    tm = ROW_TILE
    n_tiles = rows // tm
    half = RET_QK_DIM // 2
    cast_in, cast_out, cast_shapes = _cast_specs(cast_weights)
    return pl.pallas_call(
        functools.partial(_l1_mix_kernel, tiles_per_seq=tiles_per_seq, n_tiles=n_tiles,
                          n_cast=len(cast_in)),
        grid=(n_tiles + 1,),
        in_specs=[pl.BlockSpec((tm, D_MODEL), lambda s: (jnp.minimum(s, n_tiles - 1), 0)),
                  _const_spec((1, D_MODEL)), _const_spec(w_in.shape), _const_spec((1, half))]
        + cast_in,
        out_specs=[pl.BlockSpec((tm, 2 * D_MODEL), lambda s: (jnp.maximum(s - 1, 0), 0))]
        + cast_out,
        out_shape=[jax.ShapeDtypeStruct((rows, 2 * D_MODEL), BF16)] + cast_shapes,
        scratch_shapes=[pltpu.VMEM((2, tm, 6 * D_MODEL), BF16),
                        pltpu.VMEM((RET_HEADS, RET_QK_DIM, RET_V_DIM), F32),
                        pltpu.VMEM((tm, half), F32), pltpu.VMEM((tm, half), F32)],
        compiler_params=pltpu.CompilerParams(dimension_semantics=("arbitrary",)),
        name="l1_mix",
    )(h, gain, w_in, freq, *[w for _, w in cast_weights])


def _inv_freq(half, theta):
    return jnp.power(jnp.asarray(theta, F32), -jnp.arange(half, dtype=F32) / half)


def _block_diag(w):
    heads, wi, wo = w.shape
    eye = jnp.eye(heads, dtype=w.dtype)
    return (eye[:, None, :, None] * w[:, :, None, :]).reshape(heads * wi, heads * wo)


def kernel(x, meta_tokens, mix_norm_ab, ab_w_in, lru_conv_w, lru_conv_b, lru_w_a, lru_b_a, lru_w_i, lru_b_i, lru_lambda, q_norm, k_norm, attn_sinks, ab_w_out, mix_norm_ret, ret_w_in, ret_w_out, ffn_norm, ffn_w_gu, ffn_w_down):
    batch, seq, _ = x.shape
    seq_rows = META_PAD + N_META + seq
    tiles_per_seq = seq_rows // ROW_TILE

    x2d = x.reshape(batch * seq, D_MODEL)
    meta_blk = jnp.concatenate([jnp.zeros((META_PAD, D_MODEL), x.dtype),
                                meta_tokens.astype(x.dtype)], axis=0)

    f_att = _inv_freq(ROT_DIM // 2, ROPE_THETA)
    f_att = jnp.concatenate([f_att, f_att, jnp.zeros((ATT_HEAD_DIM - ROT_DIM,), F32)])
    f_att = jnp.tile(f_att, LANES // ATT_HEAD_DIM).reshape(1, LANES)
    f_ret = _inv_freq(RET_QK_DIM // 2, RET_THETA).reshape(1, RET_QK_DIM // 2)

    row_vec = lambda v: v.reshape(1, -1).astype(F32)
    two_heads = lambda v: jnp.tile(v.reshape(1, -1).astype(F32), (1, 2))

    xr, gate, q, k, v, w_gu0, w_down0 = _inproj0(
        x2d, meta_blk, row_vec(mix_norm_ab[0]), ab_w_in[0].astype(BF16), two_heads(q_norm[0]),
        two_heads(k_norm[0]), f_att, [(0, ffn_w_gu), (0, ffn_w_down)], batch, tiles_per_seq, seq)
    w_gates = jnp.concatenate([_block_diag(lru_w_a[0]), _block_diag(lru_w_i[0])], axis=1)
    b_gates = jnp.concatenate([lru_b_a[0].reshape(1, -1), lru_b_i[0].reshape(1, -1)], axis=1)
    w_att = ab_w_out[0][LRU_WIDTH:].reshape(ATT_KV_HEADS, ATT_GROUP, ATT_HEAD_DIM, D_MODEL)
    w_att = w_att.transpose(1, 0, 2, 3).reshape(Q_WIDTH, D_MODEL)
    w_out0 = jnp.concatenate([ab_w_out[0][:LRU_WIDTH], w_att], axis=0).astype(BF16)
    h, w_in1, w_out1 = _l0_mix_ffn(
        attn_sinks[0].astype(F32), x2d, meta_blk, xr, gate, q, k, v, lru_conv_w[0],
        row_vec(lru_conv_b[0]), w_gates.astype(BF16), b_gates.astype(F32), row_vec(lru_lambda[0]),
        w_out0, row_vec(ffn_norm[0]), w_gu0, w_down0, [(0, ret_w_in), (0, ret_w_out)],
        tiles_per_seq, seq)

    y_ret, w_gu1, w_down1 = _l1_mix(h, row_vec(mix_norm_ret[0]), w_in1, f_ret,
                                    [(1, ffn_w_gu), (1, ffn_w_down)], tiles_per_seq)
    out = _final_outproj_ffn(h, y_ret, w_out1, row_vec(ffn_norm[1]), w_gu1, w_down1,
                             batch, seq_rows, seq)
    return out.reshape(batch, seq, D_MODEL)
```

```python
import functools
import itertools
import math

import jax
import jax.numpy as jnp
from jax import lax
from jax.experimental import pallas as pl
from jax.experimental.pallas import tpu as pltpu

F32 = jnp.float32
BF16 = jnp.bfloat16

D_MODEL = 1024
N_META = 16
BLOCK = 128
META_PAD = BLOCK - N_META
RMS_EPS = 1e-6
NEG_INF = -1e30

LRU_WIDTH = 512
LRU_HEADS = 8
LRU_BLOCK_W = 64
CONV_W = 4
LRU_C = 8.0

ATT_HEADS = 8
ATT_KV_HEADS = 2
ATT_GROUP = ATT_HEADS // ATT_KV_HEADS
ATT_HEAD_DIM = 64
ROPE_THETA = 500000.0
ROT_DIM = 16
Q_WIDTH = 512
KV_WIDTH = 128
AB_IN_WIDTH = 2 * LRU_WIDTH + Q_WIDTH + 2 * KV_WIDTH

RET_HEADS = 4
RET_QK_DIM = 256
RET_V_DIM = 512
RET_THETA = 10000.0
RET_LOG_G = tuple(math.log1p(-(2.0 ** (-5.0 - h))) for h in range(RET_HEADS))

D_FF = 2816

LANES = 128
SUBLANES = 8
ROW_TILE = 640
OUT_TILE = 512
SQRT_GUARD = 1.1754944e-38
GATE_TILE = 256
INPROJ_CHUNK = 256
CAST_STEPS = 16
FFN_CHUNK = 256
DOWN_CHUNK = 256


def _rms_scale(x):
    return lax.rsqrt(jnp.mean(x * x, axis=-1, keepdims=True) + RMS_EPS)


def _sigmoid(x):
    return 0.5 * jnp.tanh(0.5 * x) + 0.5


def _silu(x):
    half = 0.5 * x
    return half + half * jnp.tanh(half)


def _gelu_tanh(x):
    half = 0.5 * x
    return half + half * jnp.tanh(0.7978845608028654 * (x + 0.044715 * (x * x * x)))


def _interleave(primary, secondary):
    out, done = [], 0
    for i, step in enumerate(primary):
        out.append(step)
        upto = ((i + 1) * len(secondary)) // len(primary)
        out.extend(secondary[done:upto])
        done = upto
    return out + list(secondary[done:])


def _const_spec(shape):
    zeros = (0,) * len(shape)
    return pl.BlockSpec(shape, lambda *_: zeros, pipeline_mode=pl.Buffered(1))


def _padded_rows(x_ref, meta_ref, first):
    xw = x_ref[...]
    tm = xw.shape[0]
    return jnp.concatenate([jnp.where(first, meta_ref[...], xw[0:BLOCK]),
                            jnp.where(first, xw[0:tm - BLOCK], xw[BLOCK:tm])], axis=0)


def _token_window_spec(tm, tiles_per_seq, seq, tile_of_step):
    def index(s):
        t = tile_of_step(s)
        start = jnp.maximum((t % tiles_per_seq) * tm - BLOCK, 0)
        return (pl.multiple_of((t // tiles_per_seq) * seq + start, BLOCK), 0)
    return pl.BlockSpec((pl.Element(tm), pl.Element(D_MODEL)), index)


def _rope_offsets(freq_ref, cos_ref, sin_ref):
    r = lax.broadcasted_iota(jnp.int32, cos_ref.shape, 0).astype(F32)
    ang = r * freq_ref[...]
    cos_ref[...] = jnp.cos(ang)
    sin_ref[...] = jnp.sin(ang)


def _rope_tables(base_pos, freq_ref, cos_ref, sin_ref):
    ang = base_pos.astype(F32) * freq_ref[...]
    cb = jnp.cos(ang)
    sb = jnp.sin(ang)
    oc = cos_ref[...]
    os_ = sin_ref[...]
    return cb * oc - sb * os_, sb * oc + cb * os_


def _cast_blocks(cast_refs):
    n = len(cast_refs) // 2
    for src, dst in zip(cast_refs[:n], cast_refs[n:]):
        dst[...] = src[...].astype(BF16)


def _cast_specs(weights):
    ins, outs, shapes = [], [], []
    step = lambda s: jnp.minimum(s, CAST_STEPS - 1)
    for layer, w in weights:
        _, rows, cols = w.shape
        blk = rows // CAST_STEPS
        ins.append(pl.BlockSpec((None, blk, cols), lambda s, layer=layer: (layer, step(s), 0)))
        outs.append(pl.BlockSpec((blk, cols), lambda s: (step(s), 0)))
        shapes.append(jax.ShapeDtypeStruct((rows, cols), BF16))
    return ins, outs, shapes


def _inproj0_kernel(*refs, tiles_per_seq, n_cast):
    (x_ref, meta_ref, gain_ref, w_ref, qg_ref, kg_ref, freq_ref) = refs[:7]
    cast_in = refs[7:7 + n_cast]
    xr_ref, gate_ref, q_ref, k_ref, v_ref = refs[7 + n_cast:12 + n_cast]
    cast_out = refs[12 + n_cast:12 + 2 * n_cast]
    ocos, osin = refs[12 + 2 * n_cast:]
    step = pl.program_id(0)
    tile_in_seq = step % tiles_per_seq

    @pl.when(step == 0)
    def _():
        _rope_offsets(freq_ref, ocos, osin)

    _cast_blocks(cast_in + cast_out)

    h = _padded_rows(x_ref, meta_ref, tile_in_seq == 0)
    rows = h.shape[0]
    xn = (h * _rms_scale(h) * gain_ref[...]).astype(BF16)

    q0 = 2 * LRU_WIDTH
    k0 = q0 + Q_WIDTH
    chunk_dot = lambda lo: jnp.dot(xn, w_ref[:, lo:lo + INPROJ_CHUNK], preferred_element_type=F32)
    y_q = [chunk_dot(q0 + lo) for lo in range(0, Q_WIDTH, INPROJ_CHUNK)]
    y_kv = chunk_dot(k0)
    v_ref[...] = y_kv[:, KV_WIDTH:].astype(BF16)

    cos, sin = _rope_tables(tile_in_seq * rows - META_PAD, freq_ref, ocos, osin)
    lane = lax.broadcasted_iota(jnp.int32, (1, LANES), 1) & (ATT_HEAD_DIM - 1)
    half = ROT_DIM // 2
    sin_lo = sin * jnp.where(lane < half, -1.0, 0.0)
    sin_hi = sin * jnp.where((lane >= half) & (lane < ROT_DIM), 1.0, 0.0)
    hi = lax.broadcasted_iota(jnp.int32, (2 * LANES, LANES), 0) & (LANES - 1)
    hj = lax.broadcasted_iota(jnp.int32, (2 * LANES, LANES), 1)
    head_mean = jnp.where(hi // ATT_HEAD_DIM == hj // ATT_HEAD_DIM,
                          1.0 / ATT_HEAD_DIM, 0.0).astype(BF16)

    def norm_rope(x, gain, scale):
        sq = x * x
        sq_hi = sq.astype(BF16)
        sq_lo = (sq - sq_hi.astype(F32)).astype(BF16)
        ms = jnp.dot(jnp.concatenate([sq_hi, sq_lo], axis=1), head_mean,
                     preferred_element_type=F32)
        xg = x * gain
        rot = (xg * cos + pltpu.roll(xg, LANES - half, 1) * sin_lo
               + pltpu.roll(xg, half, 1) * sin_hi)
        return rot * (lax.rsqrt(ms + RMS_EPS) * scale)

    def qk_step(j):
        if j < Q_WIDTH // LANES:
            lo = (j * LANES) % INPROJ_CHUNK
            q_ref[:, j * LANES:(j + 1) * LANES] = norm_rope(
                y_q[j * LANES // INPROJ_CHUNK][:, lo:lo + LANES], qg_ref[...],
                ATT_HEAD_DIM ** -0.5).astype(BF16)
        else:
            k_ref[...] = norm_rope(y_kv[:, :KV_WIDTH], kg_ref[...], 1.0).astype(BF16)

    def proj_step(c):
        out_ref = xr_ref if c < LRU_WIDTH // INPROJ_CHUNK else gate_ref
        dst = slice((c * INPROJ_CHUNK) % LRU_WIDTH, (c * INPROJ_CHUNK) % LRU_WIDTH + INPROJ_CHUNK)
        out_ref[:, dst] = chunk_dot(c * INPROJ_CHUNK)

    proj_steps = [functools.partial(proj_step, c) for c in range(2 * LRU_WIDTH // INPROJ_CHUNK)]
    qk_steps = [functools.partial(qk_step, j) for j in range(Q_WIDTH // LANES + 1)]
    for step in _interleave(proj_steps, qk_steps):
        step()


def _inproj0(x2d, meta_blk, gain, w_in, q_gain, k_gain, freq, cast_weights, batch,
             tiles_per_seq, seq):
    tm = ROW_TILE
    rows = batch * tiles_per_seq * tm
    row = lambda w: pl.BlockSpec((tm, w), lambda i: (i, 0))
    cast_in, cast_out, cast_shapes = _cast_specs(cast_weights)
    return pl.pallas_call(
        functools.partial(_inproj0_kernel, tiles_per_seq=tiles_per_seq, n_cast=len(cast_in)),
        grid=(rows // tm,),
        in_specs=[_token_window_spec(tm, tiles_per_seq, seq, lambda s: s),
                  _const_spec((BLOCK, D_MODEL)), _const_spec((1, D_MODEL)),
                  _const_spec((D_MODEL, AB_IN_WIDTH)), _const_spec((1, LANES)),
                  _const_spec((1, LANES)), _const_spec((1, LANES))] + cast_in,
        out_specs=[row(LRU_WIDTH), row(LRU_WIDTH), row(Q_WIDTH), row(KV_WIDTH), row(KV_WIDTH)]
        + cast_out,
        out_shape=[jax.ShapeDtypeStruct((rows, LRU_WIDTH), F32),
                   jax.ShapeDtypeStruct((rows, LRU_WIDTH), F32),
                   jax.ShapeDtypeStruct((rows, Q_WIDTH), BF16),
                   jax.ShapeDtypeStruct((rows, KV_WIDTH), BF16),
                   jax.ShapeDtypeStruct((rows, KV_WIDTH), BF16)] + cast_shapes,
        scratch_shapes=[pltpu.VMEM((tm, LANES), F32), pltpu.VMEM((tm, LANES), F32)],
        compiler_params=pltpu.CompilerParams(dimension_semantics=("arbitrary",)),
        name="l0_inproj",
    )(x2d, meta_blk, gain, w_in, q_gain, k_gain, freq, *[w for _, w in cast_weights])


def _attn_probs(n, sinks, q, kc, kp, km):
    j = lax.broadcasted_iota(jnp.int32, (BLOCK, BLOCK), 0)
    i = lax.broadcasted_iota(jnp.int32, (BLOCK, BLOCK), 1)
    causal = j <= i
    win_ok = n >= jnp.where(causal, 1, 2)
    meta_ok = (j >= META_PAD) & (n >= jnp.where(causal, 0, 1))
    contract_last = (((1,), (1,)), ((), ()))

    qs = jnp.concatenate([q[:, a * ATT_HEAD_DIM:(a + 1) * ATT_HEAD_DIM]
                          for a in range(ATT_GROUP)], axis=0)
    s_c = lax.dot_general(kc, qs, contract_last, preferred_element_type=F32)
    s_p = lax.dot_general(kp, qs, contract_last, preferred_element_type=F32)
    s_m = lax.dot_general(km, qs, contract_last, preferred_element_type=F32)
    p_c, p_p, p_m, inv_den = [], [], [], []
    for a in range(ATT_GROUP):
        head = slice(a * BLOCK, (a + 1) * BLOCK)
        sw = jnp.where(win_ok, jnp.where(causal, s_c[:, head], s_p[:, head]), NEG_INF)
        sm = jnp.where(meta_ok, s_m[:, head], NEG_INF)
        m = jnp.maximum(jnp.maximum(jnp.max(sw, axis=0, keepdims=True),
                                    jnp.max(sm, axis=0, keepdims=True)), sinks[a])
        pw = jnp.exp(sw - m)
        pm = jnp.exp(sm - m)
        den = (jnp.sum(pw, axis=0, keepdims=True) + jnp.sum(pm, axis=0, keepdims=True)
               + jnp.exp(sinks[a] - m))
        inv_den.append(1.0 / den)
        p_c.append(jnp.where(causal, pw, 0.0).astype(BF16))
        p_p.append(jnp.where(causal, 0.0, pw).astype(BF16))
        p_m.append(pm.astype(BF16))
    lanes = lambda parts: jnp.concatenate(parts, axis=1)
    return lanes(p_c), lanes(p_p), lanes(p_m), lanes(inv_den)


def _attn_out_t(probs, vc, vp, vm):
    p_c, p_p, p_m, inv_den = probs
    contract_rows = (((0,), (0,)), ((), ()))
    o_t = (lax.dot_general(vc, p_c, contract_rows, preferred_element_type=F32)
           + lax.dot_general(vp, p_p, contract_rows, preferred_element_type=F32)
           + lax.dot_general(vm, p_m, contract_rows, preferred_element_type=F32))
    return o_t * inv_den


def _attn_untranspose(o_t):
    o_t = jnp.concatenate(o_t, axis=0)
    return jnp.concatenate([o_t[:, a * BLOCK:(a + 1) * BLOCK].T for a in range(ATT_GROUP)],
                           axis=1).astype(BF16)


def _lru_gates(lo, xr_ref, rows, cw, cb, wg_ref, xbuf):
    tl = BLOCK
    cols = slice(lo, lo + GATE_TILE)
    x = xr_ref[rows, cols]
    xbuf[SUBLANES:SUBLANES + tl, cols] = x
    xc = x * cw[CONV_W - 1:CONV_W, cols] + cb[:, cols]
    for d in range(1, CONV_W):
        xc = xc + (xbuf[SUBLANES - d:SUBLANES - d + tl, cols]
                   * cw[CONV_W - 1 - d:CONV_W - d, cols])
    xbuf[0:SUBLANES, cols] = x[tl - SUBLANES:tl]
    xcb = xc.astype(BF16)
    ga_r = jnp.dot(xcb, wg_ref[cols, cols], preferred_element_type=F32)
    ga_i = jnp.dot(xcb, wg_ref[cols, LRU_WIDTH + lo:LRU_WIDTH + lo + GATE_TILE],
                   preferred_element_type=F32)
    return xc, ga_r, ga_i


def _lru_scan(n, xc, ga_r, ga_i, grp, gate_ref, rows, bg, softplus, hcar, y_ref):
    tl = BLOCK
    row = lax.broadcasted_iota(jnp.int32, (tl, 1), 0)
    t = n * tl + row
    r = _sigmoid(ga_r + bg[:, grp])
    gi = _sigmoid(ga_i + bg[:, LRU_WIDTH + grp.start:LRU_WIDTH + grp.stop])
    log_a = (-LRU_C * softplus[:, grp]) * r
    a = jnp.exp(log_a)
    mult2 = jnp.tanh(-log_a) * (a * a + 1.0)
    mult = mult2 * lax.rsqrt(jnp.maximum(mult2, SQRT_GUARD))
    mult = jnp.where(t == META_PAD, 1.0, mult)
    b = jnp.where(t < META_PAD, 0.0, mult * gi * xc)

    d = 1
    while d < SUBLANES:
        keep = row >= d
        b = jnp.where(keep, a * pltpu.roll(b, d, 0), 0.0) + b
        a = jnp.where(keep, a * pltpu.roll(a, d, 0), a)
        d *= 2
    while d < tl:
        b = jnp.concatenate([b[:d], a[d:] * b[:tl - d] + b[d:]], axis=0)
        a = jnp.concatenate([a[:d], a[d:] * a[:tl - d]], axis=0)
        d *= 2
    h = b + a * hcar[0:1, grp]
    hcar[:, grp] = jnp.broadcast_to(h[tl - 1:tl], (SUBLANES, LANES))
    y_ref[rows, grp] = (_gelu_tanh(gate_ref[rows, grp]) * h).astype(BF16)


def _ffn(h1, gain, wgu_ref, wd_ref):
    xn = (h1 * _rms_scale(h1) * gain).astype(BF16)
    acts = []
    for c in range(D_FF // FFN_CHUNK):
        lo = c * FFN_CHUNK
        g = jnp.dot(xn, wgu_ref[:, lo:lo + FFN_CHUNK], preferred_element_type=F32)
        u = jnp.dot(xn, wgu_ref[:, D_FF + lo:D_FF + lo + FFN_CHUNK], preferred_element_type=F32)
        acts.append((_silu(g) * u).astype(BF16))
    act = jnp.concatenate(acts, axis=1)
    return h1 + jnp.dot(act, wd_ref[...], preferred_element_type=F32)


def _l0_mix_ffn_kernel(*refs, tiles_per_seq, n_tiles, n_cast):
    k_ref, v_ref = refs[6:8]
    cast_in = refs[17:17 + n_cast]
    cast_out = refs[18 + n_cast:18 + 2 * n_cast]
    _, xbuf, hcar, kprev, vprev, kmeta, vmeta = refs[18 + 2 * n_cast:]
    s = pl.program_id(0)
    tile_in_seq = jnp.minimum(s, n_tiles - 1) % tiles_per_seq
    _cast_blocks(cast_in + cast_out)

    @pl.when(tile_in_seq == 0)
    def _():
        xbuf[0:SUBLANES, :] = jnp.zeros((SUBLANES, LRU_WIDTH), F32)
        hcar[...] = jnp.zeros_like(hcar)
        kmeta[...] = k_ref[0:BLOCK, :]
        vmeta[...] = v_ref[0:BLOCK, :]
        kprev[...] = k_ref[0:BLOCK, :]
        vprev[...] = v_ref[0:BLOCK, :]

    body = functools.partial(_l0_mix_ffn_body, refs, n_cast, tile_in_seq, s, tiles_per_seq)
    pl.when(s == 0)(functools.partial(body, with_ffn=False, with_mix=True))
    pl.when((s > 0) & (s < n_tiles))(functools.partial(body, with_ffn=True, with_mix=True))
    pl.when(s == n_tiles)(functools.partial(body, with_ffn=True, with_mix=False))


def _l0_mix_ffn_body(refs, n_cast, tile_in_seq, s, tiles_per_seq, *, with_ffn, with_mix):
    (sink_ref, x_ref, meta_ref, xr_ref, gate_ref, q_ref, k_ref, v_ref, cw_ref, cb_ref, wg_ref,
     bg_ref, lam_ref, wo_ref, gain_ref, wgu_ref, wd_ref) = refs[:17]
    o_ref = refs[17 + n_cast]
    ybuf, xbuf, hcar, kprev, vprev, kmeta, vmeta = refs[18 + 2 * n_cast:]
    blocks_per_tile = x_ref.shape[0] // BLOCK
    last = slice((blocks_per_tile - 1) * BLOCK, blocks_per_tile * BLOCK)

    if with_ffn:
        ffn_first = (s - 1) % tiles_per_seq == 0
        h1 = (_padded_rows(x_ref, meta_ref, ffn_first)
              + jnp.dot(ybuf[...], wo_ref[...], preferred_element_type=F32))
        xn = (h1 * _rms_scale(h1) * gain_ref[...]).astype(BF16)

    cw = cw_ref[...]
    cb = cb_ref[...]
    bg = bg_ref[...]
    z = -lam_ref[...]
    softplus = jnp.maximum(z, 0.0) + jnp.log1p(jnp.exp(-jnp.abs(z)))

    live = {}

    def block_rows(blk):
        return tile_in_seq * blocks_per_tile + blk, slice(blk * BLOCK, (blk + 1) * BLOCK)

    def lru_gates_step(blk, lo):
        _, rows = block_rows(blk)
        live["lru", blk, lo] = _lru_gates(lo, xr_ref, rows, cw, cb, wg_ref, xbuf)

    def lru_scan_step(blk, lo, sub):
        n, rows = block_rows(blk)
        xc, ga_r, ga_i = live["lru", blk, lo]
        part = slice(sub, sub + LANES)
        _lru_scan(n, xc[:, part], ga_r[:, part], ga_i[:, part],
                  slice(lo + sub, lo + sub + LANES), gate_ref, rows, bg, softplus, hcar, ybuf)
        if sub + LANES == GATE_TILE:
            del live["lru", blk, lo]

    def kv_blocks(ref, prev_ref, meta_ref, blk, g):
        _, rows = block_rows(blk)
        lanes = slice(g * ATT_HEAD_DIM, (g + 1) * ATT_HEAD_DIM)
        prev = prev_ref[:, lanes] if blk == 0 else ref[(blk - 1) * BLOCK:blk * BLOCK, lanes]
        return ref[rows, lanes], prev, meta_ref[:, lanes]

    def attn_probs_step(blk, g):
        n, rows = block_rows(blk)
        width = ATT_GROUP * ATT_HEAD_DIM
        sinks = [sink_ref[g * ATT_GROUP + a] for a in range(ATT_GROUP)]
        live["probs", blk, g] = _attn_probs(n, sinks, q_ref[rows, g * width:(g + 1) * width],
                                            *kv_blocks(k_ref, kprev, kmeta, blk, g))

    def attn_out_step(blk, g):
        _, rows = block_rows(blk)
        live["out", blk, g] = _attn_out_t(live.pop(("probs", blk, g)),
                                          *kv_blocks(v_ref, vprev, vmeta, blk, g))
        if g + 1 == ATT_KV_HEADS:
            ybuf[rows, LRU_WIDTH:LRU_WIDTH + Q_WIDTH] = _attn_untranspose(
                [live.pop(("out", blk, h)) for h in range(ATT_KV_HEADS)])

    acts = []

    def gate_step(c):
        lo = c * FFN_CHUNK
        live["g", c] = jnp.dot(xn, wgu_ref[:, lo:lo + FFN_CHUNK], preferred_element_type=F32)

    def up_step(c):
        lo = c * FFN_CHUNK
        u = jnp.dot(xn, wgu_ref[:, D_FF + lo:D_FF + lo + FFN_CHUNK], preferred_element_type=F32)
        acts.append((_silu(live.pop(("g", c))) * u).astype(BF16))

    def down_step(c):
        cols = slice(c * DOWN_CHUNK, (c + 1) * DOWN_CHUNK)
        if len(acts) > 1:
            acts[:] = [jnp.concatenate(acts, axis=1)]
        o_ref[:, cols] = h1[:, cols] + jnp.dot(acts[0], wd_ref[:, cols],
                                               preferred_element_type=F32)

    ffn_steps = []
    if with_ffn:
        for c in range(D_FF // FFN_CHUNK):
            ffn_steps += [functools.partial(gate_step, c), functools.partial(up_step, c)]
        ffn_steps += [functools.partial(down_step, c) for c in range(D_MODEL // DOWN_CHUNK)]
    mix_steps = []
    if with_mix:
        for blk in range(blocks_per_tile):
            for lo in range(0, LRU_WIDTH, GATE_TILE):
                mix_steps.append(functools.partial(lru_gates_step, blk, lo))
                mix_steps += [functools.partial(lru_scan_step, blk, lo, sub)
                              for sub in range(0, GATE_TILE, LANES)]
            mix_steps += [functools.partial(attn_probs_step, blk, g)
                          for g in range(ATT_KV_HEADS)]
            mix_steps += [functools.partial(attn_out_step, blk, g) for g in range(ATT_KV_HEADS)]
    for step in (_interleave(ffn_steps, mix_steps) if ffn_steps else mix_steps):
        step()
    if with_mix:
        kprev[...] = k_ref[last, :]
        vprev[...] = v_ref[last, :]


def _l0_mix_ffn(sinks, x2d, meta_blk, xr, gate, q, k, v, conv_w, conv_b, w_gates, b_gates, lam,
                w_out, gain, w_gu, w_down, cast_weights, tiles_per_seq, seq):
    rows = xr.shape[0]
    tm = ROW_TILE
    n_tiles = rows // tm
    mix = lambda w: pl.BlockSpec((tm, w), lambda s: (jnp.minimum(s, n_tiles - 1), 0))
    cast_in, cast_out, cast_shapes = _cast_specs(cast_weights)
    return pl.pallas_call(
        functools.partial(_l0_mix_ffn_kernel, tiles_per_seq=tiles_per_seq, n_tiles=n_tiles,
                          n_cast=len(cast_in)),
        grid=(n_tiles + 1,),
        in_specs=[pl.BlockSpec(memory_space=pltpu.SMEM),
                  _token_window_spec(tm, tiles_per_seq, seq, lambda s: jnp.maximum(s - 1, 0)),
                  _const_spec((BLOCK, D_MODEL)),
                  mix(LRU_WIDTH), mix(LRU_WIDTH), mix(Q_WIDTH), mix(KV_WIDTH), mix(KV_WIDTH),
                  _const_spec((CONV_W, LRU_WIDTH)), _const_spec((1, LRU_WIDTH)),
                  _const_spec((LRU_WIDTH, 2 * LRU_WIDTH)), _const_spec((1, 2 * LRU_WIDTH)),
                  _const_spec((1, LRU_WIDTH)), _const_spec(w_out.shape), _const_spec((1, D_MODEL)),
                  _const_spec(w_gu.shape), _const_spec(w_down.shape)] + cast_in,
        out_specs=[pl.BlockSpec((tm, D_MODEL), lambda s: (jnp.maximum(s - 1, 0), 0))] + cast_out,
        out_shape=[jax.ShapeDtypeStruct((rows, D_MODEL), F32)] + cast_shapes,
        scratch_shapes=[pltpu.VMEM((tm, LRU_WIDTH + Q_WIDTH), BF16),
                        pltpu.VMEM((SUBLANES + BLOCK, LRU_WIDTH), F32),
                        pltpu.VMEM((SUBLANES, LRU_WIDTH), F32),
                        pltpu.VMEM((BLOCK, KV_WIDTH), BF16), pltpu.VMEM((BLOCK, KV_WIDTH), BF16),
                        pltpu.VMEM((BLOCK, KV_WIDTH), BF16), pltpu.VMEM((BLOCK, KV_WIDTH), BF16)],
        compiler_params=pltpu.CompilerParams(dimension_semantics=("arbitrary",)),
        name="l0_mix_ffn",
    )(sinks, x2d, meta_blk, xr, gate, q, k, v, conv_w, conv_b, w_gates, b_gates, lam, w_out, gain,
      w_gu, w_down, *[w for _, w in cast_weights])


def _outproj_ffn_kernel(h_ref, y_ref, wo_ref, gain_ref, wgu_ref, wd_ref, o_ref):
    h1 = h_ref[...] + jnp.dot(y_ref[...], wo_ref[...], preferred_element_type=F32)
    o_ref[...] = _ffn(h1, gain_ref[...], wgu_ref, wd_ref)


def _final_outproj_ffn(h, y, w_out, gain, w_gu, w_down, batch, seq_rows, seq):
    tm = OUT_TILE
    tiles = seq // tm
    win = lambda w: pl.BlockSpec((pl.Element(tm), pl.Element(w)),
                                 lambda b, j: (pl.multiple_of(b * seq_rows + BLOCK + j * tm, BLOCK),
                                               0))
    return pl.pallas_call(
        _outproj_ffn_kernel,
        grid=(batch, tiles),
        in_specs=[win(D_MODEL), win(y.shape[1]), _const_spec(w_out.shape),
                  _const_spec((1, D_MODEL)), _const_spec(w_gu.shape), _const_spec(w_down.shape)],
        out_specs=pl.BlockSpec((tm, D_MODEL), lambda b, j: (b * tiles + j, 0)),
        out_shape=jax.ShapeDtypeStruct((batch * seq, D_MODEL), F32),
        compiler_params=pltpu.CompilerParams(dimension_semantics=("arbitrary", "arbitrary")),
        name="l1_outproj_ffn",
    )(h, y, w_out, gain, w_gu, w_down)


_Q0, _K0, _V0, _G0 = 0, D_MODEL, 2 * D_MODEL, 4 * D_MODEL


def _retention_decays(hd):
    ii = lax.broadcasted_iota(jnp.int32, (BLOCK, BLOCK), 0)
    jj = lax.broadcasted_iota(jnp.int32, (BLOCK, BLOCK), 1)
    diff = (ii - jj).astype(F32)
    idx = lax.broadcasted_iota(jnp.int32, (BLOCK, 1), 0).astype(F32)
    log_g = RET_LOG_G[hd]
    decay_intra = jnp.where(diff >= 0.0, jnp.exp(jnp.maximum(diff, 0.0) * log_g), 0.0)
    return (decay_intra, jnp.exp((idx + 1.0) * log_g), jnp.exp((BLOCK - 1.0 - idx) * log_g),
            math.exp(BLOCK * log_g))


def _l1_mix_kernel(*refs, tiles_per_seq, n_tiles, n_cast):
    h_ref, gain_ref, w_ref, freq_ref = refs[:4]
    cast_in = refs[4:4 + n_cast]
    y_ref = refs[4 + n_cast]
    cast_out = refs[5 + n_cast:5 + 2 * n_cast]
    qkvg, state, ocos, osin = refs[5 + 2 * n_cast:]
    _cast_blocks(cast_in + cast_out)
    s = pl.program_id(0)

    @pl.when(s == 0)
    def _():
        _rope_offsets(freq_ref, ocos, osin)

    @pl.when(jnp.maximum(s - 1, 0) % tiles_per_seq == 0)
    def _():
        state[...] = jnp.zeros_like(state)

    body = functools.partial(_l1_mix_body, refs, n_cast, s, tiles_per_seq)
    pl.when(s == 0)(functools.partial(body, with_proj=True, with_ret=False))
    pl.when((s > 0) & (s < n_tiles))(functools.partial(body, with_proj=True, with_ret=True))
    pl.when(s == n_tiles)(functools.partial(body, with_proj=False, with_ret=True))


def _l1_mix_body(refs, n_cast, s, tiles_per_seq, *, with_proj, with_ret):
    h_ref, gain_ref, w_ref, freq_ref = refs[:4]
    y_ref = refs[4 + n_cast]
    qkvg, state, ocos, osin = refs[5 + 2 * n_cast:]
    proj_slot = s % 2
    ret_slot = 1 - proj_slot
    tm = h_ref.shape[0]
    half = RET_QK_DIM // 2
    if with_proj:
        h = h_ref[...]
        xn = (h * _rms_scale(h) * gain_ref[...]).astype(BF16)
        cos, sin = _rope_tables((s % tiles_per_seq) * tm - META_PAD, freq_ref, ocos, osin)

    def rope_step(col0, hd, scale):
        lo = col0 + hd * RET_QK_DIM
        y = jnp.dot(xn, w_ref[:, lo:lo + RET_QK_DIM], preferred_element_type=F32)
        x1 = y[:, :half]
        x2 = y[:, half:]
        qkvg[proj_slot, :, lo:lo + half] = ((x1 * cos - x2 * sin) * scale).astype(BF16)
        qkvg[proj_slot, :, lo + half:lo + RET_QK_DIM] = ((x2 * cos + x1 * sin)
                                                         * scale).astype(BF16)

    def value_step(col0, hd, act):
        lo = col0 + hd * RET_V_DIM
        y = jnp.dot(xn, w_ref[:, lo:lo + RET_V_DIM], preferred_element_type=F32)
        qkvg[proj_slot, :, lo:lo + RET_V_DIM] = act(y).astype(BF16)

    def ret_operand(c, hd, col0, width):
        return qkvg[ret_slot, c * BLOCK:(c + 1) * BLOCK, col0 + hd * width:col0 + (hd + 1) * width]

    heads = range(RET_HEADS)
    decays = [_retention_decays(hd) for hd in heads]
    live = {}

    def prep_step(c, hd):
        live["kdt", c, hd] = (ret_operand(c, hd, _K0, RET_QK_DIM).astype(F32)
                              * decays[hd][2]).T.astype(BF16)

    def qk_step(c, hd):
        live["qk", c, hd] = lax.dot_general(
            ret_operand(c, hd, _Q0, RET_QK_DIM), ret_operand(c, hd, _K0, RET_QK_DIM),
            (((1,), (1,)), ((), ())), preferred_element_type=F32)

    def decay_step(c, hd):
        qkd = (live.pop(("qk", c, hd)) * decays[hd][0]).astype(BF16)
        live["lhs", c, hd] = jnp.concatenate([qkd, live.pop(("kdt", c, hd))], axis=0)
        live["stb", c, hd] = state[hd].astype(BF16)

    def output_step(c, hd):
        both = jnp.dot(live.pop(("lhs", c, hd)), ret_operand(c, hd, _V0, RET_V_DIM),
                       preferred_element_type=F32)
        live["kv", c, hd] = both[BLOCK:]
        live["o", c, hd] = both[:BLOCK] + jnp.dot(
            ret_operand(c, hd, _Q0, RET_QK_DIM), live.pop(("stb", c, hd)),
            preferred_element_type=F32) * decays[hd][1]

    def post_step(c, hd):
        o = live.pop(("o", c, hd))
        state[hd] = decays[hd][3] * state[hd] + live.pop(("kv", c, hd))
        gate = ret_operand(c, hd, _G0, RET_V_DIM).astype(F32)
        y_ref[c * BLOCK:(c + 1) * BLOCK, hd * RET_V_DIM:(hd + 1) * RET_V_DIM] = (
            o * _rms_scale(o) * gate).astype(BF16)

    proj_steps = []
    for hd in heads if with_proj else ():
        proj_steps.append([functools.partial(rope_step, _Q0, hd, 1.0),
                           functools.partial(rope_step, _K0, hd, RET_QK_DIM ** -0.5)])
        proj_steps.append([functools.partial(value_step, _V0, hd, lambda y: y)])
        proj_steps.append([functools.partial(value_step, _G0, hd, _silu)])
    n_chunks = tm // BLOCK
    each_head = lambda step, c: [functools.partial(step, c, hd) for hd in heads]
    ret_steps = []
    for c in range(n_chunks) if with_ret else ():
        ret_steps.append(each_head(qk_step, c) + each_head(decay_step, c)
                         + (each_head(prep_step, c + 1) if c + 1 < n_chunks else []))
        ret_steps.append(each_head(output_step, c) + each_head(post_step, c))
    for step in each_head(prep_step, 0) if with_ret else ():
        step()
    for proj, ret in itertools.zip_longest(proj_steps, ret_steps, fillvalue=()):
        for step in (*proj, *ret):
            step()


def _l1_mix(h, gain, w_in, freq, cast_weights, tiles_per_seq):
    rows = h.shape[0]
    tm = ROW_TILE
    n_tiles = rows // tm
    half = RET_QK_DIM // 2
    cast_in, cast_out, cast_shapes = _cast_specs(cast_weights)
    return pl.pallas_call(
        functools.partial(_l1_mix_kernel, tiles_per_seq=tiles_per_seq, n_tiles=n_tiles,
                          n_cast=len(cast_in)),
        grid=(n_tiles + 1,),
        in_specs=[pl.BlockSpec((tm, D_MODEL), lambda s: (jnp.minimum(s, n_tiles - 1), 0)),
                  _const_spec((1, D_MODEL)), _const_spec(w_in.shape), _const_spec((1, half))]
        + cast_in,
        out_specs=[pl.BlockSpec((tm, 2 * D_MODEL), lambda s: (jnp.maximum(s - 1, 0), 0))]
        + cast_out,
        out_shape=[jax.ShapeDtypeStruct((rows, 2 * D_MODEL), BF16)] + cast_shapes,
        scratch_shapes=[pltpu.VMEM((2, tm, 6 * D_MODEL), BF16),
                        pltpu.VMEM((RET_HEADS, RET_QK_DIM, RET_V_DIM), F32),
                        pltpu.VMEM((tm, half), F32), pltpu.VMEM((tm, half), F32)],
        compiler_params=pltpu.CompilerParams(dimension_semantics=("arbitrary",)),
        name="l1_mix",
    )(h, gain, w_in, freq, *[w for _, w in cast_weights])


def _inv_freq(half, theta):
    return jnp.power(jnp.asarray(theta, F32), -jnp.arange(half, dtype=F32) / half)


def _block_diag(w):
    heads, wi, wo = w.shape
    eye = jnp.eye(heads, dtype=w.dtype)
    return (eye[:, None, :, None] * w[:, :, None, :]).reshape(heads * wi, heads * wo)


def kernel(x, meta_tokens, mix_norm_ab, ab_w_in, lru_conv_w, lru_conv_b, lru_w_a, lru_b_a, lru_w_i, lru_b_i, lru_lambda, q_norm, k_norm, attn_sinks, ab_w_out, mix_norm_ret, ret_w_in, ret_w_out, ffn_norm, ffn_w_gu, ffn_w_down):
    batch, seq, _ = x.shape
    seq_rows = META_PAD + N_META + seq
    tiles_per_seq = seq_rows // ROW_TILE

    x2d = x.reshape(batch * seq, D_MODEL)
    meta_blk = jnp.concatenate([jnp.zeros((META_PAD, D_MODEL), x.dtype),
                                meta_tokens.astype(x.dtype)], axis=0)

    f_att = _inv_freq(ROT_DIM // 2, ROPE_THETA)
    f_att = jnp.concatenate([f_att, f_att, jnp.zeros((ATT_HEAD_DIM - ROT_DIM,), F32)])
    f_att = jnp.tile(f_att, LANES // ATT_HEAD_DIM).reshape(1, LANES)
    f_ret = _inv_freq(RET_QK_DIM // 2, RET_THETA).reshape(1, RET_QK_DIM // 2)

    row_vec = lambda v: v.reshape(1, -1).astype(F32)
    two_heads = lambda v: jnp.tile(v.reshape(1, -1).astype(F32), (1, 2))

    xr, gate, q, k, v, w_gu0, w_down0 = _inproj0(
        x2d, meta_blk, row_vec(mix_norm_ab[0]), ab_w_in[0].astype(BF16), two_heads(q_norm[0]),
        two_heads(k_norm[0]), f_att, [(0, ffn_w_gu), (0, ffn_w_down)], batch, tiles_per_seq, seq)
    w_gates = jnp.concatenate([_block_diag(lru_w_a[0]), _block_diag(lru_w_i[0])], axis=1)
    b_gates = jnp.concatenate([lru_b_a[0].reshape(1, -1), lru_b_i[0].reshape(1, -1)], axis=1)
    w_att = ab_w_out[0][LRU_WIDTH:].reshape(ATT_KV_HEADS, ATT_GROUP, ATT_HEAD_DIM, D_MODEL)
    w_att = w_att.transpose(1, 0, 2, 3).reshape(Q_WIDTH, D_MODEL)
    w_out0 = jnp.concatenate([ab_w_out[0][:LRU_WIDTH], w_att], axis=0).astype(BF16)
    h, w_in1, w_out1 = _l0_mix_ffn(
        attn_sinks[0].astype(F32), x2d, meta_blk, xr, gate, q, k, v, lru_conv_w[0],
        row_vec(lru_conv_b[0]), w_gates.astype(BF16), b_gates.astype(F32), row_vec(lru_lambda[0]),
        w_out0, row_vec(ffn_norm[0]), w_gu0, w_down0, [(0, ret_w_in), (0, ret_w_out)],
        tiles_per_seq, seq)

    y_ret, w_gu1, w_down1 = _l1_mix(h, row_vec(mix_norm_ret[0]), w_in1, f_ret,
                                    [(1, ffn_w_gu), (1, ffn_w_down)], tiles_per_seq)
    out = _final_outproj_ffn(h, y_ret, w_out1, row_vec(ffn_norm[1]), w_gu1, w_down1,
                             batch, seq_rows, seq)
    return out.reshape(batch, seq, D_MODEL)
```

```python
import functools
import itertools
import math

import jax
import jax.numpy as jnp
from jax import lax
from jax.experimental import pallas as pl
from jax.experimental.pallas import tpu as pltpu

F32 = jnp.float32
BF16 = jnp.bfloat16

D_MODEL = 1024
N_META = 16
BLOCK = 128
META_PAD = BLOCK - N_META
RMS_EPS = 1e-6
NEG_INF = -1e30

LRU_WIDTH = 512
LRU_HEADS = 8
LRU_BLOCK_W = 64
CONV_W = 4
LRU_C = 8.0

ATT_HEADS = 8
ATT_KV_HEADS = 2
ATT_GROUP = ATT_HEADS // ATT_KV_HEADS
ATT_HEAD_DIM = 64
ROPE_THETA = 500000.0
ROT_DIM = 16
Q_WIDTH = 512
KV_WIDTH = 128
AB_IN_WIDTH = 2 * LRU_WIDTH + Q_WIDTH + 2 * KV_WIDTH

RET_HEADS = 4
RET_QK_DIM = 256
RET_V_DIM = 512
RET_THETA = 10000.0
RET_LOG_G = tuple(math.log1p(-(2.0 ** (-5.0 - h))) for h in range(RET_HEADS))

D_FF = 2816

LANES = 128
SUBLANES = 8
ROW_TILE = 640
OUT_TILE = 512
SQRT_GUARD = 1.1754944e-38
GATE_TILE = 256
INPROJ_CHUNK = 256
CAST_STEPS = 16
FFN_CHUNK = 256
DOWN_CHUNK = 256


def _rms_scale(x):
    return lax.rsqrt(jnp.mean(x * x, axis=-1, keepdims=True) + RMS_EPS)


def _sigmoid(x):
    return 0.5 * jnp.tanh(0.5 * x) + 0.5


def _silu(x):
    half = 0.5 * x
    return half + half * jnp.tanh(half)


def _gelu_tanh(x):
    half = 0.5 * x
    return half + half * jnp.tanh(0.7978845608028654 * (x + 0.044715 * (x * x * x)))


def _interleave(primary, secondary):
    out, done = [], 0
    for i, step in enumerate(primary):
        out.append(step)
        upto = ((i + 1) * len(secondary)) // len(primary)
        out.extend(secondary[done:upto])
        done = upto
    return out + list(secondary[done:])


def _const_spec(shape):
    zeros = (0,) * len(shape)
    return pl.BlockSpec(shape, lambda *_: zeros, pipeline_mode=pl.Buffered(1))


def _padded_rows(x_ref, meta_ref, first):
    xw = x_ref[...]
    tm = xw.shape[0]
    return jnp.concatenate([jnp.where(first, meta_ref[...], xw[0:BLOCK]),
                            jnp.where(first, xw[0:tm - BLOCK], xw[BLOCK:tm])], axis=0)


def _token_window_spec(tm, tiles_per_seq, seq, tile_of_step):
    def index(s):
        t = tile_of_step(s)
        start = jnp.maximum((t % tiles_per_seq) * tm - BLOCK, 0)
        return (pl.multiple_of((t // tiles_per_seq) * seq + start, BLOCK), 0)
    return pl.BlockSpec((pl.Element(tm), pl.Element(D_MODEL)), index)


def _rope_offsets(freq_ref, cos_ref, sin_ref):
    r = lax.broadcasted_iota(jnp.int32, cos_ref.shape, 0).astype(F32)
    ang = r * freq_ref[...]
    cos_ref[...] = jnp.cos(ang)
    sin_ref[...] = jnp.sin(ang)


def _rope_tables(base_pos, freq_ref, cos_ref, sin_ref):
    ang = base_pos.astype(F32) * freq_ref[...]
    cb = jnp.cos(ang)
    sb = jnp.sin(ang)
    oc = cos_ref[...]
    os_ = sin_ref[...]
    return cb * oc - sb * os_, sb * oc + cb * os_


def _cast_blocks(cast_refs):
    n = len(cast_refs) // 2
    for src, dst in zip(cast_refs[:n], cast_refs[n:]):
        dst[...] = src[...].astype(BF16)


def _cast_specs(weights):
    ins, outs, shapes = [], [], []
    step = lambda s: jnp.minimum(s, CAST_STEPS - 1)
    for layer, w in weights:
        _, rows, cols = w.shape
        blk = rows // CAST_STEPS
        ins.append(pl.BlockSpec((None, blk, cols), lambda s, layer=layer: (layer, step(s), 0)))
        outs.append(pl.BlockSpec((blk, cols), lambda s: (step(s), 0)))
        shapes.append(jax.ShapeDtypeStruct((rows, cols), BF16))
    return ins, outs, shapes


def _inproj0_kernel(*refs, tiles_per_seq, n_cast):
    (x_ref, meta_ref, gain_ref, w_ref, qg_ref, kg_ref, freq_ref) = refs[:7]
    cast_in = refs[7:7 + n_cast]
    xr_ref, gate_ref, q_ref, k_ref, v_ref = refs[7 + n_cast:12 + n_cast]
    cast_out = refs[12 + n_cast:12 + 2 * n_cast]
    ocos, osin = refs[12 + 2 * n_cast:]
    step = pl.program_id(0)
    tile_in_seq = step % tiles_per_seq

    @pl.when(step == 0)
    def _():
        _rope_offsets(freq_ref, ocos, osin)

    _cast_blocks(cast_in + cast_out)

    h = _padded_rows(x_ref, meta_ref, tile_in_seq == 0)
    rows = h.shape[0]
    xn = (h * _rms_scale(h) * gain_ref[...]).astype(BF16)

    q0 = 2 * LRU_WIDTH
    k0 = q0 + Q_WIDTH
    chunk_dot = lambda lo: jnp.dot(xn, w_ref[:, lo:lo + INPROJ_CHUNK], preferred_element_type=F32)
    y_q = [chunk_dot(q0 + lo) for lo in range(0, Q_WIDTH, INPROJ_CHUNK)]
    y_kv = chunk_dot(k0)
    v_ref[...] = y_kv[:, KV_WIDTH:].astype(BF16)

    cos, sin = _rope_tables(tile_in_seq * rows - META_PAD, freq_ref, ocos, osin)
    lane = lax.broadcasted_iota(jnp.int32, (1, LANES), 1) & (ATT_HEAD_DIM - 1)
    half = ROT_DIM // 2
    sin_lo = sin * jnp.where(lane < half, -1.0, 0.0)
    sin_hi = sin * jnp.where((lane >= half) & (lane < ROT_DIM), 1.0, 0.0)
    hi = lax.broadcasted_iota(jnp.int32, (2 * LANES, LANES), 0) & (LANES - 1)
    hj = lax.broadcasted_iota(jnp.int32, (2 * LANES, LANES), 1)
    head_mean = jnp.where(hi // ATT_HEAD_DIM == hj // ATT_HEAD_DIM,
                          1.0 / ATT_HEAD_DIM, 0.0).astype(BF16)

    def norm_rope(x, gain, scale):
        sq = x * x
        sq_hi = sq.astype(BF16)
        sq_lo = (sq - sq_hi.astype(F32)).astype(BF16)
        ms = jnp.dot(jnp.concatenate([sq_hi, sq_lo], axis=1), head_mean,
                     preferred_element_type=F32)
        xg = x * gain
        rot = (xg * cos + pltpu.roll(xg, LANES - half, 1) * sin_lo
               + pltpu.roll(xg, half, 1) * sin_hi)
        return rot * (lax.rsqrt(ms + RMS_EPS) * scale)

    def qk_step(j):
        if j < Q_WIDTH // LANES:
            lo = (j * LANES) % INPROJ_CHUNK
            q_ref[:, j * LANES:(j + 1) * LANES] = norm_rope(
                y_q[j * LANES // INPROJ_CHUNK][:, lo:lo + LANES], qg_ref[...],
                ATT_HEAD_DIM ** -0.5).astype(BF16)
        else:
            k_ref[...] = norm_rope(y_kv[:, :KV_WIDTH], kg_ref[...], 1.0).astype(BF16)

    def proj_step(c):
        out_ref = xr_ref if c < LRU_WIDTH // INPROJ_CHUNK else gate_ref
        dst = slice((c * INPROJ_CHUNK) % LRU_WIDTH, (c * INPROJ_CHUNK) % LRU_WIDTH + INPROJ_CHUNK)
        out_ref[:, dst] = chunk_dot(c * INPROJ_CHUNK)

    proj_steps = [functools.partial(proj_step, c) for c in range(2 * LRU_WIDTH // INPROJ_CHUNK)]
    qk_steps = [functools.partial(qk_step, j) for j in range(Q_WIDTH // LANES + 1)]
    for step in _interleave(proj_steps, qk_steps):
        step()


def _inproj0(x2d, meta_blk, gain, w_in, q_gain, k_gain, freq, cast_weights, batch,
             tiles_per_seq, seq):
    tm = ROW_TILE
    rows = batch * tiles_per_seq * tm
    row = lambda w: pl.BlockSpec((tm, w), lambda i: (i, 0))
    cast_in, cast_out, cast_shapes = _cast_specs(cast_weights)
    return pl.pallas_call(
        functools.partial(_inproj0_kernel, tiles_per_seq=tiles_per_seq, n_cast=len(cast_in)),
        grid=(rows // tm,),
        in_specs=[_token_window_spec(tm, tiles_per_seq, seq, lambda s: s),
                  _const_spec((BLOCK, D_MODEL)), _const_spec((1, D_MODEL)),
                  _const_spec((D_MODEL, AB_IN_WIDTH)), _const_spec((1, LANES)),
                  _const_spec((1, LANES)), _const_spec((1, LANES))] + cast_in,
        out_specs=[row(LRU_WIDTH), row(LRU_WIDTH), row(Q_WIDTH), row(KV_WIDTH), row(KV_WIDTH)]
        + cast_out,
        out_shape=[jax.ShapeDtypeStruct((rows, LRU_WIDTH), F32),
                   jax.ShapeDtypeStruct((rows, LRU_WIDTH), F32),
                   jax.ShapeDtypeStruct((rows, Q_WIDTH), BF16),
                   jax.ShapeDtypeStruct((rows, KV_WIDTH), BF16),
                   jax.ShapeDtypeStruct((rows, KV_WIDTH), BF16)] + cast_shapes,
        scratch_shapes=[pltpu.VMEM((tm, LANES), F32), pltpu.VMEM((tm, LANES), F32)],
        compiler_params=pltpu.CompilerParams(dimension_semantics=("arbitrary",)),
        name="l0_inproj",
    )(x2d, meta_blk, gain, w_in, q_gain, k_gain, freq, *[w for _, w in cast_weights])


def _attn_probs(n, sinks, q, kc, kp, km):
    j = lax.broadcasted_iota(jnp.int32, (BLOCK, BLOCK), 0)
    i = lax.broadcasted_iota(jnp.int32, (BLOCK, BLOCK), 1)
    causal = j <= i
    win_ok = n >= jnp.where(causal, 1, 2)
    meta_ok = (j >= META_PAD) & (n >= jnp.where(causal, 0, 1))
    contract_last = (((1,), (1,)), ((), ()))

    qs = jnp.concatenate([q[:, a * ATT_HEAD_DIM:(a + 1) * ATT_HEAD_DIM]
                          for a in range(ATT_GROUP)], axis=0)
    s_c = lax.dot_general(kc, qs, contract_last, preferred_element_type=F32)
    s_p = lax.dot_general(kp, qs, contract_last, preferred_element_type=F32)
    s_m = lax.dot_general(km, qs, contract_last, preferred_element_type=F32)
    p_c, p_p, p_m, inv_den = [], [], [], []
    for a in range(ATT_GROUP):
        head = slice(a * BLOCK, (a + 1) * BLOCK)
        sw = jnp.where(win_ok, jnp.where(causal, s_c[:, head], s_p[:, head]), NEG_INF)
        sm = jnp.where(meta_ok, s_m[:, head], NEG_INF)
        m = jnp.maximum(jnp.maximum(jnp.max(sw, axis=0, keepdims=True),
                                    jnp.max(sm, axis=0, keepdims=True)), sinks[a])
        pw = jnp.exp(sw - m)
        pm = jnp.exp(sm - m)
        den = (jnp.sum(pw, axis=0, keepdims=True) + jnp.sum(pm, axis=0, keepdims=True)
               + jnp.exp(sinks[a] - m))
        inv_den.append(1.0 / den)
        p_c.append(jnp.where(causal, pw, 0.0).astype(BF16))
        p_p.append(jnp.where(causal, 0.0, pw).astype(BF16))
        p_m.append(pm.astype(BF16))
    lanes = lambda parts: jnp.concatenate(parts, axis=1)
    return lanes(p_c), lanes(p_p), lanes(p_m), lanes(inv_den)


def _attn_out_t(probs, vc, vp, vm):
    p_c, p_p, p_m, inv_den = probs
    contract_rows = (((0,), (0,)), ((), ()))
    o_t = (lax.dot_general(vc, p_c, contract_rows, preferred_element_type=F32)
           + lax.dot_general(vp, p_p, contract_rows, preferred_element_type=F32)
           + lax.dot_general(vm, p_m, contract_rows, preferred_element_type=F32))
    return o_t * inv_den


def _attn_untranspose(o_t):
    o_t = jnp.concatenate(o_t, axis=0)
    return jnp.concatenate([o_t[:, a * BLOCK:(a + 1) * BLOCK].T for a in range(ATT_GROUP)],
                           axis=1).astype(BF16)


def _lru_gates(lo, xr_ref, rows, cw, cb, wg_ref, xbuf):
    tl = BLOCK
    cols = slice(lo, lo + GATE_TILE)
    x = xr_ref[rows, cols]
    xbuf[SUBLANES:SUBLANES + tl, cols] = x
    xc = x * cw[CONV_W - 1:CONV_W, cols] + cb[:, cols]
    for d in range(1, CONV_W):
        xc = xc + (xbuf[SUBLANES - d:SUBLANES - d + tl, cols]
                   * cw[CONV_W - 1 - d:CONV_W - d, cols])
    xbuf[0:SUBLANES, cols] = x[tl - SUBLANES:tl]
    xcb = xc.astype(BF16)
    ga_r = jnp.dot(xcb, wg_ref[cols, cols], preferred_element_type=F32)
    ga_i = jnp.dot(xcb, wg_ref[cols, LRU_WIDTH + lo:LRU_WIDTH + lo + GATE_TILE],
                   preferred_element_type=F32)
    return xc, ga_r, ga_i


def _lru_scan(n, xc, ga_r, ga_i, grp, gate_ref, rows, bg, softplus, hcar, y_ref):
    tl = BLOCK
    row = lax.broadcasted_iota(jnp.int32, (tl, 1), 0)
    t = n * tl + row
    r = _sigmoid(ga_r + bg[:, grp])
    gi = _sigmoid(ga_i + bg[:, LRU_WIDTH + grp.start:LRU_WIDTH + grp.stop])
    log_a = (-LRU_C * softplus[:, grp]) * r
    a = jnp.exp(log_a)
    mult2 = jnp.tanh(-log_a) * (a * a + 1.0)
    mult = mult2 * lax.rsqrt(jnp.maximum(mult2, SQRT_GUARD))
    mult = jnp.where(t == META_PAD, 1.0, mult)
    b = jnp.where(t < META_PAD, 0.0, mult * gi * xc)

    d = 1
    while d < SUBLANES:
        keep = row >= d
        b = jnp.where(keep, a * pltpu.roll(b, d, 0), 0.0) + b
        a = jnp.where(keep, a * pltpu.roll(a, d, 0), a)
        d *= 2
    while d < tl:
        b = jnp.concatenate([b[:d], a[d:] * b[:tl - d] + b[d:]], axis=0)
        a = jnp.concatenate([a[:d], a[d:] * a[:tl - d]], axis=0)
        d *= 2
    h = b + a * hcar[0:1, grp]
    hcar[:, grp] = jnp.broadcast_to(h[tl - 1:tl], (SUBLANES, LANES))
    y_ref[rows, grp] = (_gelu_tanh(gate_ref[rows, grp]) * h).astype(BF16)


def _ffn(h1, gain, wgu_ref, wd_ref):
    xn = (h1 * _rms_scale(h1) * gain).astype(BF16)
    acts = []
    for c in range(D_FF // FFN_CHUNK):
        lo = c * FFN_CHUNK
        g = jnp.dot(xn, wgu_ref[:, lo:lo + FFN_CHUNK], preferred_element_type=F32)
        u = jnp.dot(xn, wgu_ref[:, D_FF + lo:D_FF + lo + FFN_CHUNK], preferred_element_type=F32)
        acts.append((_silu(g) * u).astype(BF16))
    act = jnp.concatenate(acts, axis=1)
    return h1 + jnp.dot(act, wd_ref[...], preferred_element_type=F32)


def _l0_mix_ffn_kernel(*refs, tiles_per_seq, n_tiles, n_cast):
    k_ref, v_ref = refs[6:8]
    cast_in = refs[17:17 + n_cast]
    cast_out = refs[18 + n_cast:18 + 2 * n_cast]
    _, xbuf, hcar, kprev, vprev, kmeta, vmeta = refs[18 + 2 * n_cast:]
    s = pl.program_id(0)
    tile_in_seq = jnp.minimum(s, n_tiles - 1) % tiles_per_seq
    _cast_blocks(cast_in + cast_out)

    @pl.when(tile_in_seq == 0)
    def _():
        xbuf[0:SUBLANES, :] = jnp.zeros((SUBLANES, LRU_WIDTH), F32)
        hcar[...] = jnp.zeros_like(hcar)
        kmeta[...] = k_ref[0:BLOCK, :]
        vmeta[...] = v_ref[0:BLOCK, :]
        kprev[...] = k_ref[0:BLOCK, :]
        vprev[...] = v_ref[0:BLOCK, :]

    body = functools.partial(_l0_mix_ffn_body, refs, n_cast, tile_in_seq, s, tiles_per_seq)
    pl.when(s == 0)(functools.partial(body, with_ffn=False, with_mix=True))
    pl.when(s > 0)(functools.partial(body, with_ffn=True, with_mix=True))


def _l0_mix_ffn_body(refs, n_cast, tile_in_seq, s, tiles_per_seq, *, with_ffn, with_mix):
    (sink_ref, x_ref, meta_ref, xr_ref, gate_ref, q_ref, k_ref, v_ref, cw_ref, cb_ref, wg_ref,
     bg_ref, lam_ref, wo_ref, gain_ref, wgu_ref, wd_ref) = refs[:17]
    o_ref = refs[17 + n_cast]
    ybuf, xbuf, hcar, kprev, vprev, kmeta, vmeta = refs[18 + 2 * n_cast:]
    blocks_per_tile = x_ref.shape[0] // BLOCK
    last = slice((blocks_per_tile - 1) * BLOCK, blocks_per_tile * BLOCK)

    if with_ffn:
        ffn_first = (s - 1) % tiles_per_seq == 0
        h1 = (_padded_rows(x_ref, meta_ref, ffn_first)
              + jnp.dot(ybuf[...], wo_ref[...], preferred_element_type=F32))
        xn = (h1 * _rms_scale(h1) * gain_ref[...]).astype(BF16)

    cw = cw_ref[...]
    cb = cb_ref[...]
    bg = bg_ref[...]
    z = -lam_ref[...]
    softplus = jnp.maximum(z, 0.0) + jnp.log1p(jnp.exp(-jnp.abs(z)))

    live = {}

    def block_rows(blk):
        return tile_in_seq * blocks_per_tile + blk, slice(blk * BLOCK, (blk + 1) * BLOCK)

    def lru_gates_step(blk, lo):
        _, rows = block_rows(blk)
        live["lru", blk, lo] = _lru_gates(lo, xr_ref, rows, cw, cb, wg_ref, xbuf)

    def lru_scan_step(blk, lo, sub):
        n, rows = block_rows(blk)
        xc, ga_r, ga_i = live["lru", blk, lo]
        part = slice(sub, sub + LANES)
        _lru_scan(n, xc[:, part], ga_r[:, part], ga_i[:, part],
                  slice(lo + sub, lo + sub + LANES), gate_ref, rows, bg, softplus, hcar, ybuf)
        if sub + LANES == GATE_TILE:
            del live["lru", blk, lo]

    def kv_blocks(ref, prev_ref, meta_ref, blk, g):
        _, rows = block_rows(blk)
        lanes = slice(g * ATT_HEAD_DIM, (g + 1) * ATT_HEAD_DIM)
        prev = prev_ref[:, lanes] if blk == 0 else ref[(blk - 1) * BLOCK:blk * BLOCK, lanes]
        return ref[rows, lanes], prev, meta_ref[:, lanes]

    def attn_probs_step(blk, g):
        n, rows = block_rows(blk)
        width = ATT_GROUP * ATT_HEAD_DIM
        sinks = [sink_ref[g * ATT_GROUP + a] for a in range(ATT_GROUP)]
        live["probs", blk, g] = _attn_probs(n, sinks, q_ref[rows, g * width:(g + 1) * width],
                                            *kv_blocks(k_ref, kprev, kmeta, blk, g))

    def attn_out_step(blk, g):
        _, rows = block_rows(blk)
        live["out", blk, g] = _attn_out_t(live.pop(("probs", blk, g)),
                                          *kv_blocks(v_ref, vprev, vmeta, blk, g))
        if g + 1 == ATT_KV_HEADS:
            ybuf[rows, LRU_WIDTH:LRU_WIDTH + Q_WIDTH] = _attn_untranspose(
                [live.pop(("out", blk, h)) for h in range(ATT_KV_HEADS)])

    acts = []

    def gate_step(c):
        lo = c * FFN_CHUNK
        live["g", c] = jnp.dot(xn, wgu_ref[:, lo:lo + FFN_CHUNK], preferred_element_type=F32)

    def up_step(c):
        lo = c * FFN_CHUNK
        u = jnp.dot(xn, wgu_ref[:, D_FF + lo:D_FF + lo + FFN_CHUNK], preferred_element_type=F32)
        acts.append((_silu(live.pop(("g", c))) * u).astype(BF16))

    def down_step(c):
        cols = slice(c * DOWN_CHUNK, (c + 1) * DOWN_CHUNK)
        if len(acts) > 1:
            acts[:] = [jnp.concatenate(acts, axis=1)]
        o_ref[:, cols] = h1[:, cols] + jnp.dot(acts[0], wd_ref[:, cols],
                                               preferred_element_type=F32)

    ffn_steps = []
    if with_ffn:
        for c in range(D_FF // FFN_CHUNK):
            ffn_steps += [functools.partial(gate_step, c), functools.partial(up_step, c)]
        ffn_steps += [functools.partial(down_step, c) for c in range(D_MODEL // DOWN_CHUNK)]
    mix_steps = []
    if with_mix:
        for blk in range(blocks_per_tile):
            for lo in range(0, LRU_WIDTH, GATE_TILE):
                mix_steps.append(functools.partial(lru_gates_step, blk, lo))
                mix_steps += [functools.partial(lru_scan_step, blk, lo, sub)
                              for sub in range(0, GATE_TILE, LANES)]
            mix_steps += [functools.partial(attn_probs_step, blk, g)
                          for g in range(ATT_KV_HEADS)]
            mix_steps += [functools.partial(attn_out_step, blk, g) for g in range(ATT_KV_HEADS)]
    for step in (_interleave(ffn_steps, mix_steps) if ffn_steps else mix_steps):
        step()
    if with_mix:
        kprev[...] = k_ref[last, :]
        vprev[...] = v_ref[last, :]


def _l0_mix_ffn(sinks, x2d, meta_blk, xr, gate, q, k, v, conv_w, conv_b, w_gates, b_gates, lam,
                w_out, gain, w_gu, w_down, cast_weights, tiles_per_seq, seq):
    rows = xr.shape[0]
    tm = ROW_TILE
    n_tiles = rows // tm
    mix = lambda w: pl.BlockSpec((tm, w), lambda s: (jnp.minimum(s, n_tiles - 1), 0))
    cast_in, cast_out, cast_shapes = _cast_specs(cast_weights)
    return pl.pallas_call(
        functools.partial(_l0_mix_ffn_kernel, tiles_per_seq=tiles_per_seq, n_tiles=n_tiles,
                          n_cast=len(cast_in)),
        grid=(n_tiles + 1,),
        in_specs=[pl.BlockSpec(memory_space=pltpu.SMEM),
                  _token_window_spec(tm, tiles_per_seq, seq, lambda s: jnp.maximum(s - 1, 0)),
                  _const_spec((BLOCK, D_MODEL)),
                  mix(LRU_WIDTH), mix(LRU_WIDTH), mix(Q_WIDTH), mix(KV_WIDTH), mix(KV_WIDTH),
                  _const_spec((CONV_W, LRU_WIDTH)), _const_spec((1, LRU_WIDTH)),
                  _const_spec((LRU_WIDTH, 2 * LRU_WIDTH)), _const_spec((1, 2 * LRU_WIDTH)),
                  _const_spec((1, LRU_WIDTH)), _const_spec(w_out.shape), _const_spec((1, D_MODEL)),
                  _const_spec(w_gu.shape), _const_spec(w_down.shape)] + cast_in,
        out_specs=[pl.BlockSpec((tm, D_MODEL), lambda s: (jnp.maximum(s - 1, 0), 0))] + cast_out,
        out_shape=[jax.ShapeDtypeStruct((rows, D_MODEL), F32)] + cast_shapes,
        scratch_shapes=[pltpu.VMEM((tm, LRU_WIDTH + Q_WIDTH), BF16),
                        pltpu.VMEM((SUBLANES + BLOCK, LRU_WIDTH), F32),
                        pltpu.VMEM((SUBLANES, LRU_WIDTH), F32),
                        pltpu.VMEM((BLOCK, KV_WIDTH), BF16), pltpu.VMEM((BLOCK, KV_WIDTH), BF16),
                        pltpu.VMEM((BLOCK, KV_WIDTH), BF16), pltpu.VMEM((BLOCK, KV_WIDTH), BF16)],
        compiler_params=pltpu.CompilerParams(dimension_semantics=("arbitrary",)),
        name="l0_mix_ffn",
    )(sinks, x2d, meta_blk, xr, gate, q, k, v, conv_w, conv_b, w_gates, b_gates, lam, w_out, gain,
      w_gu, w_down, *[w for _, w in cast_weights])


def _outproj_ffn_kernel(h_ref, y_ref, wo_ref, gain_ref, wgu_ref, wd_ref, o_ref):
    h1 = h_ref[...] + jnp.dot(y_ref[...], wo_ref[...], preferred_element_type=F32)
    o_ref[...] = _ffn(h1, gain_ref[...], wgu_ref, wd_ref)


def _final_outproj_ffn(h, y, w_out, gain, w_gu, w_down, batch, seq_rows, seq):
    tm = OUT_TILE
    tiles = seq // tm
    win = lambda w: pl.BlockSpec((pl.Element(tm), pl.Element(w)),
                                 lambda b, j: (pl.multiple_of(b * seq_rows + BLOCK + j * tm, BLOCK),
                                               0))
    return pl.pallas_call(
        _outproj_ffn_kernel,
        grid=(batch, tiles),
        in_specs=[win(D_MODEL), win(y.shape[1]), _const_spec(w_out.shape),
                  _const_spec((1, D_MODEL)), _const_spec(w_gu.shape), _const_spec(w_down.shape)],
        out_specs=pl.BlockSpec((tm, D_MODEL), lambda b, j: (b * tiles + j, 0)),
        out_shape=jax.ShapeDtypeStruct((batch * seq, D_MODEL), F32),
        compiler_params=pltpu.CompilerParams(dimension_semantics=("arbitrary", "arbitrary")),
        name="l1_outproj_ffn",
    )(h, y, w_out, gain, w_gu, w_down)


_Q0, _K0, _V0, _G0 = 0, D_MODEL, 2 * D_MODEL, 4 * D_MODEL


def _retention_decays(hd):
    ii = lax.broadcasted_iota(jnp.int32, (BLOCK, BLOCK), 0)
    jj = lax.broadcasted_iota(jnp.int32, (BLOCK, BLOCK), 1)
    diff = (ii - jj).astype(F32)
    idx = lax.broadcasted_iota(jnp.int32, (BLOCK, 1), 0).astype(F32)
    log_g = RET_LOG_G[hd]
    decay_intra = jnp.where(diff >= 0.0, jnp.exp(jnp.maximum(diff, 0.0) * log_g), 0.0)
    return (decay_intra, jnp.exp((idx + 1.0) * log_g), jnp.exp((BLOCK - 1.0 - idx) * log_g),
            math.exp(BLOCK * log_g))


def _l1_mix_kernel(*refs, tiles_per_seq, n_tiles, n_cast):
    h_ref, gain_ref, w_ref, freq_ref = refs[:4]
    cast_in = refs[4:4 + n_cast]
    y_ref = refs[4 + n_cast]
    cast_out = refs[5 + n_cast:5 + 2 * n_cast]
    qkvg, state, ocos, osin = refs[5 + 2 * n_cast:]
    _cast_blocks(cast_in + cast_out)
    s = pl.program_id(0)

    @pl.when(s == 0)
    def _():
        _rope_offsets(freq_ref, ocos, osin)

    @pl.when(jnp.maximum(s - 1, 0) % tiles_per_seq == 0)
    def _():
        state[...] = jnp.zeros_like(state)

    body = functools.partial(_l1_mix_body, refs, n_cast, s, tiles_per_seq)
    pl.when(s == 0)(functools.partial(body, with_proj=True, with_ret=False))
    pl.when((s > 0) & (s < n_tiles))(functools.partial(body, with_proj=True, with_ret=True))
    pl.when(s == n_tiles)(functools.partial(body, with_proj=False, with_ret=True))


def _l1_mix_body(refs, n_cast, s, tiles_per_seq, *, with_proj, with_ret):
    h_ref, gain_ref, w_ref, freq_ref = refs[:4]
    y_ref = refs[4 + n_cast]
    qkvg, state, ocos, osin = refs[5 + 2 * n_cast:]
    proj_slot = s % 2
    ret_slot = 1 - proj_slot
    tm = h_ref.shape[0]
    half = RET_QK_DIM // 2
    if with_proj:
        h = h_ref[...]
        xn = (h * _rms_scale(h) * gain_ref[...]).astype(BF16)
        cos, sin = _rope_tables((s % tiles_per_seq) * tm - META_PAD, freq_ref, ocos, osin)

    def rope_step(col0, hd, scale):
        lo = col0 + hd * RET_QK_DIM
        y = jnp.dot(xn, w_ref[:, lo:lo + RET_QK_DIM], preferred_element_type=F32)
        x1 = y[:, :half]
        x2 = y[:, half:]
        qkvg[proj_slot, :, lo:lo + half] = ((x1 * cos - x2 * sin) * scale).astype(BF16)
        qkvg[proj_slot, :, lo + half:lo + RET_QK_DIM] = ((x2 * cos + x1 * sin)
                                                         * scale).astype(BF16)

    def value_step(col0, hd, act):
        lo = col0 + hd * RET_V_DIM
        y = jnp.dot(xn, w_ref[:, lo:lo + RET_V_DIM], preferred_element_type=F32)
        qkvg[proj_slot, :, lo:lo + RET_V_DIM] = act(y).astype(BF16)

    def ret_operand(c, hd, col0, width):
        return qkvg[ret_slot, c * BLOCK:(c + 1) * BLOCK, col0 + hd * width:col0 + (hd + 1) * width]

    heads = range(RET_HEADS)
    decays = [_retention_decays(hd) for hd in heads]
    live = {}

    def prep_step(c, hd):
        live["kdt", c, hd] = (ret_operand(c, hd, _K0, RET_QK_DIM).astype(F32)
                              * decays[hd][2]).T.astype(BF16)

    def qk_step(c, hd):
        live["qk", c, hd] = lax.dot_general(
            ret_operand(c, hd, _Q0, RET_QK_DIM), ret_operand(c, hd, _K0, RET_QK_DIM),
            (((1,), (1,)), ((), ())), preferred_element_type=F32)

    def decay_step(c, hd):
        qkd = (live.pop(("qk", c, hd)) * decays[hd][0]).astype(BF16)
        live["lhs", c, hd] = jnp.concatenate([qkd, live.pop(("kdt", c, hd))], axis=0)
        live["stb", c, hd] = state[hd].astype(BF16)

    def output_step(c, hd):
        both = jnp.dot(live.pop(("lhs", c, hd)), ret_operand(c, hd, _V0, RET_V_DIM),
                       preferred_element_type=F32)
        live["kv", c, hd] = both[BLOCK:]
        live["o", c, hd] = both[:BLOCK] + jnp.dot(
            ret_operand(c, hd, _Q0, RET_QK_DIM), live.pop(("stb", c, hd)),
            preferred_element_type=F32) * decays[hd][1]

    def post_step(c, hd):
        o = live.pop(("o", c, hd))
        state[hd] = decays[hd][3] * state[hd] + live.pop(("kv", c, hd))
        gate = ret_operand(c, hd, _G0, RET_V_DIM).astype(F32)
        y_ref[c * BLOCK:(c + 1) * BLOCK, hd * RET_V_DIM:(hd + 1) * RET_V_DIM] = (
            o * _rms_scale(o) * gate).astype(BF16)

    proj_steps = []
    for hd in heads if with_proj else ():
        proj_steps.append([functools.partial(rope_step, _Q0, hd, 1.0),
                           functools.partial(rope_step, _K0, hd, RET_QK_DIM ** -0.5)])
        proj_steps.append([functools.partial(value_step, _V0, hd, lambda y: y)])
        proj_steps.append([functools.partial(value_step, _G0, hd, _silu)])
    n_chunks = tm // BLOCK
    each_head = lambda step, c: [functools.partial(step, c, hd) for hd in heads]
    ret_steps = []
    for c in range(n_chunks) if with_ret else ():
        ret_steps.append(each_head(qk_step, c) + each_head(decay_step, c)
                         + (each_head(prep_step, c + 1) if c + 1 < n_chunks else []))
        ret_steps.append(each_head(output_step, c) + each_head(post_step, c))
    for step in each_head(prep_step, 0) if with_ret else ():
        step()
    for proj, ret in itertools.zip_longest(proj_steps, ret_steps, fillvalue=()):
        for step in (*proj, *ret):
            step()


def _l1_mix(h, gain, w_in, freq, cast_weights, tiles_per_seq):
    rows = h.shape[0]
    tm = ROW_TILE
    n_tiles = rows // tm
    half = RET_QK_DIM // 2
    cast_in, cast_out, cast_shapes = _cast_specs(cast_weights)
    return pl.pallas_call(
        functools.partial(_l1_mix_kernel, tiles_per_seq=tiles_per_seq, n_tiles=n_tiles,
                          n_cast=len(cast_in)),
        grid=(n_tiles + 1,),
        in_specs=[pl.BlockSpec((tm, D_MODEL), lambda s: (jnp.minimum(s, n_tiles - 1), 0)),
                  _const_spec((1, D_MODEL)), _const_spec(w_in.shape), _const_spec((1, half))]
        + cast_in,
        out_specs=[pl.BlockSpec((tm, 2 * D_MODEL), lambda s: (jnp.maximum(s - 1, 0), 0))]
        + cast_out,
        out_shape=[jax.ShapeDtypeStruct((rows, 2 * D_MODEL), BF16)] + cast_shapes,
        scratch_shapes=[pltpu.VMEM((2, tm, 6 * D_MODEL), BF16),
                        pltpu.VMEM((RET_HEADS, RET_QK_DIM, RET_V_DIM), F32),
                        pltpu.VMEM((tm, half), F32), pltpu.VMEM((tm, half), F32)],
        compiler_params=pltpu.CompilerParams(dimension_semantics=("arbitrary",)),
        name="l1_mix",
    )(h, gain, w_in, freq, *[w for _, w in cast_weights])


def _inv_freq(half, theta):
    return jnp.power(jnp.asarray(theta, F32), -jnp.arange(half, dtype=F32) / half)


def _block_diag(w):
    heads, wi, wo = w.shape
    eye = jnp.eye(heads, dtype=w.dtype)
    return (eye[:, None, :, None] * w[:, :, None, :]).reshape(heads * wi, heads * wo)


def kernel(x, meta_tokens, mix_norm_ab, ab_w_in, lru_conv_w, lru_conv_b, lru_w_a, lru_b_a, lru_w_i, lru_b_i, lru_lambda, q_norm, k_norm, attn_sinks, ab_w_out, mix_norm_ret, ret_w_in, ret_w_out, ffn_norm, ffn_w_gu, ffn_w_down):
    batch, seq, _ = x.shape
    seq_rows = META_PAD + N_META + seq
    tiles_per_seq = seq_rows // ROW_TILE

    x2d = x.reshape(batch * seq, D_MODEL)
    meta_blk = jnp.concatenate([jnp.zeros((META_PAD, D_MODEL), x.dtype),
                                meta_tokens.astype(x.dtype)], axis=0)

    f_att = _inv_freq(ROT_DIM // 2, ROPE_THETA)
    f_att = jnp.concatenate([f_att, f_att, jnp.zeros((ATT_HEAD_DIM - ROT_DIM,), F32)])
    f_att = jnp.tile(f_att, LANES // ATT_HEAD_DIM).reshape(1, LANES)
    f_ret = _inv_freq(RET_QK_DIM // 2, RET_THETA).reshape(1, RET_QK_DIM // 2)

    row_vec = lambda v: v.reshape(1, -1).astype(F32)
    two_heads = lambda v: jnp.tile(v.reshape(1, -1).astype(F32), (1, 2))

    xr, gate, q, k, v, w_gu0, w_down0 = _inproj0(
        x2d, meta_blk, row_vec(mix_norm_ab[0]), ab_w_in[0].astype(BF16), two_heads(q_norm[0]),
        two_heads(k_norm[0]), f_att, [(0, ffn_w_gu), (0, ffn_w_down)], batch, tiles_per_seq, seq)
    w_gates = jnp.concatenate([_block_diag(lru_w_a[0]), _block_diag(lru_w_i[0])], axis=1)
    b_gates = jnp.concatenate([lru_b_a[0].reshape(1, -1), lru_b_i[0].reshape(1, -1)], axis=1)
    w_att = ab_w_out[0][LRU_WIDTH:].reshape(ATT_KV_HEADS, ATT_GROUP, ATT_HEAD_DIM, D_MODEL)
    w_att = w_att.transpose(1, 0, 2, 3).reshape(Q_WIDTH, D_MODEL)
    w_out0 = jnp.concatenate([ab_w_out[0][:LRU_WIDTH], w_att], axis=0).astype(BF16)
    h, w_in1, w_out1 = _l0_mix_ffn(
        attn_sinks[0].astype(F32), x2d, meta_blk, xr, gate, q, k, v, lru_conv_w[0],
        row_vec(lru_conv_b[0]), w_gates.astype(BF16), b_gates.astype(F32), row_vec(lru_lambda[0]),
        w_out0, row_vec(ffn_norm[0]), w_gu0, w_down0, [(0, ret_w_in), (0, ret_w_out)],
        tiles_per_seq, seq)

    y_ret, w_gu1, w_down1 = _l1_mix(h, row_vec(mix_norm_ret[0]), w_in1, f_ret,
                                    [(1, ffn_w_gu), (1, ffn_w_down)], tiles_per_seq)
    out = _final_outproj_ffn(h, y_ret, w_out1, row_vec(ffn_norm[1]), w_gu1, w_down1,
                             batch, seq_rows, seq)
    return out.reshape(batch, seq, D_MODEL)
```

```python
import functools
import itertools
import math

import jax
import jax.numpy as jnp
from jax import lax
from jax.experimental import pallas as pl
from jax.experimental.pallas import tpu as pltpu

F32 = jnp.float32
BF16 = jnp.bfloat16

D_MODEL = 1024
N_META = 16
BLOCK = 128
META_PAD = BLOCK - N_META
RMS_EPS = 1e-6
NEG_INF = -1e30

LRU_WIDTH = 512
LRU_HEADS = 8
LRU_BLOCK_W = 64
CONV_W = 4
LRU_C = 8.0

ATT_HEADS = 8
ATT_KV_HEADS = 2
ATT_GROUP = ATT_HEADS // ATT_KV_HEADS
ATT_HEAD_DIM = 64
ROPE_THETA = 500000.0
ROT_DIM = 16
Q_WIDTH = 512
KV_WIDTH = 128
AB_IN_WIDTH = 2 * LRU_WIDTH + Q_WIDTH + 2 * KV_WIDTH

RET_HEADS = 4
RET_QK_DIM = 256
RET_V_DIM = 512
RET_THETA = 10000.0
RET_LOG_G = tuple(math.log1p(-(2.0 ** (-5.0 - h))) for h in range(RET_HEADS))

D_FF = 2816

LANES = 128
SUBLANES = 8
ROW_TILE = 640
OUT_TILE = 512
SQRT_GUARD = 1.1754944e-38
GATE_TILE = 256
INPROJ_CHUNK = 256
RET_CHUNK = 256
CAST_STEPS = 16
FFN_CHUNK = 256
DOWN_CHUNK = 512


def _rms_scale(x):
    return lax.rsqrt(jnp.mean(x * x, axis=-1, keepdims=True) + RMS_EPS)


def _sigmoid(x):
    return 0.5 * jnp.tanh(0.5 * x) + 0.5


def _silu(x):
    half = 0.5 * x
    return half + half * jnp.tanh(half)


def _gelu_tanh(x):
    half = 0.5 * x
    return half + half * jnp.tanh(0.7978845608028654 * (x + 0.044715 * (x * x * x)))


def _interleave(primary, secondary):
    out, done = [], 0
    for i, step in enumerate(primary):
        out.append(step)
        upto = ((i + 1) * len(secondary)) // len(primary)
        out.extend(secondary[done:upto])
        done = upto
    return out + list(secondary[done:])


def _const_spec(shape):
    zeros = (0,) * len(shape)
    return pl.BlockSpec(shape, lambda *_: zeros, pipeline_mode=pl.Buffered(1))


def _padded_rows(x_ref, meta_ref, first):
    xw = x_ref[...]
    tm = xw.shape[0]
    return jnp.concatenate([jnp.where(first, meta_ref[...], xw[0:BLOCK]),
                            jnp.where(first, xw[0:tm - BLOCK], xw[BLOCK:tm])], axis=0)


def _token_window_spec(tm, tiles_per_seq, seq, tile_of_step):
    def index(s):
        t = tile_of_step(s)
        start = jnp.maximum((t % tiles_per_seq) * tm - BLOCK, 0)
        return (pl.multiple_of((t // tiles_per_seq) * seq + start, BLOCK), 0)
    return pl.BlockSpec((pl.Element(tm), pl.Element(D_MODEL)), index)


def _rope_offsets(freq_ref, cos_ref, sin_ref):
    r = lax.broadcasted_iota(jnp.int32, cos_ref.shape, 0).astype(F32)
    ang = r * freq_ref[...]
    cos_ref[...] = jnp.cos(ang)
    sin_ref[...] = jnp.sin(ang)


def _rope_tables(base_pos, freq_ref, cos_ref, sin_ref):
    ang = base_pos.astype(F32) * freq_ref[...]
    cb = jnp.cos(ang)
    sb = jnp.sin(ang)
    oc = cos_ref[...]
    os_ = sin_ref[...]
    return cb * oc - sb * os_, sb * oc + cb * os_


def _cast_blocks(cast_refs):
    n = len(cast_refs) // 2
    for src, dst in zip(cast_refs[:n], cast_refs[n:]):
        dst[...] = src[...].astype(BF16)


def _cast_specs(weights):
    ins, outs, shapes = [], [], []
    step = lambda s: jnp.minimum(s, CAST_STEPS - 1)
    for layer, w in weights:
        _, rows, cols = w.shape
        blk = rows // CAST_STEPS
        ins.append(pl.BlockSpec((None, blk, cols), lambda s, layer=layer: (layer, step(s), 0)))
        outs.append(pl.BlockSpec((blk, cols), lambda s: (step(s), 0)))
        shapes.append(jax.ShapeDtypeStruct((rows, cols), BF16))
    return ins, outs, shapes


def _inproj0_kernel(*refs, tiles_per_seq, n_cast):
    (x_ref, meta_ref, gain_ref, w_ref, qg_ref, kg_ref, freq_ref) = refs[:7]
    cast_in = refs[7:7 + n_cast]
    xr_ref, gate_ref, q_ref, k_ref, v_ref = refs[7 + n_cast:12 + n_cast]
    cast_out = refs[12 + n_cast:12 + 2 * n_cast]
    ocos, osin = refs[12 + 2 * n_cast:]
    step = pl.program_id(0)
    tile_in_seq = step % tiles_per_seq

    @pl.when(step == 0)
    def _():
        _rope_offsets(freq_ref, ocos, osin)

    _cast_blocks(cast_in + cast_out)

    h = _padded_rows(x_ref, meta_ref, tile_in_seq == 0)
    rows = h.shape[0]
    xn = (h * _rms_scale(h) * gain_ref[...]).astype(BF16)

    q0 = 2 * LRU_WIDTH
    k0 = q0 + Q_WIDTH
    chunk_dot = lambda lo: jnp.dot(xn, w_ref[:, lo:lo + INPROJ_CHUNK], preferred_element_type=F32)
    y_q = [chunk_dot(q0 + lo) for lo in range(0, Q_WIDTH, INPROJ_CHUNK)]
    y_kv = chunk_dot(k0)
    v_ref[...] = y_kv[:, KV_WIDTH:].astype(BF16)

    cos, sin = _rope_tables(tile_in_seq * rows - META_PAD, freq_ref, ocos, osin)
    lane = lax.broadcasted_iota(jnp.int32, (1, LANES), 1) & (ATT_HEAD_DIM - 1)
    half = ROT_DIM // 2
    sin_lo = sin * jnp.where(lane < half, -1.0, 0.0)
    sin_hi = sin * jnp.where((lane >= half) & (lane < ROT_DIM), 1.0, 0.0)
    hi = lax.broadcasted_iota(jnp.int32, (2 * LANES, LANES), 0) & (LANES - 1)
    hj = lax.broadcasted_iota(jnp.int32, (2 * LANES, LANES), 1)
    head_mean = jnp.where(hi // ATT_HEAD_DIM == hj // ATT_HEAD_DIM,
                          1.0 / ATT_HEAD_DIM, 0.0).astype(BF16)

    def norm_rope(x, gain, scale):
        sq = x * x
        sq_hi = sq.astype(BF16)
        sq_lo = (sq - sq_hi.astype(F32)).astype(BF16)
        ms = jnp.dot(jnp.concatenate([sq_hi, sq_lo], axis=1), head_mean,
                     preferred_element_type=F32)
        xg = x * gain
        rot = (xg * cos + pltpu.roll(xg, LANES - half, 1) * sin_lo
               + pltpu.roll(xg, half, 1) * sin_hi)
        return rot * (lax.rsqrt(ms + RMS_EPS) * scale)

    def qk_step(j):
        if j < Q_WIDTH // LANES:
            lo = (j * LANES) % INPROJ_CHUNK
            q_ref[:, j * LANES:(j + 1) * LANES] = norm_rope(
                y_q[j * LANES // INPROJ_CHUNK][:, lo:lo + LANES], qg_ref[...],
                ATT_HEAD_DIM ** -0.5).astype(BF16)
        else:
            k_ref[...] = norm_rope(y_kv[:, :KV_WIDTH], kg_ref[...], 1.0).astype(BF16)

    def proj_step(c):
        out_ref = xr_ref if c < LRU_WIDTH // INPROJ_CHUNK else gate_ref
        dst = slice((c * INPROJ_CHUNK) % LRU_WIDTH, (c * INPROJ_CHUNK) % LRU_WIDTH + INPROJ_CHUNK)
        out_ref[:, dst] = chunk_dot(c * INPROJ_CHUNK)

    proj_steps = [functools.partial(proj_step, c) for c in range(2 * LRU_WIDTH // INPROJ_CHUNK)]
    qk_steps = [functools.partial(qk_step, j) for j in range(Q_WIDTH // LANES + 1)]
    for step in _interleave(proj_steps, qk_steps):
        step()


def _inproj0(x2d, meta_blk, gain, w_in, q_gain, k_gain, freq, cast_weights, batch,
             tiles_per_seq, seq):
    tm = ROW_TILE
    rows = batch * tiles_per_seq * tm
    row = lambda w: pl.BlockSpec((tm, w), lambda i: (i, 0))
    cast_in, cast_out, cast_shapes = _cast_specs(cast_weights)
    return pl.pallas_call(
        functools.partial(_inproj0_kernel, tiles_per_seq=tiles_per_seq, n_cast=len(cast_in)),
        grid=(rows // tm,),
        in_specs=[_token_window_spec(tm, tiles_per_seq, seq, lambda s: s),
                  _const_spec((BLOCK, D_MODEL)), _const_spec((1, D_MODEL)),
                  _const_spec((D_MODEL, AB_IN_WIDTH)), _const_spec((1, LANES)),
                  _const_spec((1, LANES)), _const_spec((1, LANES))] + cast_in,
        out_specs=[row(LRU_WIDTH), row(LRU_WIDTH), row(Q_WIDTH), row(KV_WIDTH), row(KV_WIDTH)]
        + cast_out,
        out_shape=[jax.ShapeDtypeStruct((rows, LRU_WIDTH), F32),
                   jax.ShapeDtypeStruct((rows, LRU_WIDTH), F32),
                   jax.ShapeDtypeStruct((rows, Q_WIDTH), BF16),
                   jax.ShapeDtypeStruct((rows, KV_WIDTH), BF16),
                   jax.ShapeDtypeStruct((rows, KV_WIDTH), BF16)] + cast_shapes,
        scratch_shapes=[pltpu.VMEM((tm, LANES), F32), pltpu.VMEM((tm, LANES), F32)],
        compiler_params=pltpu.CompilerParams(dimension_semantics=("arbitrary",)),
        name="l0_inproj",
    )(x2d, meta_blk, gain, w_in, q_gain, k_gain, freq, *[w for _, w in cast_weights])


def _attn_probs(n, sinks, q, kc, kp, km):
    j = lax.broadcasted_iota(jnp.int32, (BLOCK, BLOCK), 0)
    i = lax.broadcasted_iota(jnp.int32, (BLOCK, BLOCK), 1)
    causal = j <= i
    win_ok = n >= jnp.where(causal, 1, 2)
    meta_ok = (j >= META_PAD) & (n >= jnp.where(causal, 0, 1))
    contract_last = (((1,), (1,)), ((), ()))

    qs = jnp.concatenate([q[:, a * ATT_HEAD_DIM:(a + 1) * ATT_HEAD_DIM]
                          for a in range(ATT_GROUP)], axis=0)
    s_c = lax.dot_general(kc, qs, contract_last, preferred_element_type=F32)
    s_p = lax.dot_general(kp, qs, contract_last, preferred_element_type=F32)
    s_m = lax.dot_general(km, qs, contract_last, preferred_element_type=F32)
    p_c, p_p, p_m, inv_den = [], [], [], []
    for a in range(ATT_GROUP):
        head = slice(a * BLOCK, (a + 1) * BLOCK)
        sw = jnp.where(win_ok, jnp.where(causal, s_c[:, head], s_p[:, head]), NEG_INF)
        sm = jnp.where(meta_ok, s_m[:, head], NEG_INF)
        m = jnp.maximum(jnp.maximum(jnp.max(sw, axis=0, keepdims=True),
                                    jnp.max(sm, axis=0, keepdims=True)), sinks[a])
        pw = jnp.exp(sw - m)
        pm = jnp.exp(sm - m)
        den = (jnp.sum(pw, axis=0, keepdims=True) + jnp.sum(pm, axis=0, keepdims=True)
               + jnp.exp(sinks[a] - m))
        inv_den.append(1.0 / den)
        p_c.append(jnp.where(causal, pw, 0.0).astype(BF16))
        p_p.append(jnp.where(causal, 0.0, pw).astype(BF16))
        p_m.append(pm.astype(BF16))
    lanes = lambda parts: jnp.concatenate(parts, axis=1)
    return lanes(p_c), lanes(p_p), lanes(p_m), lanes(inv_den)


def _attn_out_t(probs, vc, vp, vm):
    p_c, p_p, p_m, inv_den = probs
    contract_rows = (((0,), (0,)), ((), ()))
    o_t = (lax.dot_general(vc, p_c, contract_rows, preferred_element_type=F32)
           + lax.dot_general(vp, p_p, contract_rows, preferred_element_type=F32)
           + lax.dot_general(vm, p_m, contract_rows, preferred_element_type=F32))
    return o_t * inv_den


def _attn_untranspose(o_t):
    o_t = jnp.concatenate(o_t, axis=0)
    return jnp.concatenate([o_t[:, a * BLOCK:(a + 1) * BLOCK].T for a in range(ATT_GROUP)],
                           axis=1).astype(BF16)


def _lru_gates(lo, xr_ref, rows, cw, cb, wg_ref, xbuf):
    tl = BLOCK
    cols = slice(lo, lo + GATE_TILE)
    x = xr_ref[rows, cols]
    xbuf[SUBLANES:SUBLANES + tl, cols] = x
    xc = x * cw[CONV_W - 1:CONV_W, cols] + cb[:, cols]
    for d in range(1, CONV_W):
        xc = xc + (xbuf[SUBLANES - d:SUBLANES - d + tl, cols]
                   * cw[CONV_W - 1 - d:CONV_W - d, cols])
    xbuf[0:SUBLANES, cols] = x[tl - SUBLANES:tl]
    xcb = xc.astype(BF16)
    ga_r = jnp.dot(xcb, wg_ref[cols, cols], preferred_element_type=F32)
    ga_i = jnp.dot(xcb, wg_ref[cols, LRU_WIDTH + lo:LRU_WIDTH + lo + GATE_TILE],
                   preferred_element_type=F32)
    return xc, ga_r, ga_i


def _lru_scan(n, xc, ga_r, ga_i, grp, gate_ref, rows, bg, softplus, hcar, y_ref):
    tl = BLOCK
    row = lax.broadcasted_iota(jnp.int32, (tl, 1), 0)
    t = n * tl + row
    r = _sigmoid(ga_r + bg[:, grp])
    gi = _sigmoid(ga_i + bg[:, LRU_WIDTH + grp.start:LRU_WIDTH + grp.stop])
    log_a = (-LRU_C * softplus[:, grp]) * r
    a = jnp.exp(log_a)
    mult2 = jnp.tanh(-log_a) * (a * a + 1.0)
    mult = mult2 * lax.rsqrt(jnp.maximum(mult2, SQRT_GUARD))
    mult = jnp.where(t == META_PAD, 1.0, mult)
    b = jnp.where(t < META_PAD, 0.0, mult * gi * xc)

    d = 1
    while d < SUBLANES:
        keep = row >= d
        b = jnp.where(keep, a * pltpu.roll(b, d, 0), 0.0) + b
        a = jnp.where(keep, a * pltpu.roll(a, d, 0), a)
        d *= 2
    while d < tl:
        b = jnp.concatenate([b[:d], a[d:] * b[:tl - d] + b[d:]], axis=0)
        a = jnp.concatenate([a[:d], a[d:] * a[:tl - d]], axis=0)
        d *= 2
    h = b + a * hcar[0:1, grp]
    hcar[:, grp] = jnp.broadcast_to(h[tl - 1:tl], (SUBLANES, LANES))
    y_ref[rows, grp] = (_gelu_tanh(gate_ref[rows, grp]) * h).astype(BF16)


def _ffn(h1, gain, wgu_ref, wd_ref):
    xn = (h1 * _rms_scale(h1) * gain).astype(BF16)
    acts = []
    for c in range(D_FF // FFN_CHUNK):
        lo = c * FFN_CHUNK
        g = jnp.dot(xn, wgu_ref[:, lo:lo + FFN_CHUNK], preferred_element_type=F32)
        u = jnp.dot(xn, wgu_ref[:, D_FF + lo:D_FF + lo + FFN_CHUNK], preferred_element_type=F32)
        acts.append((_silu(g) * u).astype(BF16))
    act = jnp.concatenate(acts, axis=1)
    return h1 + jnp.dot(act, wd_ref[...], preferred_element_type=F32)


def _l0_mix_ffn_kernel(*refs, tiles_per_seq, n_tiles, n_cast):
    k_ref, v_ref = refs[6:8]
    cast_in = refs[17:17 + n_cast]
    cast_out = refs[18 + n_cast:18 + 2 * n_cast]
    _, xbuf, hcar, kprev, vprev, kmeta, vmeta = refs[18 + 2 * n_cast:]
    s = pl.program_id(0)
    tile_in_seq = jnp.minimum(s, n_tiles - 1) % tiles_per_seq
    _cast_blocks(cast_in + cast_out)

    @pl.when(tile_in_seq == 0)
    def _():
        xbuf[0:SUBLANES, :] = jnp.zeros((SUBLANES, LRU_WIDTH), F32)
        hcar[...] = jnp.zeros_like(hcar)
        kmeta[...] = k_ref[0:BLOCK, :]
        vmeta[...] = v_ref[0:BLOCK, :]
        kprev[...] = k_ref[0:BLOCK, :]
        vprev[...] = v_ref[0:BLOCK, :]

    body = functools.partial(_l0_mix_ffn_body, refs, n_cast, tile_in_seq, s, tiles_per_seq)
    pl.when(s == 0)(functools.partial(body, with_ffn=False, with_mix=True))
    pl.when(s > 0)(functools.partial(body, with_ffn=True, with_mix=True))


def _l0_mix_ffn_body(refs, n_cast, tile_in_seq, s, tiles_per_seq, *, with_ffn, with_mix):
    (sink_ref, x_ref, meta_ref, xr_ref, gate_ref, q_ref, k_ref, v_ref, cw_ref, cb_ref, wg_ref,
     bg_ref, lam_ref, wo_ref, gain_ref, wgu_ref, wd_ref) = refs[:17]
    o_ref = refs[17 + n_cast]
    ybuf, xbuf, hcar, kprev, vprev, kmeta, vmeta = refs[18 + 2 * n_cast:]
    blocks_per_tile = x_ref.shape[0] // BLOCK
    last = slice((blocks_per_tile - 1) * BLOCK, blocks_per_tile * BLOCK)

    if with_ffn:
        ffn_first = (s - 1) % tiles_per_seq == 0
        h1 = (_padded_rows(x_ref, meta_ref, ffn_first)
              + jnp.dot(ybuf[...], wo_ref[...], preferred_element_type=F32))
        xn = (h1 * _rms_scale(h1) * gain_ref[...]).astype(BF16)

    cw = cw_ref[...]
    cb = cb_ref[...]
    bg = bg_ref[...]
    z = -lam_ref[...]
    softplus = jnp.maximum(z, 0.0) + jnp.log1p(jnp.exp(-jnp.abs(z)))

    live = {}

    def block_rows(blk):
        return tile_in_seq * blocks_per_tile + blk, slice(blk * BLOCK, (blk + 1) * BLOCK)

    def lru_gates_step(blk, lo):
        _, rows = block_rows(blk)
        live["lru", blk, lo] = _lru_gates(lo, xr_ref, rows, cw, cb, wg_ref, xbuf)

    def lru_scan_step(blk, lo, sub):
        n, rows = block_rows(blk)
        xc, ga_r, ga_i = live["lru", blk, lo]
        part = slice(sub, sub + LANES)
        _lru_scan(n, xc[:, part], ga_r[:, part], ga_i[:, part],
                  slice(lo + sub, lo + sub + LANES), gate_ref, rows, bg, softplus, hcar, ybuf)
        if sub + LANES == GATE_TILE:
            del live["lru", blk, lo]

    def kv_blocks(ref, prev_ref, meta_ref, blk, g):
        _, rows = block_rows(blk)
        lanes = slice(g * ATT_HEAD_DIM, (g + 1) * ATT_HEAD_DIM)
        prev = prev_ref[:, lanes] if blk == 0 else ref[(blk - 1) * BLOCK:blk * BLOCK, lanes]
        return ref[rows, lanes], prev, meta_ref[:, lanes]

    def attn_probs_step(blk, g):
        n, rows = block_rows(blk)
        width = ATT_GROUP * ATT_HEAD_DIM
        sinks = [sink_ref[g * ATT_GROUP + a] for a in range(ATT_GROUP)]
        live["probs", blk, g] = _attn_probs(n, sinks, q_ref[rows, g * width:(g + 1) * width],
                                            *kv_blocks(k_ref, kprev, kmeta, blk, g))

    def attn_out_step(blk, g):
        _, rows = block_rows(blk)
        live["out", blk, g] = _attn_out_t(live.pop(("probs", blk, g)),
                                          *kv_blocks(v_ref, vprev, vmeta, blk, g))
        if g + 1 == ATT_KV_HEADS:
            ybuf[rows, LRU_WIDTH:LRU_WIDTH + Q_WIDTH] = _attn_untranspose(
                [live.pop(("out", blk, h)) for h in range(ATT_KV_HEADS)])

    acts = []

    def gate_step(c):
        lo = c * FFN_CHUNK
        live["g", c] = jnp.dot(xn, wgu_ref[:, lo:lo + FFN_CHUNK], preferred_element_type=F32)

    def up_step(c):
        lo = c * FFN_CHUNK
        u = jnp.dot(xn, wgu_ref[:, D_FF + lo:D_FF + lo + FFN_CHUNK], preferred_element_type=F32)
        acts.append((_silu(live.pop(("g", c))) * u).astype(BF16))

    def down_step(c):
        cols = slice(c * DOWN_CHUNK, (c + 1) * DOWN_CHUNK)
        if len(acts) > 1:
            acts[:] = [jnp.concatenate(acts, axis=1)]
        o_ref[:, cols] = h1[:, cols] + jnp.dot(acts[0], wd_ref[:, cols],
                                               preferred_element_type=F32)

    ffn_steps = []
    if with_ffn:
        for c in range(D_FF // FFN_CHUNK):
            ffn_steps += [functools.partial(gate_step, c), functools.partial(up_step, c)]
        ffn_steps += [functools.partial(down_step, c) for c in range(D_MODEL // DOWN_CHUNK)]
    mix_steps = []
    if with_mix:
        for blk in range(blocks_per_tile):
            for lo in range(0, LRU_WIDTH, GATE_TILE):
                mix_steps.append(functools.partial(lru_gates_step, blk, lo))
                mix_steps += [functools.partial(lru_scan_step, blk, lo, sub)
                              for sub in range(0, GATE_TILE, LANES)]
            mix_steps += [functools.partial(attn_probs_step, blk, g)
                          for g in range(ATT_KV_HEADS)]
            mix_steps += [functools.partial(attn_out_step, blk, g) for g in range(ATT_KV_HEADS)]
    for step in (_interleave(ffn_steps, mix_steps) if ffn_steps else mix_steps):
        step()
    if with_mix:
        kprev[...] = k_ref[last, :]
        vprev[...] = v_ref[last, :]


def _l0_mix_ffn(sinks, x2d, meta_blk, xr, gate, q, k, v, conv_w, conv_b, w_gates, b_gates, lam,
                w_out, gain, w_gu, w_down, cast_weights, tiles_per_seq, seq):
    rows = xr.shape[0]
    tm = ROW_TILE
    n_tiles = rows // tm
    mix = lambda w: pl.BlockSpec((tm, w), lambda s: (jnp.minimum(s, n_tiles - 1), 0))
    cast_in, cast_out, cast_shapes = _cast_specs(cast_weights)
    return pl.pallas_call(
        functools.partial(_l0_mix_ffn_kernel, tiles_per_seq=tiles_per_seq, n_tiles=n_tiles,
                          n_cast=len(cast_in)),
        grid=(n_tiles + 1,),
        in_specs=[pl.BlockSpec(memory_space=pltpu.SMEM),
                  _token_window_spec(tm, tiles_per_seq, seq, lambda s: jnp.maximum(s - 1, 0)),
                  _const_spec((BLOCK, D_MODEL)),
                  mix(LRU_WIDTH), mix(LRU_WIDTH), mix(Q_WIDTH), mix(KV_WIDTH), mix(KV_WIDTH),
                  _const_spec((CONV_W, LRU_WIDTH)), _const_spec((1, LRU_WIDTH)),
                  _const_spec((LRU_WIDTH, 2 * LRU_WIDTH)), _const_spec((1, 2 * LRU_WIDTH)),
                  _const_spec((1, LRU_WIDTH)), _const_spec(w_out.shape), _const_spec((1, D_MODEL)),
                  _const_spec(w_gu.shape), _const_spec(w_down.shape)] + cast_in,
        out_specs=[pl.BlockSpec((tm, D_MODEL), lambda s: (jnp.maximum(s - 1, 0), 0))] + cast_out,
        out_shape=[jax.ShapeDtypeStruct((rows, D_MODEL), F32)] + cast_shapes,
        scratch_shapes=[pltpu.VMEM((tm, LRU_WIDTH + Q_WIDTH), BF16),
                        pltpu.VMEM((SUBLANES + BLOCK, LRU_WIDTH), F32),
                        pltpu.VMEM((SUBLANES, LRU_WIDTH), F32),
                        pltpu.VMEM((BLOCK, KV_WIDTH), BF16), pltpu.VMEM((BLOCK, KV_WIDTH), BF16),
                        pltpu.VMEM((BLOCK, KV_WIDTH), BF16), pltpu.VMEM((BLOCK, KV_WIDTH), BF16)],
        compiler_params=pltpu.CompilerParams(dimension_semantics=("arbitrary",)),
        name="l0_mix_ffn",
    )(sinks, x2d, meta_blk, xr, gate, q, k, v, conv_w, conv_b, w_gates, b_gates, lam, w_out, gain,
      w_gu, w_down, *[w for _, w in cast_weights])


def _outproj_ffn_kernel(h_ref, y_ref, wo_ref, gain_ref, wgu_ref, wd_ref, o_ref):
    h1 = h_ref[...] + jnp.dot(y_ref[...], wo_ref[...], preferred_element_type=F32)
    o_ref[...] = _ffn(h1, gain_ref[...], wgu_ref, wd_ref)


def _final_outproj_ffn(h, y, w_out, gain, w_gu, w_down, batch, seq_rows, seq):
    tm = OUT_TILE
    tiles = seq // tm
    win = lambda w: pl.BlockSpec((pl.Element(tm), pl.Element(w)),
                                 lambda b, j: (pl.multiple_of(b * seq_rows + BLOCK + j * tm, BLOCK),
                                               0))
    return pl.pallas_call(
        _outproj_ffn_kernel,
        grid=(batch, tiles),
        in_specs=[win(D_MODEL), win(y.shape[1]), _const_spec(w_out.shape),
                  _const_spec((1, D_MODEL)), _const_spec(w_gu.shape), _const_spec(w_down.shape)],
        out_specs=pl.BlockSpec((tm, D_MODEL), lambda b, j: (b * tiles + j, 0)),
        out_shape=jax.ShapeDtypeStruct((batch * seq, D_MODEL), F32),
        compiler_params=pltpu.CompilerParams(dimension_semantics=("arbitrary", "arbitrary")),
        name="l1_outproj_ffn",
    )(h, y, w_out, gain, w_gu, w_down)


_Q0, _K0, _V0, _G0 = 0, D_MODEL, 2 * D_MODEL, 4 * D_MODEL


def _retention_decays(hd, length):
    ii = lax.broadcasted_iota(jnp.int32, (length, length), 0)
    jj = lax.broadcasted_iota(jnp.int32, (length, length), 1)
    diff = (ii - jj).astype(F32)
    idx = lax.broadcasted_iota(jnp.int32, (length, 1), 0).astype(F32)
    log_g = RET_LOG_G[hd]
    decay_intra = jnp.where(diff >= 0.0, jnp.exp(jnp.maximum(diff, 0.0) * log_g), 0.0)
    return (decay_intra, jnp.exp((idx + 1.0) * log_g), jnp.exp((length - 1.0 - idx) * log_g),
            math.exp(length * log_g))


def _l1_mix_kernel(*refs, tiles_per_seq, n_tiles, n_cast):
    h_ref, gain_ref, w_ref, freq_ref = refs[:4]
    cast_in = refs[4:4 + n_cast]
    y_ref = refs[4 + n_cast]
    cast_out = refs[5 + n_cast:5 + 2 * n_cast]
    qkvg, state, ocos, osin = refs[5 + 2 * n_cast:]
    _cast_blocks(cast_in + cast_out)
    s = pl.program_id(0)

    @pl.when(s == 0)
    def _():
        _rope_offsets(freq_ref, ocos, osin)
        qkvg[1] = jnp.zeros(qkvg.shape[1:], BF16)

    @pl.when(jnp.maximum(s - 1, 0) % tiles_per_seq == 0)
    def _():
        state[...] = jnp.zeros_like(state)

    _l1_mix_body(refs, n_cast, s, jnp.minimum(s, n_tiles - 1), tiles_per_seq,
                 with_proj=True, with_ret=True)


def _l1_mix_body(refs, n_cast, s, proj_tile, tiles_per_seq, *, with_proj, with_ret):
    h_ref, gain_ref, w_ref, freq_ref = refs[:4]
    y_ref = refs[4 + n_cast]
    qkvg, state, ocos, osin = refs[5 + 2 * n_cast:]
    proj_slot = s % 2
    ret_slot = 1 - proj_slot
    tm = h_ref.shape[0]
    half = RET_QK_DIM // 2
    if with_proj:
        h = h_ref[...]
        xn = (h * _rms_scale(h) * gain_ref[...]).astype(BF16)
        cos, sin = _rope_tables((proj_tile % tiles_per_seq) * tm - META_PAD, freq_ref, ocos, osin)

    def rope_step(col0, hd, scale):
        lo = col0 + hd * RET_QK_DIM
        y = jnp.dot(xn, w_ref[:, lo:lo + RET_QK_DIM], preferred_element_type=F32)
        x1 = y[:, :half]
        x2 = y[:, half:]
        qkvg[proj_slot, :, lo:lo + half] = ((x1 * cos - x2 * sin) * scale).astype(BF16)
        qkvg[proj_slot, :, lo + half:lo + RET_QK_DIM] = ((x2 * cos + x1 * sin)
                                                         * scale).astype(BF16)

    def value_step(col0, hd, act):
        lo = col0 + hd * RET_V_DIM
        y = jnp.dot(xn, w_ref[:, lo:lo + RET_V_DIM], preferred_element_type=F32)
        qkvg[proj_slot, :, lo:lo + RET_V_DIM] = act(y).astype(BF16)

    chunks = []
    while sum(length for _, length in chunks) < tm:
        row0 = sum(length for _, length in chunks)
        chunks.append((row0, min(RET_CHUNK, tm - row0)))
    n_chunks = len(chunks)

    def chunk_rows(c):
        return slice(chunks[c][0], chunks[c][0] + chunks[c][1])

    def ret_operand(c, hd, col0, width):
        return qkvg[ret_slot, chunk_rows(c), col0 + hd * width:col0 + (hd + 1) * width]

    heads = range(RET_HEADS)
    decay_tables = {length: [_retention_decays(hd, length) for hd in heads]
                    for length in sorted({length for _, length in chunks})}
    decays = lambda c, hd: decay_tables[chunks[c][1]][hd]
    live = {}

    def prep_step(c, hd):
        live["kdt", c, hd] = (ret_operand(c, hd, _K0, RET_QK_DIM).astype(F32)
                              * decays(c, hd)[2]).T.astype(BF16)

    def qk_step(c, hd):
        live["qk", c, hd] = lax.dot_general(
            ret_operand(c, hd, _Q0, RET_QK_DIM), ret_operand(c, hd, _K0, RET_QK_DIM),
            (((1,), (1,)), ((), ())), preferred_element_type=F32)

    def decay_step(c, hd):
        qkd = (live.pop(("qk", c, hd)) * decays(c, hd)[0]).astype(BF16)
        live["lhs", c, hd] = jnp.concatenate([qkd, live.pop(("kdt", c, hd))], axis=0)
        live["stb", c, hd] = state[hd].astype(BF16)

    def output_step(c, hd):
        length = chunks[c][1]
        both = jnp.dot(live.pop(("lhs", c, hd)), ret_operand(c, hd, _V0, RET_V_DIM),
                       preferred_element_type=F32)
        live["kv", c, hd] = both[length:]
        live["o", c, hd] = both[:length] + jnp.dot(
            ret_operand(c, hd, _Q0, RET_QK_DIM), live.pop(("stb", c, hd)),
            preferred_element_type=F32) * decays(c, hd)[1]

    def post_step(c, hd):
        o = live.pop(("o", c, hd))
        state[hd] = decays(c, hd)[3] * state[hd] + live.pop(("kv", c, hd))
        gate = ret_operand(c, hd, _G0, RET_V_DIM).astype(F32)
        y_ref[chunk_rows(c), hd * RET_V_DIM:(hd + 1) * RET_V_DIM] = (
            o * _rms_scale(o) * gate).astype(BF16)

    proj_steps = []
    for hd in heads if with_proj else ():
        proj_steps.append([functools.partial(rope_step, _Q0, hd, 1.0),
                           functools.partial(rope_step, _K0, hd, RET_QK_DIM ** -0.5)])
        proj_steps.append([functools.partial(value_step, _V0, hd, lambda y: y)])
        proj_steps.append([functools.partial(value_step, _G0, hd, _silu)])
    each_head = lambda step, c: [functools.partial(step, c, hd) for hd in heads]
    ret_steps = []
    for c in range(n_chunks) if with_ret else ():
        ret_steps.append(each_head(qk_step, c) + each_head(decay_step, c)
                         + (each_head(prep_step, c + 1) if c + 1 < n_chunks else []))
        ret_steps.append(each_head(output_step, c) + each_head(post_step, c))
    for step in each_head(prep_step, 0) if with_ret else ():
        step()
    for proj, ret in itertools.zip_longest(proj_steps, ret_steps, fillvalue=()):
        for step in (*proj, *ret):
            step()


def _l1_mix(h, gain, w_in, freq, cast_weights, tiles_per_seq):
    rows = h.shape[0]
    tm = ROW_TILE
    n_tiles = rows // tm
    half = RET_QK_DIM // 2
    cast_in, cast_out, cast_shapes = _cast_specs(cast_weights)
    return pl.pallas_call(
        functools.partial(_l1_mix_kernel, tiles_per_seq=tiles_per_seq, n_tiles=n_tiles,
                          n_cast=len(cast_in)),
        grid=(n_tiles + 1,),
        in_specs=[pl.BlockSpec((tm, D_MODEL), lambda s: (jnp.minimum(s, n_tiles - 1), 0)),
                  _const_spec((1, D_MODEL)), _const_spec(w_in.shape), _const_spec((1, half))]
        + cast_in,
        out_specs=[pl.BlockSpec((tm, 2 * D_MODEL), lambda s: (jnp.maximum(s - 1, 0), 0))]
        + cast_out,
        out_shape=[jax.ShapeDtypeStruct((rows, 2 * D_MODEL), BF16)] + cast_shapes,
        scratch_shapes=[pltpu.VMEM((2, tm, 6 * D_MODEL), BF16),
                        pltpu.VMEM((RET_HEADS, RET_QK_DIM, RET_V_DIM), F32),
                        pltpu.VMEM((tm, half), F32), pltpu.VMEM((tm, half), F32)],
        compiler_params=pltpu.CompilerParams(dimension_semantics=("arbitrary",)),
        name="l1_mix",
    )(h, gain, w_in, freq, *[w for _, w in cast_weights])


def _inv_freq(half, theta):
    return jnp.power(jnp.asarray(theta, F32), -jnp.arange(half, dtype=F32) / half)


def _block_diag(w):
    heads, wi, wo = w.shape
    eye = jnp.eye(heads, dtype=w.dtype)
    return (eye[:, None, :, None] * w[:, :, None, :]).reshape(heads * wi, heads * wo)


def kernel(x, meta_tokens, mix_norm_ab, ab_w_in, lru_conv_w, lru_conv_b, lru_w_a, lru_b_a, lru_w_i, lru_b_i, lru_lambda, q_norm, k_norm, attn_sinks, ab_w_out, mix_norm_ret, ret_w_in, ret_w_out, ffn_norm, ffn_w_gu, ffn_w_down):
    batch, seq, _ = x.shape
    seq_rows = META_PAD + N_META + seq
    tiles_per_seq = seq_rows // ROW_TILE

    x2d = x.reshape(batch * seq, D_MODEL)
    meta_blk = jnp.concatenate([jnp.zeros((META_PAD, D_MODEL), x.dtype),
                                meta_tokens.astype(x.dtype)], axis=0)

    f_att = _inv_freq(ROT_DIM // 2, ROPE_THETA)
    f_att = jnp.concatenate([f_att, f_att, jnp.zeros((ATT_HEAD_DIM - ROT_DIM,), F32)])
    f_att = jnp.tile(f_att, LANES // ATT_HEAD_DIM).reshape(1, LANES)
    f_ret = _inv_freq(RET_QK_DIM // 2, RET_THETA).reshape(1, RET_QK_DIM // 2)

    row_vec = lambda v: v.reshape(1, -1).astype(F32)
    two_heads = lambda v: jnp.tile(v.reshape(1, -1).astype(F32), (1, 2))

    xr, gate, q, k, v, w_gu0, w_down0 = _inproj0(
        x2d, meta_blk, row_vec(mix_norm_ab[0]), ab_w_in[0].astype(BF16), two_heads(q_norm[0]),
        two_heads(k_norm[0]), f_att, [(0, ffn_w_gu), (0, ffn_w_down)], batch, tiles_per_seq, seq)
    w_gates = jnp.concatenate([_block_diag(lru_w_a[0]), _block_diag(lru_w_i[0])], axis=1)
    b_gates = jnp.concatenate([lru_b_a[0].reshape(1, -1), lru_b_i[0].reshape(1, -1)], axis=1)
    w_att = ab_w_out[0][LRU_WIDTH:].reshape(ATT_KV_HEADS, ATT_GROUP, ATT_HEAD_DIM, D_MODEL)
    w_att = w_att.transpose(1, 0, 2, 3).reshape(Q_WIDTH, D_MODEL)
    w_out0 = jnp.concatenate([ab_w_out[0][:LRU_WIDTH], w_att], axis=0).astype(BF16)
    h, w_in1, w_out1 = _l0_mix_ffn(
        attn_sinks[0].astype(F32), x2d, meta_blk, xr, gate, q, k, v, lru_conv_w[0],
        row_vec(lru_conv_b[0]), w_gates.astype(BF16), b_gates.astype(F32), row_vec(lru_lambda[0]),
        w_out0, row_vec(ffn_norm[0]), w_gu0, w_down0, [(0, ret_w_in), (0, ret_w_out)],
        tiles_per_seq, seq)

    y_ret, w_gu1, w_down1 = _l1_mix(h, row_vec(mix_norm_ret[0]), w_in1, f_ret,
                                    [(1, ffn_w_gu), (1, ffn_w_down)], tiles_per_seq)
    out = _final_outproj_ffn(h, y_ret, w_out1, row_vec(ffn_norm[1]), w_gu1, w_down1,
                             batch, seq_rows, seq)
    return out.reshape(batch, seq, D_MODEL)
```

```python
import functools
import itertools
import math

import jax
import jax.numpy as jnp
from jax import lax
from jax.experimental import pallas as pl
from jax.experimental.pallas import tpu as pltpu

F32 = jnp.float32
BF16 = jnp.bfloat16

D_MODEL = 1024
N_META = 16
BLOCK = 128
META_PAD = BLOCK - N_META
RMS_EPS = 1e-6
NEG_INF = -1e30

LRU_WIDTH = 512
CONV_W = 4
LRU_C = 8.0

ATT_HEADS = 8
ATT_KV_HEADS = 2
ATT_GROUP = ATT_HEADS // ATT_KV_HEADS
ATT_HEAD_DIM = 64
ROPE_THETA = 500000.0
ROT_DIM = 16
Q_WIDTH = 512
KV_WIDTH = 128
AB_IN_WIDTH = 2 * LRU_WIDTH + Q_WIDTH + 2 * KV_WIDTH

RET_HEADS = 4
RET_QK_DIM = 256
RET_V_DIM = 512
RET_THETA = 10000.0
RET_LOG_G = tuple(math.log1p(-(2.0 ** (-5.0 - h))) for h in range(RET_HEADS))

D_FF = 2816

LANES = 128
SUBLANES = 8
ROW_TILE = 640
OUT_TILE = 512
SQRT_GUARD = 1.1754944e-38
GATE_TILE = 256
INPROJ_CHUNK = 256
RET_CHUNK = 256
CAST_STEPS = 16
FFN_CHUNK = 256
DOWN_CHUNK = 512


def _rms_scale(x):
    return lax.rsqrt(jnp.mean(x * x, axis=-1, keepdims=True) + RMS_EPS)


def _sigmoid(x):
    return 0.5 * jnp.tanh(0.5 * x) + 0.5


def _silu(x):
    half = 0.5 * x
    return half + half * jnp.tanh(half)


def _gelu_tanh(x):
    half = 0.5 * x
    return half + half * jnp.tanh(0.7978845608028654 * (x + 0.044715 * (x * x * x)))


def _interleave(primary, secondary):
    out, done = [], 0
    for i, step in enumerate(primary):
        out.append(step)
        upto = ((i + 1) * len(secondary)) // len(primary)
        out.extend(secondary[done:upto])
        done = upto
    return out + list(secondary[done:])


def _const_spec(shape):
    zeros = (0,) * len(shape)
    return pl.BlockSpec(shape, lambda *_: zeros, pipeline_mode=pl.Buffered(1))


def _padded_rows(x_ref, meta_ref, first):
    xw = x_ref[...]
    tm = xw.shape[0]
    return jnp.concatenate([jnp.where(first, meta_ref[...], xw[0:BLOCK]),
                            jnp.where(first, xw[0:tm - BLOCK], xw[BLOCK:tm])], axis=0)


def _token_window_spec(tm, tiles_per_seq, seq, tile_of_step):
    def index(s):
        t = tile_of_step(s)
        start = jnp.maximum((t % tiles_per_seq) * tm - BLOCK, 0)
        return (pl.multiple_of((t // tiles_per_seq) * seq + start, BLOCK), 0)
    return pl.BlockSpec((pl.Element(tm), pl.Element(D_MODEL)), index)


def _rope_offsets(freq_ref, cos_ref, sin_ref):
    r = lax.broadcasted_iota(jnp.int32, cos_ref.shape, 0).astype(F32)
    ang = r * freq_ref[...]
    cos_ref[...] = jnp.cos(ang)
    sin_ref[...] = jnp.sin(ang)


def _rope_tables(base_pos, freq_ref, cos_ref, sin_ref):
    ang = base_pos.astype(F32) * freq_ref[...]
    cb = jnp.cos(ang)
    sb = jnp.sin(ang)
    oc = cos_ref[...]
    os_ = sin_ref[...]
    return cb * oc - sb * os_, sb * oc + cb * os_


def _cast_blocks(cast_refs):
    n = len(cast_refs) // 2
    for src, dst in zip(cast_refs[:n], cast_refs[n:]):
        dst[...] = src[...].astype(BF16)


def _cast_specs(weights):
    ins, outs, shapes = [], [], []
    step = lambda s: jnp.minimum(s, CAST_STEPS - 1)
    for layer, w in weights:
        _, rows, cols = w.shape
        blk = rows // CAST_STEPS
        ins.append(pl.BlockSpec((None, blk, cols), lambda s, layer=layer: (layer, step(s), 0)))
        outs.append(pl.BlockSpec((blk, cols), lambda s: (step(s), 0)))
        shapes.append(jax.ShapeDtypeStruct((rows, cols), BF16))
    return ins, outs, shapes


def _inproj0_kernel(*refs, tiles_per_seq, n_cast):
    (x_ref, meta_ref, gain_ref, w_ref, qg_ref, kg_ref, freq_ref) = refs[:7]
    cast_in = refs[7:7 + n_cast]
    xr_ref, gate_ref, q_ref, k_ref, v_ref = refs[7 + n_cast:12 + n_cast]
    cast_out = refs[12 + n_cast:12 + 2 * n_cast]
    ocos, osin = refs[12 + 2 * n_cast:]
    step = pl.program_id(0)
    tile_in_seq = step % tiles_per_seq

    @pl.when(step == 0)
    def _():
        _rope_offsets(freq_ref, ocos, osin)

    _cast_blocks(cast_in + cast_out)

    h = _padded_rows(x_ref, meta_ref, tile_in_seq == 0)
    rows = h.shape[0]
    xn = (h * _rms_scale(h) * gain_ref[...]).astype(BF16)

    q0 = 2 * LRU_WIDTH
    k0 = q0 + Q_WIDTH
    chunk_dot = lambda lo: jnp.dot(xn, w_ref[:, lo:lo + INPROJ_CHUNK], preferred_element_type=F32)
    y_q = [chunk_dot(q0 + lo) for lo in range(0, Q_WIDTH, INPROJ_CHUNK)]
    y_kv = chunk_dot(k0)
    v_ref[...] = y_kv[:, KV_WIDTH:].astype(BF16)

    cos, sin = _rope_tables(tile_in_seq * rows - META_PAD, freq_ref, ocos, osin)
    lane = lax.broadcasted_iota(jnp.int32, (1, LANES), 1) & (ATT_HEAD_DIM - 1)
    half = ROT_DIM // 2
    sin_lo = sin * jnp.where(lane < half, -1.0, 0.0)
    sin_hi = sin * jnp.where((lane >= half) & (lane < ROT_DIM), 1.0, 0.0)
    hi = lax.broadcasted_iota(jnp.int32, (2 * LANES, LANES), 0) & (LANES - 1)
    hj = lax.broadcasted_iota(jnp.int32, (2 * LANES, LANES), 1)
    head_mean = jnp.where(hi // ATT_HEAD_DIM == hj // ATT_HEAD_DIM,
                          1.0 / ATT_HEAD_DIM, 0.0).astype(BF16)

    def norm_rope(x, gain, scale):
        sq = x * x
        sq_hi = sq.astype(BF16)
        sq_lo = (sq - sq_hi.astype(F32)).astype(BF16)
        ms = jnp.dot(jnp.concatenate([sq_hi, sq_lo], axis=1), head_mean,
                     preferred_element_type=F32)
        xg = x * gain
        rot = (xg * cos + pltpu.roll(xg, LANES - half, 1) * sin_lo
               + pltpu.roll(xg, half, 1) * sin_hi)
        return rot * (lax.rsqrt(ms + RMS_EPS) * scale)

    def qk_step(j):
        if j < Q_WIDTH // LANES:
            lo = (j * LANES) % INPROJ_CHUNK
            q_ref[:, j * LANES:(j + 1) * LANES] = norm_rope(
                y_q[j * LANES // INPROJ_CHUNK][:, lo:lo + LANES], qg_ref[...],
                ATT_HEAD_DIM ** -0.5).astype(BF16)
        else:
            k_ref[...] = norm_rope(y_kv[:, :KV_WIDTH], kg_ref[...], 1.0).astype(BF16)

    def proj_step(c):
        out_ref = xr_ref if c < LRU_WIDTH // INPROJ_CHUNK else gate_ref
        dst = slice((c * INPROJ_CHUNK) % LRU_WIDTH, (c * INPROJ_CHUNK) % LRU_WIDTH + INPROJ_CHUNK)
        out_ref[:, dst] = chunk_dot(c * INPROJ_CHUNK)

    proj_steps = [functools.partial(proj_step, c) for c in range(2 * LRU_WIDTH // INPROJ_CHUNK)]
    qk_steps = [functools.partial(qk_step, j) for j in range(Q_WIDTH // LANES + 1)]
    for step in _interleave(proj_steps, qk_steps):
        step()


def _inproj0(x2d, meta_blk, gain, w_in, q_gain, k_gain, freq, cast_weights, batch,
             tiles_per_seq, seq):
    tm = ROW_TILE
    rows = batch * tiles_per_seq * tm
    row = lambda w: pl.BlockSpec((tm, w), lambda i: (i, 0))
    cast_in, cast_out, cast_shapes = _cast_specs(cast_weights)
    return pl.pallas_call(
        functools.partial(_inproj0_kernel, tiles_per_seq=tiles_per_seq, n_cast=len(cast_in)),
        grid=(rows // tm,),
        in_specs=[_token_window_spec(tm, tiles_per_seq, seq, lambda s: s),
                  _const_spec((BLOCK, D_MODEL)), _const_spec((1, D_MODEL)),
                  _const_spec((D_MODEL, AB_IN_WIDTH)), _const_spec((1, LANES)),
                  _const_spec((1, LANES)), _const_spec((1, LANES))] + cast_in,
        out_specs=[row(LRU_WIDTH), row(LRU_WIDTH), row(Q_WIDTH), row(KV_WIDTH), row(KV_WIDTH)]
        + cast_out,
        out_shape=[jax.ShapeDtypeStruct((rows, LRU_WIDTH), F32),
                   jax.ShapeDtypeStruct((rows, LRU_WIDTH), F32),
                   jax.ShapeDtypeStruct((rows, Q_WIDTH), BF16),
                   jax.ShapeDtypeStruct((rows, KV_WIDTH), BF16),
                   jax.ShapeDtypeStruct((rows, KV_WIDTH), BF16)] + cast_shapes,
        scratch_shapes=[pltpu.VMEM((tm, LANES), F32), pltpu.VMEM((tm, LANES), F32)],
        compiler_params=pltpu.CompilerParams(dimension_semantics=("arbitrary",)),
        name="l0_inproj",
    )(x2d, meta_blk, gain, w_in, q_gain, k_gain, freq, *[w for _, w in cast_weights])


def _attn_probs(n, sinks, q, kc, kp, km):
    j = lax.broadcasted_iota(jnp.int32, (BLOCK, BLOCK), 0)
    i = lax.broadcasted_iota(jnp.int32, (BLOCK, BLOCK), 1)
    causal = j <= i
    win_ok = n >= jnp.where(causal, 1, 2)
    meta_ok = (j >= META_PAD) & (n >= jnp.where(causal, 0, 1))
    contract_last = (((1,), (1,)), ((), ()))

    qs = jnp.concatenate([q[:, a * ATT_HEAD_DIM:(a + 1) * ATT_HEAD_DIM]
                          for a in range(ATT_GROUP)], axis=0)
    s_c = lax.dot_general(kc, qs, contract_last, preferred_element_type=F32)
    s_p = lax.dot_general(kp, qs, contract_last, preferred_element_type=F32)
    s_m = lax.dot_general(km, qs, contract_last, preferred_element_type=F32)
    p_c, p_p, p_m, inv_den = [], [], [], []
    for a in range(ATT_GROUP):
        head = slice(a * BLOCK, (a + 1) * BLOCK)
        sw = jnp.where(win_ok, jnp.where(causal, s_c[:, head], s_p[:, head]), NEG_INF)
        sm = jnp.where(meta_ok, s_m[:, head], NEG_INF)
        m = jnp.maximum(jnp.maximum(jnp.max(sw, axis=0, keepdims=True),
                                    jnp.max(sm, axis=0, keepdims=True)), sinks[a])
        pw = jnp.exp(sw - m)
        pm = jnp.exp(sm - m)
        den = (jnp.sum(pw, axis=0, keepdims=True) + jnp.sum(pm, axis=0, keepdims=True)
               + jnp.exp(sinks[a] - m))
        inv_den.append(1.0 / den)
        p_c.append(jnp.where(causal, pw, 0.0).astype(BF16))
        p_p.append(jnp.where(causal, 0.0, pw).astype(BF16))
        p_m.append(pm.astype(BF16))
    lanes = lambda parts: jnp.concatenate(parts, axis=1)
    return lanes(p_c), lanes(p_p), lanes(p_m), lanes(inv_den)


def _attn_out_t(probs, vc, vp, vm):
    p_c, p_p, p_m, inv_den = probs
    contract_rows = (((0,), (0,)), ((), ()))
    o_t = (lax.dot_general(vc, p_c, contract_rows, preferred_element_type=F32)
           + lax.dot_general(vp, p_p, contract_rows, preferred_element_type=F32)
           + lax.dot_general(vm, p_m, contract_rows, preferred_element_type=F32))
    return o_t * inv_den


def _attn_untranspose(o_t):
    o_t = jnp.concatenate(o_t, axis=0)
    return jnp.concatenate([o_t[:, a * BLOCK:(a + 1) * BLOCK].T for a in range(ATT_GROUP)],
                           axis=1).astype(BF16)


def _lru_gates(lo, xr_ref, rows, cw, cb, wg_ref, xbuf):
    tl = BLOCK
    cols = slice(lo, lo + GATE_TILE)
    x = xr_ref[rows, cols]
    xbuf[SUBLANES:SUBLANES + tl, cols] = x
    xc = x * cw[CONV_W - 1:CONV_W, cols] + cb[:, cols]
    for d in range(1, CONV_W):
        xc = xc + (xbuf[SUBLANES - d:SUBLANES - d + tl, cols]
                   * cw[CONV_W - 1 - d:CONV_W - d, cols])
    xbuf[0:SUBLANES, cols] = x[tl - SUBLANES:tl]
    xcb = xc.astype(BF16)
    ga_r = jnp.dot(xcb, wg_ref[cols, cols], preferred_element_type=F32)
    ga_i = jnp.dot(xcb, wg_ref[cols, LRU_WIDTH + lo:LRU_WIDTH + lo + GATE_TILE],
                   preferred_element_type=F32)
    return xc, ga_r, ga_i


def _lru_scan(n, xc, ga_r, ga_i, grp, gate_ref, rows, bg, softplus, hcar, y_ref):
    tl = BLOCK
    row = lax.broadcasted_iota(jnp.int32, (tl, 1), 0)
    t = n * tl + row
    r = _sigmoid(ga_r + bg[:, grp])
    gi = _sigmoid(ga_i + bg[:, LRU_WIDTH + grp.start:LRU_WIDTH + grp.stop])
    log_a = (-LRU_C * softplus[:, grp]) * r
    a = jnp.exp(log_a)
    mult2 = jnp.tanh(-log_a) * (a * a + 1.0)
    mult = mult2 * lax.rsqrt(jnp.maximum(mult2, SQRT_GUARD))
    mult = jnp.where(t == META_PAD, 1.0, mult)
    b = jnp.where(t < META_PAD, 0.0, mult * gi * xc)

    d = 1
    while d < SUBLANES:
        keep = row >= d
        b = jnp.where(keep, a * pltpu.roll(b, d, 0), 0.0) + b
        a = jnp.where(keep, a * pltpu.roll(a, d, 0), a)
        d *= 2
    while d < tl:
        b = jnp.concatenate([b[:d], a[d:] * b[:tl - d] + b[d:]], axis=0)
        a = jnp.concatenate([a[:d], a[d:] * a[:tl - d]], axis=0)
        d *= 2
    h = b + a * hcar[0:1, grp]
    hcar[:, grp] = jnp.broadcast_to(h[tl - 1:tl], (SUBLANES, LANES))
    y_ref[rows, grp] = (_gelu_tanh(gate_ref[rows, grp]) * h).astype(BF16)


def _ffn(h1, gain, wgu_ref, wd_ref):
    xn = (h1 * _rms_scale(h1) * gain).astype(BF16)
    acts = []
    for c in range(D_FF // FFN_CHUNK):
        lo = c * FFN_CHUNK
        g = jnp.dot(xn, wgu_ref[:, lo:lo + FFN_CHUNK], preferred_element_type=F32)
        u = jnp.dot(xn, wgu_ref[:, D_FF + lo:D_FF + lo + FFN_CHUNK], preferred_element_type=F32)
        acts.append((_silu(g) * u).astype(BF16))
    act = jnp.concatenate(acts, axis=1)
    return h1 + jnp.dot(act, wd_ref[...], preferred_element_type=F32)


def _l0_mix_ffn_kernel(*refs, tiles_per_seq, n_tiles, n_cast):
    k_ref, v_ref = refs[6:8]
    cast_in = refs[17:17 + n_cast]
    cast_out = refs[18 + n_cast:18 + 2 * n_cast]
    ybuf, xbuf, hcar, kprev, vprev, kmeta, vmeta = refs[18 + 2 * n_cast:]
    s = pl.program_id(0)
    tile_in_seq = jnp.minimum(s, n_tiles - 1) % tiles_per_seq
    _cast_blocks(cast_in + cast_out)

    @pl.when(tile_in_seq == 0)
    def _():
        xbuf[0:SUBLANES, :] = jnp.zeros((SUBLANES, LRU_WIDTH), F32)
        hcar[...] = jnp.zeros_like(hcar)
        kmeta[...] = k_ref[0:BLOCK, :]
        vmeta[...] = v_ref[0:BLOCK, :]
        kprev[...] = k_ref[0:BLOCK, :]
        vprev[...] = v_ref[0:BLOCK, :]

    @pl.when(s == 0)
    def _():
        ybuf[...] = jnp.zeros_like(ybuf)

    _l0_mix_ffn_body(refs, n_cast, tile_in_seq, s, tiles_per_seq, with_ffn=True, with_mix=True)


def _l0_mix_ffn_body(refs, n_cast, tile_in_seq, s, tiles_per_seq, *, with_ffn, with_mix):
    (sink_ref, x_ref, meta_ref, xr_ref, gate_ref, q_ref, k_ref, v_ref, cw_ref, cb_ref, wg_ref,
     bg_ref, lam_ref, wo_ref, gain_ref, wgu_ref, wd_ref) = refs[:17]
    o_ref = refs[17 + n_cast]
    ybuf, xbuf, hcar, kprev, vprev, kmeta, vmeta = refs[18 + 2 * n_cast:]
    blocks_per_tile = x_ref.shape[0] // BLOCK
    last = slice((blocks_per_tile - 1) * BLOCK, blocks_per_tile * BLOCK)

    if with_ffn:
        ffn_first = jnp.maximum(s - 1, 0) % tiles_per_seq == 0
        h1 = (_padded_rows(x_ref, meta_ref, ffn_first)
              + jnp.dot(ybuf[...], wo_ref[...], preferred_element_type=F32))
        xn = (h1 * _rms_scale(h1) * gain_ref[...]).astype(BF16)

    cw = cw_ref[...]
    cb = cb_ref[...]
    bg = bg_ref[...]
    z = -lam_ref[...]
    softplus = jnp.maximum(z, 0.0) + jnp.log1p(jnp.exp(-jnp.abs(z)))

    live = {}

    def block_rows(blk):
        return tile_in_seq * blocks_per_tile + blk, slice(blk * BLOCK, (blk + 1) * BLOCK)

    def lru_gates_step(blk, lo):
        _, rows = block_rows(blk)
        live["lru", blk, lo] = _lru_gates(lo, xr_ref, rows, cw, cb, wg_ref, xbuf)

    def lru_scan_step(blk, lo, sub):
        n, rows = block_rows(blk)
        xc, ga_r, ga_i = live["lru", blk, lo]
        part = slice(sub, sub + LANES)
        _lru_scan(n, xc[:, part], ga_r[:, part], ga_i[:, part],
                  slice(lo + sub, lo + sub + LANES), gate_ref, rows, bg, softplus, hcar, ybuf)
        if sub + LANES == GATE_TILE:
            del live["lru", blk, lo]

    def kv_blocks(ref, prev_ref, meta_ref, blk, g):
        _, rows = block_rows(blk)
        lanes = slice(g * ATT_HEAD_DIM, (g + 1) * ATT_HEAD_DIM)
        prev = prev_ref[:, lanes] if blk == 0 else ref[(blk - 1) * BLOCK:blk * BLOCK, lanes]
        return ref[rows, lanes], prev, meta_ref[:, lanes]

    def attn_probs_step(blk, g):
        n, rows = block_rows(blk)
        width = ATT_GROUP * ATT_HEAD_DIM
        sinks = [sink_ref[g * ATT_GROUP + a] for a in range(ATT_GROUP)]
        live["probs", blk, g] = _attn_probs(n, sinks, q_ref[rows, g * width:(g + 1) * width],
                                            *kv_blocks(k_ref, kprev, kmeta, blk, g))

    def attn_out_step(blk, g):
        _, rows = block_rows(blk)
        live["out", blk, g] = _attn_out_t(live.pop(("probs", blk, g)),
                                          *kv_blocks(v_ref, vprev, vmeta, blk, g))
        if g + 1 == ATT_KV_HEADS:
            ybuf[rows, LRU_WIDTH:LRU_WIDTH + Q_WIDTH] = _attn_untranspose(
                [live.pop(("out", blk, h)) for h in range(ATT_KV_HEADS)])

    acts = []

    def gate_step(c):
        lo = c * FFN_CHUNK
        live["g", c] = jnp.dot(xn, wgu_ref[:, lo:lo + FFN_CHUNK], preferred_element_type=F32)

    def up_step(c):
        lo = c * FFN_CHUNK
        u = jnp.dot(xn, wgu_ref[:, D_FF + lo:D_FF + lo + FFN_CHUNK], preferred_element_type=F32)
        acts.append((_silu(live.pop(("g", c))) * u).astype(BF16))

    def down_step(c):
        cols = slice(c * DOWN_CHUNK, (c + 1) * DOWN_CHUNK)
        if len(acts) > 1:
            acts[:] = [jnp.concatenate(acts, axis=1)]
        o_ref[:, cols] = h1[:, cols] + jnp.dot(acts[0], wd_ref[:, cols],
                                               preferred_element_type=F32)

    ffn_steps = []
    if with_ffn:
        for c in range(D_FF // FFN_CHUNK):
            ffn_steps += [functools.partial(gate_step, c), functools.partial(up_step, c)]
        ffn_steps += [functools.partial(down_step, c) for c in range(D_MODEL // DOWN_CHUNK)]
    mix_steps = []
    if with_mix:
        for blk in range(blocks_per_tile):
            for lo in range(0, LRU_WIDTH, GATE_TILE):
                mix_steps.append(functools.partial(lru_gates_step, blk, lo))
                mix_steps += [functools.partial(lru_scan_step, blk, lo, sub)
                              for sub in range(0, GATE_TILE, LANES)]
            mix_steps += [functools.partial(attn_probs_step, blk, g)
                          for g in range(ATT_KV_HEADS)]
            mix_steps += [functools.partial(attn_out_step, blk, g) for g in range(ATT_KV_HEADS)]
    for step in (_interleave(ffn_steps, mix_steps) if ffn_steps else mix_steps):
        step()
    if with_mix:
        kprev[...] = k_ref[last, :]
        vprev[...] = v_ref[last, :]


def _l0_mix_ffn(sinks, x2d, meta_blk, xr, gate, q, k, v, conv_w, conv_b, w_gates, b_gates, lam,
                w_out, gain, w_gu, w_down, cast_weights, tiles_per_seq, seq):
    rows = xr.shape[0]
    tm = ROW_TILE
    n_tiles = rows // tm
    mix = lambda w: pl.BlockSpec((tm, w), lambda s: (jnp.minimum(s, n_tiles - 1), 0))
    cast_in, cast_out, cast_shapes = _cast_specs(cast_weights)
    return pl.pallas_call(
        functools.partial(_l0_mix_ffn_kernel, tiles_per_seq=tiles_per_seq, n_tiles=n_tiles,
                          n_cast=len(cast_in)),
        grid=(n_tiles + 1,),
        in_specs=[pl.BlockSpec(memory_space=pltpu.SMEM),
                  _token_window_spec(tm, tiles_per_seq, seq, lambda s: jnp.maximum(s - 1, 0)),
                  _const_spec((BLOCK, D_MODEL)),
                  mix(LRU_WIDTH), mix(LRU_WIDTH), mix(Q_WIDTH), mix(KV_WIDTH), mix(KV_WIDTH),
                  _const_spec((CONV_W, LRU_WIDTH)), _const_spec((1, LRU_WIDTH)),
                  _const_spec((LRU_WIDTH, 2 * LRU_WIDTH)), _const_spec((1, 2 * LRU_WIDTH)),
                  _const_spec((1, LRU_WIDTH)), _const_spec(w_out.shape), _const_spec((1, D_MODEL)),
                  _const_spec(w_gu.shape), _const_spec(w_down.shape)] + cast_in,
        out_specs=[pl.BlockSpec((tm, D_MODEL), lambda s: (jnp.maximum(s - 1, 0), 0))] + cast_out,
        out_shape=[jax.ShapeDtypeStruct((rows, D_MODEL), F32)] + cast_shapes,
        scratch_shapes=[pltpu.VMEM((tm, LRU_WIDTH + Q_WIDTH), BF16),
                        pltpu.VMEM((SUBLANES + BLOCK, LRU_WIDTH), F32),
                        pltpu.VMEM((SUBLANES, LRU_WIDTH), F32),
                        pltpu.VMEM((BLOCK, KV_WIDTH), BF16), pltpu.VMEM((BLOCK, KV_WIDTH), BF16),
                        pltpu.VMEM((BLOCK, KV_WIDTH), BF16), pltpu.VMEM((BLOCK, KV_WIDTH), BF16)],
        compiler_params=pltpu.CompilerParams(dimension_semantics=("arbitrary",)),
        name="l0_mix_ffn",
    )(sinks, x2d, meta_blk, xr, gate, q, k, v, conv_w, conv_b, w_gates, b_gates, lam, w_out, gain,
      w_gu, w_down, *[w for _, w in cast_weights])


def _outproj_ffn_kernel(h_ref, y_ref, wo_ref, gain_ref, wgu_ref, wd_ref, o_ref):
    h1 = h_ref[...] + jnp.dot(y_ref[...], wo_ref[...], preferred_element_type=F32)
    o_ref[...] = _ffn(h1, gain_ref[...], wgu_ref, wd_ref)


def _final_outproj_ffn(h, y, w_out, gain, w_gu, w_down, batch, seq_rows, seq):
    tm = OUT_TILE
    tiles = seq // tm
    win = lambda w: pl.BlockSpec((pl.Element(tm), pl.Element(w)),
                                 lambda b, j: (pl.multiple_of(b * seq_rows + BLOCK + j * tm, BLOCK),
                                               0))
    return pl.pallas_call(
        _outproj_ffn_kernel,
        grid=(batch, tiles),
        in_specs=[win(D_MODEL), win(y.shape[1]), _const_spec(w_out.shape),
                  _const_spec((1, D_MODEL)), _const_spec(w_gu.shape), _const_spec(w_down.shape)],
        out_specs=pl.BlockSpec((tm, D_MODEL), lambda b, j: (b * tiles + j, 0)),
        out_shape=jax.ShapeDtypeStruct((batch * seq, D_MODEL), F32),
        compiler_params=pltpu.CompilerParams(dimension_semantics=("arbitrary", "arbitrary")),
        name="l1_outproj_ffn",
    )(h, y, w_out, gain, w_gu, w_down)


_Q0, _K0, _V0, _G0 = 0, D_MODEL, 2 * D_MODEL, 4 * D_MODEL


def _retention_decays(hd, length):
    ii = lax.broadcasted_iota(jnp.int32, (length, length), 0)
    jj = lax.broadcasted_iota(jnp.int32, (length, length), 1)
    diff = (ii - jj).astype(F32)
    idx = lax.broadcasted_iota(jnp.int32, (length, 1), 0).astype(F32)
    log_g = RET_LOG_G[hd]
    decay_intra = jnp.where(diff >= 0.0, jnp.exp(jnp.maximum(diff, 0.0) * log_g), 0.0)
    return (decay_intra, jnp.exp((idx + 1.0) * log_g), jnp.exp((length - 1.0 - idx) * log_g),
            math.exp(length * log_g))


def _l1_mix_kernel(*refs, tiles_per_seq, n_tiles, n_cast):
    h_ref, gain_ref, w_ref, freq_ref = refs[:4]
    cast_in = refs[4:4 + n_cast]
    y_ref = refs[4 + n_cast]
    cast_out = refs[5 + n_cast:5 + 2 * n_cast]
    qkvg, state, ocos, osin = refs[5 + 2 * n_cast:]
    _cast_blocks(cast_in + cast_out)
    s = pl.program_id(0)

    @pl.when(s == 0)
    def _():
        _rope_offsets(freq_ref, ocos, osin)
        qkvg[1] = jnp.zeros(qkvg.shape[1:], BF16)

    @pl.when(jnp.maximum(s - 1, 0) % tiles_per_seq == 0)
    def _():
        state[...] = jnp.zeros_like(state)

    body = functools.partial(_l1_mix_body, refs, n_cast, s, tiles_per_seq)
    pl.when(s < n_tiles)(functools.partial(body, with_proj=True, with_ret=True))
    pl.when(s == n_tiles)(functools.partial(body, with_proj=False, with_ret=True))


def _l1_mix_body(refs, n_cast, s, tiles_per_seq, *, with_proj, with_ret):
    h_ref, gain_ref, w_ref, freq_ref = refs[:4]
    y_ref = refs[4 + n_cast]
    qkvg, state, ocos, osin = refs[5 + 2 * n_cast:]
    proj_slot = s % 2
    ret_slot = 1 - proj_slot
    tm = h_ref.shape[0]
    half = RET_QK_DIM // 2
    if with_proj:
        h = h_ref[...]
        xn = (h * _rms_scale(h) * gain_ref[...]).astype(BF16)
        cos, sin = _rope_tables((s % tiles_per_seq) * tm - META_PAD, freq_ref, ocos, osin)

    def rope_step(col0, hd, scale):
        lo = col0 + hd * RET_QK_DIM
        y = jnp.dot(xn, w_ref[:, lo:lo + RET_QK_DIM], preferred_element_type=F32)
        x1 = y[:, :half]
        x2 = y[:, half:]
        qkvg[proj_slot, :, lo:lo + half] = ((x1 * cos - x2 * sin) * scale).astype(BF16)
        qkvg[proj_slot, :, lo + half:lo + RET_QK_DIM] = ((x2 * cos + x1 * sin)
                                                         * scale).astype(BF16)

    def value_step(col0, hd, act):
        lo = col0 + hd * RET_V_DIM
        y = jnp.dot(xn, w_ref[:, lo:lo + RET_V_DIM], preferred_element_type=F32)
        qkvg[proj_slot, :, lo:lo + RET_V_DIM] = act(y).astype(BF16)

    chunks = []
    while sum(length for _, length in chunks) < tm:
        row0 = sum(length for _, length in chunks)
        chunks.append((row0, min(RET_CHUNK, tm - row0)))
    n_chunks = len(chunks)

    def chunk_rows(c):
        return slice(chunks[c][0], chunks[c][0] + chunks[c][1])

    def ret_operand(c, hd, col0, width):
        return qkvg[ret_slot, chunk_rows(c), col0 + hd * width:col0 + (hd + 1) * width]

    heads = range(RET_HEADS)
    decay_tables = {length: [_retention_decays(hd, length) for hd in heads]
                    for length in sorted({length for _, length in chunks})}
    decays = lambda c, hd: decay_tables[chunks[c][1]][hd]
    live = {}

    def prep_step(c, hd):
        live["kdt", c, hd] = (ret_operand(c, hd, _K0, RET_QK_DIM).astype(F32)
                              * decays(c, hd)[2]).T.astype(BF16)

    def qk_step(c, hd):
        live["qk", c, hd] = lax.dot_general(
            ret_operand(c, hd, _Q0, RET_QK_DIM), ret_operand(c, hd, _K0, RET_QK_DIM),
            (((1,), (1,)), ((), ())), preferred_element_type=F32)

    def decay_step(c, hd):
        qkd = (live.pop(("qk", c, hd)) * decays(c, hd)[0]).astype(BF16)
        live["lhs", c, hd] = jnp.concatenate([qkd, live.pop(("kdt", c, hd))], axis=0)
        live["stb", c, hd] = state[hd].astype(BF16)

    def output_step(c, hd):
        length = chunks[c][1]
        both = jnp.dot(live.pop(("lhs", c, hd)), ret_operand(c, hd, _V0, RET_V_DIM),
                       preferred_element_type=F32)
        live["kv", c, hd] = both[length:]
        live["o", c, hd] = both[:length] + jnp.dot(
            ret_operand(c, hd, _Q0, RET_QK_DIM), live.pop(("stb", c, hd)),
            preferred_element_type=F32) * decays(c, hd)[1]

    def post_step(c, hd):
        o = live.pop(("o", c, hd))
        state[hd] = decays(c, hd)[3] * state[hd] + live.pop(("kv", c, hd))
        gate = ret_operand(c, hd, _G0, RET_V_DIM).astype(F32)
        y_ref[chunk_rows(c), hd * RET_V_DIM:(hd + 1) * RET_V_DIM] = (
            o * _rms_scale(o) * gate).astype(BF16)

    proj_steps = []
    for hd in heads if with_proj else ():
        proj_steps.append([functools.partial(rope_step, _Q0, hd, 1.0),
                           functools.partial(rope_step, _K0, hd, RET_QK_DIM ** -0.5)])
        proj_steps.append([functools.partial(value_step, _V0, hd, lambda y: y)])
        proj_steps.append([functools.partial(value_step, _G0, hd, _silu)])
    each_head = lambda step, c: [functools.partial(step, c, hd) for hd in heads]
    ret_steps = []
    for c in range(n_chunks) if with_ret else ():
        ret_steps.append(each_head(qk_step, c) + each_head(decay_step, c)
                         + (each_head(prep_step, c + 1) if c + 1 < n_chunks else []))
        ret_steps.append(each_head(output_step, c) + each_head(post_step, c))
    for step in each_head(prep_step, 0) if with_ret else ():
        step()
    for proj, ret in itertools.zip_longest(proj_steps, ret_steps, fillvalue=()):
        for step in (*proj, *ret):
            step()


def _l1_mix(h, gain, w_in, freq, cast_weights, tiles_per_seq):
    rows = h.shape[0]
    tm = ROW_TILE
    n_tiles = rows // tm
    half = RET_QK_DIM // 2
    cast_in, cast_out, cast_shapes = _cast_specs(cast_weights)
    return pl.pallas_call(
        functools.partial(_l1_mix_kernel, tiles_per_seq=tiles_per_seq, n_tiles=n_tiles,
                          n_cast=len(cast_in)),
        grid=(n_tiles + 1,),
        in_specs=[pl.BlockSpec((tm, D_MODEL), lambda s: (jnp.minimum(s, n_tiles - 1), 0)),
                  _const_spec((1, D_MODEL)), _const_spec(w_in.shape), _const_spec((1, half))]
        + cast_in,
        out_specs=[pl.BlockSpec((tm, 2 * D_MODEL), lambda s: (jnp.maximum(s - 1, 0), 0))]
        + cast_out,
        out_shape=[jax.ShapeDtypeStruct((rows, 2 * D_MODEL), BF16)] + cast_shapes,
        scratch_shapes=[pltpu.VMEM((2, tm, 6 * D_MODEL), BF16),
                        pltpu.VMEM((RET_HEADS, RET_QK_DIM, RET_V_DIM), F32),
                        pltpu.VMEM((tm, half), F32), pltpu.VMEM((tm, half), F32)],
        compiler_params=pltpu.CompilerParams(dimension_semantics=("arbitrary",)),
        name="l1_mix",
    )(h, gain, w_in, freq, *[w for _, w in cast_weights])


def _inv_freq(half, theta):
    return jnp.power(jnp.asarray(theta, F32), -jnp.arange(half, dtype=F32) / half)


def _block_diag(w):
    heads, wi, wo = w.shape
    eye = jnp.eye(heads, dtype=w.dtype)
    return (eye[:, None, :, None] * w[:, :, None, :]).reshape(heads * wi, heads * wo)


def kernel(x, meta_tokens, mix_norm_ab, ab_w_in, lru_conv_w, lru_conv_b, lru_w_a, lru_b_a, lru_w_i, lru_b_i, lru_lambda, q_norm, k_norm, attn_sinks, ab_w_out, mix_norm_ret, ret_w_in, ret_w_out, ffn_norm, ffn_w_gu, ffn_w_down):
    batch, seq, _ = x.shape
    seq_rows = META_PAD + N_META + seq
    tiles_per_seq = seq_rows // ROW_TILE

    x2d = x.reshape(batch * seq, D_MODEL)
    meta_blk = jnp.concatenate([jnp.zeros((META_PAD, D_MODEL), x.dtype),
                                meta_tokens.astype(x.dtype)], axis=0)

    f_att = _inv_freq(ROT_DIM // 2, ROPE_THETA)
    f_att = jnp.concatenate([f_att, f_att, jnp.zeros((ATT_HEAD_DIM - ROT_DIM,), F32)])
    f_att = jnp.tile(f_att, LANES // ATT_HEAD_DIM).reshape(1, LANES)
    f_ret = _inv_freq(RET_QK_DIM // 2, RET_THETA).reshape(1, RET_QK_DIM // 2)

    row_vec = lambda v: v.reshape(1, -1).astype(F32)
    two_heads = lambda v: jnp.tile(v.reshape(1, -1).astype(F32), (1, 2))

    xr, gate, q, k, v, w_gu0, w_down0 = _inproj0(
        x2d, meta_blk, row_vec(mix_norm_ab[0]), ab_w_in[0].astype(BF16), two_heads(q_norm[0]),
        two_heads(k_norm[0]), f_att, [(0, ffn_w_gu), (0, ffn_w_down)], batch, tiles_per_seq, seq)
    w_gates = jnp.concatenate([_block_diag(lru_w_a[0]), _block_diag(lru_w_i[0])], axis=1)
    b_gates = jnp.concatenate([lru_b_a[0].reshape(1, -1), lru_b_i[0].reshape(1, -1)], axis=1)
    w_att = ab_w_out[0][LRU_WIDTH:].reshape(ATT_KV_HEADS, ATT_GROUP, ATT_HEAD_DIM, D_MODEL)
    w_att = w_att.transpose(1, 0, 2, 3).reshape(Q_WIDTH, D_MODEL)
    w_out0 = jnp.concatenate([ab_w_out[0][:LRU_WIDTH], w_att], axis=0).astype(BF16)
    h, w_in1, w_out1 = _l0_mix_ffn(
        attn_sinks[0].astype(F32), x2d, meta_blk, xr, gate, q, k, v, lru_conv_w[0],
        row_vec(lru_conv_b[0]), w_gates.astype(BF16), b_gates.astype(F32), row_vec(lru_lambda[0]),
        w_out0, row_vec(ffn_norm[0]), w_gu0, w_down0, [(0, ret_w_in), (0, ret_w_out)],
        tiles_per_seq, seq)

    y_ret, w_gu1, w_down1 = _l1_mix(h, row_vec(mix_norm_ret[0]), w_in1, f_ret,
                                    [(1, ffn_w_gu), (1, ffn_w_down)], tiles_per_seq)
    out = _final_outproj_ffn(h, y_ret, w_out1, row_vec(ffn_norm[1]), w_gu1, w_down1,
                             batch, seq_rows, seq)
    return out.reshape(batch, seq, D_MODEL)
```

```python
import functools
import itertools
import math

import jax
import jax.numpy as jnp
from jax import lax
from jax.experimental import pallas as pl
from jax.experimental.pallas import tpu as pltpu

F32 = jnp.float32
BF16 = jnp.bfloat16

D_MODEL = 1024
N_META = 16
BLOCK = 128
META_PAD = BLOCK - N_META
RMS_EPS = 1e-6
NEG_INF = -1e30

LRU_WIDTH = 512
LRU_HEADS = 8
LRU_BLOCK_W = 64
CONV_W = 4
LRU_C = 8.0

ATT_HEADS = 8
ATT_KV_HEADS = 2
ATT_GROUP = ATT_HEADS // ATT_KV_HEADS
ATT_HEAD_DIM = 64
ROPE_THETA = 500000.0
ROT_DIM = 16
Q_WIDTH = 512
KV_WIDTH = 128
AB_IN_WIDTH = 2 * LRU_WIDTH + Q_WIDTH + 2 * KV_WIDTH

RET_HEADS = 4
RET_QK_DIM = 256
RET_V_DIM = 512
RET_THETA = 10000.0
RET_LOG_G = tuple(math.log1p(-(2.0 ** (-5.0 - h))) for h in range(RET_HEADS))

D_FF = 2816

LANES = 128
SUBLANES = 8
ROW_TILE = 640
OUT_TILE = 512
SQRT_GUARD = 1.1754944e-38
GATE_TILE = 256
INPROJ_CHUNK = 256
RET_CHUNK = 256
CAST_STEPS = 16
FFN_CHUNK = 256
DOWN_CHUNK = 512


def _rms_scale(x):
    return lax.rsqrt(jnp.mean(x * x, axis=-1, keepdims=True) + RMS_EPS)


def _sigmoid(x):
    return 0.5 * jnp.tanh(0.5 * x) + 0.5


def _silu(x):
    half = 0.5 * x
    return half + half * jnp.tanh(half)


def _gelu_tanh(x):
    half = 0.5 * x
    return half + half * jnp.tanh(0.7978845608028654 * (x + 0.044715 * (x * x * x)))


def _interleave(primary, secondary):
    out, done = [], 0
    for i, step in enumerate(primary):
        out.append(step)
        upto = ((i + 1) * len(secondary)) // len(primary)
        out.extend(secondary[done:upto])
        done = upto
    return out + list(secondary[done:])


def _const_spec(shape):
    zeros = (0,) * len(shape)
    return pl.BlockSpec(shape, lambda *_: zeros, pipeline_mode=pl.Buffered(1))


def _padded_rows(x_ref, meta_ref, first):
    tm = x_ref.shape[0]
    start = pl.multiple_of(jnp.where(first, 0, BLOCK), BLOCK)
    return jnp.concatenate([jnp.where(first, meta_ref[...], x_ref[0:BLOCK, :]),
                            x_ref[pl.ds(start, tm - BLOCK), :]], axis=0)


def _token_window_spec(tm, tiles_per_seq, seq, tile_of_step):
    def index(s):
        t = tile_of_step(s)
        start = jnp.maximum((t % tiles_per_seq) * tm - BLOCK, 0)
        return (pl.multiple_of((t // tiles_per_seq) * seq + start, BLOCK), 0)
    return pl.BlockSpec((pl.Element(tm), pl.Element(D_MODEL)), index)


def _rope_offsets(freq_ref, cos_ref, sin_ref):
    r = lax.broadcasted_iota(jnp.int32, cos_ref.shape, 0).astype(F32)
    ang = r * freq_ref[...]
    cos_ref[...] = jnp.cos(ang)
    sin_ref[...] = jnp.sin(ang)


def _rope_tables(base_pos, freq_ref, cos_ref, sin_ref):
    ang = base_pos.astype(F32) * freq_ref[...]
    cb = jnp.cos(ang)
    sb = jnp.sin(ang)
    oc = cos_ref[...]
    os_ = sin_ref[...]
    return cb * oc - sb * os_, sb * oc + cb * os_


def _cast_blocks(cast_refs):
    n = len(cast_refs) // 2
    for src, dst in zip(cast_refs[:n], cast_refs[n:]):
        dst[...] = src[...].astype(BF16)


def _cast_specs(weights):
    ins, outs, shapes = [], [], []
    step = lambda s: jnp.minimum(s, CAST_STEPS - 1)
    for layer, w in weights:
        _, rows, cols = w.shape
        blk = rows // CAST_STEPS
        ins.append(pl.BlockSpec((None, blk, cols), lambda s, layer=layer: (layer, step(s), 0)))
        outs.append(pl.BlockSpec((blk, cols), lambda s: (step(s), 0)))
        shapes.append(jax.ShapeDtypeStruct((rows, cols), BF16))
    return ins, outs, shapes


def _inproj0_kernel(*refs, tiles_per_seq, n_cast):
    (x_ref, meta_ref, gain_ref, w_ref, qg_ref, kg_ref, freq_ref) = refs[:7]
    cast_in = refs[7:7 + n_cast]
    xr_ref, gate_ref, q_ref, k_ref, v_ref = refs[7 + n_cast:12 + n_cast]
    cast_out = refs[12 + n_cast:12 + 2 * n_cast]
    ocos, osin = refs[12 + 2 * n_cast:]
    step = pl.program_id(0)
    tile_in_seq = step % tiles_per_seq

    @pl.when(step == 0)
    def _():
        _rope_offsets(freq_ref, ocos, osin)

    _cast_blocks(cast_in + cast_out)

    h = _padded_rows(x_ref, meta_ref, tile_in_seq == 0)
    rows = h.shape[0]
    xn = (h * _rms_scale(h) * gain_ref[...]).astype(BF16)

    q0 = 2 * LRU_WIDTH
    k0 = q0 + Q_WIDTH
    chunk_dot = lambda lo: jnp.dot(xn, w_ref[:, lo:lo + INPROJ_CHUNK], preferred_element_type=F32)
    y_q = [chunk_dot(q0 + lo) for lo in range(0, Q_WIDTH, INPROJ_CHUNK)]
    y_kv = chunk_dot(k0)
    v_ref[...] = y_kv[:, KV_WIDTH:].astype(BF16)

    cos, sin = _rope_tables(tile_in_seq * rows - META_PAD, freq_ref, ocos, osin)
    lane = lax.broadcasted_iota(jnp.int32, (1, LANES), 1) & (ATT_HEAD_DIM - 1)
    half = ROT_DIM // 2
    sin_lo = sin * jnp.where(lane < half, -1.0, 0.0)
    sin_hi = sin * jnp.where((lane >= half) & (lane < ROT_DIM), 1.0, 0.0)
    hi = lax.broadcasted_iota(jnp.int32, (2 * LANES, LANES), 0) & (LANES - 1)
    hj = lax.broadcasted_iota(jnp.int32, (2 * LANES, LANES), 1)
    head_mean = jnp.where(hi // ATT_HEAD_DIM == hj // ATT_HEAD_DIM,
                          1.0 / ATT_HEAD_DIM, 0.0).astype(BF16)

    def norm_rope(x, gain, scale):
        sq = x * x
        sq_hi = sq.astype(BF16)
        sq_lo = (sq - sq_hi.astype(F32)).astype(BF16)
        ms = jnp.dot(jnp.concatenate([sq_hi, sq_lo], axis=1), head_mean,
                     preferred_element_type=F32)
        xg = x * gain
        rot = (xg * cos + pltpu.roll(xg, LANES - half, 1) * sin_lo
               + pltpu.roll(xg, half, 1) * sin_hi)
        return rot * (lax.rsqrt(ms + RMS_EPS) * scale)

    def qk_step(j):
        if j < Q_WIDTH // LANES:
            lo = (j * LANES) % INPROJ_CHUNK
            q_ref[:, j * LANES:(j + 1) * LANES] = norm_rope(
                y_q[j * LANES // INPROJ_CHUNK][:, lo:lo + LANES], qg_ref[...],
                ATT_HEAD_DIM ** -0.5).astype(BF16)
        else:
            k_ref[...] = norm_rope(y_kv[:, :KV_WIDTH], kg_ref[...], 1.0).astype(BF16)

    def proj_step(c):
        out_ref = xr_ref if c < LRU_WIDTH // INPROJ_CHUNK else gate_ref
        dst = slice((c * INPROJ_CHUNK) % LRU_WIDTH, (c * INPROJ_CHUNK) % LRU_WIDTH + INPROJ_CHUNK)
        out_ref[:, dst] = chunk_dot(c * INPROJ_CHUNK)

    proj_steps = [functools.partial(proj_step, c) for c in range(2 * LRU_WIDTH // INPROJ_CHUNK)]
    qk_steps = [functools.partial(qk_step, j) for j in range(Q_WIDTH // LANES + 1)]
    for step in _interleave(proj_steps, qk_steps):
        step()


def _inproj0(x2d, meta_blk, gain, w_in, q_gain, k_gain, freq, cast_weights, batch,
             tiles_per_seq, seq):
    tm = ROW_TILE
    rows = batch * tiles_per_seq * tm
    row = lambda w: pl.BlockSpec((tm, w), lambda i: (i, 0))
    cast_in, cast_out, cast_shapes = _cast_specs(cast_weights)
    return pl.pallas_call(
        functools.partial(_inproj0_kernel, tiles_per_seq=tiles_per_seq, n_cast=len(cast_in)),
        grid=(rows // tm,),
        in_specs=[_token_window_spec(tm, tiles_per_seq, seq, lambda s: s),
                  _const_spec((BLOCK, D_MODEL)), _const_spec((1, D_MODEL)),
                  _const_spec((D_MODEL, AB_IN_WIDTH)), _const_spec((1, LANES)),
                  _const_spec((1, LANES)), _const_spec((1, LANES))] + cast_in,
        out_specs=[row(LRU_WIDTH), row(LRU_WIDTH), row(Q_WIDTH), row(KV_WIDTH), row(KV_WIDTH)]
        + cast_out,
        out_shape=[jax.ShapeDtypeStruct((rows, LRU_WIDTH), F32),
                   jax.ShapeDtypeStruct((rows, LRU_WIDTH), F32),
                   jax.ShapeDtypeStruct((rows, Q_WIDTH), BF16),
                   jax.ShapeDtypeStruct((rows, KV_WIDTH), BF16),
                   jax.ShapeDtypeStruct((rows, KV_WIDTH), BF16)] + cast_shapes,
        scratch_shapes=[pltpu.VMEM((tm, LANES), F32), pltpu.VMEM((tm, LANES), F32)],
        compiler_params=pltpu.CompilerParams(dimension_semantics=("arbitrary",)),
        name="l0_inproj",
    )(x2d, meta_blk, gain, w_in, q_gain, k_gain, freq, *[w for _, w in cast_weights])


def _attn_probs(n, sinks, q, kc, kp, km):
    j = lax.broadcasted_iota(jnp.int32, (BLOCK, BLOCK), 0)
    i = lax.broadcasted_iota(jnp.int32, (BLOCK, BLOCK), 1)
    causal = j <= i
    win_ok = n >= jnp.where(causal, 1, 2)
    jm = lax.broadcasted_iota(jnp.int32, (N_META, BLOCK), 0) + META_PAD
    im = lax.broadcasted_iota(jnp.int32, (N_META, BLOCK), 1)
    meta_ok = n >= jnp.where(jm <= im, 0, 1)
    contract_last = (((1,), (1,)), ((), ()))

    qs = jnp.concatenate([q[:, a * ATT_HEAD_DIM:(a + 1) * ATT_HEAD_DIM]
                          for a in range(ATT_GROUP)], axis=0)
    s_c = lax.dot_general(kc, qs, contract_last, preferred_element_type=F32)
    s_p = lax.dot_general(kp, qs, contract_last, preferred_element_type=F32)
    s_m = lax.dot_general(km[META_PAD:], qs, contract_last, preferred_element_type=F32)
    p_c, p_p, p_m, inv_den = [], [], [], []
    for a in range(ATT_GROUP):
        head = slice(a * BLOCK, (a + 1) * BLOCK)
        sw = jnp.where(win_ok, jnp.where(causal, s_c[:, head], s_p[:, head]), NEG_INF)
        sm = jnp.where(meta_ok, s_m[:, head], NEG_INF)
        m = jnp.maximum(jnp.maximum(jnp.max(sw, axis=0, keepdims=True),
                                    jnp.max(sm, axis=0, keepdims=True)), sinks[a])
        pw = jnp.exp(sw - m)
        pm = jnp.exp(sm - m)
        den = (jnp.sum(pw, axis=0, keepdims=True) + jnp.sum(pm, axis=0, keepdims=True)
               + jnp.exp(sinks[a] - m))
        inv_den.append(1.0 / den)
        p_c.append(jnp.where(causal, pw, 0.0).astype(BF16))
        p_p.append(jnp.where(causal, 0.0, pw).astype(BF16))
        p_m.append(pm.astype(BF16))
    lanes = lambda parts: jnp.concatenate(parts, axis=1)
    return lanes(p_c), lanes(p_p), lanes(p_m), lanes(inv_den)


def _attn_out_t(probs, vc, vp, vm):
    p_c, p_p, p_m, inv_den = probs
    contract_rows = (((0,), (0,)), ((), ()))
    o_t = (lax.dot_general(vc, p_c, contract_rows, preferred_element_type=F32)
           + lax.dot_general(vp, p_p, contract_rows, preferred_element_type=F32)
           + lax.dot_general(vm[META_PAD:], p_m, contract_rows, preferred_element_type=F32))
    return o_t * inv_den


def _attn_untranspose(o_t):
    o_t = jnp.concatenate(o_t, axis=0)
    return jnp.concatenate([o_t[:, a * BLOCK:(a + 1) * BLOCK].T for a in range(ATT_GROUP)],
                           axis=1).astype(BF16)


def _lru_gates(lo, xr_ref, rows, cw, cb, wg_ref, xbuf):
    tl = BLOCK
    cols = slice(lo, lo + GATE_TILE)
    x = xr_ref[rows, cols]
    xbuf[SUBLANES:SUBLANES + tl, cols] = x
    xc = x * cw[CONV_W - 1:CONV_W, cols] + cb[:, cols]
    for d in range(1, CONV_W):
        xc = xc + (xbuf[SUBLANES - d:SUBLANES - d + tl, cols]
                   * cw[CONV_W - 1 - d:CONV_W - d, cols])
    xbuf[0:SUBLANES, cols] = x[tl - SUBLANES:tl]
    xcb = xc.astype(BF16)
    ga_r = jnp.dot(xcb, wg_ref[cols, cols], preferred_element_type=F32)
    ga_i = jnp.dot(xcb, wg_ref[cols, LRU_WIDTH + lo:LRU_WIDTH + lo + GATE_TILE],
                   preferred_element_type=F32)
    return xc, ga_r, ga_i


def _lru_scan(n, xc, ga_r, ga_i, grp, gate_ref, rows, bg, softplus, hcar, y_ref):
    tl = BLOCK
    row = lax.broadcasted_iota(jnp.int32, (tl, 1), 0)
    t = n * tl + row
    r = _sigmoid(ga_r + bg[:, grp])
    gi = _sigmoid(ga_i + bg[:, LRU_WIDTH + grp.start:LRU_WIDTH + grp.stop])
    log_a = (-LRU_C * softplus[:, grp]) * r
    a = jnp.exp(log_a)
    mult2 = jnp.tanh(-log_a) * (a * a + 1.0)
    mult = mult2 * lax.rsqrt(jnp.maximum(mult2, SQRT_GUARD))
    mult = jnp.where(t == META_PAD, 1.0, mult)
    b = jnp.where(t < META_PAD, 0.0, mult * gi * xc)

    d = 1
    while d < SUBLANES:
        keep = row >= d
        b = jnp.where(keep, a * pltpu.roll(b, d, 0), 0.0) + b
        a = jnp.where(keep, a * pltpu.roll(a, d, 0), a)
        d *= 2
    while d < tl:
        b = jnp.concatenate([b[:d], a[d:] * b[:tl - d] + b[d:]], axis=0)
        a = jnp.concatenate([a[:d], a[d:] * a[:tl - d]], axis=0)
        d *= 2
    h = b + a * hcar[0:1, grp]
    hcar[:, grp] = jnp.broadcast_to(h[tl - 1:tl], (SUBLANES, LANES))
    y_ref[rows, grp] = (_gelu_tanh(gate_ref[rows, grp]) * h).astype(BF16)


def _ffn(h1, gain, wgu_ref, wd_ref):
    xn = (h1 * _rms_scale(h1) * gain).astype(BF16)
    acts = []
    for c in range(D_FF // FFN_CHUNK):
        lo = c * FFN_CHUNK
        g = jnp.dot(xn, wgu_ref[:, lo:lo + FFN_CHUNK], preferred_element_type=F32)
        u = jnp.dot(xn, wgu_ref[:, D_FF + lo:D_FF + lo + FFN_CHUNK], preferred_element_type=F32)
        acts.append((_silu(g) * u).astype(BF16))
    act = jnp.concatenate(acts, axis=1)
    return h1 + jnp.dot(act, wd_ref[...], preferred_element_type=F32)


def _l0_mix_ffn_kernel(*refs, tiles_per_seq, n_tiles, n_cast):
    k_ref, v_ref = refs[6:8]
    cast_in = refs[17:17 + n_cast]
    cast_out = refs[18 + n_cast:18 + 2 * n_cast]
    _, xbuf, hcar, kprev, vprev, kmeta, vmeta = refs[18 + 2 * n_cast:]
    s = pl.program_id(0)
    tile_in_seq = jnp.minimum(s, n_tiles - 1) % tiles_per_seq
    _cast_blocks(cast_in + cast_out)

    @pl.when(tile_in_seq == 0)
    def _():
        xbuf[0:SUBLANES, :] = jnp.zeros((SUBLANES, LRU_WIDTH), F32)
        hcar[...] = jnp.zeros_like(hcar)
        kmeta[...] = k_ref[0:BLOCK, :]
        vmeta[...] = v_ref[0:BLOCK, :]
        kprev[...] = k_ref[0:BLOCK, :]
        vprev[...] = v_ref[0:BLOCK, :]

    body = functools.partial(_l0_mix_ffn_body, refs, n_cast, tile_in_seq, s, tiles_per_seq)
    pl.when(s == 0)(functools.partial(body, with_ffn=False, with_mix=True))
    pl.when(s > 0)(functools.partial(body, with_ffn=True, with_mix=True))


def _l0_mix_ffn_body(refs, n_cast, tile_in_seq, s, tiles_per_seq, *, with_ffn, with_mix):
    (sink_ref, x_ref, meta_ref, xr_ref, gate_ref, q_ref, k_ref, v_ref, cw_ref, cb_ref, wg_ref,
     bg_ref, lam_ref, wo_ref, gain_ref, wgu_ref, wd_ref) = refs[:17]
    o_ref = refs[17 + n_cast]
    ybuf, xbuf, hcar, kprev, vprev, kmeta, vmeta = refs[18 + 2 * n_cast:]
    blocks_per_tile = x_ref.shape[0] // BLOCK
    last = slice((blocks_per_tile - 1) * BLOCK, blocks_per_tile * BLOCK)

    if with_ffn:
        ffn_first = (s - 1) % tiles_per_seq == 0
        h1 = (_padded_rows(x_ref, meta_ref, ffn_first)
              + jnp.dot(ybuf[...], wo_ref[...], preferred_element_type=F32))
        xn = (h1 * _rms_scale(h1) * gain_ref[...]).astype(BF16)

    cw = cw_ref[...]
    cb = cb_ref[...]
    bg = bg_ref[...]
    z = -lam_ref[...]
    softplus = jnp.maximum(z, 0.0) + jnp.log1p(jnp.exp(-jnp.abs(z)))

    live = {}

    def block_rows(blk):
        return tile_in_seq * blocks_per_tile + blk, slice(blk * BLOCK, (blk + 1) * BLOCK)

    def lru_gates_step(blk, lo):
        _, rows = block_rows(blk)
        live["lru", blk, lo] = _lru_gates(lo, xr_ref, rows, cw, cb, wg_ref, xbuf)

    def lru_scan_step(blk, lo, sub):
        n, rows = block_rows(blk)
        xc, ga_r, ga_i = live["lru", blk, lo]
        part = slice(sub, sub + LANES)
        _lru_scan(n, xc[:, part], ga_r[:, part], ga_i[:, part],
                  slice(lo + sub, lo + sub + LANES), gate_ref, rows, bg, softplus, hcar, ybuf)
        if sub + LANES == GATE_TILE:
            del live["lru", blk, lo]

    def kv_blocks(ref, prev_ref, meta_ref, blk, g):
        _, rows = block_rows(blk)
        lanes = slice(g * ATT_HEAD_DIM, (g + 1) * ATT_HEAD_DIM)
        prev = prev_ref[:, lanes] if blk == 0 else ref[(blk - 1) * BLOCK:blk * BLOCK, lanes]
        return ref[rows, lanes], prev, meta_ref[:, lanes]

    def attn_probs_step(blk, g):
        n, rows = block_rows(blk)
        width = ATT_GROUP * ATT_HEAD_DIM
        sinks = [sink_ref[g * ATT_GROUP + a] for a in range(ATT_GROUP)]
        live["probs", blk, g] = _attn_probs(n, sinks, q_ref[rows, g * width:(g + 1) * width],
                                            *kv_blocks(k_ref, kprev, kmeta, blk, g))

    def attn_out_step(blk, g):
        _, rows = block_rows(blk)
        live["out", blk, g] = _attn_out_t(live.pop(("probs", blk, g)),
                                          *kv_blocks(v_ref, vprev, vmeta, blk, g))
        if g + 1 == ATT_KV_HEADS:
            ybuf[rows, LRU_WIDTH:LRU_WIDTH + Q_WIDTH] = _attn_untranspose(
                [live.pop(("out", blk, h)) for h in range(ATT_KV_HEADS)])

    acts = []

    def gate_step(c):
        lo = c * FFN_CHUNK
        live["g", c] = jnp.dot(xn, wgu_ref[:, lo:lo + FFN_CHUNK], preferred_element_type=F32)

    def up_step(c):
        lo = c * FFN_CHUNK
        u = jnp.dot(xn, wgu_ref[:, D_FF + lo:D_FF + lo + FFN_CHUNK], preferred_element_type=F32)
        acts.append((_silu(live.pop(("g", c))) * u).astype(BF16))

    def down_step(c):
        cols = slice(c * DOWN_CHUNK, (c + 1) * DOWN_CHUNK)
        if len(acts) > 1:
            acts[:] = [jnp.concatenate(acts, axis=1)]
        o_ref[:, cols] = h1[:, cols] + jnp.dot(acts[0], wd_ref[:, cols],
                                               preferred_element_type=F32)

    ffn_steps = []
    if with_ffn:
        for c in range(D_FF // FFN_CHUNK):
            ffn_steps += [functools.partial(gate_step, c), functools.partial(up_step, c)]
        ffn_steps += [functools.partial(down_step, c) for c in range(D_MODEL // DOWN_CHUNK)]
    mix_steps = []
    if with_mix:
        for blk in range(blocks_per_tile):
            for lo in range(0, LRU_WIDTH, GATE_TILE):
                mix_steps.append(functools.partial(lru_gates_step, blk, lo))
                mix_steps += [functools.partial(lru_scan_step, blk, lo, sub)
                              for sub in range(0, GATE_TILE, LANES)]
            mix_steps += [functools.partial(attn_probs_step, blk, g)
                          for g in range(ATT_KV_HEADS)]
            mix_steps += [functools.partial(attn_out_step, blk, g) for g in range(ATT_KV_HEADS)]
    for step in (_interleave(ffn_steps, mix_steps) if ffn_steps else mix_steps):
        step()
    if with_mix:
        kprev[...] = k_ref[last, :]
        vprev[...] = v_ref[last, :]


def _l0_mix_ffn(sinks, x2d, meta_blk, xr, gate, q, k, v, conv_w, conv_b, w_gates, b_gates, lam,
                w_out, gain, w_gu, w_down, cast_weights, tiles_per_seq, seq):
    rows = xr.shape[0]
    tm = ROW_TILE
    n_tiles = rows // tm
    mix = lambda w: pl.BlockSpec((tm, w), lambda s: (jnp.minimum(s, n_tiles - 1), 0))
    cast_in, cast_out, cast_shapes = _cast_specs(cast_weights)
    return pl.pallas_call(
        functools.partial(_l0_mix_ffn_kernel, tiles_per_seq=tiles_per_seq, n_tiles=n_tiles,
                          n_cast=len(cast_in)),
        grid=(n_tiles + 1,),
        in_specs=[pl.BlockSpec(memory_space=pltpu.SMEM),
                  _token_window_spec(tm, tiles_per_seq, seq, lambda s: jnp.maximum(s - 1, 0)),
                  _const_spec((BLOCK, D_MODEL)),
                  mix(LRU_WIDTH), mix(LRU_WIDTH), mix(Q_WIDTH), mix(KV_WIDTH), mix(KV_WIDTH),
                  _const_spec((CONV_W, LRU_WIDTH)), _const_spec((1, LRU_WIDTH)),
                  _const_spec((LRU_WIDTH, 2 * LRU_WIDTH)), _const_spec((1, 2 * LRU_WIDTH)),
                  _const_spec((1, LRU_WIDTH)), _const_spec(w_out.shape), _const_spec((1, D_MODEL)),
                  _const_spec(w_gu.shape), _const_spec(w_down.shape)] + cast_in,
        out_specs=[pl.BlockSpec((tm, D_MODEL), lambda s: (jnp.maximum(s - 1, 0), 0))] + cast_out,
        out_shape=[jax.ShapeDtypeStruct((rows, D_MODEL), F32)] + cast_shapes,
        scratch_shapes=[pltpu.VMEM((tm, LRU_WIDTH + Q_WIDTH), BF16),
                        pltpu.VMEM((SUBLANES + BLOCK, LRU_WIDTH), F32),
                        pltpu.VMEM((SUBLANES, LRU_WIDTH), F32),
                        pltpu.VMEM((BLOCK, KV_WIDTH), BF16), pltpu.VMEM((BLOCK, KV_WIDTH), BF16),
                        pltpu.VMEM((BLOCK, KV_WIDTH), BF16), pltpu.VMEM((BLOCK, KV_WIDTH), BF16)],
        compiler_params=pltpu.CompilerParams(dimension_semantics=("arbitrary",)),
        name="l0_mix_ffn",
    )(sinks, x2d, meta_blk, xr, gate, q, k, v, conv_w, conv_b, w_gates, b_gates, lam, w_out, gain,
      w_gu, w_down, *[w for _, w in cast_weights])


def _outproj_ffn_kernel(h_ref, y_ref, wo_ref, gain_ref, wgu_ref, wd_ref, o_ref):
    h1 = h_ref[...] + jnp.dot(y_ref[...], wo_ref[...], preferred_element_type=F32)
    o_ref[...] = _ffn(h1, gain_ref[...], wgu_ref, wd_ref)


def _final_outproj_ffn(h, y, w_out, gain, w_gu, w_down, batch, seq_rows, seq):
    tm = OUT_TILE
    tiles = seq // tm
    win = lambda w: pl.BlockSpec((pl.Element(tm), pl.Element(w)),
                                 lambda b, j: (pl.multiple_of(b * seq_rows + BLOCK + j * tm, BLOCK),
                                               0))
    return pl.pallas_call(
        _outproj_ffn_kernel,
        grid=(batch, tiles),
        in_specs=[win(D_MODEL), win(y.shape[1]), _const_spec(w_out.shape),
                  _const_spec((1, D_MODEL)), _const_spec(w_gu.shape), _const_spec(w_down.shape)],
        out_specs=pl.BlockSpec((tm, D_MODEL), lambda b, j: (b * tiles + j, 0)),
        out_shape=jax.ShapeDtypeStruct((batch * seq, D_MODEL), F32),
        compiler_params=pltpu.CompilerParams(dimension_semantics=("arbitrary", "arbitrary")),
        name="l1_outproj_ffn",
    )(h, y, w_out, gain, w_gu, w_down)


_Q0, _K0, _V0, _G0 = 0, D_MODEL, 2 * D_MODEL, 4 * D_MODEL


def _retention_intra_decay(hd, length):
    ii = lax.broadcasted_iota(jnp.int32, (length, length), 0)
    jj = lax.broadcasted_iota(jnp.int32, (length, length), 1)
    diff = (ii - jj).astype(F32)
    return jnp.where(diff >= 0.0, jnp.exp(jnp.maximum(diff, 0.0) * RET_LOG_G[hd]), 0.0)


def _retention_decays(hd, length):
    idx = lax.broadcasted_iota(jnp.int32, (length, 1), 0).astype(F32)
    log_g = RET_LOG_G[hd]
    return (jnp.exp((idx + 1.0) * log_g), jnp.exp((length - 1.0 - idx) * log_g),
            math.exp(length * log_g))


def _l1_mix_kernel(*refs, tiles_per_seq, n_tiles, n_cast):
    h_ref, gain_ref, w_ref, freq_ref = refs[:4]
    cast_in = refs[4:4 + n_cast]
    y_ref = refs[4 + n_cast]
    cast_out = refs[5 + n_cast:5 + 2 * n_cast]
    qkvg, state, ocos, osin, intra = refs[5 + 2 * n_cast:]
    _cast_blocks(cast_in + cast_out)
    s = pl.program_id(0)

    @pl.when(s == 0)
    def _():
        _rope_offsets(freq_ref, ocos, osin)
        for hd in range(RET_HEADS):
            intra[hd] = _retention_intra_decay(hd, RET_CHUNK)
        qkvg[1] = jnp.zeros(qkvg.shape[1:], BF16)

    @pl.when(jnp.maximum(s - 1, 0) % tiles_per_seq == 0)
    def _():
        state[...] = jnp.zeros_like(state)

    body = functools.partial(_l1_mix_body, refs, n_cast, s, tiles_per_seq)
    pl.when(s < n_tiles)(functools.partial(body, with_proj=True, with_ret=True))
    pl.when(s == n_tiles)(functools.partial(body, with_proj=False, with_ret=True))


def _l1_mix_body(refs, n_cast, s, tiles_per_seq, *, with_proj, with_ret):
    h_ref, gain_ref, w_ref, freq_ref = refs[:4]
    y_ref = refs[4 + n_cast]
    qkvg, state, ocos, osin, intra = refs[5 + 2 * n_cast:]
    proj_slot = s % 2
    ret_slot = 1 - proj_slot
    tm = h_ref.shape[0]
    half = RET_QK_DIM // 2
    if with_proj:
        h = h_ref[...]
        xn = (h * _rms_scale(h) * gain_ref[...]).astype(BF16)
        cos, sin = _rope_tables((s % tiles_per_seq) * tm - META_PAD, freq_ref, ocos, osin)

    def rope_step(col0, hd, scale):
        lo = col0 + hd * RET_QK_DIM
        y = jnp.dot(xn, w_ref[:, lo:lo + RET_QK_DIM], preferred_element_type=F32)
        x1 = y[:, :half]
        x2 = y[:, half:]
        qkvg[proj_slot, :, lo:lo + half] = ((x1 * cos - x2 * sin) * scale).astype(BF16)
        qkvg[proj_slot, :, lo + half:lo + RET_QK_DIM] = ((x2 * cos + x1 * sin)
                                                         * scale).astype(BF16)

    def value_step(col0, hd, act):
        lo = col0 + hd * RET_V_DIM
        y = jnp.dot(xn, w_ref[:, lo:lo + RET_V_DIM], preferred_element_type=F32)
        qkvg[proj_slot, :, lo:lo + RET_V_DIM] = act(y).astype(BF16)

    chunks = []
    while sum(length for _, length in chunks) < tm:
        row0 = sum(length for _, length in chunks)
        chunks.append((row0, min(RET_CHUNK, tm - row0)))
    n_chunks = len(chunks)

    def chunk_rows(c):
        return slice(chunks[c][0], chunks[c][0] + chunks[c][1])

    def ret_operand(c, hd, col0, width):
        return qkvg[ret_slot, chunk_rows(c), col0 + hd * width:col0 + (hd + 1) * width]

    heads = range(RET_HEADS)
    decay_tables = {length: [_retention_decays(hd, length) for hd in heads]
                    for length in sorted({length for _, length in chunks})}
    decays = lambda c, hd: decay_tables[chunks[c][1]][hd]
    decay_intra = lambda c, hd: intra[hd, :chunks[c][1], :chunks[c][1]]
    live = {}

    def prep_step(c, hd):
        live["kdt", c, hd] = (ret_operand(c, hd, _K0, RET_QK_DIM).astype(F32)
                              * decays(c, hd)[1]).T.astype(BF16)

    def qk_step(c, hd):
        live["qk", c, hd] = lax.dot_general(
            ret_operand(c, hd, _Q0, RET_QK_DIM), ret_operand(c, hd, _K0, RET_QK_DIM),
            (((1,), (1,)), ((), ())), preferred_element_type=F32)

    def decay_step(c, hd):
        qkd = (live.pop(("qk", c, hd)) * decay_intra(c, hd)).astype(BF16)
        live["lhs", c, hd] = jnp.concatenate([qkd, live.pop(("kdt", c, hd))], axis=0)
        live["stb", c, hd] = state[hd].astype(BF16)

    def output_step(c, hd):
        length = chunks[c][1]
        both = jnp.dot(live.pop(("lhs", c, hd)), ret_operand(c, hd, _V0, RET_V_DIM),
                       preferred_element_type=F32)
        live["kv", c, hd] = both[length:]
        live["o", c, hd] = both[:length] + jnp.dot(
            ret_operand(c, hd, _Q0, RET_QK_DIM), live.pop(("stb", c, hd)),
            preferred_element_type=F32) * decays(c, hd)[0]

    def post_step(c, hd):
        o = live.pop(("o", c, hd))
        state[hd] = decays(c, hd)[2] * state[hd] + live.pop(("kv", c, hd))
        gate = ret_operand(c, hd, _G0, RET_V_DIM).astype(F32)
        y_ref[chunk_rows(c), hd * RET_V_DIM:(hd + 1) * RET_V_DIM] = (
            o * _rms_scale(o) * gate).astype(BF16)

    proj_steps = []
    for hd in heads if with_proj else ():
        proj_steps.append([functools.partial(rope_step, _Q0, hd, 1.0),
                           functools.partial(rope_step, _K0, hd, RET_QK_DIM ** -0.5)])
        proj_steps.append([functools.partial(value_step, _V0, hd, lambda y: y)])
        proj_steps.append([functools.partial(value_step, _G0, hd, _silu)])
    each_head = lambda step, c: [functools.partial(step, c, hd) for hd in heads]
    ret_steps = []
    for c in range(n_chunks) if with_ret else ():
        ret_steps.append(each_head(qk_step, c) + each_head(decay_step, c)
                         + (each_head(prep_step, c + 1) if c + 1 < n_chunks else []))
        ret_steps.append(each_head(output_step, c) + each_head(post_step, c))
    for step in each_head(prep_step, 0) if with_ret else ():
        step()
    for proj, ret in itertools.zip_longest(proj_steps, ret_steps, fillvalue=()):
        for step in (*proj, *ret):
            step()


def _l1_mix(h, gain, w_in, freq, cast_weights, tiles_per_seq):
    rows = h.shape[0]
    tm = ROW_TILE
    n_tiles = rows // tm
    half = RET_QK_DIM // 2
    cast_in, cast_out, cast_shapes = _cast_specs(cast_weights)
    return pl.pallas_call(
        functools.partial(_l1_mix_kernel, tiles_per_seq=tiles_per_seq, n_tiles=n_tiles,
                          n_cast=len(cast_in)),
        grid=(n_tiles + 1,),
        in_specs=[pl.BlockSpec((tm, D_MODEL), lambda s: (jnp.minimum(s, n_tiles - 1), 0)),
                  _const_spec((1, D_MODEL)), _const_spec(w_in.shape), _const_spec((1, half))]
        + cast_in,
        out_specs=[pl.BlockSpec((tm, 2 * D_MODEL), lambda s: (jnp.maximum(s - 1, 0), 0))]
        + cast_out,
        out_shape=[jax.ShapeDtypeStruct((rows, 2 * D_MODEL), BF16)] + cast_shapes,
        scratch_shapes=[pltpu.VMEM((2, tm, 6 * D_MODEL), BF16),
                        pltpu.VMEM((RET_HEADS, RET_QK_DIM, RET_V_DIM), F32),
                        pltpu.VMEM((tm, half), F32), pltpu.VMEM((tm, half), F32),
                        pltpu.VMEM((RET_HEADS, RET_CHUNK, RET_CHUNK), F32)],
        compiler_params=pltpu.CompilerParams(dimension_semantics=("arbitrary",)),
        name="l1_mix",
    )(h, gain, w_in, freq, *[w for _, w in cast_weights])


def _inv_freq(half, theta):
    return jnp.power(jnp.asarray(theta, F32), -jnp.arange(half, dtype=F32) / half)


def _block_diag(w):
    heads, wi, wo = w.shape
    eye = jnp.eye(heads, dtype=w.dtype)
    return (eye[:, None, :, None] * w[:, :, None, :]).reshape(heads * wi, heads * wo)


def kernel(x, meta_tokens, mix_norm_ab, ab_w_in, lru_conv_w, lru_conv_b, lru_w_a, lru_b_a, lru_w_i, lru_b_i, lru_lambda, q_norm, k_norm, attn_sinks, ab_w_out, mix_norm_ret, ret_w_in, ret_w_out, ffn_norm, ffn_w_gu, ffn_w_down):
    batch, seq, _ = x.shape
    seq_rows = META_PAD + N_META + seq
    tiles_per_seq = seq_rows // ROW_TILE

    x2d = x.reshape(batch * seq, D_MODEL)
    meta_blk = jnp.concatenate([jnp.zeros((META_PAD, D_MODEL), x.dtype),
                                meta_tokens.astype(x.dtype)], axis=0)

    f_att = _inv_freq(ROT_DIM // 2, ROPE_THETA)
    f_att = jnp.concatenate([f_att, f_att, jnp.zeros((ATT_HEAD_DIM - ROT_DIM,), F32)])
    f_att = jnp.tile(f_att, LANES // ATT_HEAD_DIM).reshape(1, LANES)
    f_ret = _inv_freq(RET_QK_DIM // 2, RET_THETA).reshape(1, RET_QK_DIM // 2)

    row_vec = lambda v: v.reshape(1, -1).astype(F32)
    two_heads = lambda v: jnp.tile(v.reshape(1, -1).astype(F32), (1, 2))

    xr, gate, q, k, v, w_gu0, w_down0 = _inproj0(
        x2d, meta_blk, row_vec(mix_norm_ab[0]), ab_w_in[0].astype(BF16), two_heads(q_norm[0]),
        two_heads(k_norm[0]), f_att, [(0, ffn_w_gu), (0, ffn_w_down)], batch, tiles_per_seq, seq)
    w_gates = jnp.concatenate([_block_diag(lru_w_a[0]), _block_diag(lru_w_i[0])], axis=1)
    b_gates = jnp.concatenate([lru_b_a[0].reshape(1, -1), lru_b_i[0].reshape(1, -1)], axis=1)
    w_att = ab_w_out[0][LRU_WIDTH:].reshape(ATT_KV_HEADS, ATT_GROUP, ATT_HEAD_DIM, D_MODEL)
    w_att = w_att.transpose(1, 0, 2, 3).reshape(Q_WIDTH, D_MODEL)
    w_out0 = jnp.concatenate([ab_w_out[0][:LRU_WIDTH], w_att], axis=0).astype(BF16)
    h, w_in1, w_out1 = _l0_mix_ffn(
        attn_sinks[0].astype(F32), x2d, meta_blk, xr, gate, q, k, v, lru_conv_w[0],
        row_vec(lru_conv_b[0]), w_gates.astype(BF16), b_gates.astype(F32), row_vec(lru_lambda[0]),
        w_out0, row_vec(ffn_norm[0]), w_gu0, w_down0, [(0, ret_w_in), (0, ret_w_out)],
        tiles_per_seq, seq)

    y_ret, w_gu1, w_down1 = _l1_mix(h, row_vec(mix_norm_ret[0]), w_in1, f_ret,
                                    [(1, ffn_w_gu), (1, ffn_w_down)], tiles_per_seq)
    out = _final_outproj_ffn(h, y_ret, w_out1, row_vec(ffn_norm[1]), w_gu1, w_down1,
                             batch, seq_rows, seq)
    return out.reshape(batch, seq, D_MODEL)
```

```python
import functools
import itertools
import math

import jax
import jax.numpy as jnp
from jax import lax
from jax.experimental import pallas as pl
from jax.experimental.pallas import tpu as pltpu

F32 = jnp.float32
BF16 = jnp.bfloat16

D_MODEL = 1024
N_META = 16
BLOCK = 128
META_PAD = BLOCK - N_META
RMS_EPS = 1e-6
NEG_INF = -1e30

LRU_WIDTH = 512
LRU_HEADS = 8
LRU_BLOCK_W = 64
CONV_W = 4
LRU_C = 8.0

ATT_HEADS = 8
ATT_KV_HEADS = 2
ATT_GROUP = ATT_HEADS // ATT_KV_HEADS
ATT_HEAD_DIM = 64
ROPE_THETA = 500000.0
ROT_DIM = 16
Q_WIDTH = 512
KV_WIDTH = 128
AB_IN_WIDTH = 2 * LRU_WIDTH + Q_WIDTH + 2 * KV_WIDTH

RET_HEADS = 4
RET_QK_DIM = 256
RET_V_DIM = 512
RET_THETA = 10000.0
RET_LOG_G = tuple(math.log1p(-(2.0 ** (-5.0 - h))) for h in range(RET_HEADS))

D_FF = 2816

LANES = 128
SUBLANES = 8
ROW_TILE = 640
OUT_TILE = 512
SQRT_GUARD = 1.1754944e-38
GATE_TILE = 256
INPROJ_CHUNK = 256
RET_CHUNK = 256
CAST_STEPS = 16
FFN_CHUNK = 256
DOWN_CHUNK = 512


def _rms_scale(x):
    return lax.rsqrt(jnp.mean(x * x, axis=-1, keepdims=True) + RMS_EPS)


def _sigmoid(x):
    return 0.5 * jnp.tanh(0.5 * x) + 0.5


def _silu(x):
    half = 0.5 * x
    return half + half * jnp.tanh(half)


def _gelu_tanh(x):
    half = 0.5 * x
    return half + half * jnp.tanh(0.7978845608028654 * (x + 0.044715 * (x * x * x)))


def _interleave(primary, secondary):
    out, done = [], 0
    for i, step in enumerate(primary):
        out.append(step)
        upto = ((i + 1) * len(secondary)) // len(primary)
        out.extend(secondary[done:upto])
        done = upto
    return out + list(secondary[done:])


def _const_spec(shape):
    zeros = (0,) * len(shape)
    return pl.BlockSpec(shape, lambda *_: zeros, pipeline_mode=pl.Buffered(1))


def _padded_rows(x_ref, meta_ref, first):
    tm = x_ref.shape[0]
    start = pl.multiple_of(jnp.where(first, 0, BLOCK), BLOCK)
    return jnp.concatenate([jnp.where(first, meta_ref[...], x_ref[0:BLOCK, :]),
                            x_ref[pl.ds(start, tm - BLOCK), :]], axis=0)


def _token_window_spec(tm, tiles_per_seq, seq, tile_of_step):
    def index(s):
        t = tile_of_step(s)
        start = jnp.maximum((t % tiles_per_seq) * tm - BLOCK, 0)
        return (pl.multiple_of((t // tiles_per_seq) * seq + start, BLOCK), 0)
    return pl.BlockSpec((pl.Element(tm), pl.Element(D_MODEL)), index)


def _rope_offsets(freq_ref, cos_ref, sin_ref):
    r = lax.broadcasted_iota(jnp.int32, cos_ref.shape, 0).astype(F32)
    ang = r * freq_ref[...]
    cos_ref[...] = jnp.cos(ang)
    sin_ref[...] = jnp.sin(ang)


def _rope_tables(base_pos, freq_ref, cos_ref, sin_ref):
    ang = base_pos.astype(F32) * freq_ref[...]
    cb = jnp.cos(ang)
    sb = jnp.sin(ang)
    oc = cos_ref[...]
    os_ = sin_ref[...]
    return cb * oc - sb * os_, sb * oc + cb * os_


def _cast_blocks(cast_refs):
    n = len(cast_refs) // 2
    for src, dst in zip(cast_refs[:n], cast_refs[n:]):
        dst[...] = src[...].astype(BF16)


def _cast_specs(weights):
    ins, outs, shapes = [], [], []
    step = lambda s: jnp.minimum(s, CAST_STEPS - 1)
    for layer, w in weights:
        _, rows, cols = w.shape
        blk = rows // CAST_STEPS
        ins.append(pl.BlockSpec((None, blk, cols), lambda s, layer=layer: (layer, step(s), 0)))
        outs.append(pl.BlockSpec((blk, cols), lambda s: (step(s), 0)))
        shapes.append(jax.ShapeDtypeStruct((rows, cols), BF16))
    return ins, outs, shapes


def _inproj0_kernel(*refs, tiles_per_seq, n_cast):
    (x_ref, meta_ref, gain_ref, w_ref, qg_ref, kg_ref, freq_ref) = refs[:7]
    cast_in = refs[7:7 + n_cast]
    xr_ref, gate_ref, q_ref, k_ref, v_ref = refs[7 + n_cast:12 + n_cast]
    cast_out = refs[12 + n_cast:12 + 2 * n_cast]
    ocos, osin = refs[12 + 2 * n_cast:]
    step = pl.program_id(0)
    tile_in_seq = step % tiles_per_seq

    @pl.when(step == 0)
    def _():
        _rope_offsets(freq_ref, ocos, osin)

    _cast_blocks(cast_in + cast_out)

    h = _padded_rows(x_ref, meta_ref, tile_in_seq == 0)
    rows = h.shape[0]
    xn = (h * _rms_scale(h) * gain_ref[...]).astype(BF16)

    q0 = 2 * LRU_WIDTH
    k0 = q0 + Q_WIDTH
    chunk_dot = lambda lo: jnp.dot(xn, w_ref[:, lo:lo + INPROJ_CHUNK], preferred_element_type=F32)
    y_q = [chunk_dot(q0 + lo) for lo in range(0, Q_WIDTH, INPROJ_CHUNK)]
    y_kv = chunk_dot(k0)
    v_ref[...] = y_kv[:, KV_WIDTH:].astype(BF16)

    cos, sin = _rope_tables(tile_in_seq * rows - META_PAD, freq_ref, ocos, osin)
    lane = lax.broadcasted_iota(jnp.int32, (1, LANES), 1) & (ATT_HEAD_DIM - 1)
    half = ROT_DIM // 2
    sin_lo = sin * jnp.where(lane < half, -1.0, 0.0)
    sin_hi = sin * jnp.where((lane >= half) & (lane < ROT_DIM), 1.0, 0.0)
    hi = lax.broadcasted_iota(jnp.int32, (2 * LANES, LANES), 0) & (LANES - 1)
    hj = lax.broadcasted_iota(jnp.int32, (2 * LANES, LANES), 1)
    head_mean = jnp.where(hi // ATT_HEAD_DIM == hj // ATT_HEAD_DIM,
                          1.0 / ATT_HEAD_DIM, 0.0).astype(BF16)

    def norm_rope(x, gain, scale):
        sq = x * x
        sq_hi = sq.astype(BF16)
        sq_lo = (sq - sq_hi.astype(F32)).astype(BF16)
        ms = jnp.dot(jnp.concatenate([sq_hi, sq_lo], axis=1), head_mean,
                     preferred_element_type=F32)
        xg = x * gain
        rot = (xg * cos + pltpu.roll(xg, LANES - half, 1) * sin_lo
               + pltpu.roll(xg, half, 1) * sin_hi)
        return rot * (lax.rsqrt(ms + RMS_EPS) * scale)

    def qk_step(j):
        if j < Q_WIDTH // LANES:
            lo = (j * LANES) % INPROJ_CHUNK
            q_ref[:, j * LANES:(j + 1) * LANES] = norm_rope(
                y_q[j * LANES // INPROJ_CHUNK][:, lo:lo + LANES], qg_ref[...],
                ATT_HEAD_DIM ** -0.5).astype(BF16)
        else:
            k_ref[...] = norm_rope(y_kv[:, :KV_WIDTH], kg_ref[...], 1.0).astype(BF16)

    def proj_step(c):
        out_ref = xr_ref if c < LRU_WIDTH // INPROJ_CHUNK else gate_ref
        dst = slice((c * INPROJ_CHUNK) % LRU_WIDTH, (c * INPROJ_CHUNK) % LRU_WIDTH + INPROJ_CHUNK)
        out_ref[:, dst] = chunk_dot(c * INPROJ_CHUNK)

    proj_steps = [functools.partial(proj_step, c) for c in range(2 * LRU_WIDTH // INPROJ_CHUNK)]
    qk_steps = [functools.partial(qk_step, j) for j in range(Q_WIDTH // LANES + 1)]
    for step in _interleave(proj_steps, qk_steps):
        step()


def _inproj0(x2d, meta_blk, gain, w_in, q_gain, k_gain, freq, cast_weights, batch,
             tiles_per_seq, seq):
    tm = ROW_TILE
    rows = batch * tiles_per_seq * tm
    row = lambda w: pl.BlockSpec((tm, w), lambda i: (i, 0))
    cast_in, cast_out, cast_shapes = _cast_specs(cast_weights)
    return pl.pallas_call(
        functools.partial(_inproj0_kernel, tiles_per_seq=tiles_per_seq, n_cast=len(cast_in)),
        grid=(rows // tm,),
        in_specs=[_token_window_spec(tm, tiles_per_seq, seq, lambda s: s),
                  _const_spec((BLOCK, D_MODEL)), _const_spec((1, D_MODEL)),
                  _const_spec((D_MODEL, AB_IN_WIDTH)), _const_spec((1, LANES)),
                  _const_spec((1, LANES)), _const_spec((1, LANES))] + cast_in,
        out_specs=[row(LRU_WIDTH), row(LRU_WIDTH), row(Q_WIDTH), row(KV_WIDTH), row(KV_WIDTH)]
        + cast_out,
        out_shape=[jax.ShapeDtypeStruct((rows, LRU_WIDTH), F32),
                   jax.ShapeDtypeStruct((rows, LRU_WIDTH), F32),
                   jax.ShapeDtypeStruct((rows, Q_WIDTH), BF16),
                   jax.ShapeDtypeStruct((rows, KV_WIDTH), BF16),
                   jax.ShapeDtypeStruct((rows, KV_WIDTH), BF16)] + cast_shapes,
        scratch_shapes=[pltpu.VMEM((tm, LANES), F32), pltpu.VMEM((tm, LANES), F32)],
        compiler_params=pltpu.CompilerParams(dimension_semantics=("arbitrary",)),
        name="l0_inproj",
    )(x2d, meta_blk, gain, w_in, q_gain, k_gain, freq, *[w for _, w in cast_weights])


def _attn_probs(n, sinks, q, kc, kp, km):
    j = lax.broadcasted_iota(jnp.int32, (BLOCK, BLOCK), 0)
    i = lax.broadcasted_iota(jnp.int32, (BLOCK, BLOCK), 1)
    causal = j <= i
    win_ok = n >= jnp.where(causal, 1, 2)
    jm = lax.broadcasted_iota(jnp.int32, (N_META, BLOCK), 0) + META_PAD
    im = lax.broadcasted_iota(jnp.int32, (N_META, BLOCK), 1)
    meta_ok = n >= jnp.where(jm <= im, 0, 1)
    contract_last = (((1,), (1,)), ((), ()))

    qs = jnp.concatenate([q[:, a * ATT_HEAD_DIM:(a + 1) * ATT_HEAD_DIM]
                          for a in range(ATT_GROUP)], axis=0)
    s_c = lax.dot_general(kc, qs, contract_last, preferred_element_type=F32)
    s_p = lax.dot_general(kp, qs, contract_last, preferred_element_type=F32)
    s_m = lax.dot_general(km[META_PAD:], qs, contract_last, preferred_element_type=F32)
    p_c, p_p, p_m, inv_den = [], [], [], []
    for a in range(ATT_GROUP):
        head = slice(a * BLOCK, (a + 1) * BLOCK)
        sw = jnp.where(win_ok, jnp.where(causal, s_c[:, head], s_p[:, head]), NEG_INF)
        sm = jnp.where(meta_ok, s_m[:, head], NEG_INF)
        m = jnp.maximum(jnp.maximum(jnp.max(sw, axis=0, keepdims=True),
                                    jnp.max(sm, axis=0, keepdims=True)), sinks[a])
        pw = jnp.exp(sw - m)
        pm = jnp.exp(sm - m)
        den = (jnp.sum(pw, axis=0, keepdims=True) + jnp.sum(pm, axis=0, keepdims=True)
               + jnp.exp(sinks[a] - m))
        inv_den.append(1.0 / den)
        p_c.append(jnp.where(causal, pw, 0.0).astype(BF16))
        p_p.append(jnp.where(causal, 0.0, pw).astype(BF16))
        p_m.append(pm.astype(BF16))
    lanes = lambda parts: jnp.concatenate(parts, axis=1)
    return lanes(p_c), lanes(p_p), lanes(p_m), lanes(inv_den)


def _attn_out_t(probs, vc, vp, vm):
    p_c, p_p, p_m, inv_den = probs
    contract_rows = (((0,), (0,)), ((), ()))
    o_t = (lax.dot_general(vc, p_c, contract_rows, preferred_element_type=F32)
           + lax.dot_general(vp, p_p, contract_rows, preferred_element_type=F32)
           + lax.dot_general(vm[META_PAD:], p_m, contract_rows, preferred_element_type=F32))
    return o_t * inv_den


def _attn_untranspose(o_t):
    o_t = jnp.concatenate(o_t, axis=0)
    return jnp.concatenate([o_t[:, a * BLOCK:(a + 1) * BLOCK].T for a in range(ATT_GROUP)],
                           axis=1).astype(BF16)


def _lru_gates(lo, xr_ref, rows, cw, cb, wg_ref, xbuf):
    tl = BLOCK
    cols = slice(lo, lo + GATE_TILE)
    x = xr_ref[rows, cols]
    xbuf[SUBLANES:SUBLANES + tl, cols] = x
    xc = x * cw[CONV_W - 1:CONV_W, cols] + cb[:, cols]
    for d in range(1, CONV_W):
        xc = xc + (xbuf[SUBLANES - d:SUBLANES - d + tl, cols]
                   * cw[CONV_W - 1 - d:CONV_W - d, cols])
    xbuf[0:SUBLANES, cols] = x[tl - SUBLANES:tl]
    xcb = xc.astype(BF16)
    ga_r = jnp.dot(xcb, wg_ref[cols, cols], preferred_element_type=F32)
    ga_i = jnp.dot(xcb, wg_ref[cols, LRU_WIDTH + lo:LRU_WIDTH + lo + GATE_TILE],
                   preferred_element_type=F32)
    return xc, ga_r, ga_i


def _lru_scan(n, xc, ga_r, ga_i, grp, gate_ref, rows, bg, softplus, hcar, y_ref):
    tl = BLOCK
    row = lax.broadcasted_iota(jnp.int32, (tl, 1), 0)
    t = n * tl + row
    r = _sigmoid(ga_r + bg[:, grp])
    gi = _sigmoid(ga_i + bg[:, LRU_WIDTH + grp.start:LRU_WIDTH + grp.stop])
    log_a = (-LRU_C * softplus[:, grp]) * r
    a = jnp.exp(log_a)
    mult2 = jnp.tanh(-log_a) * (a * a + 1.0)
    mult = mult2 * lax.rsqrt(jnp.maximum(mult2, SQRT_GUARD))
    mult = jnp.where(t == META_PAD, 1.0, mult)
    b = jnp.where(t < META_PAD, 0.0, mult * gi * xc)

    d = 1
    while d < SUBLANES:
        keep = row >= d
        b = jnp.where(keep, a * pltpu.roll(b, d, 0), 0.0) + b
        a = jnp.where(keep, a * pltpu.roll(a, d, 0), a)
        d *= 2
    while d < tl:
        b = jnp.concatenate([b[:d], a[d:] * b[:tl - d] + b[d:]], axis=0)
        a = jnp.concatenate([a[:d], a[d:] * a[:tl - d]], axis=0)
        d *= 2
    h = b + a * hcar[0:1, grp]
    hcar[:, grp] = jnp.broadcast_to(h[tl - 1:tl], (SUBLANES, LANES))
    y_ref[rows, grp] = (_gelu_tanh(gate_ref[rows, grp]) * h).astype(BF16)


def _ffn(h1, gain, wgu_ref, wd_ref):
    xn = (h1 * _rms_scale(h1) * gain).astype(BF16)
    acts = []
    for c in range(D_FF // FFN_CHUNK):
        lo = c * FFN_CHUNK
        g = jnp.dot(xn, wgu_ref[:, lo:lo + FFN_CHUNK], preferred_element_type=F32)
        u = jnp.dot(xn, wgu_ref[:, D_FF + lo:D_FF + lo + FFN_CHUNK], preferred_element_type=F32)
        acts.append((_silu(g) * u).astype(BF16))
    act = jnp.concatenate(acts, axis=1)
    return h1 + jnp.dot(act, wd_ref[...], preferred_element_type=F32)


def _l0_mix_ffn_kernel(*refs, tiles_per_seq, n_tiles, n_cast):
    k_ref, v_ref = refs[6:8]
    cast_in = refs[17:17 + n_cast]
    cast_out = refs[18 + n_cast:18 + 2 * n_cast]
    _, xbuf, hcar, kprev, vprev, kmeta, vmeta = refs[18 + 2 * n_cast:]
    s = pl.program_id(0)
    tile_in_seq = jnp.minimum(s, n_tiles - 1) % tiles_per_seq
    _cast_blocks(cast_in + cast_out)

    @pl.when(tile_in_seq == 0)
    def _():
        xbuf[0:SUBLANES, :] = jnp.zeros((SUBLANES, LRU_WIDTH), F32)
        hcar[...] = jnp.zeros_like(hcar)
        kmeta[...] = k_ref[0:BLOCK, :]
        vmeta[...] = v_ref[0:BLOCK, :]
        kprev[...] = k_ref[0:BLOCK, :]
        vprev[...] = v_ref[0:BLOCK, :]

    body = functools.partial(_l0_mix_ffn_body, refs, n_cast, tile_in_seq, s, tiles_per_seq)
    pl.when(s == 0)(functools.partial(body, with_ffn=False, with_mix=True))
    pl.when(s > 0)(functools.partial(body, with_ffn=True, with_mix=True))


def _l0_mix_ffn_body(refs, n_cast, tile_in_seq, s, tiles_per_seq, *, with_ffn, with_mix):
    (sink_ref, x_ref, meta_ref, xr_ref, gate_ref, q_ref, k_ref, v_ref, cw_ref, cb_ref, wg_ref,
     bg_ref, lam_ref, wo_ref, gain_ref, wgu_ref, wd_ref) = refs[:17]
    o_ref = refs[17 + n_cast]
    ybuf, xbuf, hcar, kprev, vprev, kmeta, vmeta = refs[18 + 2 * n_cast:]
    blocks_per_tile = x_ref.shape[0] // BLOCK
    last = slice((blocks_per_tile - 1) * BLOCK, blocks_per_tile * BLOCK)

    if with_ffn:
        ffn_first = (s - 1) % tiles_per_seq == 0
        h1 = (_padded_rows(x_ref, meta_ref, ffn_first)
              + jnp.dot(ybuf[...], wo_ref[...], preferred_element_type=F32))
        xn = (h1 * _rms_scale(h1) * gain_ref[...]).astype(BF16)

    cw = cw_ref[...]
    cb = cb_ref[...]
    bg = bg_ref[...]
    z = -lam_ref[...]
    softplus = jnp.maximum(z, 0.0) + jnp.log1p(jnp.exp(-jnp.abs(z)))

    live = {}

    def block_rows(blk):
        return tile_in_seq * blocks_per_tile + blk, slice(blk * BLOCK, (blk + 1) * BLOCK)

    def lru_gates_step(blk, lo):
        _, rows = block_rows(blk)
        live["lru", blk, lo] = _lru_gates(lo, xr_ref, rows, cw, cb, wg_ref, xbuf)

    def lru_scan_step(blk, lo, sub):
        n, rows = block_rows(blk)
        xc, ga_r, ga_i = live["lru", blk, lo]
        part = slice(sub, sub + LANES)
        _lru_scan(n, xc[:, part], ga_r[:, part], ga_i[:, part],
                  slice(lo + sub, lo + sub + LANES), gate_ref, rows, bg, softplus, hcar, ybuf)
        if sub + LANES == GATE_TILE:
            del live["lru", blk, lo]

    def kv_blocks(ref, prev_ref, meta_ref, blk, g):
        _, rows = block_rows(blk)
        lanes = slice(g * ATT_HEAD_DIM, (g + 1) * ATT_HEAD_DIM)
        prev = prev_ref[:, lanes] if blk == 0 else ref[(blk - 1) * BLOCK:blk * BLOCK, lanes]
        return ref[rows, lanes], prev, meta_ref[:, lanes]

    def attn_probs_step(blk, g):
        n, rows = block_rows(blk)
        width = ATT_GROUP * ATT_HEAD_DIM
        sinks = [sink_ref[g * ATT_GROUP + a] for a in range(ATT_GROUP)]
        live["probs", blk, g] = _attn_probs(n, sinks, q_ref[rows, g * width:(g + 1) * width],
                                            *kv_blocks(k_ref, kprev, kmeta, blk, g))

    def attn_out_step(blk, g):
        _, rows = block_rows(blk)
        live["out", blk, g] = _attn_out_t(live.pop(("probs", blk, g)),
                                          *kv_blocks(v_ref, vprev, vmeta, blk, g))
        if g + 1 == ATT_KV_HEADS:
            ybuf[rows, LRU_WIDTH:LRU_WIDTH + Q_WIDTH] = _attn_untranspose(
                [live.pop(("out", blk, h)) for h in range(ATT_KV_HEADS)])

    acts = []

    def gate_step(c):
        lo = c * FFN_CHUNK
        live["g", c] = jnp.dot(xn, wgu_ref[:, lo:lo + FFN_CHUNK], preferred_element_type=F32)

    def up_step(c):
        lo = c * FFN_CHUNK
        u = jnp.dot(xn, wgu_ref[:, D_FF + lo:D_FF + lo + FFN_CHUNK], preferred_element_type=F32)
        acts.append((_silu(live.pop(("g", c))) * u).astype(BF16))

    def down_step(c):
        cols = slice(c * DOWN_CHUNK, (c + 1) * DOWN_CHUNK)
        if len(acts) > 1:
            acts[:] = [jnp.concatenate(acts, axis=1)]
        o_ref[:, cols] = h1[:, cols] + jnp.dot(acts[0], wd_ref[:, cols],
                                               preferred_element_type=F32)

    ffn_steps = []
    if with_ffn:
        for c in range(D_FF // FFN_CHUNK):
            ffn_steps += [functools.partial(gate_step, c), functools.partial(up_step, c)]
        ffn_steps += [functools.partial(down_step, c) for c in range(D_MODEL // DOWN_CHUNK)]
    mix_steps = []
    if with_mix:
        for blk in range(blocks_per_tile):
            for lo in range(0, LRU_WIDTH, GATE_TILE):
                mix_steps.append(functools.partial(lru_gates_step, blk, lo))
                mix_steps += [functools.partial(lru_scan_step, blk, lo, sub)
                              for sub in range(0, GATE_TILE, LANES)]
            mix_steps += [functools.partial(attn_probs_step, blk, g)
                          for g in range(ATT_KV_HEADS)]
            mix_steps += [functools.partial(attn_out_step, blk, g) for g in range(ATT_KV_HEADS)]
    for step in (_interleave(ffn_steps, mix_steps) if ffn_steps else mix_steps):
        step()
    if with_mix:
        kprev[...] = k_ref[last, :]
        vprev[...] = v_ref[last, :]


def _l0_mix_ffn(sinks, x2d, meta_blk, xr, gate, q, k, v, conv_w, conv_b, w_gates, b_gates, lam,
                w_out, gain, w_gu, w_down, cast_weights, tiles_per_seq, seq):
    rows = xr.shape[0]
    tm = ROW_TILE
    n_tiles = rows // tm
    mix = lambda w: pl.BlockSpec((tm, w), lambda s: (jnp.minimum(s, n_tiles - 1), 0))
    cast_in, cast_out, cast_shapes = _cast_specs(cast_weights)
    return pl.pallas_call(
        functools.partial(_l0_mix_ffn_kernel, tiles_per_seq=tiles_per_seq, n_tiles=n_tiles,
                          n_cast=len(cast_in)),
        grid=(n_tiles + 1,),
        in_specs=[pl.BlockSpec(memory_space=pltpu.SMEM),
                  _token_window_spec(tm, tiles_per_seq, seq, lambda s: jnp.maximum(s - 1, 0)),
                  _const_spec((BLOCK, D_MODEL)),
                  mix(LRU_WIDTH), mix(LRU_WIDTH), mix(Q_WIDTH), mix(KV_WIDTH), mix(KV_WIDTH),
                  _const_spec((CONV_W, LRU_WIDTH)), _const_spec((1, LRU_WIDTH)),
                  _const_spec((LRU_WIDTH, 2 * LRU_WIDTH)), _const_spec((1, 2 * LRU_WIDTH)),
                  _const_spec((1, LRU_WIDTH)), _const_spec(w_out.shape), _const_spec((1, D_MODEL)),
                  _const_spec(w_gu.shape), _const_spec(w_down.shape)] + cast_in,
        out_specs=[pl.BlockSpec((tm, D_MODEL), lambda s: (jnp.maximum(s - 1, 0), 0))] + cast_out,
        out_shape=[jax.ShapeDtypeStruct((rows, D_MODEL), F32)] + cast_shapes,
        scratch_shapes=[pltpu.VMEM((tm, LRU_WIDTH + Q_WIDTH), BF16),
                        pltpu.VMEM((SUBLANES + BLOCK, LRU_WIDTH), F32),
                        pltpu.VMEM((SUBLANES, LRU_WIDTH), F32),
                        pltpu.VMEM((BLOCK, KV_WIDTH), BF16), pltpu.VMEM((BLOCK, KV_WIDTH), BF16),
                        pltpu.VMEM((BLOCK, KV_WIDTH), BF16), pltpu.VMEM((BLOCK, KV_WIDTH), BF16)],
        compiler_params=pltpu.CompilerParams(dimension_semantics=("arbitrary",)),
        name="l0_mix_ffn",
    )(sinks, x2d, meta_blk, xr, gate, q, k, v, conv_w, conv_b, w_gates, b_gates, lam, w_out, gain,
      w_gu, w_down, *[w for _, w in cast_weights])


def _outproj_ffn_kernel(h_ref, y_ref, wo_ref, gain_ref, wgu_ref, wd_ref, o_ref, h1s, xns):
    s = pl.program_id(0)
    slot = s % 2
    live = {}

    def outproj_step(c):
        cols = slice(c * DOWN_CHUNK, (c + 1) * DOWN_CHUNK)
        live["h1", c] = h_ref[:, cols] + jnp.dot(y_ref[...], wo_ref[:, cols],
                                                 preferred_element_type=F32)

    def norm_step():
        h1 = jnp.concatenate([live.pop(("h1", c)) for c in range(D_MODEL // DOWN_CHUNK)], axis=1)
        h1s[slot] = h1
        xns[slot] = (h1 * _rms_scale(h1) * gain_ref[...]).astype(BF16)

    head_steps = [functools.partial(outproj_step, c) for c in range(D_MODEL // DOWN_CHUNK)]
    head_steps.append(norm_step)

    @pl.when(s == 0)
    def _():
        for step in head_steps:
            step()

    @pl.when(s > 0)
    def _():
        acts = []

        def pair_step(c):
            lo = c * FFN_CHUNK
            xn = xns[1 - slot]
            g = jnp.dot(xn, wgu_ref[:, lo:lo + FFN_CHUNK], preferred_element_type=F32)
            u = jnp.dot(xn, wgu_ref[:, D_FF + lo:D_FF + lo + FFN_CHUNK],
                        preferred_element_type=F32)
            acts.append((_silu(g) * u).astype(BF16))

        def down_step():
            act = jnp.concatenate(acts, axis=1)
            o_ref[...] = h1s[1 - slot] + jnp.dot(act, wd_ref[...], preferred_element_type=F32)

        pair_steps = [functools.partial(pair_step, c) for c in range(D_FF // FFN_CHUNK)]
        for step in _interleave(pair_steps, head_steps) + [down_step]:
            step()


def _final_outproj_ffn(h, y, w_out, gain, w_gu, w_down, batch, seq_rows, seq):
    tm = OUT_TILE
    tiles = seq // tm
    n_tiles = batch * tiles

    def window_row(s):
        t = jnp.minimum(s, n_tiles - 1)
        return pl.multiple_of((t // tiles) * seq_rows + BLOCK + (t % tiles) * tm, BLOCK)

    win = lambda w: pl.BlockSpec((pl.Element(tm), pl.Element(w)), lambda s: (window_row(s), 0))
    return pl.pallas_call(
        _outproj_ffn_kernel,
        grid=(n_tiles + 1,),
        in_specs=[win(D_MODEL), win(y.shape[1]), _const_spec(w_out.shape),
                  _const_spec((1, D_MODEL)), _const_spec(w_gu.shape), _const_spec(w_down.shape)],
        out_specs=pl.BlockSpec((tm, D_MODEL), lambda s: (jnp.maximum(s - 1, 0), 0)),
        out_shape=jax.ShapeDtypeStruct((batch * seq, D_MODEL), F32),
        scratch_shapes=[pltpu.VMEM((2, tm, D_MODEL), F32), pltpu.VMEM((2, tm, D_MODEL), BF16)],
        compiler_params=pltpu.CompilerParams(dimension_semantics=("arbitrary",)),
        name="l1_outproj_ffn",
    )(h, y, w_out, gain, w_gu, w_down)


_Q0, _K0, _V0, _G0 = 0, D_MODEL, 2 * D_MODEL, 4 * D_MODEL


def _retention_intra_decay(hd, length):
    ii = lax.broadcasted_iota(jnp.int32, (length, length), 0)
    jj = lax.broadcasted_iota(jnp.int32, (length, length), 1)
    diff = (ii - jj).astype(F32)
    return jnp.where(diff >= 0.0, jnp.exp(jnp.maximum(diff, 0.0) * RET_LOG_G[hd]), 0.0)


def _retention_decays(hd, length):
    idx = lax.broadcasted_iota(jnp.int32, (length, 1), 0).astype(F32)
    log_g = RET_LOG_G[hd]
    return (jnp.exp((idx + 1.0) * log_g), jnp.exp((length - 1.0 - idx) * log_g),
            math.exp(length * log_g))


def _l1_mix_kernel(*refs, tiles_per_seq, n_tiles, n_cast):
    h_ref, gain_ref, w_ref, freq_ref = refs[:4]
    cast_in = refs[4:4 + n_cast]
    y_ref = refs[4 + n_cast]
    cast_out = refs[5 + n_cast:5 + 2 * n_cast]
    qkvg, state, ocos, osin, intra = refs[5 + 2 * n_cast:]
    _cast_blocks(cast_in + cast_out)
    s = pl.program_id(0)

    @pl.when(s == 0)
    def _():
        _rope_offsets(freq_ref, ocos, osin)
        for hd in range(RET_HEADS):
            intra[hd] = _retention_intra_decay(hd, RET_CHUNK)
        qkvg[1] = jnp.zeros(qkvg.shape[1:], BF16)

    @pl.when(jnp.maximum(s - 1, 0) % tiles_per_seq == 0)
    def _():
        state[...] = jnp.zeros_like(state)

    body = functools.partial(_l1_mix_body, refs, n_cast, s, tiles_per_seq)
    pl.when(s < n_tiles)(functools.partial(body, with_proj=True, with_ret=True))
    pl.when(s == n_tiles)(functools.partial(body, with_proj=False, with_ret=True))


def _l1_mix_body(refs, n_cast, s, tiles_per_seq, *, with_proj, with_ret):
    h_ref, gain_ref, w_ref, freq_ref = refs[:4]
    y_ref = refs[4 + n_cast]
    qkvg, state, ocos, osin, intra = refs[5 + 2 * n_cast:]
    proj_slot = s % 2
    ret_slot = 1 - proj_slot
    tm = h_ref.shape[0]
    half = RET_QK_DIM // 2
    if with_proj:
        h = h_ref[...]
        xn = (h * _rms_scale(h) * gain_ref[...]).astype(BF16)
        cos, sin = _rope_tables((s % tiles_per_seq) * tm - META_PAD, freq_ref, ocos, osin)

    def rope_step(col0, hd, scale):
        lo = col0 + hd * RET_QK_DIM
        y = jnp.dot(xn, w_ref[:, lo:lo + RET_QK_DIM], preferred_element_type=F32)
        x1 = y[:, :half]
        x2 = y[:, half:]
        qkvg[proj_slot, :, lo:lo + half] = ((x1 * cos - x2 * sin) * scale).astype(BF16)
        qkvg[proj_slot, :, lo + half:lo + RET_QK_DIM] = ((x2 * cos + x1 * sin)
                                                         * scale).astype(BF16)

    def value_step(col0, hd, act):
        lo = col0 + hd * RET_V_DIM
        y = jnp.dot(xn, w_ref[:, lo:lo + RET_V_DIM], preferred_element_type=F32)
        qkvg[proj_slot, :, lo:lo + RET_V_DIM] = act(y).astype(BF16)

    chunks = []
    while sum(length for _, length in chunks) < tm:
        row0 = sum(length for _, length in chunks)
        chunks.append((row0, min(RET_CHUNK, tm - row0)))
    n_chunks = len(chunks)

    def chunk_rows(c):
        return slice(chunks[c][0], chunks[c][0] + chunks[c][1])

    def ret_operand(c, hd, col0, width):
        return qkvg[ret_slot, chunk_rows(c), col0 + hd * width:col0 + (hd + 1) * width]

    heads = range(RET_HEADS)
    decay_tables = {length: [_retention_decays(hd, length) for hd in heads]
                    for length in sorted({length for _, length in chunks})}
    decays = lambda c, hd: decay_tables[chunks[c][1]][hd]
    decay_intra = lambda c, hd: intra[hd, :chunks[c][1], :chunks[c][1]]
    live = {}

    def prep_step(c, hd):
        live["kdt", c, hd] = (ret_operand(c, hd, _K0, RET_QK_DIM).astype(F32)
                              * decays(c, hd)[1]).T.astype(BF16)

    def qk_step(c, hd):
        live["qk", c, hd] = lax.dot_general(
            ret_operand(c, hd, _Q0, RET_QK_DIM), ret_operand(c, hd, _K0, RET_QK_DIM),
            (((1,), (1,)), ((), ())), preferred_element_type=F32)

    def decay_step(c, hd):
        qkd = (live.pop(("qk", c, hd)) * decay_intra(c, hd)).astype(BF16)
        live["lhs", c, hd] = jnp.concatenate([qkd, live.pop(("kdt", c, hd))], axis=0)
        live["stb", c, hd] = state[hd].astype(BF16)

    def output_step(c, hd):
        length = chunks[c][1]
        both = jnp.dot(live.pop(("lhs", c, hd)), ret_operand(c, hd, _V0, RET_V_DIM),
                       preferred_element_type=F32)
        live["kv", c, hd] = both[length:]
        live["o", c, hd] = both[:length] + jnp.dot(
            ret_operand(c, hd, _Q0, RET_QK_DIM), live.pop(("stb", c, hd)),
            preferred_element_type=F32) * decays(c, hd)[0]

    def post_step(c, hd):
        o = live.pop(("o", c, hd))
        state[hd] = decays(c, hd)[2] * state[hd] + live.pop(("kv", c, hd))
        gate = ret_operand(c, hd, _G0, RET_V_DIM).astype(F32)
        y_ref[chunk_rows(c), hd * RET_V_DIM:(hd + 1) * RET_V_DIM] = (
            o * _rms_scale(o) * gate).astype(BF16)

    proj_steps = []
    for hd in heads if with_proj else ():
        proj_steps.append([functools.partial(rope_step, _Q0, hd, 1.0),
                           functools.partial(rope_step, _K0, hd, RET_QK_DIM ** -0.5)])
        proj_steps.append([functools.partial(value_step, _V0, hd, lambda y: y)])
        proj_steps.append([functools.partial(value_step, _G0, hd, _silu)])
    each_head = lambda step, c: [functools.partial(step, c, hd) for hd in heads]
    ret_steps = []
    for c in range(n_chunks) if with_ret else ():
        ret_steps.append(each_head(qk_step, c) + each_head(decay_step, c)
                         + (each_head(prep_step, c + 1) if c + 1 < n_chunks else []))
        ret_steps.append(each_head(output_step, c) + each_head(post_step, c))
    for step in each_head(prep_step, 0) if with_ret else ():
        step()
    for proj, ret in itertools.zip_longest(proj_steps, ret_steps, fillvalue=()):
        for step in (*proj, *ret):
            step()


def _l1_mix(h, gain, w_in, freq, cast_weights, tiles_per_seq):
    rows = h.shape[0]
    tm = ROW_TILE
    n_tiles = rows // tm
    half = RET_QK_DIM // 2
    cast_in, cast_out, cast_shapes = _cast_specs(cast_weights)
    return pl.pallas_call(
        functools.partial(_l1_mix_kernel, tiles_per_seq=tiles_per_seq, n_tiles=n_tiles,
                          n_cast=len(cast_in)),
        grid=(n_tiles + 1,),
        in_specs=[pl.BlockSpec((tm, D_MODEL), lambda s: (jnp.minimum(s, n_tiles - 1), 0)),
                  _const_spec((1, D_MODEL)), _const_spec(w_in.shape), _const_spec((1, half))]
        + cast_in,
        out_specs=[pl.BlockSpec((tm, 2 * D_MODEL), lambda s: (jnp.maximum(s - 1, 0), 0))]
        + cast_out,
        out_shape=[jax.ShapeDtypeStruct((rows, 2 * D_MODEL), BF16)] + cast_shapes,
        scratch_shapes=[pltpu.VMEM((2, tm, 6 * D_MODEL), BF16),
                        pltpu.VMEM((RET_HEADS, RET_QK_DIM, RET_V_DIM), F32),
                        pltpu.VMEM((tm, half), F32), pltpu.VMEM((tm, half), F32),
                        pltpu.VMEM((RET_HEADS, RET_CHUNK, RET_CHUNK), F32)],
        compiler_params=pltpu.CompilerParams(dimension_semantics=("arbitrary",)),
        name="l1_mix",
    )(h, gain, w_in, freq, *[w for _, w in cast_weights])


def _inv_freq(half, theta):
    return jnp.power(jnp.asarray(theta, F32), -jnp.arange(half, dtype=F32) / half)


def _block_diag(w):
    heads, wi, wo = w.shape
    eye = jnp.eye(heads, dtype=w.dtype)
    return (eye[:, None, :, None] * w[:, :, None, :]).reshape(heads * wi, heads * wo)


def kernel(x, meta_tokens, mix_norm_ab, ab_w_in, lru_conv_w, lru_conv_b, lru_w_a, lru_b_a, lru_w_i, lru_b_i, lru_lambda, q_norm, k_norm, attn_sinks, ab_w_out, mix_norm_ret, ret_w_in, ret_w_out, ffn_norm, ffn_w_gu, ffn_w_down):
    batch, seq, _ = x.shape
    seq_rows = META_PAD + N_META + seq
    tiles_per_seq = seq_rows // ROW_TILE

    x2d = x.reshape(batch * seq, D_MODEL)
    meta_blk = jnp.concatenate([jnp.zeros((META_PAD, D_MODEL), x.dtype),
                                meta_tokens.astype(x.dtype)], axis=0)

    f_att = _inv_freq(ROT_DIM // 2, ROPE_THETA)
    f_att = jnp.concatenate([f_att, f_att, jnp.zeros((ATT_HEAD_DIM - ROT_DIM,), F32)])
    f_att = jnp.tile(f_att, LANES // ATT_HEAD_DIM).reshape(1, LANES)
    f_ret = _inv_freq(RET_QK_DIM // 2, RET_THETA).reshape(1, RET_QK_DIM // 2)

    row_vec = lambda v: v.reshape(1, -1).astype(F32)
    two_heads = lambda v: jnp.tile(v.reshape(1, -1).astype(F32), (1, 2))

    xr, gate, q, k, v, w_gu0, w_down0 = _inproj0(
        x2d, meta_blk, row_vec(mix_norm_ab[0]), ab_w_in[0].astype(BF16), two_heads(q_norm[0]),
        two_heads(k_norm[0]), f_att, [(0, ffn_w_gu), (0, ffn_w_down)], batch, tiles_per_seq, seq)
    w_gates = jnp.concatenate([_block_diag(lru_w_a[0]), _block_diag(lru_w_i[0])], axis=1)
    b_gates = jnp.concatenate([lru_b_a[0].reshape(1, -1), lru_b_i[0].reshape(1, -1)], axis=1)
    w_att = ab_w_out[0][LRU_WIDTH:].reshape(ATT_KV_HEADS, ATT_GROUP, ATT_HEAD_DIM, D_MODEL)
    w_att = w_att.transpose(1, 0, 2, 3).reshape(Q_WIDTH, D_MODEL)
    w_out0 = jnp.concatenate([ab_w_out[0][:LRU_WIDTH], w_att], axis=0).astype(BF16)
    h, w_in1, w_out1 = _l0_mix_ffn(
        attn_sinks[0].astype(F32), x2d, meta_blk, xr, gate, q, k, v, lru_conv_w[0],
        row_vec(lru_conv_b[0]), w_gates.astype(BF16), b_gates.astype(F32), row_vec(lru_lambda[0]),
        w_out0, row_vec(ffn_norm[0]), w_gu0, w_down0, [(0, ret_w_in), (0, ret_w_out)],
        tiles_per_seq, seq)

    y_ret, w_gu1, w_down1 = _l1_mix(h, row_vec(mix_norm_ret[0]), w_in1, f_ret,
                                    [(1, ffn_w_gu), (1, ffn_w_down)], tiles_per_seq)
    out = _final_outproj_ffn(h, y_ret, w_out1, row_vec(ffn_norm[1]), w_gu1, w_down1,
                             batch, seq_rows, seq)
    return out.reshape(batch, seq, D_MODEL)
```

```python
import functools
import itertools
import math

import jax
import jax.numpy as jnp
from jax import lax
from jax.experimental import pallas as pl
from jax.experimental.pallas import tpu as pltpu

F32 = jnp.float32
BF16 = jnp.bfloat16

D_MODEL = 1024
N_META = 16
BLOCK = 128
META_PAD = BLOCK - N_META
RMS_EPS = 1e-6
NEG_INF = -1e30

LRU_WIDTH = 512
LRU_HEADS = 8
LRU_BLOCK_W = 64
CONV_W = 4
LRU_C = 8.0

ATT_HEADS = 8
ATT_KV_HEADS = 2
ATT_GROUP = ATT_HEADS // ATT_KV_HEADS
ATT_HEAD_DIM = 64
ROPE_THETA = 500000.0
ROT_DIM = 16
Q_WIDTH = 512
KV_WIDTH = 128
AB_IN_WIDTH = 2 * LRU_WIDTH + Q_WIDTH + 2 * KV_WIDTH

RET_HEADS = 4
RET_QK_DIM = 256
RET_V_DIM = 512
RET_THETA = 10000.0
RET_LOG_G = tuple(math.log1p(-(2.0 ** (-5.0 - h))) for h in range(RET_HEADS))

D_FF = 2816

LANES = 128
SUBLANES = 8
ROW_TILE = 640
OUT_TILE = 512
SQRT_GUARD = 1.1754944e-38
GATE_TILE = 256
INPROJ_CHUNK = 256
RET_CHUNK = 256
CAST_STEPS = 16
FFN_CHUNK = 256
DOWN_CHUNK = 512


def _rms_scale(x):
    return lax.rsqrt(jnp.mean(x * x, axis=-1, keepdims=True) + RMS_EPS)


def _sigmoid(x):
    return 0.5 * jnp.tanh(0.5 * x) + 0.5


def _silu(x):
    half = 0.5 * x
    return half + half * jnp.tanh(half)


def _gelu_tanh(x):
    half = 0.5 * x
    return half + half * jnp.tanh(0.7978845608028654 * (x + 0.044715 * (x * x * x)))


def _interleave(primary, secondary):
    out, done = [], 0
    for i, step in enumerate(primary):
        out.append(step)
        upto = ((i + 1) * len(secondary)) // len(primary)
        out.extend(secondary[done:upto])
        done = upto
    return out + list(secondary[done:])


def _const_spec(shape):
    zeros = (0,) * len(shape)
    return pl.BlockSpec(shape, lambda *_: zeros, pipeline_mode=pl.Buffered(1))


def _padded_rows(x_ref, meta_ref, first):
    tm = x_ref.shape[0]
    start = pl.multiple_of(jnp.where(first, 0, BLOCK), BLOCK)
    return jnp.concatenate([jnp.where(first, meta_ref[...], x_ref[0:BLOCK, :]),
                            x_ref[pl.ds(start, tm - BLOCK), :]], axis=0)


def _token_window_spec(tm, tiles_per_seq, seq, tile_of_step):
    def index(s):
        t = tile_of_step(s)
        start = jnp.maximum((t % tiles_per_seq) * tm - BLOCK, 0)
        return (pl.multiple_of((t // tiles_per_seq) * seq + start, BLOCK), 0)
    return pl.BlockSpec((pl.Element(tm), pl.Element(D_MODEL)), index)


def _rope_offsets(freq_ref, cos_ref, sin_ref):
    r = lax.broadcasted_iota(jnp.int32, cos_ref.shape, 0).astype(F32)
    ang = r * freq_ref[...]
    cos_ref[...] = jnp.cos(ang)
    sin_ref[...] = jnp.sin(ang)


def _rope_tables(base_pos, freq_ref, cos_ref, sin_ref):
    ang = base_pos.astype(F32) * freq_ref[...]
    cb = jnp.cos(ang)
    sb = jnp.sin(ang)
    oc = cos_ref[...]
    os_ = sin_ref[...]
    return cb * oc - sb * os_, sb * oc + cb * os_


def _cast_blocks(cast_refs):
    n = len(cast_refs) // 2
    for src, dst in zip(cast_refs[:n], cast_refs[n:]):
        dst[...] = src[...].astype(BF16)


def _cast_specs(weights):
    ins, outs, shapes = [], [], []
    step = lambda s: jnp.minimum(s, CAST_STEPS - 1)
    for layer, w in weights:
        _, rows, cols = w.shape
        blk = rows // CAST_STEPS
        ins.append(pl.BlockSpec((None, blk, cols), lambda s, layer=layer: (layer, step(s), 0)))
        outs.append(pl.BlockSpec((blk, cols), lambda s: (step(s), 0)))
        shapes.append(jax.ShapeDtypeStruct((rows, cols), BF16))
    return ins, outs, shapes


def _inproj0_kernel(*refs, tiles_per_seq, n_cast):
    (x_ref, meta_ref, gain_ref, w_ref, qg_ref, kg_ref, freq_ref) = refs[:7]
    cast_in = refs[7:7 + n_cast]
    xr_ref, gate_ref, q_ref, k_ref, v_ref = refs[7 + n_cast:12 + n_cast]
    cast_out = refs[12 + n_cast:12 + 2 * n_cast]
    ocos, osin = refs[12 + 2 * n_cast:]
    step = pl.program_id(0)
    tile_in_seq = step % tiles_per_seq

    @pl.when(step == 0)
    def _():
        _rope_offsets(freq_ref, ocos, osin)

    _cast_blocks(cast_in + cast_out)

    h = _padded_rows(x_ref, meta_ref, tile_in_seq == 0)
    rows = h.shape[0]
    xn = (h * _rms_scale(h) * gain_ref[...]).astype(BF16)

    q0 = 2 * LRU_WIDTH
    k0 = q0 + Q_WIDTH
    chunk_dot = lambda lo: jnp.dot(xn, w_ref[:, lo:lo + INPROJ_CHUNK], preferred_element_type=F32)
    y_q = [chunk_dot(q0 + lo) for lo in range(0, Q_WIDTH, INPROJ_CHUNK)]
    y_kv = chunk_dot(k0)
    v_ref[...] = y_kv[:, KV_WIDTH:].astype(BF16)

    cos, sin = _rope_tables(tile_in_seq * rows - META_PAD, freq_ref, ocos, osin)
    lane = lax.broadcasted_iota(jnp.int32, (1, LANES), 1) & (ATT_HEAD_DIM - 1)
    half = ROT_DIM // 2
    sin_lo = sin * jnp.where(lane < half, -1.0, 0.0)
    sin_hi = sin * jnp.where((lane >= half) & (lane < ROT_DIM), 1.0, 0.0)
    hi = lax.broadcasted_iota(jnp.int32, (2 * LANES, LANES), 0) & (LANES - 1)
    hj = lax.broadcasted_iota(jnp.int32, (2 * LANES, LANES), 1)
    head_mean = jnp.where(hi // ATT_HEAD_DIM == hj // ATT_HEAD_DIM,
                          1.0 / ATT_HEAD_DIM, 0.0).astype(BF16)

    def norm_rope(x, gain, scale):
        sq = x * x
        sq_hi = sq.astype(BF16)
        sq_lo = (sq - sq_hi.astype(F32)).astype(BF16)
        ms = jnp.dot(jnp.concatenate([sq_hi, sq_lo], axis=1), head_mean,
                     preferred_element_type=F32)
        xg = x * gain
        rot = (xg * cos + pltpu.roll(xg, LANES - half, 1) * sin_lo
               + pltpu.roll(xg, half, 1) * sin_hi)
        return rot * (lax.rsqrt(ms + RMS_EPS) * scale)

    def qk_step(j):
        if j < Q_WIDTH // LANES:
            lo = (j * LANES) % INPROJ_CHUNK
            q_ref[:, j * LANES:(j + 1) * LANES] = norm_rope(
                y_q[j * LANES // INPROJ_CHUNK][:, lo:lo + LANES], qg_ref[...],
                ATT_HEAD_DIM ** -0.5).astype(BF16)
        else:
            k_ref[...] = norm_rope(y_kv[:, :KV_WIDTH], kg_ref[...], 1.0).astype(BF16)

    def proj_step(c):
        out_ref = xr_ref if c < LRU_WIDTH // INPROJ_CHUNK else gate_ref
        dst = slice((c * INPROJ_CHUNK) % LRU_WIDTH, (c * INPROJ_CHUNK) % LRU_WIDTH + INPROJ_CHUNK)
        out_ref[:, dst] = chunk_dot(c * INPROJ_CHUNK)

    proj_steps = [functools.partial(proj_step, c) for c in range(2 * LRU_WIDTH // INPROJ_CHUNK)]
    qk_steps = [functools.partial(qk_step, j) for j in range(Q_WIDTH // LANES + 1)]
    for step in _interleave(proj_steps, qk_steps):
        step()


def _inproj0(x2d, meta_blk, gain, w_in, q_gain, k_gain, freq, cast_weights, batch,
             tiles_per_seq, seq):
    tm = ROW_TILE
    rows = batch * tiles_per_seq * tm
    row = lambda w: pl.BlockSpec((tm, w), lambda i: (i, 0))
    cast_in, cast_out, cast_shapes = _cast_specs(cast_weights)
    return pl.pallas_call(
        functools.partial(_inproj0_kernel, tiles_per_seq=tiles_per_seq, n_cast=len(cast_in)),
        grid=(rows // tm,),
        in_specs=[_token_window_spec(tm, tiles_per_seq, seq, lambda s: s),
                  _const_spec((BLOCK, D_MODEL)), _const_spec((1, D_MODEL)),
                  _const_spec((D_MODEL, AB_IN_WIDTH)), _const_spec((1, LANES)),
                  _const_spec((1, LANES)), _const_spec((1, LANES))] + cast_in,
        out_specs=[row(LRU_WIDTH), row(LRU_WIDTH), row(Q_WIDTH), row(KV_WIDTH), row(KV_WIDTH)]
        + cast_out,
        out_shape=[jax.ShapeDtypeStruct((rows, LRU_WIDTH), F32),
                   jax.ShapeDtypeStruct((rows, LRU_WIDTH), F32),
                   jax.ShapeDtypeStruct((rows, Q_WIDTH), BF16),
                   jax.ShapeDtypeStruct((rows, KV_WIDTH), BF16),
                   jax.ShapeDtypeStruct((rows, KV_WIDTH), BF16)] + cast_shapes,
        scratch_shapes=[pltpu.VMEM((tm, LANES), F32), pltpu.VMEM((tm, LANES), F32)],
        compiler_params=pltpu.CompilerParams(dimension_semantics=("arbitrary",)),
        name="l0_inproj",
    )(x2d, meta_blk, gain, w_in, q_gain, k_gain, freq, *[w for _, w in cast_weights])


def _attn_probs(n, sinks, q, kc, kp, km):
    j = lax.broadcasted_iota(jnp.int32, (BLOCK, BLOCK), 0)
    i = lax.broadcasted_iota(jnp.int32, (BLOCK, BLOCK), 1)
    causal = j <= i
    win_ok = n >= jnp.where(causal, 1, 2)
    jm = lax.broadcasted_iota(jnp.int32, (N_META, BLOCK), 0) + META_PAD
    im = lax.broadcasted_iota(jnp.int32, (N_META, BLOCK), 1)
    meta_ok = n >= jnp.where(jm <= im, 0, 1)
    contract_last = (((1,), (1,)), ((), ()))

    qs = jnp.concatenate([q[:, a * ATT_HEAD_DIM:(a + 1) * ATT_HEAD_DIM]
                          for a in range(ATT_GROUP)], axis=0)
    s_c = lax.dot_general(kc, qs, contract_last, preferred_element_type=F32)
    s_p = lax.dot_general(kp, qs, contract_last, preferred_element_type=F32)
    s_m = lax.dot_general(km[META_PAD:], qs, contract_last, preferred_element_type=F32)
    p_c, p_p, p_m, inv_den = [], [], [], []
    for a in range(ATT_GROUP):
        head = slice(a * BLOCK, (a + 1) * BLOCK)
        sw = jnp.where(win_ok, jnp.where(causal, s_c[:, head], s_p[:, head]), NEG_INF)
        sm = jnp.where(meta_ok, s_m[:, head], NEG_INF)
        m = jnp.maximum(jnp.maximum(jnp.max(sw, axis=0, keepdims=True),
                                    jnp.max(sm, axis=0, keepdims=True)), sinks[a])
        pw = jnp.exp(sw - m)
        pm = jnp.exp(sm - m)
        den = (jnp.sum(pw, axis=0, keepdims=True) + jnp.sum(pm, axis=0, keepdims=True)
               + jnp.exp(sinks[a] - m))
        inv_den.append(1.0 / den)
        p_c.append(jnp.where(causal, pw, 0.0).astype(BF16))
        p_p.append(jnp.where(causal, 0.0, pw).astype(BF16))
        p_m.append(pm.astype(BF16))
    lanes = lambda parts: jnp.concatenate(parts, axis=1)
    return lanes(p_c), lanes(p_p), lanes(p_m), lanes(inv_den)


def _attn_out_t(probs, vc, vp, vm):
    p_c, p_p, p_m, inv_den = probs
    contract_rows = (((0,), (0,)), ((), ()))
    o_t = (lax.dot_general(vc, p_c, contract_rows, preferred_element_type=F32)
           + lax.dot_general(vp, p_p, contract_rows, preferred_element_type=F32)
           + lax.dot_general(vm[META_PAD:], p_m, contract_rows, preferred_element_type=F32))
    return o_t * inv_den


def _attn_untranspose(o_t):
    o_t = jnp.concatenate(o_t, axis=0)
    return jnp.concatenate([o_t[:, a * BLOCK:(a + 1) * BLOCK].T for a in range(ATT_GROUP)],
                           axis=1).astype(BF16)


def _lru_gates(lo, xr_ref, rows, cw, cb, wg_ref, xbuf):
    tl = BLOCK
    cols = slice(lo, lo + GATE_TILE)
    x = xr_ref[rows, cols]
    xbuf[SUBLANES:SUBLANES + tl, cols] = x
    xc = x * cw[CONV_W - 1:CONV_W, cols] + cb[:, cols]
    for d in range(1, CONV_W):
        xc = xc + (xbuf[SUBLANES - d:SUBLANES - d + tl, cols]
                   * cw[CONV_W - 1 - d:CONV_W - d, cols])
    xbuf[0:SUBLANES, cols] = x[tl - SUBLANES:tl]
    xcb = xc.astype(BF16)
    ga_r = jnp.dot(xcb, wg_ref[cols, cols], preferred_element_type=F32)
    ga_i = jnp.dot(xcb, wg_ref[cols, LRU_WIDTH + lo:LRU_WIDTH + lo + GATE_TILE],
                   preferred_element_type=F32)
    return xc, ga_r, ga_i


def _lru_scan(n, xc, ga_r, ga_i, grp, gate_ref, rows, bg, softplus, hcar, y_ref):
    tl = BLOCK
    row = lax.broadcasted_iota(jnp.int32, (tl, 1), 0)
    t = n * tl + row
    r = _sigmoid(ga_r + bg[:, grp])
    gi = _sigmoid(ga_i + bg[:, LRU_WIDTH + grp.start:LRU_WIDTH + grp.stop])
    log_a = (-LRU_C * softplus[:, grp]) * r
    a = jnp.exp(log_a)
    mult2 = jnp.tanh(-log_a) * (a * a + 1.0)
    mult = mult2 * lax.rsqrt(jnp.maximum(mult2, SQRT_GUARD))
    mult = jnp.where(t == META_PAD, 1.0, mult)
    b = jnp.where(t < META_PAD, 0.0, mult * gi * xc)

    d = 1
    while d < SUBLANES:
        keep = row >= d
        b = jnp.where(keep, a * pltpu.roll(b, d, 0), 0.0) + b
        a = jnp.where(keep, a * pltpu.roll(a, d, 0), a)
        d *= 2
    while d < tl:
        b = jnp.concatenate([b[:d], a[d:] * b[:tl - d] + b[d:]], axis=0)
        a = jnp.concatenate([a[:d], a[d:] * a[:tl - d]], axis=0)
        d *= 2
    h = b + a * hcar[0:1, grp]
    hcar[:, grp] = jnp.broadcast_to(h[tl - 1:tl], (SUBLANES, LANES))
    y_ref[rows, grp] = (_gelu_tanh(gate_ref[rows, grp]) * h).astype(BF16)


def _ffn(h1, gain, wgu_ref, wd_ref):
    xn = (h1 * _rms_scale(h1) * gain).astype(BF16)
    acts = []
    for c in range(D_FF // FFN_CHUNK):
        lo = c * FFN_CHUNK
        g = jnp.dot(xn, wgu_ref[:, lo:lo + FFN_CHUNK], preferred_element_type=F32)
        u = jnp.dot(xn, wgu_ref[:, D_FF + lo:D_FF + lo + FFN_CHUNK], preferred_element_type=F32)
        acts.append((_silu(g) * u).astype(BF16))
    act = jnp.concatenate(acts, axis=1)
    return h1 + jnp.dot(act, wd_ref[...], preferred_element_type=F32)


def _l0_mix_ffn_kernel(*refs, tiles_per_seq, n_tiles, n_cast):
    k_ref, v_ref = refs[6:8]
    cast_in = refs[17:17 + n_cast]
    cast_out = refs[18 + n_cast:18 + 2 * n_cast]
    _, xbuf, hcar, kprev, vprev, kmeta, vmeta, h1s, xns = refs[18 + 2 * n_cast:]
    s = pl.program_id(0)
    tile_in_seq = jnp.minimum(s, n_tiles - 1) % tiles_per_seq
    _cast_blocks(cast_in + cast_out)

    @pl.when(s == 0)
    def _():
        h1s[...] = jnp.zeros(h1s.shape, F32)
        xns[...] = jnp.zeros(xns.shape, BF16)

    @pl.when(tile_in_seq == 0)
    def _():
        xbuf[0:SUBLANES, :] = jnp.zeros((SUBLANES, LRU_WIDTH), F32)
        hcar[...] = jnp.zeros_like(hcar)
        kmeta[...] = k_ref[0:BLOCK, :]
        vmeta[...] = v_ref[0:BLOCK, :]
        kprev[...] = k_ref[0:BLOCK, :]
        vprev[...] = v_ref[0:BLOCK, :]

    body = functools.partial(_l0_mix_ffn_body, refs, n_cast, tile_in_seq, s, tiles_per_seq)
    pl.when(s == 0)(functools.partial(body, with_ffn=False, with_mix=True))
    pl.when(s > 0)(functools.partial(body, with_ffn=True, with_mix=True))


def _l0_mix_ffn_body(refs, n_cast, tile_in_seq, s, tiles_per_seq, *, with_ffn, with_mix):
    (sink_ref, x_ref, meta_ref, xr_ref, gate_ref, q_ref, k_ref, v_ref, cw_ref, cb_ref, wg_ref,
     bg_ref, lam_ref, wo_ref, gain_ref, wgu_ref, wd_ref) = refs[:17]
    o_ref = refs[17 + n_cast]
    ybuf, xbuf, hcar, kprev, vprev, kmeta, vmeta, h1s, xns = refs[18 + 2 * n_cast:]
    blocks_per_tile = x_ref.shape[0] // BLOCK
    last = slice((blocks_per_tile - 1) * BLOCK, blocks_per_tile * BLOCK)

    slot = s % 2
    ymix = ybuf.at[slot]
    if with_ffn:
        ffn_first = (s - 1) % tiles_per_seq == 0
        x_prev = _padded_rows(x_ref, meta_ref, ffn_first)

    cw = cw_ref[...]
    cb = cb_ref[...]
    bg = bg_ref[...]
    z = -lam_ref[...]
    softplus = jnp.maximum(z, 0.0) + jnp.log1p(jnp.exp(-jnp.abs(z)))

    live = {}

    def block_rows(blk):
        return tile_in_seq * blocks_per_tile + blk, slice(blk * BLOCK, (blk + 1) * BLOCK)

    def lru_gates_step(blk, lo):
        _, rows = block_rows(blk)
        live["lru", blk, lo] = _lru_gates(lo, xr_ref, rows, cw, cb, wg_ref, xbuf)

    def lru_scan_step(blk, lo, sub):
        n, rows = block_rows(blk)
        xc, ga_r, ga_i = live["lru", blk, lo]
        part = slice(sub, sub + LANES)
        _lru_scan(n, xc[:, part], ga_r[:, part], ga_i[:, part],
                  slice(lo + sub, lo + sub + LANES), gate_ref, rows, bg, softplus, hcar, ymix)
        if sub + LANES == GATE_TILE:
            del live["lru", blk, lo]

    def kv_blocks(ref, prev_ref, meta_ref, blk, g):
        _, rows = block_rows(blk)
        lanes = slice(g * ATT_HEAD_DIM, (g + 1) * ATT_HEAD_DIM)
        prev = prev_ref[:, lanes] if blk == 0 else ref[(blk - 1) * BLOCK:blk * BLOCK, lanes]
        return ref[rows, lanes], prev, meta_ref[:, lanes]

    def attn_probs_step(blk, g):
        n, rows = block_rows(blk)
        width = ATT_GROUP * ATT_HEAD_DIM
        sinks = [sink_ref[g * ATT_GROUP + a] for a in range(ATT_GROUP)]
        live["probs", blk, g] = _attn_probs(n, sinks, q_ref[rows, g * width:(g + 1) * width],
                                            *kv_blocks(k_ref, kprev, kmeta, blk, g))

    def attn_out_step(blk, g):
        _, rows = block_rows(blk)
        live["out", blk, g] = _attn_out_t(live.pop(("probs", blk, g)),
                                          *kv_blocks(v_ref, vprev, vmeta, blk, g))
        if g + 1 == ATT_KV_HEADS:
            ymix[rows, LRU_WIDTH:LRU_WIDTH + Q_WIDTH] = _attn_untranspose(
                [live.pop(("out", blk, h)) for h in range(ATT_KV_HEADS)])

    acts = []

    def outproj_step(c):
        cols = slice(c * DOWN_CHUNK, (c + 1) * DOWN_CHUNK)
        live["h1", c] = x_prev[:, cols] + jnp.dot(ybuf[1 - slot], wo_ref[:, cols],
                                                  preferred_element_type=F32)

    def norm_step():
        h1 = jnp.concatenate([live.pop(("h1", c)) for c in range(D_MODEL // DOWN_CHUNK)], axis=1)
        h1s[...] = h1
        xns[...] = (h1 * _rms_scale(h1) * gain_ref[...]).astype(BF16)

    def gate_step(c):
        lo = c * FFN_CHUNK
        live["g", c] = jnp.dot(xns[...], wgu_ref[:, lo:lo + FFN_CHUNK],
                               preferred_element_type=F32)

    def up_step(c):
        lo = c * FFN_CHUNK
        u = jnp.dot(xns[...], wgu_ref[:, D_FF + lo:D_FF + lo + FFN_CHUNK],
                    preferred_element_type=F32)
        acts.append((_silu(live.pop(("g", c))) * u).astype(BF16))

    def down_step(c):
        cols = slice(c * DOWN_CHUNK, (c + 1) * DOWN_CHUNK)
        if len(acts) > 1:
            acts[:] = [jnp.concatenate(acts, axis=1)]
        o_ref[:, cols] = h1s[:, cols] + jnp.dot(acts[0], wd_ref[:, cols],
                                                preferred_element_type=F32)

    ffn_steps = []
    if with_ffn:
        for c in range(D_FF // FFN_CHUNK):
            ffn_steps += [functools.partial(gate_step, c), functools.partial(up_step, c)]
            head = c - (D_FF // FFN_CHUNK - D_MODEL // DOWN_CHUNK)
            if head >= 0:
                ffn_steps.append(functools.partial(outproj_step, head))
        ffn_steps += [functools.partial(down_step, c) for c in range(D_MODEL // DOWN_CHUNK)]
        ffn_steps.append(norm_step)
    mix_steps = []
    if with_mix:
        for blk in range(blocks_per_tile):
            for lo in range(0, LRU_WIDTH, GATE_TILE):
                mix_steps.append(functools.partial(lru_gates_step, blk, lo))
                mix_steps += [functools.partial(lru_scan_step, blk, lo, sub)
                              for sub in range(0, GATE_TILE, LANES)]
            mix_steps += [functools.partial(attn_probs_step, blk, g)
                          for g in range(ATT_KV_HEADS)]
            mix_steps += [functools.partial(attn_out_step, blk, g) for g in range(ATT_KV_HEADS)]
    for step in (_interleave(ffn_steps, mix_steps) if ffn_steps else mix_steps):
        step()
    if with_mix:
        kprev[...] = k_ref[last, :]
        vprev[...] = v_ref[last, :]


def _l0_mix_ffn(sinks, x2d, meta_blk, xr, gate, q, k, v, conv_w, conv_b, w_gates, b_gates, lam,
                w_out, gain, w_gu, w_down, cast_weights, tiles_per_seq, seq):
    rows = xr.shape[0]
    tm = ROW_TILE
    n_tiles = rows // tm
    mix = lambda w: pl.BlockSpec((tm, w), lambda s: (jnp.minimum(s, n_tiles - 1), 0))
    cast_in, cast_out, cast_shapes = _cast_specs(cast_weights)
    return pl.pallas_call(
        functools.partial(_l0_mix_ffn_kernel, tiles_per_seq=tiles_per_seq, n_tiles=n_tiles,
                          n_cast=len(cast_in)),
        grid=(n_tiles + 2,),
        in_specs=[pl.BlockSpec(memory_space=pltpu.SMEM),
                  _token_window_spec(tm, tiles_per_seq, seq,
                                     lambda s: jnp.clip(s - 1, 0, n_tiles - 1)),
                  _const_spec((BLOCK, D_MODEL)),
                  mix(LRU_WIDTH), mix(LRU_WIDTH), mix(Q_WIDTH), mix(KV_WIDTH), mix(KV_WIDTH),
                  _const_spec((CONV_W, LRU_WIDTH)), _const_spec((1, LRU_WIDTH)),
                  _const_spec((LRU_WIDTH, 2 * LRU_WIDTH)), _const_spec((1, 2 * LRU_WIDTH)),
                  _const_spec((1, LRU_WIDTH)), _const_spec(w_out.shape), _const_spec((1, D_MODEL)),
                  _const_spec(w_gu.shape), _const_spec(w_down.shape)] + cast_in,
        out_specs=[pl.BlockSpec((tm, D_MODEL), lambda s: (jnp.maximum(s - 2, 0), 0))] + cast_out,
        out_shape=[jax.ShapeDtypeStruct((rows, D_MODEL), F32)] + cast_shapes,
        scratch_shapes=[pltpu.VMEM((2, tm, LRU_WIDTH + Q_WIDTH), BF16),
                        pltpu.VMEM((SUBLANES + BLOCK, LRU_WIDTH), F32),
                        pltpu.VMEM((SUBLANES, LRU_WIDTH), F32),
                        pltpu.VMEM((BLOCK, KV_WIDTH), BF16), pltpu.VMEM((BLOCK, KV_WIDTH), BF16),
                        pltpu.VMEM((BLOCK, KV_WIDTH), BF16), pltpu.VMEM((BLOCK, KV_WIDTH), BF16),
                        pltpu.VMEM((tm, D_MODEL), F32), pltpu.VMEM((tm, D_MODEL), BF16)],
        compiler_params=pltpu.CompilerParams(dimension_semantics=("arbitrary",)),
        name="l0_mix_ffn",
    )(sinks, x2d, meta_blk, xr, gate, q, k, v, conv_w, conv_b, w_gates, b_gates, lam, w_out, gain,
      w_gu, w_down, *[w for _, w in cast_weights])


def _outproj_ffn_kernel(h_ref, y_ref, wo_ref, gain_ref, wgu_ref, wd_ref, o_ref):
    h1 = h_ref[...] + jnp.dot(y_ref[...], wo_ref[...], preferred_element_type=F32)
    o_ref[...] = _ffn(h1, gain_ref[...], wgu_ref, wd_ref)


def _final_outproj_ffn(h, y, w_out, gain, w_gu, w_down, batch, seq_rows, seq):
    tm = OUT_TILE
    tiles = seq // tm
    win = lambda w: pl.BlockSpec((pl.Element(tm), pl.Element(w)),
                                 lambda b, j: (pl.multiple_of(b * seq_rows + BLOCK + j * tm, BLOCK),
                                               0))
    return pl.pallas_call(
        _outproj_ffn_kernel,
        grid=(batch, tiles),
        in_specs=[win(D_MODEL), win(y.shape[1]), _const_spec(w_out.shape),
                  _const_spec((1, D_MODEL)), _const_spec(w_gu.shape), _const_spec(w_down.shape)],
        out_specs=pl.BlockSpec((tm, D_MODEL), lambda b, j: (b * tiles + j, 0)),
        out_shape=jax.ShapeDtypeStruct((batch * seq, D_MODEL), F32),
        compiler_params=pltpu.CompilerParams(dimension_semantics=("arbitrary", "arbitrary")),
        name="l1_outproj_ffn",
    )(h, y, w_out, gain, w_gu, w_down)


_Q0, _K0, _V0, _G0 = 0, D_MODEL, 2 * D_MODEL, 4 * D_MODEL


def _retention_intra_decay(hd, length):
    ii = lax.broadcasted_iota(jnp.int32, (length, length), 0)
    jj = lax.broadcasted_iota(jnp.int32, (length, length), 1)
    diff = (ii - jj).astype(F32)
    return jnp.where(diff >= 0.0, jnp.exp(jnp.maximum(diff, 0.0) * RET_LOG_G[hd]), 0.0)


def _retention_decays(hd, length):
    idx = lax.broadcasted_iota(jnp.int32, (length, 1), 0).astype(F32)
    log_g = RET_LOG_G[hd]
    return (jnp.exp((idx + 1.0) * log_g), jnp.exp((length - 1.0 - idx) * log_g),
            math.exp(length * log_g))


def _l1_mix_kernel(*refs, tiles_per_seq, n_tiles, n_cast):
    h_ref, gain_ref, w_ref, freq_ref = refs[:4]
    cast_in = refs[4:4 + n_cast]
    y_ref = refs[4 + n_cast]
    cast_out = refs[5 + n_cast:5 + 2 * n_cast]
    qkvg, state, ocos, osin, intra = refs[5 + 2 * n_cast:]
    _cast_blocks(cast_in + cast_out)
    s = pl.program_id(0)

    @pl.when(s == 0)
    def _():
        _rope_offsets(freq_ref, ocos, osin)
        for hd in range(RET_HEADS):
            intra[hd] = _retention_intra_decay(hd, RET_CHUNK)
        qkvg[1] = jnp.zeros(qkvg.shape[1:], BF16)

    @pl.when(jnp.maximum(s - 1, 0) % tiles_per_seq == 0)
    def _():
        state[...] = jnp.zeros_like(state)

    body = functools.partial(_l1_mix_body, refs, n_cast, s, tiles_per_seq)
    pl.when(s < n_tiles)(functools.partial(body, with_proj=True, with_ret=True))
    pl.when(s == n_tiles)(functools.partial(body, with_proj=False, with_ret=True))


def _l1_mix_body(refs, n_cast, s, tiles_per_seq, *, with_proj, with_ret):
    h_ref, gain_ref, w_ref, freq_ref = refs[:4]
    y_ref = refs[4 + n_cast]
    qkvg, state, ocos, osin, intra = refs[5 + 2 * n_cast:]
    proj_slot = s % 2
    ret_slot = 1 - proj_slot
    tm = h_ref.shape[0]
    half = RET_QK_DIM // 2
    if with_proj:
        h = h_ref[...]
        xn = (h * _rms_scale(h) * gain_ref[...]).astype(BF16)
        cos, sin = _rope_tables((s % tiles_per_seq) * tm - META_PAD, freq_ref, ocos, osin)

    def rope_step(col0, hd, scale):
        lo = col0 + hd * RET_QK_DIM
        y = jnp.dot(xn, w_ref[:, lo:lo + RET_QK_DIM], preferred_element_type=F32)
        x1 = y[:, :half]
        x2 = y[:, half:]
        qkvg[proj_slot, :, lo:lo + half] = ((x1 * cos - x2 * sin) * scale).astype(BF16)
        qkvg[proj_slot, :, lo + half:lo + RET_QK_DIM] = ((x2 * cos + x1 * sin)
                                                         * scale).astype(BF16)

    def value_step(col0, hd, act):
        lo = col0 + hd * RET_V_DIM
        y = jnp.dot(xn, w_ref[:, lo:lo + RET_V_DIM], preferred_element_type=F32)
        qkvg[proj_slot, :, lo:lo + RET_V_DIM] = act(y).astype(BF16)

    chunks = []
    while sum(length for _, length in chunks) < tm:
        row0 = sum(length for _, length in chunks)
        chunks.append((row0, min(RET_CHUNK, tm - row0)))
    n_chunks = len(chunks)

    def chunk_rows(c):
        return slice(chunks[c][0], chunks[c][0] + chunks[c][1])

    def ret_operand(c, hd, col0, width):
        return qkvg[ret_slot, chunk_rows(c), col0 + hd * width:col0 + (hd + 1) * width]

    heads = range(RET_HEADS)
    decay_tables = {length: [_retention_decays(hd, length) for hd in heads]
                    for length in sorted({length for _, length in chunks})}
    decays = lambda c, hd: decay_tables[chunks[c][1]][hd]
    decay_intra = lambda c, hd: intra[hd, :chunks[c][1], :chunks[c][1]]
    live = {}

    def prep_step(c, hd):
        live["kdt", c, hd] = (ret_operand(c, hd, _K0, RET_QK_DIM).astype(F32)
                              * decays(c, hd)[1]).T.astype(BF16)

    def qk_step(c, hd):
        live["qk", c, hd] = lax.dot_general(
            ret_operand(c, hd, _Q0, RET_QK_DIM), ret_operand(c, hd, _K0, RET_QK_DIM),
            (((1,), (1,)), ((), ())), preferred_element_type=F32)

    def decay_step(c, hd):
        qkd = (live.pop(("qk", c, hd)) * decay_intra(c, hd)).astype(BF16)
        live["lhs", c, hd] = jnp.concatenate([qkd, live.pop(("kdt", c, hd))], axis=0)
        live["stb", c, hd] = state[hd].astype(BF16)

    def output_step(c, hd):
        length = chunks[c][1]
        both = jnp.dot(live.pop(("lhs", c, hd)), ret_operand(c, hd, _V0, RET_V_DIM),
                       preferred_element_type=F32)
        live["kv", c, hd] = both[length:]
        live["o", c, hd] = both[:length] + jnp.dot(
            ret_operand(c, hd, _Q0, RET_QK_DIM), live.pop(("stb", c, hd)),
            preferred_element_type=F32) * decays(c, hd)[0]

    def post_step(c, hd):
        o = live.pop(("o", c, hd))
        state[hd] = decays(c, hd)[2] * state[hd] + live.pop(("kv", c, hd))
        gate = ret_operand(c, hd, _G0, RET_V_DIM).astype(F32)
        y_ref[chunk_rows(c), hd * RET_V_DIM:(hd + 1) * RET_V_DIM] = (
            o * _rms_scale(o) * gate).astype(BF16)

    proj_steps = []
    for hd in heads if with_proj else ():
        proj_steps.append([functools.partial(rope_step, _Q0, hd, 1.0),
                           functools.partial(rope_step, _K0, hd, RET_QK_DIM ** -0.5)])
        proj_steps.append([functools.partial(value_step, _V0, hd, lambda y: y)])
        proj_steps.append([functools.partial(value_step, _G0, hd, _silu)])
    each_head = lambda step, c: [functools.partial(step, c, hd) for hd in heads]
    ret_steps = []
    for c in range(n_chunks) if with_ret else ():
        ret_steps.append(each_head(qk_step, c) + each_head(decay_step, c)
                         + (each_head(prep_step, c + 1) if c + 1 < n_chunks else []))
        ret_steps.append(each_head(output_step, c) + each_head(post_step, c))
    for step in each_head(prep_step, 0) if with_ret else ():
        step()
    for proj, ret in itertools.zip_longest(proj_steps, ret_steps, fillvalue=()):
        for step in (*proj, *ret):
            step()


def _l1_mix(h, gain, w_in, freq, cast_weights, tiles_per_seq):
    rows = h.shape[0]
    tm = ROW_TILE
    n_tiles = rows // tm
    half = RET_QK_DIM // 2
    cast_in, cast_out, cast_shapes = _cast_specs(cast_weights)
    return pl.pallas_call(
        functools.partial(_l1_mix_kernel, tiles_per_seq=tiles_per_seq, n_tiles=n_tiles,
                          n_cast=len(cast_in)),
        grid=(n_tiles + 1,),
        in_specs=[pl.BlockSpec((tm, D_MODEL), lambda s: (jnp.minimum(s, n_tiles - 1), 0)),
                  _const_spec((1, D_MODEL)), _const_spec(w_in.shape), _const_spec((1, half))]
        + cast_in,
        out_specs=[pl.BlockSpec((tm, 2 * D_MODEL), lambda s: (jnp.maximum(s - 1, 0), 0))]
        + cast_out,
        out_shape=[jax.ShapeDtypeStruct((rows, 2 * D_MODEL), BF16)] + cast_shapes,
        scratch_shapes=[pltpu.VMEM((2, tm, 6 * D_MODEL), BF16),
                        pltpu.VMEM((RET_HEADS, RET_QK_DIM, RET_V_DIM), F32),
                        pltpu.VMEM((tm, half), F32), pltpu.VMEM((tm, half), F32),
                        pltpu.VMEM((RET_HEADS, RET_CHUNK, RET_CHUNK), F32)],
        compiler_params=pltpu.CompilerParams(dimension_semantics=("arbitrary",)),
        name="l1_mix",
    )(h, gain, w_in, freq, *[w for _, w in cast_weights])


def _inv_freq(half, theta):
    return jnp.power(jnp.asarray(theta, F32), -jnp.arange(half, dtype=F32) / half)


def _block_diag(w):
    heads, wi, wo = w.shape
    eye = jnp.eye(heads, dtype=w.dtype)
    return (eye[:, None, :, None] * w[:, :, None, :]).reshape(heads * wi, heads * wo)


def kernel(x, meta_tokens, mix_norm_ab, ab_w_in, lru_conv_w, lru_conv_b, lru_w_a, lru_b_a, lru_w_i, lru_b_i, lru_lambda, q_norm, k_norm, attn_sinks, ab_w_out, mix_norm_ret, ret_w_in, ret_w_out, ffn_norm, ffn_w_gu, ffn_w_down):
    batch, seq, _ = x.shape
    seq_rows = META_PAD + N_META + seq
    tiles_per_seq = seq_rows // ROW_TILE

    x2d = x.reshape(batch * seq, D_MODEL)
    meta_blk = jnp.concatenate([jnp.zeros((META_PAD, D_MODEL), x.dtype),
                                meta_tokens.astype(x.dtype)], axis=0)

    f_att = _inv_freq(ROT_DIM // 2, ROPE_THETA)
    f_att = jnp.concatenate([f_att, f_att, jnp.zeros((ATT_HEAD_DIM - ROT_DIM,), F32)])
    f_att = jnp.tile(f_att, LANES // ATT_HEAD_DIM).reshape(1, LANES)
    f_ret = _inv_freq(RET_QK_DIM // 2, RET_THETA).reshape(1, RET_QK_DIM // 2)

    row_vec = lambda v: v.reshape(1, -1).astype(F32)
    two_heads = lambda v: jnp.tile(v.reshape(1, -1).astype(F32), (1, 2))

    xr, gate, q, k, v, w_gu0, w_down0 = _inproj0(
        x2d, meta_blk, row_vec(mix_norm_ab[0]), ab_w_in[0].astype(BF16), two_heads(q_norm[0]),
        two_heads(k_norm[0]), f_att, [(0, ffn_w_gu), (0, ffn_w_down)], batch, tiles_per_seq, seq)
    w_gates = jnp.concatenate([_block_diag(lru_w_a[0]), _block_diag(lru_w_i[0])], axis=1)
    b_gates = jnp.concatenate([lru_b_a[0].reshape(1, -1), lru_b_i[0].reshape(1, -1)], axis=1)
    w_att = ab_w_out[0][LRU_WIDTH:].reshape(ATT_KV_HEADS, ATT_GROUP, ATT_HEAD_DIM, D_MODEL)
    w_att = w_att.transpose(1, 0, 2, 3).reshape(Q_WIDTH, D_MODEL)
    w_out0 = jnp.concatenate([ab_w_out[0][:LRU_WIDTH], w_att], axis=0).astype(BF16)
    h, w_in1, w_out1 = _l0_mix_ffn(
        attn_sinks[0].astype(F32), x2d, meta_blk, xr, gate, q, k, v, lru_conv_w[0],
        row_vec(lru_conv_b[0]), w_gates.astype(BF16), b_gates.astype(F32), row_vec(lru_lambda[0]),
        w_out0, row_vec(ffn_norm[0]), w_gu0, w_down0, [(0, ret_w_in), (0, ret_w_out)],
        tiles_per_seq, seq)

    y_ret, w_gu1, w_down1 = _l1_mix(h, row_vec(mix_norm_ret[0]), w_in1, f_ret,
                                    [(1, ffn_w_gu), (1, ffn_w_down)], tiles_per_seq)
    out = _final_outproj_ffn(h, y_ret, w_out1, row_vec(ffn_norm[1]), w_gu1, w_down1,
                             batch, seq_rows, seq)
    return out.reshape(batch, seq, D_MODEL)
```
